```python
import math
import jax, jax.numpy as jnp
from jax import lax
import numpy as np

D_MODEL = 1024
BATCH = 4
SEQ = 8192
DEPTH = 2

N_MIXERS = 2
A_CONFIGS = ((128, 1), (512, 4), (2048, 16))
A_GROUPS = len(A_CONFIGS)
A_HEADS = 16
A_HEAD_DIM = D_MODEL // A_HEADS
A_WIDTH = A_HEADS * A_HEAD_DIM
N_BUCKETS = 32
MAX_DISTANCE = 2048
B_HEADS = 4
B_DK = D_MODEL // 2 // B_HEADS
B_DV = D_MODEL // B_HEADS
B_QK = B_HEADS * B_DK
B_V = B_HEADS * B_DV
B_GATE_RANK = 16
B_TAU = 16.0
B_CHUNK = 64
B_IN = 2 * B_QK + B_V + B_GATE_RANK + B_V
D_FF = 2816
CONV_W = 3
EPS = 1e-6
NEG_INF = -1e30
N_A = (DEPTH + 1) // 2
N_B = DEPTH // 2

kernel_name = "hybrid_dilated_gla_convffn"


def rmsnorm(x, g):
    x32 = x.astype(jnp.float32)
    y = x32 * lax.rsqrt(jnp.mean(x32 * x32, axis=-1, keepdims=True) + EPS)
    return (y * g.astype(jnp.float32)).astype(x.dtype)


def t5_bucket(dist):
    max_exact = N_BUCKETS // 2
    n = jnp.maximum(dist, max_exact).astype(jnp.float32)
    large = max_exact + (jnp.log(n / max_exact) / math.log(MAX_DISTANCE / max_exact)
                         * (N_BUCKETS - max_exact)).astype(jnp.int32)
    large = jnp.minimum(large, N_BUCKETS - 1)
    return jnp.where(dist < max_exact, dist, large)


def dilated_branch(q, k, v, table, window, dilation):
    B, S, H, E = q.shape
    blk = window // dilation
    L = S // dilation
    nb = -(-L // blk)
    Lp = nb * blk

    def to_blocks(t):
        t = t.reshape(B, L, dilation, H, E)
        t = jnp.pad(t, ((0, 0), (0, Lp - L), (0, 0), (0, 0), (0, 0)))
        return t.reshape(B, nb, blk, dilation, H, E)

    def with_prev(t):
        prev = jnp.pad(t[:, :-1], ((0, 0), (1, 0), (0, 0), (0, 0), (0, 0), (0, 0)))
        return jnp.concatenate([prev, t], axis=2)

    qb = to_blocks(q)
    kk = with_prev(to_blocks(k))
    vv = with_prev(to_blocks(v))

    qi = jnp.arange(blk)[:, None]
    ki = jnp.arange(2 * blk)[None, :]
    steps = qi + blk - ki
    band = (steps >= 0) & (steps <= blk)
    first = (jnp.arange(nb)[:, None, None] == 0) & (ki[None] < blk)
    mask = band[None] & ~first
    bucket = t5_bucket(jnp.clip(steps, 0, blk) * dilation)
    bias = jnp.transpose(table[bucket], (2, 0, 1)).astype(jnp.float32)

    s = jnp.einsum('bnqrhe,bnkrhe->bnrhqk', qb, kk).astype(jnp.float32) + bias[None, None, None]
    s = jnp.where(mask[None, :, None, None], s, NEG_INF)
    m = jnp.max(s, axis=-1, keepdims=True)
    p = jnp.exp(s - m)
    den = jnp.sum(p, axis=-1)
    o = jnp.einsum('bnrhqk,bnkrhe->bnqrhe', p, vv.astype(jnp.float32))
    den_t = jnp.transpose(den, (0, 1, 4, 2, 3))
    o = o / den_t[..., None]
    lse = jnp.transpose(m[..., 0] + jnp.log(den), (0, 1, 4, 2, 3))
    o = o.reshape(B, Lp, dilation, H, E)[:, :L].reshape(B, S, H, E)
    lse = lse.reshape(B, Lp, dilation, H)[:, :L].reshape(B, S, H)
    return o, lse


def dilated_mixture(h, w_in, w_out, rel_bias):
    B, S, _ = h.shape
    qkv = (h @ w_in).reshape(B, S, A_GROUPS, 3, A_HEADS, A_HEAD_DIM)
    scale = A_HEAD_DIM ** -0.5
    outs, lses = [], []
    for g, (window, dilation) in enumerate(A_CONFIGS):
        o, lse = dilated_branch(qkv[:, :, g, 0] * scale, qkv[:, :, g, 1], qkv[:, :, g, 2],
                                rel_bias[:, g * A_HEADS:(g + 1) * A_HEADS], window, dilation)
        outs.append(o)
        lses.append(lse)
    wts = jax.nn.softmax(jnp.stack(lses), axis=0)
    o = jnp.sum(wts[..., None] * jnp.stack(outs), axis=0)
    return o.reshape(B, S, A_WIDTH).astype(h.dtype) @ w_out


def gla_mixer(h, w_in, w_gate, b_gate, g_norm, w_out):
    B, S, _ = h.shape
    N = S // B_CHUNK
    proj = h @ w_in
    o0 = 0
    q = proj[..., o0:o0 + B_QK]; o0 += B_QK
    k = proj[..., o0:o0 + B_QK]; o0 += B_QK
    v = proj[..., o0:o0 + B_V]; o0 += B_V
    glr = proj[..., o0:o0 + B_GATE_RANK]; o0 += B_GATE_RANK
    r = proj[..., o0:o0 + B_V]
    gk = jax.nn.log_sigmoid((glr @ w_gate + b_gate).astype(jnp.float32)) / B_TAU

    def chunks(t, e):
        t = t.astype(jnp.float32).reshape(B, N, B_CHUNK, B_HEADS, e)
        return jnp.transpose(t, (0, 3, 1, 2, 4))

    q, k, gk = chunks(q, B_DK) * (B_DK ** -0.5), chunks(k, B_DK), chunks(gk, B_DK)
    v = chunks(v, B_DV)
    bcum = jnp.cumsum(gk, axis=3)
    blast = bcum[:, :, :, -1:, :]
    q_t = q * jnp.exp(bcum)
    k_t = k * jnp.exp(-bcum)
    k_d = k * jnp.exp(blast - bcum)
    causal = jnp.tril(jnp.ones((B_CHUNK, B_CHUNK), dtype=bool))
    A = jnp.where(causal, jnp.einsum('bhncd,bhnsd->bhncs', q_t, k_t), 0.0)
    o_intra = jnp.einsum('bhncs,bhnse->bhnce', A, v)
    kv = jnp.einsum('bhncd,bhnce->bhnde', k_d, v)
    decay = jnp.exp(blast[:, :, :, 0, :])

    def step(state, inp):
        dec, inc = inp
        return dec[..., None] * state + inc, state

    _, s_prev = lax.scan(step, jnp.zeros((B, B_HEADS, B_DK, B_DV), jnp.float32),
                         (jnp.moveaxis(decay, 2, 0), jnp.moveaxis(kv, 2, 0)))
    s_prev = jnp.moveaxis(s_prev, 0, 2)
    o = o_intra + jnp.einsum('bhncd,bhnde->bhnce', q_t, s_prev)
    o = jnp.transpose(o, (0, 2, 3, 1, 4)).reshape(B, S, B_HEADS, B_DV)
    o = rmsnorm(o, g_norm).reshape(B, S, B_V).astype(h.dtype)
    return (o * jax.nn.silu(r)) @ w_out


def conv_ffn(h, w_up, conv_w, conv_b, w_down):
    u = h @ w_up
    up = jnp.pad(u, ((0, 0), (CONV_W - 1, 0), (0, 0)))
    S = h.shape[1]
    u = conv_b + sum(conv_w[j] * up[:, j:j + S] for j in range(CONV_W))
    a, b = jnp.split(u, 2, axis=-1)
    return (jax.nn.silu(a) * b) @ w_down


def setup_inputs(seed: int = 0) -> dict:
    key = jax.random.key(seed)
    ks = jax.random.split(key, 24)
    f32 = jnp.float32
    nrm = lambda k, shape, s: jax.random.normal(k, shape, f32) * s
    D = D_MODEL
    return {
        "x": nrm(ks[0], (BATCH, SEQ, D), 1.0),
        "c": nrm(ks[1], (BATCH, D), 1.0),
        "w_in_a": nrm(ks[2], (N_A, D, A_GROUPS * 3 * A_WIDTH), D ** -0.5),
        "w_out_a": nrm(ks[3], (N_A, A_WIDTH, D), A_WIDTH ** -0.5),
        "rel_bias": nrm(ks[4], (N_BUCKETS, A_GROUPS * A_HEADS), 0.5),
        "w_in_b": nrm(ks[5], (N_B, D, B_IN), D ** -0.5),
        "w_gate_b": nrm(ks[6], (N_B, B_GATE_RANK, B_QK), B_GATE_RANK ** -0.5),
        "b_gate_b": nrm(ks[7], (N_B, B_QK), 0.1),
        "gnorm_b": 1.0 + nrm(ks[8], (N_B, B_DV), 0.02),
        "w_out_b": nrm(ks[9], (N_B, B_V, D), B_V ** -0.5),
        "norm_mix": 1.0 + nrm(ks[10], (DEPTH, D), 0.02),
        "norm_ffn": 1.0 + nrm(ks[11], (DEPTH, D), 0.02),
        "w_ada": nrm(ks[12], (DEPTH, D, 6 * D), 0.5 * D ** -0.5),
        "b_ada": nrm(ks[13], (DEPTH, 6 * D), 0.02),
        "w_up": nrm(ks[14], (DEPTH, D, 2 * D_FF), D ** -0.5),
        "conv_w": nrm(ks[15], (DEPTH, CONV_W, 2 * D_FF), CONV_W ** -0.5),
        "conv_b": nrm(ks[16], (DEPTH, 2 * D_FF), 0.02),
        "w_down": nrm(ks[17], (DEPTH, D_FF, D), D_FF ** -0.5),
        "norm_final": 1.0 + nrm(ks[18], (D,), 0.02),
    }


def reference(x, c, w_in_a, w_out_a, rel_bias, w_in_b, w_gate_b, b_gate_b, gnorm_b, w_out_b,
              norm_mix, norm_ffn, w_ada, b_ada, w_up, conv_w, conv_b, w_down, norm_final):
    for i in range(DEPTH):
        mod = jax.nn.silu(c) @ w_ada[i] + b_ada[i]
        sh1, sc1, g1, sh2, sc2, g2 = [m[:, None, :] for m in jnp.split(mod, 6, axis=-1)]
        h = rmsnorm(x, norm_mix[i]) * (1.0 + sc1) + sh1
        if i % N_MIXERS == 0:
            j = i // N_MIXERS
            y = dilated_mixture(h, w_in_a[j], w_out_a[j], rel_bias)
        else:
            j = i // N_MIXERS
            y = gla_mixer(h, w_in_b[j], w_gate_b[j], b_gate_b[j], gnorm_b[j], w_out_b[j])
        x = x + g1 * y
        h = rmsnorm(x, norm_ffn[i]) * (1.0 + sc2) + sh2
        x = x + g2 * conv_ffn(h, w_up[i], conv_w[i], conv_b[i], w_down[i])
    return rmsnorm(x, norm_final)
```

```python
import functools
import math

import jax
import jax.numpy as jnp
from jax import lax
from jax.experimental import pallas as pl
from jax.experimental.pallas import tpu as pltpu

A_CONFIGS = ((128, 1), (512, 4), (2048, 16))
A_GROUPS = len(A_CONFIGS)
A_HEADS = 16
A_HEAD_DIM = 64
N_BUCKETS = 32
MAX_DISTANCE = 2048
B_HEADS = 4
B_DK = 128
B_DV = 256
B_QK = B_HEADS * B_DK
B_V = B_HEADS * B_DV
B_GATE_RANK = 16
B_TAU = 16.0
B_CHUNK = 64
CONV_W = 3
EPS = 1e-6
NEG_INF = -1e30

LANES = 128
MXU_DTYPE = jnp.bfloat16
ATT_BLK = 128
ATT_TILE = 2048
ATT_UNITS = ATT_TILE // ATT_BLK
PAIR_W = 3 * LANES
N_PAIRS = A_HEADS // 2
QKV_TN = 2 * PAIR_W
ROW_CHUNK = 512
FFN_TM = 1024
FFN_TF = 256
FFN_ROWS = 256
HALO = 8
OUT_TM = 1024
GLA_TM = 512
VMEM_LIMIT = 56 * 1024 * 1024

f32 = jnp.float32


def _mm(a, b):
    return jnp.dot(a, b, preferred_element_type=f32)


def _mm_nt(a, b):
    return lax.dot_general(a, b, (((1,), (1,)), ((), ())), preferred_element_type=f32)


def _mm_tn(a, b):
    return lax.dot_general(a, b, (((0,), (0,)), ((), ())), preferred_element_type=f32)


def _modnorm(x, gamma, scale, shift):
    ms = jnp.mean(x * x, axis=-1, keepdims=True)
    y = x * lax.rsqrt(ms + EPS) * gamma
    return y * (1.0 + scale) + shift


def _silu(x):
    return x * (1.0 / (1.0 + jnp.exp(-x)))


def _adaln_kernel(c_ref, w_ref, b_ref, o_ref):
    s = _silu(c_ref[...]).astype(MXU_DTYPE)
    o_ref[0] = _mm(s, w_ref[0].astype(MXU_DTYPE)) + b_ref[0]


def _adaln(c, w_ada, b_ada):
    depth, d, n = w_ada.shape
    bsz = c.shape[0]
    rows = 8 * pl.cdiv(bsz, 8)
    c_pad = jnp.pad(c, ((0, rows - bsz), (0, 0)))
    out = pl.pallas_call(
        _adaln_kernel,
        grid=(depth, n // d),
        in_specs=[
            pl.BlockSpec((rows, d), lambda l, j: (0, 0)),
            pl.BlockSpec((1, d, d), lambda l, j: (l, 0, j)),
            pl.BlockSpec((1, 1, d), lambda l, j: (l, 0, j)),
        ],
        out_specs=pl.BlockSpec((1, rows, d), lambda l, j: (l, 0, j)),
        out_shape=jax.ShapeDtypeStruct((depth, rows, n), f32),
        compiler_params=pltpu.CompilerParams(
            dimension_semantics=("parallel", "parallel"), vmem_limit_bytes=VMEM_LIMIT),
        name="adaln",
    )(c_pad, w_ada, b_ada.reshape(depth, 1, n))
    return out[:, :bsz].reshape(depth, bsz, n // d, d)


def _qkv_kernel(x_ref, mod_ref, gam_ref, w_ref, o_ref, h_ref, xs_ref, *, dilations, tiles_per_group):
    j = pl.program_id(2)
    nlb = xs_ref.shape[0]

    @pl.when(j == 0)
    def _():
        for lb in range(nlb):
            xs_ref[lb] = x_ref[0, :, lb * LANES:(lb + 1) * LANES]

    for gi, d in enumerate(dilations):
        @pl.when(j == gi * tiles_per_group)
        def _(d=d):
            per = ATT_UNITS // d
            for c in range(ATT_UNITS):
                r, q = divmod(c, per)
                if d == 1:
                    xs = x_ref[0, c * ATT_BLK:(c + 1) * ATT_BLK, :]
                else:
                    rows = pl.ds(r + q * ATT_BLK * d, ATT_BLK, stride=d)
                    xs = jnp.concatenate([xs_ref[lb, rows, :] for lb in range(nlb)], axis=1)
                h = _modnorm(xs, gam_ref[...], mod_ref[0, 1:2, :], mod_ref[0, 0:1, :])
                h_ref[c * ATT_BLK:(c + 1) * ATT_BLK, :] = h.astype(h_ref.dtype)

    w = w_ref[...]
    for rc in range(ATT_TILE // ROW_CHUNK):
        rows = slice(rc * ROW_CHUNK, (rc + 1) * ROW_CHUNK)
        res = _mm(h_ref[rows, :], w)
        o_ref[0, 0, 0, rows, :] = res[:, :PAIR_W].astype(o_ref.dtype)
        o_ref[0, 0, 1, rows, :] = res[:, PAIR_W:].astype(o_ref.dtype)


def _qkv_proj(x, mod, gamma, w_perm):
    bsz, s, d = x.shape
    dilations = tuple(dl for _, dl in A_CONFIGS)
    tiles_per_group = N_PAIRS * PAIR_W // QKV_TN
    kern = functools.partial(_qkv_kernel, dilations=dilations, tiles_per_group=tiles_per_group)
    return pl.pallas_call(
        kern,
        grid=(bsz, s // ATT_TILE, A_GROUPS * tiles_per_group),
        in_specs=[
            pl.BlockSpec((1, ATT_TILE, d), lambda b, i, j: (b, i, 0)),
            pl.BlockSpec((1, 6, d), lambda b, i, j: (b, 0, 0)),
            pl.BlockSpec((1, d), lambda b, i, j: (0, 0)),
            pl.BlockSpec((d, QKV_TN), lambda b, i, j: (0, j)),
        ],
        out_specs=pl.BlockSpec(
            (1, 1, 2, ATT_TILE, PAIR_W),
            lambda b, i, j: (j // tiles_per_group, b, j % tiles_per_group, i, 0)),
        out_shape=jax.ShapeDtypeStruct((A_GROUPS, bsz, N_PAIRS, s, PAIR_W), MXU_DTYPE),
        scratch_shapes=[pltpu.VMEM((ATT_TILE, d), MXU_DTYPE),
                        pltpu.VMEM((d // LANES, ATT_TILE, LANES), f32)],
        compiler_params=pltpu.CompilerParams(
            dimension_semantics=("parallel", "parallel", "arbitrary"),
            vmem_limit_bytes=VMEM_LIMIT),
        name="qkv_proj",
    )(x, mod, gamma, w_perm)


def _bias_kernel(tab_ref, bkt_ref, o_ref):
    col = pl.program_id(0) * A_HEADS + pl.program_id(1)
    bkt = bkt_ref[0]
    acc = jnp.full(bkt.shape, NEG_INF, f32)
    for k in range(N_BUCKETS):
        acc = jnp.where(bkt == k, tab_ref[k, col], acc)
    o_ref[0, 0] = acc


def _t5_bucket(dist):
    max_exact = N_BUCKETS // 2
    n = jnp.maximum(dist, max_exact).astype(f32)
    large = max_exact + (jnp.log(n / max_exact) / math.log(MAX_DISTANCE / max_exact)
                         * (N_BUCKETS - max_exact)).astype(jnp.int32)
    large = jnp.minimum(large, N_BUCKETS - 1)
    return jnp.where(dist < max_exact, dist, large)


def _rel_bias(rel_bias):
    qi = jnp.arange(ATT_BLK)[:, None]
    ki = jnp.arange(2 * ATT_BLK)[None, :]
    steps = qi + ATT_BLK - ki
    band = (steps >= 0) & (steps <= ATT_BLK)
    bucket = jnp.stack([
        jnp.where(band, _t5_bucket(jnp.clip(steps, 0, ATT_BLK) * dl), -1)
        for _, dl in A_CONFIGS]).astype(jnp.int32)
    return pl.pallas_call(
        _bias_kernel,
        grid=(A_GROUPS, A_HEADS),
        in_specs=[
            pl.BlockSpec(memory_space=pltpu.SMEM),
            pl.BlockSpec((1, ATT_BLK, 2 * ATT_BLK), lambda g, h: (g, 0, 0)),
        ],
        out_specs=pl.BlockSpec((1, 1, ATT_BLK, 2 * ATT_BLK), lambda g, h: (g, h, 0, 0)),
        out_shape=jax.ShapeDtypeStruct((A_GROUPS, A_HEADS, ATT_BLK, 2 * ATT_BLK), f32),
        name="rel_bias",
    )(rel_bias, bucket)


def _attn_kernel(qkv_ref, bias_ref, out_ref, prev0, prev1, prev2, o1, l1, o2, l2):
    first = pl.program_id(2) == 0

    @pl.when(first)
    def _():
        prev0[...] = jnp.zeros_like(prev0)
        prev1[...] = jnp.zeros_like(prev1)
        prev2[...] = jnp.zeros_like(prev2)

    lane = lax.broadcasted_iota(jnp.int32, (1, LANES), 1)
    lo = lane < A_HEAD_DIM
    scale = A_HEAD_DIM ** -0.5
    qmask = (jnp.where(lo, scale, 0.0).astype(MXU_DTYPE),
             jnp.where(lo, 0.0, scale).astype(MXU_DTYPE))

    def unit(g, rc, prev_ref, rp):
        q = qkv_ref[g, 0, 0, rc:rc + ATT_BLK, 0:LANES]
        kc = qkv_ref[g, 0, 0, rc:rc + ATT_BLK, LANES:2 * LANES]
        vc = qkv_ref[g, 0, 0, rc:rc + ATT_BLK, 2 * LANES:3 * LANES]
        if prev_ref is None:
            kp = qkv_ref[g, 0, 0, rp:rp + ATT_BLK, LANES:2 * LANES]
            vp = qkv_ref[g, 0, 0, rp:rp + ATT_BLK, 2 * LANES:3 * LANES]
        else:
            kp = prev_ref[rp:rp + ATT_BLK, 0:LANES]
            vp = prev_ref[rp:rp + ATT_BLK, LANES:2 * LANES]
        accs, rdens, lses = [], [], []
        for hh in range(2):
            qm = q * qmask[hh]
            sp = _mm_nt(qm, kp) + bias_ref[g, hh, :, 0:ATT_BLK]
            sc = _mm_nt(qm, kc) + bias_ref[g, hh, :, ATT_BLK:2 * ATT_BLK]
            if prev_ref is not None:
                sp = jnp.where(first, NEG_INF, sp)
            m = jnp.maximum(jnp.max(sp, axis=-1, keepdims=True), jnp.max(sc, axis=-1, keepdims=True))
            pp = jnp.exp(sp - m)
            pc = jnp.exp(sc - m)
            den = jnp.sum(pp, axis=-1, keepdims=True) + jnp.sum(pc, axis=-1, keepdims=True)
            accs.append(_mm(pp.astype(MXU_DTYPE), vp) + _mm(pc.astype(MXU_DTYPE), vc))
            rdens.append(1.0 / den)
            lses.append(m + jnp.log(den))
        o = jnp.where(lo, accs[0], accs[1]) * jnp.where(lo, rdens[0], rdens[1])
        lse = jnp.broadcast_to(jnp.where(lo, lses[0], lses[1]), o.shape)
        return o, lse

    d2 = A_CONFIGS[2][1]
    for u in range(ATT_UNITS):
        o, lse = unit(2, u * ATT_BLK, prev2, u * ATT_BLK)
        o2[pl.ds(u, ATT_BLK, stride=d2), :] = o
        l2[pl.ds(u, ATT_BLK, stride=d2), :] = lse

    d1 = A_CONFIGS[1][1]
    per = ATT_UNITS // d1
    for u in range(ATT_UNITS):
        r, q = divmod(u, per)
        rc = (r * per + q) * ATT_BLK
        if q == 0:
            o, lse = unit(1, rc, prev1, (r * per + per - 1) * ATT_BLK)
        else:
            o, lse = unit(1, rc, None, rc - ATT_BLK)
        o1[pl.ds(q * ATT_BLK * d1 + r, ATT_BLK, stride=d1), :] = o
        l1[pl.ds(q * ATT_BLK * d1 + r, ATT_BLK, stride=d1), :] = lse

    for u in range(ATT_UNITS):
        rc = u * ATT_BLK
        if u == 0:
            o0, lse0 = unit(0, rc, prev0, 0)
        else:
            o0, lse0 = unit(0, rc, None, rc - ATT_BLK)
        rows = slice(rc, rc + ATT_BLK)
        lse1, lse2 = l1[rows, :], l2[rows, :]
        mx = jnp.maximum(jnp.maximum(lse0, lse1), lse2)
        e0, e1, e2 = jnp.exp(lse0 - mx), jnp.exp(lse1 - mx), jnp.exp(lse2 - mx)
        mix = (e0 * o0 + e1 * o1[rows, :] + e2 * o2[rows, :]) * (1.0 / (e0 + e1 + e2))
        out_ref[0, rows, :] = mix.astype(out_ref.dtype)

    prev0[...] = qkv_ref[0, 0, 0, ATT_TILE - ATT_BLK:ATT_TILE, LANES:3 * LANES]
    prev1[...] = qkv_ref[1, 0, 0, :, LANES:3 * LANES]
    prev2[...] = qkv_ref[2, 0, 0, :, LANES:3 * LANES]


def _attention(qkv, bias):
    _, bsz, _, s, _ = qkv.shape
    return pl.pallas_call(
        _attn_kernel,
        grid=(bsz, N_PAIRS, s // ATT_TILE),
        in_specs=[
            pl.BlockSpec((A_GROUPS, 1, 1, ATT_TILE, PAIR_W), lambda b, p, i: (0, b, p, i, 0)),
            pl.BlockSpec((A_GROUPS, 2, ATT_BLK, 2 * ATT_BLK), lambda b, p, i: (0, p, 0, 0)),
        ],
        out_specs=pl.BlockSpec((1, ATT_TILE, LANES), lambda b, p, i: (b, i, p)),
        out_shape=jax.ShapeDtypeStruct((bsz, s, A_HEADS * A_HEAD_DIM), MXU_DTYPE),
        scratch_shapes=[
            pltpu.VMEM((ATT_BLK, 2 * LANES), MXU_DTYPE),
            pltpu.VMEM((ATT_TILE, 2 * LANES), MXU_DTYPE),
            pltpu.VMEM((ATT_TILE, 2 * LANES), MXU_DTYPE),
            pltpu.VMEM((ATT_TILE, LANES), f32),
            pltpu.VMEM((ATT_TILE, LANES), f32),
            pltpu.VMEM((ATT_TILE, LANES), f32),
            pltpu.VMEM((ATT_TILE, LANES), f32),
        ],
        compiler_params=pltpu.CompilerParams(
            dimension_semantics=("parallel", "parallel", "arbitrary"),
            vmem_limit_bytes=VMEM_LIMIT),
        name="dilated_attn",
    )(qkv, bias)


def _outproj_kernel(o_ref, w_ref, x_ref, mod_ref, out_ref):
    w = w_ref[...]
    gate = mod_ref[0, 2:3, :]
    for rc in range(OUT_TM // ROW_CHUNK):
        rows = slice(rc * ROW_CHUNK, (rc + 1) * ROW_CHUNK)
        out_ref[0, rows, :] = x_ref[0, rows, :] + gate * _mm(o_ref[0, rows, :], w)


def _out_proj(o, w, x, mod):
    bsz, s, d = x.shape
    k = o.shape[-1]
    return pl.pallas_call(
        _outproj_kernel,
        grid=(bsz, s // OUT_TM),
        in_specs=[
            pl.BlockSpec((1, OUT_TM, k), lambda b, i: (b, i, 0)),
            pl.BlockSpec((k, d), lambda b, i: (0, 0)),
            pl.BlockSpec((1, OUT_TM, d), lambda b, i: (b, i, 0)),
            pl.BlockSpec((1, 6, d), lambda b, i: (b, 0, 0)),
        ],
        out_specs=pl.BlockSpec((1, OUT_TM, d), lambda b, i: (b, i, 0)),
        out_shape=jax.ShapeDtypeStruct((bsz, s, d), f32),
        compiler_params=pltpu.CompilerParams(
            dimension_semantics=("parallel", "parallel"), vmem_limit_bytes=VMEM_LIMIT),
        name="out_proj",
    )(o, w, x, mod)


def _ffn_kernel(x_ref, halo_ref, mod_ref, gam_ref, wup_ref, cw_ref, cb_ref, wdn_ref, *rest, final):
    if final:
        gfin_ref, out_ref, h_ref, u_ref, act_ref, acc_ref = rest
    else:
        out_ref, h_ref, u_ref, act_ref, acc_ref = rest
    gam, scale, shift = gam_ref[...], mod_ref[0, 4:5, :], mod_ref[0, 3:4, :]

    halo = _modnorm(halo_ref[0], gam, scale, shift)
    halo = jnp.where(pl.program_id(1) == 0, 0.0, halo)
    h_ref[0:HALO, :] = halo.astype(h_ref.dtype)
    for rc in range(FFN_TM // ROW_CHUNK):
        rows = slice(rc * ROW_CHUNK, (rc + 1) * ROW_CHUNK)
        h = _modnorm(x_ref[0, rows, :], gam, scale, shift)
        h_ref[HALO + rc * ROW_CHUNK:HALO + (rc + 1) * ROW_CHUNK, :] = h.astype(h_ref.dtype)
    acc_ref[...] = jnp.zeros_like(acc_ref)

    def chunk(c, carry):
        u_ref[...] = _mm(h_ref[...], wup_ref[c])
        cw = cw_ref[c]
        cb = cb_ref[c]
        for rb in range(FFN_TM // FFN_ROWS):
            base = HALO + rb * FFN_ROWS
            v = cb
            for t in range(CONV_W):
                off = base - (CONV_W - 1) + t
                v = v + cw[t:t + 1, :] * u_ref[off:off + FFN_ROWS, :]
            a, b = v[:, :FFN_TF], v[:, FFN_TF:]
            act_ref[rb * FFN_ROWS:(rb + 1) * FFN_ROWS, :] = (_silu(a) * b).astype(act_ref.dtype)
        acc_ref[...] += _mm(act_ref[...], wdn_ref[c])
        return carry

    lax.fori_loop(0, wup_ref.shape[0], chunk, 0)

    gate = mod_ref[0, 5:6, :]
    for rc in range(FFN_TM // ROW_CHUNK):
        rows = slice(rc * ROW_CHUNK, (rc + 1) * ROW_CHUNK)
        y = x_ref[0, rows, :] + gate * acc_ref[rows, :]
        if final:
            ms = jnp.mean(y * y, axis=-1, keepdims=True)
            y = y * lax.rsqrt(ms + EPS) * gfin_ref[...]
        out_ref[0, rows, :] = y


def _conv_ffn(x, mod, gamma, wup_c, cw_c, cb_c, wdn_c, gamma_final=None):
    bsz, s, d = x.shape
    nchunk = wup_c.shape[0]
    final = gamma_final is not None
    tiles_per_halo = FFN_TM // HALO
    resident = dict(pipeline_mode=pl.Buffered(1))
    in_specs = [
        pl.BlockSpec((1, FFN_TM, d), lambda b, i: (b, i, 0)),
        pl.BlockSpec((1, HALO, d), lambda b, i: (b, jnp.maximum(i * tiles_per_halo - 1, 0), 0)),
        pl.BlockSpec((1, 6, d), lambda b, i: (b, 0, 0)),
        pl.BlockSpec((1, d), lambda b, i: (0, 0)),
        pl.BlockSpec((nchunk, d, 2 * FFN_TF), lambda b, i: (0, 0, 0), **resident),
        pl.BlockSpec((nchunk, CONV_W, 2 * FFN_TF), lambda b, i: (0, 0, 0)),
        pl.BlockSpec((nchunk, 1, 2 * FFN_TF), lambda b, i: (0, 0, 0)),
        pl.BlockSpec((nchunk, FFN_TF, d), lambda b, i: (0, 0, 0), **resident),
    ]
    args = [x, x, mod, gamma, wup_c, cw_c, cb_c, wdn_c]
    if final:
        in_specs.append(pl.BlockSpec((1, d), lambda b, i: (0, 0)))
        args.append(gamma_final)
    return pl.pallas_call(
        functools.partial(_ffn_kernel, final=final),
        grid=(bsz, s // FFN_TM),
        in_specs=in_specs,
        out_specs=pl.BlockSpec((1, FFN_TM, d), lambda b, i: (b, i, 0)),
        out_shape=jax.ShapeDtypeStruct((bsz, s, d), f32),
        scratch_shapes=[
            pltpu.VMEM((FFN_TM + HALO, d), MXU_DTYPE),
            pltpu.VMEM((FFN_TM + HALO, 2 * FFN_TF), f32),
            pltpu.VMEM((FFN_TM, FFN_TF), MXU_DTYPE),
            pltpu.VMEM((FFN_TM, d), f32),
        ],
        compiler_params=pltpu.CompilerParams(
            dimension_semantics=("parallel", "parallel"), vmem_limit_bytes=VMEM_LIMIT),
        name="conv_ffn_final" if final else "conv_ffn",
    )(*args)


def _ffn_weights(w_up, conv_w, conv_b, w_down):
    d, two_ff = w_up.shape
    d_ff = two_ff // 2
    n = d_ff // FFN_TF

    def cols(t):
        lead = t.shape[:-1]
        t = t.reshape(lead + (2, n, FFN_TF))
        t = jnp.moveaxis(t, -2, 0)
        return t.reshape((n,) + lead + (2 * FFN_TF,))

    return (cols(w_up).astype(MXU_DTYPE), cols(conv_w), cols(conv_b[None, :]),
            w_down.reshape(n, FFN_TF, d).astype(MXU_DTYPE))


def _gla_kernel(x_ref, mod_ref, gam_ref, win_ref, wglr_ref, wgate_ref, bgate_ref, gnorm_ref, wout_ref,
                out_ref, proj_ref, gk_ref, og_ref, state_ref):
    @pl.when(pl.program_id(1) == 0)
    def _():
        state_ref[...] = jnp.zeros_like(state_ref)

    h = _modnorm(x_ref[0], gam_ref[...], mod_ref[0, 1:2, :], mod_ref[0, 0:1, :]).astype(MXU_DTYPE)
    proj_ref[...] = _mm(h, win_ref[...])
    glr = _mm(h, wglr_ref[...]).astype(MXU_DTYPE)
    z = _mm(glr, wgate_ref[...]) + bgate_ref[...]
    gk_ref[...] = (jnp.minimum(z, 0.0) - jnp.log1p(jnp.exp(-jnp.abs(z)))) * (1.0 / B_TAU)

    ri = lax.broadcasted_iota(jnp.int32, (B_CHUNK, B_CHUNK), 0)
    ci = lax.broadcasted_iota(jnp.int32, (B_CHUNK, B_CHUNK), 1)
    causal = ci <= ri
    tri = causal.astype(MXU_DTYPE)
    gnorm = gnorm_ref[...]
    qscale = B_DK ** -0.5
    r_off = 2 * B_QK + B_V

    def chunk(ci_, carry):
        r0 = pl.multiple_of(ci_ * B_CHUNK, B_CHUNK)
        rows = pl.ds(r0, B_CHUNK)
        gk = gk_ref[rows, :]
        g_hi = gk.astype(MXU_DTYPE)
        rem = gk - g_hi.astype(f32)
        g_mid = rem.astype(MXU_DTYPE)
        g_lo = (rem - g_mid.astype(f32)).astype(MXU_DTYPE)
        bcum = _mm(tri, g_hi) + _mm(tri, g_mid) + _mm(tri, g_lo)
        blast = bcum[B_CHUNK - 1:B_CHUNK, :]
        q_t = (proj_ref[rows, 0:B_QK] * qscale) * jnp.exp(bcum)
        k = proj_ref[rows, B_QK:2 * B_QK]
        k_t = k * jnp.exp(-bcum)
        k_d = k * jnp.exp(blast - bcum)
        for hd in range(B_HEADS):
            ks = slice(hd * B_DK, (hd + 1) * B_DK)
            vs = slice(2 * B_QK + hd * B_DV, 2 * B_QK + (hd + 1) * B_DV)
            qh = q_t[:, ks].astype(MXU_DTYPE)
            v = proj_ref[rows, vs].astype(MXU_DTYPE)
            a = jnp.where(causal, _mm_nt(qh, k_t[:, ks].astype(MXU_DTYPE)), 0.0)
            state = state_ref[hd]
            o = _mm(a.astype(MXU_DTYPE), v) + _mm(qh, state.astype(MXU_DTYPE))
            kv = _mm_tn(k_d[:, ks].astype(MXU_DTYPE), v)
            decay = jnp.exp(jnp.broadcast_to(blast[:, ks], (B_DK, B_DK)).T)
            state_ref[hd] = jnp.concatenate([decay] * (B_DV // B_DK), axis=1) * state + kv
            o = o * lax.rsqrt(jnp.mean(o * o, axis=-1, keepdims=True) + EPS) * gnorm
            r = proj_ref[rows, r_off + hd * B_DV:r_off + (hd + 1) * B_DV]
            og_ref[rows, hd * B_DV:(hd + 1) * B_DV] = (o * _silu(r)).astype(og_ref.dtype)
        return carry

    lax.fori_loop(0, GLA_TM // B_CHUNK, chunk, 0)

    out_ref[0] = x_ref[0] + mod_ref[0, 2:3, :] * _mm(og_ref[...], wout_ref[...])


def _gla_layer(x, mod, gamma, w_in, w_gate, b_gate, g_norm, w_out):
    bsz, s, d = x.shape
    glr0 = 2 * B_QK + B_V
    w_main = jnp.concatenate([w_in[:, :glr0], w_in[:, glr0 + B_GATE_RANK:]], axis=1).astype(MXU_DTYPE)
    w_glr = jnp.pad(w_in[:, glr0:glr0 + B_GATE_RANK], ((0, 0), (0, LANES - B_GATE_RANK))).astype(MXU_DTYPE)
    w_gate_p = jnp.pad(w_gate, ((0, LANES - B_GATE_RANK), (0, 0))).astype(MXU_DTYPE)
    n_main = w_main.shape[1]
    const = lambda b, i: (0, 0)
    return pl.pallas_call(
        _gla_kernel,
        grid=(bsz, s // GLA_TM),
        in_specs=[
            pl.BlockSpec((1, GLA_TM, d), lambda b, i: (b, i, 0)),
            pl.BlockSpec((1, 6, d), lambda b, i: (b, 0, 0)),
            pl.BlockSpec((1, d), const),
            pl.BlockSpec((d, n_main), const),
            pl.BlockSpec((d, LANES), const),
            pl.BlockSpec((LANES, B_QK), const),
            pl.BlockSpec((1, B_QK), const),
            pl.BlockSpec((1, B_DV), const),
            pl.BlockSpec((B_V, d), const),
        ],
        out_specs=pl.BlockSpec((1, GLA_TM, d), lambda b, i: (b, i, 0)),
        out_shape=jax.ShapeDtypeStruct((bsz, s, d), f32),
        scratch_shapes=[
            pltpu.VMEM((GLA_TM, n_main), f32),
            pltpu.VMEM((GLA_TM, B_QK), f32),
            pltpu.VMEM((GLA_TM, B_V), MXU_DTYPE),
            pltpu.VMEM((B_HEADS, B_DK, B_DV), f32),
        ],
        compiler_params=pltpu.CompilerParams(
            dimension_semantics=("parallel", "arbitrary"), vmem_limit_bytes=VMEM_LIMIT),
        name="gla_layer",
    )(x, mod, gamma, w_main, w_glr, w_gate_p, b_gate[None, :], g_norm[None, :], w_out.astype(MXU_DTYPE))


def _qkv_weight(w_in):
    d = w_in.shape[0]
    w = w_in.reshape(d, A_GROUPS, 3, N_PAIRS, 2, A_HEAD_DIM)
    w = jnp.transpose(w, (0, 1, 3, 2, 4, 5))
    return w.reshape(d, A_GROUPS * N_PAIRS * PAIR_W).astype(MXU_DTYPE)


def kernel(x, c, w_in_a, w_out_a, rel_bias, w_in_b, w_gate_b, b_gate_b, gnorm_b, w_out_b, norm_mix, norm_ffn, w_ada, b_ada, w_up, conv_w, conv_b, w_down, norm_final):
    depth = w_ada.shape[0]
    mod = _adaln(c, w_ada, b_ada)
    bias = _rel_bias(rel_bias)
    for i in range(depth):
        gam_mix = norm_mix[i][None, :]
        j = i // 2
        if i % 2 == 0:
            qkv = _qkv_proj(x, mod[i], gam_mix, _qkv_weight(w_in_a[j]))
            o = _attention(qkv, bias)
            x = _out_proj(o, w_out_a[j].astype(MXU_DTYPE), x, mod[i])
        else:
            x = _gla_layer(x, mod[i], gam_mix, w_in_b[j], w_gate_b[j], b_gate_b[j], gnorm_b[j], w_out_b[j])
        last = i == depth - 1
        x = _conv_ffn(x, mod[i], norm_ffn[i][None, :],
                      *_ffn_weights(w_up[i], conv_w[i], conv_b[i], w_down[i]),
                      gamma_final=norm_final[None, :] if last else None)
    return x
```

```python
import functools
import math

import jax
import jax.numpy as jnp
from jax import lax
from jax.experimental import pallas as pl
from jax.experimental.pallas import tpu as pltpu

A_CONFIGS = ((128, 1), (512, 4), (2048, 16))
A_GROUPS = len(A_CONFIGS)
A_HEADS = 16
A_HEAD_DIM = 64
N_BUCKETS = 32
MAX_DISTANCE = 2048
B_HEADS = 4
B_DK = 128
B_DV = 256
B_QK = B_HEADS * B_DK
B_V = B_HEADS * B_DV
B_GATE_RANK = 16
B_TAU = 16.0
B_CHUNK = 64
CONV_W = 3
EPS = 1e-6
NEG_INF = -1e30
LOG2E = math.log2(math.e)

LANES = 128
MXU_DTYPE = jnp.bfloat16
ATT_BLK = 128
ATT_TILE = 2048
ATT_UNITS = ATT_TILE // ATT_BLK
ATT_PIPELINE = 2
PAIR_W = 3 * LANES
N_PAIRS = A_HEADS // 2
QKV_TN = 2 * PAIR_W
ROW_CHUNK = 512
FFN_TM = 1024
FFN_TF = 256
FFN_ROWS = 256
HALO = 8
OUT_TM = 1024
GLA_TM = 512
VMEM_LIMIT = 56 * 1024 * 1024

f32 = jnp.float32


def _mm(a, b):
    return jnp.dot(a, b, preferred_element_type=f32)


def _mm_nt(a, b):
    return lax.dot_general(a, b, (((1,), (1,)), ((), ())), preferred_element_type=f32)


def _mm_tn(a, b):
    return lax.dot_general(a, b, (((0,), (0,)), ((), ())), preferred_element_type=f32)


def _modnorm(x, gamma, scale, shift):
    ms = jnp.mean(x * x, axis=-1, keepdims=True)
    y = x * lax.rsqrt(ms + EPS) * gamma
    return y * (1.0 + scale) + shift


def _silu(x):
    return x * (1.0 / (1.0 + jnp.exp(-x)))


def _adaln_kernel(c_ref, w_ref, b_ref, o_ref):
    s = _silu(c_ref[...]).astype(MXU_DTYPE)
    o_ref[0] = _mm(s, w_ref[0].astype(MXU_DTYPE)) + b_ref[0]


def _adaln(c, w_ada, b_ada):
    depth, d, n = w_ada.shape
    bsz = c.shape[0]
    rows = 8 * pl.cdiv(bsz, 8)
    c_pad = jnp.pad(c, ((0, rows - bsz), (0, 0)))
    out = pl.pallas_call(
        _adaln_kernel,
        grid=(depth, n // d),
        in_specs=[
            pl.BlockSpec((rows, d), lambda l, j: (0, 0)),
            pl.BlockSpec((1, d, d), lambda l, j: (l, 0, j)),
            pl.BlockSpec((1, 1, d), lambda l, j: (l, 0, j)),
        ],
        out_specs=pl.BlockSpec((1, rows, d), lambda l, j: (l, 0, j)),
        out_shape=jax.ShapeDtypeStruct((depth, rows, n), f32),
        compiler_params=pltpu.CompilerParams(
            dimension_semantics=("parallel", "parallel"), vmem_limit_bytes=VMEM_LIMIT),
        name="adaln",
    )(c_pad, w_ada, b_ada.reshape(depth, 1, n))
    return out[:, :bsz].reshape(depth, bsz, n // d, d)


def _qkv_kernel(x_ref, mod_ref, gam_ref, w_ref, o_ref, h_ref, xs_ref, *, dilations, tiles_per_group):
    j = pl.program_id(2)
    nlb = xs_ref.shape[0]

    @pl.when(j == 0)
    def _():
        for lb in range(nlb):
            xs_ref[lb] = x_ref[0, :, lb * LANES:(lb + 1) * LANES]

    for gi, d in enumerate(dilations):
        @pl.when(j == gi * tiles_per_group)
        def _(d=d):
            per = ATT_UNITS // d
            for c in range(ATT_UNITS):
                r, q = divmod(c, per)
                if d == 1:
                    xs = x_ref[0, c * ATT_BLK:(c + 1) * ATT_BLK, :]
                else:
                    rows = pl.ds(r + q * ATT_BLK * d, ATT_BLK, stride=d)
                    xs = jnp.concatenate([xs_ref[lb, rows, :] for lb in range(nlb)], axis=1)
                h = _modnorm(xs, gam_ref[...], mod_ref[0, 1:2, :], mod_ref[0, 0:1, :])
                h_ref[c * ATT_BLK:(c + 1) * ATT_BLK, :] = h.astype(h_ref.dtype)

    w = w_ref[...]
    for rc in range(ATT_TILE // ROW_CHUNK):
        rows = slice(rc * ROW_CHUNK, (rc + 1) * ROW_CHUNK)
        res = _mm(h_ref[rows, :], w)
        o_ref[0, 0, 0, rows, :] = res[:, :PAIR_W].astype(o_ref.dtype)
        o_ref[0, 0, 1, rows, :] = res[:, PAIR_W:].astype(o_ref.dtype)


def _qkv_proj(x, mod, gamma, w_perm):
    bsz, s, d = x.shape
    dilations = tuple(dl for _, dl in A_CONFIGS)
    tiles_per_group = N_PAIRS * PAIR_W // QKV_TN
    kern = functools.partial(_qkv_kernel, dilations=dilations, tiles_per_group=tiles_per_group)
    return pl.pallas_call(
        kern,
        grid=(bsz, s // ATT_TILE, A_GROUPS * tiles_per_group),
        in_specs=[
            pl.BlockSpec((1, ATT_TILE, d), lambda b, i, j: (b, i, 0)),
            pl.BlockSpec((1, 6, d), lambda b, i, j: (b, 0, 0)),
            pl.BlockSpec((1, d), lambda b, i, j: (0, 0)),
            pl.BlockSpec((d, QKV_TN), lambda b, i, j: (0, j)),
        ],
        out_specs=pl.BlockSpec(
            (1, 1, 2, ATT_TILE, PAIR_W),
            lambda b, i, j: (j // tiles_per_group, b, j % tiles_per_group, i, 0)),
        out_shape=jax.ShapeDtypeStruct((A_GROUPS, bsz, N_PAIRS, s, PAIR_W), MXU_DTYPE),
        scratch_shapes=[pltpu.VMEM((ATT_TILE, d), MXU_DTYPE),
                        pltpu.VMEM((d // LANES, ATT_TILE, LANES), f32)],
        compiler_params=pltpu.CompilerParams(
            dimension_semantics=("parallel", "parallel", "arbitrary"),
            vmem_limit_bytes=VMEM_LIMIT),
        name="qkv_proj",
    )(x, mod, gamma, w_perm)


def _bias_kernel(tab_ref, bkt_ref, o_ref):
    col = pl.program_id(0) * A_HEADS + pl.program_id(1)
    bkt = bkt_ref[0]
    acc = jnp.full(bkt.shape, NEG_INF, f32)
    for k in range(N_BUCKETS):
        acc = jnp.where(bkt == k, tab_ref[k, col] * LOG2E, acc)
    o_ref[0, 0] = acc


def _t5_bucket(dist):
    max_exact = N_BUCKETS // 2
    n = jnp.maximum(dist, max_exact).astype(f32)
    large = max_exact + (jnp.log(n / max_exact) / math.log(MAX_DISTANCE / max_exact)
                         * (N_BUCKETS - max_exact)).astype(jnp.int32)
    large = jnp.minimum(large, N_BUCKETS - 1)
    return jnp.where(dist < max_exact, dist, large)


def _rel_bias(rel_bias):
    qi = jnp.arange(ATT_BLK)[:, None]
    ki = jnp.arange(2 * ATT_BLK)[None, :]
    steps = qi + ATT_BLK - ki
    band = (steps >= 0) & (steps <= ATT_BLK)
    bucket = jnp.stack([
        jnp.where(band, _t5_bucket(jnp.clip(steps, 0, ATT_BLK) * dl), -1)
        for _, dl in A_CONFIGS]).astype(jnp.int32)
    return pl.pallas_call(
        _bias_kernel,
        grid=(A_GROUPS, A_HEADS),
        in_specs=[
            pl.BlockSpec(memory_space=pltpu.SMEM),
            pl.BlockSpec((1, ATT_BLK, 2 * ATT_BLK), lambda g, h: (g, 0, 0)),
        ],
        out_specs=pl.BlockSpec((1, 1, ATT_BLK, 2 * ATT_BLK), lambda g, h: (g, h, 0, 0)),
        out_shape=jax.ShapeDtypeStruct((A_GROUPS, A_HEADS, ATT_BLK, 2 * ATT_BLK), f32),
        name="rel_bias",
    )(rel_bias, bucket)


def _attn_kernel(qkv_ref, bias_ref, out_ref, prev0, prev1, prev2, o1, l1, o2, l2, biasp):
    first = pl.program_id(2) == 0

    @pl.when(first)
    def _():
        prev0[...] = jnp.zeros_like(prev0)
        prev1[...] = jnp.zeros_like(prev1)
        prev2[...] = jnp.zeros_like(prev2)

    for g in range(A_GROUPS):
        for hh in range(2):
            biasp[g, hh] = jnp.where(first, NEG_INF, bias_ref[g, hh, :, 0:ATT_BLK])

    lane = lax.broadcasted_iota(jnp.int32, (1, LANES), 1)
    lo = lane < A_HEAD_DIM
    qmask = (jnp.where(lo, 1.0, 0.0).astype(MXU_DTYPE), jnp.where(lo, 0.0, 1.0).astype(MXU_DTYPE))

    def scores(g, rc, prev_ref, rp):
        q = qkv_ref[g, 0, 0, rc:rc + ATT_BLK, 0:LANES]
        kc = qkv_ref[g, 0, 0, rc:rc + ATT_BLK, LANES:2 * LANES]
        if prev_ref is None:
            kp = qkv_ref[g, 0, 0, rp:rp + ATT_BLK, LANES:2 * LANES]
        else:
            kp = prev_ref[rp:rp + ATT_BLK, 0:LANES]
        out = []
        for hh in range(2):
            qm = q * qmask[hh]
            bp = bias_ref[g, hh, :, 0:ATT_BLK] if prev_ref is None else biasp[g, hh]
            out.append((_mm_nt(qm, kp) + bp, _mm_nt(qm, kc) + bias_ref[g, hh, :, ATT_BLK:2 * ATT_BLK]))
        return out

    def attend(g, rc, prev_ref, rp, s):
        vc = qkv_ref[g, 0, 0, rc:rc + ATT_BLK, 2 * LANES:3 * LANES]
        if prev_ref is None:
            vp = qkv_ref[g, 0, 0, rp:rp + ATT_BLK, 2 * LANES:3 * LANES]
        else:
            vp = prev_ref[rp:rp + ATT_BLK, LANES:2 * LANES]
        accs, dens, ms = [], [], []
        for sp, sc in s:
            m = jnp.max(jnp.maximum(sp, sc), axis=-1, keepdims=True)
            pp = jnp.exp2(sp - m)
            pc = jnp.exp2(sc - m)
            dens.append(jnp.sum(pp + pc, axis=-1, keepdims=True))
            accs.append(_mm(pp.astype(MXU_DTYPE), vp) + _mm(pc.astype(MXU_DTYPE), vc))
            ms.append(m)
        den = jnp.where(lo, dens[0], dens[1])
        o = jnp.where(lo, accs[0], accs[1]) * (1.0 / den)
        lse = jnp.where(lo, ms[0], ms[1]) + jnp.log2(den)
        return o, jnp.broadcast_to(lse, o.shape)

    def store(o_ref, l_ref, rows):
        def post(o, lse):
            o_ref[rows, :] = o
            l_ref[rows, :] = lse
        return post

    def merge(rows):
        def post(o0, lse0):
            lse1, lse2 = l1[rows, :], l2[rows, :]
            mx = jnp.maximum(jnp.maximum(lse0, lse1), lse2)
            e0, e1, e2 = jnp.exp2(lse0 - mx), jnp.exp2(lse1 - mx), jnp.exp2(lse2 - mx)
            mix = (e0 * o0 + e1 * o1[rows, :] + e2 * o2[rows, :]) * (1.0 / (e0 + e1 + e2))
            out_ref[0, rows, :] = mix.astype(out_ref.dtype)
        return post

    units = []
    d2 = A_CONFIGS[2][1]
    for u in range(ATT_UNITS):
        units.append((2, u * ATT_BLK, prev2, u * ATT_BLK, store(o2, l2, pl.ds(u, ATT_BLK, stride=d2))))
    d1 = A_CONFIGS[1][1]
    per = ATT_UNITS // d1
    for u in range(ATT_UNITS):
        r, q = divmod(u, per)
        rc = u * ATT_BLK
        post = store(o1, l1, pl.ds(q * ATT_BLK * d1 + r, ATT_BLK, stride=d1))
        if q == 0:
            units.append((1, rc, prev1, (r * per + per - 1) * ATT_BLK, post))
        else:
            units.append((1, rc, None, rc - ATT_BLK, post))
    for u in range(ATT_UNITS):
        rc = u * ATT_BLK
        post = merge(slice(rc, rc + ATT_BLK))
        units.append((0, rc, prev0, 0, post) if u == 0 else (0, rc, None, rc - ATT_BLK, post))

    pending = [scores(*un[:4]) for un in units[:ATT_PIPELINE]]
    for idx, un in enumerate(units):
        if idx + ATT_PIPELINE < len(units):
            pending.append(scores(*units[idx + ATT_PIPELINE][:4]))
        o, lse = attend(*un[:4], pending.pop(0))
        un[4](o, lse)

    prev0[...] = qkv_ref[0, 0, 0, ATT_TILE - ATT_BLK:ATT_TILE, LANES:3 * LANES]
    prev1[...] = qkv_ref[1, 0, 0, :, LANES:3 * LANES]
    prev2[...] = qkv_ref[2, 0, 0, :, LANES:3 * LANES]


def _attention(qkv, bias):
    _, bsz, _, s, _ = qkv.shape
    return pl.pallas_call(
        _attn_kernel,
        grid=(bsz, N_PAIRS, s // ATT_TILE),
        in_specs=[
            pl.BlockSpec((A_GROUPS, 1, 1, ATT_TILE, PAIR_W), lambda b, p, i: (0, b, p, i, 0)),
            pl.BlockSpec((A_GROUPS, 2, ATT_BLK, 2 * ATT_BLK), lambda b, p, i: (0, p, 0, 0)),
        ],
        out_specs=pl.BlockSpec((1, ATT_TILE, LANES), lambda b, p, i: (b, i, p)),
        out_shape=jax.ShapeDtypeStruct((bsz, s, A_HEADS * A_HEAD_DIM), MXU_DTYPE),
        scratch_shapes=[
            pltpu.VMEM((ATT_BLK, 2 * LANES), MXU_DTYPE),
            pltpu.VMEM((ATT_TILE, 2 * LANES), MXU_DTYPE),
            pltpu.VMEM((ATT_TILE, 2 * LANES), MXU_DTYPE),
            pltpu.VMEM((ATT_TILE, LANES), f32),
            pltpu.VMEM((ATT_TILE, LANES), f32),
            pltpu.VMEM((ATT_TILE, LANES), f32),
            pltpu.VMEM((ATT_TILE, LANES), f32),
            pltpu.VMEM((A_GROUPS, 2, ATT_BLK, ATT_BLK), f32),
        ],
        compiler_params=pltpu.CompilerParams(
            dimension_semantics=("parallel", "parallel", "arbitrary"),
            vmem_limit_bytes=VMEM_LIMIT),
        name="dilated_attn",
    )(qkv, bias)


def _outproj_kernel(o_ref, w_ref, x_ref, mod_ref, out_ref):
    w = w_ref[...]
    gate = mod_ref[0, 2:3, :]
    for rc in range(OUT_TM // ROW_CHUNK):
        rows = slice(rc * ROW_CHUNK, (rc + 1) * ROW_CHUNK)
        out_ref[0, rows, :] = x_ref[0, rows, :] + gate * _mm(o_ref[0, rows, :], w)


def _out_proj(o, w, x, mod):
    bsz, s, d = x.shape
    k = o.shape[-1]
    return pl.pallas_call(
        _outproj_kernel,
        grid=(bsz, s // OUT_TM),
        in_specs=[
            pl.BlockSpec((1, OUT_TM, k), lambda b, i: (b, i, 0)),
            pl.BlockSpec((k, d), lambda b, i: (0, 0)),
            pl.BlockSpec((1, OUT_TM, d), lambda b, i: (b, i, 0)),
            pl.BlockSpec((1, 6, d), lambda b, i: (b, 0, 0)),
        ],
        out_specs=pl.BlockSpec((1, OUT_TM, d), lambda b, i: (b, i, 0)),
        out_shape=jax.ShapeDtypeStruct((bsz, s, d), f32),
        compiler_params=pltpu.CompilerParams(
            dimension_semantics=("parallel", "parallel"), vmem_limit_bytes=VMEM_LIMIT),
        name="out_proj",
    )(o, w, x, mod)


def _ffn_kernel(x_ref, halo_ref, mod_ref, gam_ref, wup_ref, cw_ref, cb_ref, wdn_ref, *rest, final):
    if final:
        gfin_ref, out_ref, h_ref, u_ref, act_ref, acc_ref = rest
    else:
        out_ref, h_ref, u_ref, act_ref, acc_ref = rest
    gam, scale, shift = gam_ref[...], mod_ref[0, 4:5, :], mod_ref[0, 3:4, :]

    halo = _modnorm(halo_ref[0], gam, scale, shift)
    halo = jnp.where(pl.program_id(1) == 0, 0.0, halo)
    h_ref[0:HALO, :] = halo.astype(h_ref.dtype)
    for rc in range(FFN_TM // ROW_CHUNK):
        rows = slice(rc * ROW_CHUNK, (rc + 1) * ROW_CHUNK)
        h = _modnorm(x_ref[0, rows, :], gam, scale, shift)
        h_ref[HALO + rc * ROW_CHUNK:HALO + (rc + 1) * ROW_CHUNK, :] = h.astype(h_ref.dtype)
    acc_ref[...] = jnp.zeros_like(acc_ref)

    def chunk(c, carry):
        u_ref[...] = _mm(h_ref[...], wup_ref[c])
        cw = cw_ref[c]
        cb = cb_ref[c]
        for rb in range(FFN_TM // FFN_ROWS):
            base = HALO + rb * FFN_ROWS
            v = cb
            for t in range(CONV_W):
                off = base - (CONV_W - 1) + t
                v = v + cw[t:t + 1, :] * u_ref[off:off + FFN_ROWS, :]
            a, b = v[:, :FFN_TF], v[:, FFN_TF:]
            act_ref[rb * FFN_ROWS:(rb + 1) * FFN_ROWS, :] = (_silu(a) * b).astype(act_ref.dtype)
        acc_ref[...] += _mm(act_ref[...], wdn_ref[c])
        return carry

    lax.fori_loop(0, wup_ref.shape[0], chunk, 0)

    gate = mod_ref[0, 5:6, :]
    for rc in range(FFN_TM // ROW_CHUNK):
        rows = slice(rc * ROW_CHUNK, (rc + 1) * ROW_CHUNK)
        y = x_ref[0, rows, :] + gate * acc_ref[rows, :]
        if final:
            ms = jnp.mean(y * y, axis=-1, keepdims=True)
            y = y * lax.rsqrt(ms + EPS) * gfin_ref[...]
        out_ref[0, rows, :] = y


def _conv_ffn(x, mod, gamma, wup_c, cw_c, cb_c, wdn_c, gamma_final=None):
    bsz, s, d = x.shape
    nchunk = wup_c.shape[0]
    final = gamma_final is not None
    tiles_per_halo = FFN_TM // HALO
    resident = dict(pipeline_mode=pl.Buffered(1))
    in_specs = [
        pl.BlockSpec((1, FFN_TM, d), lambda b, i: (b, i, 0)),
        pl.BlockSpec((1, HALO, d), lambda b, i: (b, jnp.maximum(i * tiles_per_halo - 1, 0), 0)),
        pl.BlockSpec((1, 6, d), lambda b, i: (b, 0, 0)),
        pl.BlockSpec((1, d), lambda b, i: (0, 0)),
        pl.BlockSpec((nchunk, d, 2 * FFN_TF), lambda b, i: (0, 0, 0), **resident),
        pl.BlockSpec((nchunk, CONV_W, 2 * FFN_TF), lambda b, i: (0, 0, 0)),
        pl.BlockSpec((nchunk, 1, 2 * FFN_TF), lambda b, i: (0, 0, 0)),
        pl.BlockSpec((nchunk, FFN_TF, d), lambda b, i: (0, 0, 0), **resident),
    ]
    args = [x, x, mod, gamma, wup_c, cw_c, cb_c, wdn_c]
    if final:
        in_specs.append(pl.BlockSpec((1, d), lambda b, i: (0, 0)))
        args.append(gamma_final)
    return pl.pallas_call(
        functools.partial(_ffn_kernel, final=final),
        grid=(bsz, s // FFN_TM),
        in_specs=in_specs,
        out_specs=pl.BlockSpec((1, FFN_TM, d), lambda b, i: (b, i, 0)),
        out_shape=jax.ShapeDtypeStruct((bsz, s, d), f32),
        scratch_shapes=[
            pltpu.VMEM((FFN_TM + HALO, d), MXU_DTYPE),
            pltpu.VMEM((FFN_TM + HALO, 2 * FFN_TF), f32),
            pltpu.VMEM((FFN_TM, FFN_TF), MXU_DTYPE),
            pltpu.VMEM((FFN_TM, d), f32),
        ],
        compiler_params=pltpu.CompilerParams(
            dimension_semantics=("parallel", "parallel"), vmem_limit_bytes=VMEM_LIMIT),
        name="conv_ffn_final" if final else "conv_ffn",
    )(*args)


def _ffn_weights(w_up, conv_w, conv_b, w_down):
    d, two_ff = w_up.shape
    d_ff = two_ff // 2
    n = d_ff // FFN_TF

    def cols(t):
        lead = t.shape[:-1]
        t = t.reshape(lead + (2, n, FFN_TF))
        t = jnp.moveaxis(t, -2, 0)
        return t.reshape((n,) + lead + (2 * FFN_TF,))

    return (cols(w_up).astype(MXU_DTYPE), cols(conv_w), cols(conv_b[None, :]),
            w_down.reshape(n, FFN_TF, d).astype(MXU_DTYPE))


def _gla_kernel(x_ref, mod_ref, gam_ref, win_ref, wglr_ref, wgate_ref, bgate_ref, gnorm_ref, wout_ref,
                out_ref, proj_ref, gk_ref, og_ref, state_ref):
    @pl.when(pl.program_id(1) == 0)
    def _():
        state_ref[...] = jnp.zeros_like(state_ref)

    h = _modnorm(x_ref[0], gam_ref[...], mod_ref[0, 1:2, :], mod_ref[0, 0:1, :]).astype(MXU_DTYPE)
    proj_ref[...] = _mm(h, win_ref[...])
    glr = _mm(h, wglr_ref[...]).astype(MXU_DTYPE)
    z = _mm(glr, wgate_ref[...]) + bgate_ref[...]
    gk_ref[...] = (jnp.minimum(z, 0.0) - jnp.log1p(jnp.exp(-jnp.abs(z)))) * (1.0 / B_TAU)

    ri = lax.broadcasted_iota(jnp.int32, (B_CHUNK, B_CHUNK), 0)
    ci = lax.broadcasted_iota(jnp.int32, (B_CHUNK, B_CHUNK), 1)
    causal = ci <= ri
    tri = causal.astype(MXU_DTYPE)
    gnorm = gnorm_ref[...]
    qscale = B_DK ** -0.5
    r_off = 2 * B_QK + B_V

    def chunk(ci_, carry):
        r0 = pl.multiple_of(ci_ * B_CHUNK, B_CHUNK)
        rows = pl.ds(r0, B_CHUNK)
        gk = gk_ref[rows, :]
        g_hi = gk.astype(MXU_DTYPE)
        rem = gk - g_hi.astype(f32)
        g_mid = rem.astype(MXU_DTYPE)
        g_lo = (rem - g_mid.astype(f32)).astype(MXU_DTYPE)
        bcum = _mm(tri, g_hi) + _mm(tri, g_mid) + _mm(tri, g_lo)
        blast = bcum[B_CHUNK - 1:B_CHUNK, :]
        q_t = (proj_ref[rows, 0:B_QK] * qscale) * jnp.exp(bcum)
        k = proj_ref[rows, B_QK:2 * B_QK]
        k_t = k * jnp.exp(-bcum)
        k_d = k * jnp.exp(blast - bcum)
        for hd in range(B_HEADS):
            ks = slice(hd * B_DK, (hd + 1) * B_DK)
            vs = slice(2 * B_QK + hd * B_DV, 2 * B_QK + (hd + 1) * B_DV)
            qh = q_t[:, ks].astype(MXU_DTYPE)
            v = proj_ref[rows, vs].astype(MXU_DTYPE)
            a = jnp.where(causal, _mm_nt(qh, k_t[:, ks].astype(MXU_DTYPE)), 0.0)
            state = state_ref[hd]
            o = _mm(a.astype(MXU_DTYPE), v) + _mm(qh, state.astype(MXU_DTYPE))
            kv = _mm_tn(k_d[:, ks].astype(MXU_DTYPE), v)
            decay = jnp.exp(jnp.broadcast_to(blast[:, ks], (B_DK, B_DK)).T)
            state_ref[hd] = jnp.concatenate([decay] * (B_DV // B_DK), axis=1) * state + kv
            o = o * lax.rsqrt(jnp.mean(o * o, axis=-1, keepdims=True) + EPS) * gnorm
            r = proj_ref[rows, r_off + hd * B_DV:r_off + (hd + 1) * B_DV]
            og_ref[rows, hd * B_DV:(hd + 1) * B_DV] = (o * _silu(r)).astype(og_ref.dtype)
        return carry

    lax.fori_loop(0, GLA_TM // B_CHUNK, chunk, 0)

    out_ref[0] = x_ref[0] + mod_ref[0, 2:3, :] * _mm(og_ref[...], wout_ref[...])


def _gla_layer(x, mod, gamma, w_in, w_gate, b_gate, g_norm, w_out):
    bsz, s, d = x.shape
    glr0 = 2 * B_QK + B_V
    w_main = jnp.concatenate([w_in[:, :glr0], w_in[:, glr0 + B_GATE_RANK:]], axis=1).astype(MXU_DTYPE)
    w_glr = jnp.pad(w_in[:, glr0:glr0 + B_GATE_RANK], ((0, 0), (0, LANES - B_GATE_RANK))).astype(MXU_DTYPE)
    w_gate_p = jnp.pad(w_gate, ((0, LANES - B_GATE_RANK), (0, 0))).astype(MXU_DTYPE)
    n_main = w_main.shape[1]
    const = lambda b, i: (0, 0)
    return pl.pallas_call(
        _gla_kernel,
        grid=(bsz, s // GLA_TM),
        in_specs=[
            pl.BlockSpec((1, GLA_TM, d), lambda b, i: (b, i, 0)),
            pl.BlockSpec((1, 6, d), lambda b, i: (b, 0, 0)),
            pl.BlockSpec((1, d), const),
            pl.BlockSpec((d, n_main), const),
            pl.BlockSpec((d, LANES), const),
            pl.BlockSpec((LANES, B_QK), const),
            pl.BlockSpec((1, B_QK), const),
            pl.BlockSpec((1, B_DV), const),
            pl.BlockSpec((B_V, d), const),
        ],
        out_specs=pl.BlockSpec((1, GLA_TM, d), lambda b, i: (b, i, 0)),
        out_shape=jax.ShapeDtypeStruct((bsz, s, d), f32),
        scratch_shapes=[
            pltpu.VMEM((GLA_TM, n_main), f32),
            pltpu.VMEM((GLA_TM, B_QK), f32),
            pltpu.VMEM((GLA_TM, B_V), MXU_DTYPE),
            pltpu.VMEM((B_HEADS, B_DK, B_DV), f32),
        ],
        compiler_params=pltpu.CompilerParams(
            dimension_semantics=("parallel", "arbitrary"), vmem_limit_bytes=VMEM_LIMIT),
        name="gla_layer",
    )(x, mod, gamma, w_main, w_glr, w_gate_p, b_gate[None, :], g_norm[None, :], w_out.astype(MXU_DTYPE))


def _qkv_weight(w_in):
    d = w_in.shape[0]
    w = w_in.reshape(d, A_GROUPS, 3, N_PAIRS, 2, A_HEAD_DIM)
    w = w * jnp.array([A_HEAD_DIM ** -0.5 * LOG2E, 1.0, 1.0], f32)[None, None, :, None, None, None]
    w = jnp.transpose(w, (0, 1, 3, 2, 4, 5))
    return w.reshape(d, A_GROUPS * N_PAIRS * PAIR_W).astype(MXU_DTYPE)


def kernel(x, c, w_in_a, w_out_a, rel_bias, w_in_b, w_gate_b, b_gate_b, gnorm_b, w_out_b, norm_mix, norm_ffn, w_ada, b_ada, w_up, conv_w, conv_b, w_down, norm_final):
    depth = w_ada.shape[0]
    mod = _adaln(c, w_ada, b_ada)
    bias = _rel_bias(rel_bias)
    for i in range(depth):
        gam_mix = norm_mix[i][None, :]
        j = i // 2
        if i % 2 == 0:
            qkv = _qkv_proj(x, mod[i], gam_mix, _qkv_weight(w_in_a[j]))
            o = _attention(qkv, bias)
            x = _out_proj(o, w_out_a[j].astype(MXU_DTYPE), x, mod[i])
        else:
            x = _gla_layer(x, mod[i], gam_mix, w_in_b[j], w_gate_b[j], b_gate_b[j], gnorm_b[j], w_out_b[j])
        last = i == depth - 1
        x = _conv_ffn(x, mod[i], norm_ffn[i][None, :],
                      *_ffn_weights(w_up[i], conv_w[i], conv_b[i], w_down[i]),
                      gamma_final=norm_final[None, :] if last else None)
    return x
```

```python
import functools
import math

import jax
import jax.numpy as jnp
from jax import lax
from jax.experimental import pallas as pl
from jax.experimental.pallas import tpu as pltpu

A_CONFIGS = ((128, 1), (512, 4), (2048, 16))
A_GROUPS = len(A_CONFIGS)
A_HEADS = 16
A_HEAD_DIM = 64
N_BUCKETS = 32
MAX_DISTANCE = 2048
B_HEADS = 4
B_DK = 128
B_DV = 256
B_QK = B_HEADS * B_DK
B_V = B_HEADS * B_DV
B_GATE_RANK = 16
B_TAU = 16.0
B_CHUNK = 64
CONV_W = 3
EPS = 1e-6
NEG_INF = -1e30
LOG2E = math.log2(math.e)

LANES = 128
MXU_DTYPE = jnp.bfloat16
ATT_BLK = 128
ATT_TILE = 2048
ATT_UNITS = ATT_TILE // ATT_BLK
ATT_PIPELINE = 3
PAIR_W = 3 * LANES
N_PAIRS = A_HEADS // 2
QKV_TN = 2 * PAIR_W
ROW_CHUNK = 512
FFN_TM = 512
FFN_TF = 256
FFN_ROWS = 256
HALO = 8
OUT_TM = 1024
GLA_TM = 512
VMEM_LIMIT = 56 * 1024 * 1024

f32 = jnp.float32


def _mm(a, b):
    return jnp.dot(a, b, preferred_element_type=f32)


def _mm_nt(a, b):
    return lax.dot_general(a, b, (((1,), (1,)), ((), ())), preferred_element_type=f32)


def _mm_tn(a, b):
    return lax.dot_general(a, b, (((0,), (0,)), ((), ())), preferred_element_type=f32)


def _modnorm(x, gamma, scale, shift):
    ms = jnp.mean(x * x, axis=-1, keepdims=True)
    y = x * lax.rsqrt(ms + EPS) * gamma
    return y * (1.0 + scale) + shift


def _silu(x):
    return x * (1.0 / (1.0 + jnp.exp(-x)))


def _adaln_kernel(c_ref, w_ref, b_ref, o_ref):
    s = _silu(c_ref[...]).astype(MXU_DTYPE)
    o_ref[0] = _mm(s, w_ref[0].astype(MXU_DTYPE)) + b_ref[0]


def _adaln(c, w_ada, b_ada):
    depth, d, n = w_ada.shape
    bsz = c.shape[0]
    rows = 8 * pl.cdiv(bsz, 8)
    c_pad = jnp.pad(c, ((0, rows - bsz), (0, 0)))
    out = pl.pallas_call(
        _adaln_kernel,
        grid=(depth, n // d),
        in_specs=[
            pl.BlockSpec((rows, d), lambda l, j: (0, 0)),
            pl.BlockSpec((1, d, d), lambda l, j: (l, 0, j)),
            pl.BlockSpec((1, 1, d), lambda l, j: (l, 0, j)),
        ],
        out_specs=pl.BlockSpec((1, rows, d), lambda l, j: (l, 0, j)),
        out_shape=jax.ShapeDtypeStruct((depth, rows, n), f32),
        compiler_params=pltpu.CompilerParams(
            dimension_semantics=("parallel", "parallel"), vmem_limit_bytes=VMEM_LIMIT),
        name="adaln",
    )(c_pad, w_ada, b_ada.reshape(depth, 1, n))
    return out[:, :bsz].reshape(depth, bsz, n // d, d)


def _qkv_kernel(x_ref, mod_ref, gam_ref, w_ref, o_ref, h_ref, xs_ref, *, dilations, tiles_per_group):
    j = pl.program_id(2)
    nlb = xs_ref.shape[0]

    @pl.when(j == 0)
    def _():
        for lb in range(nlb):
            xs_ref[lb] = x_ref[0, :, lb * LANES:(lb + 1) * LANES]

    for gi, d in enumerate(dilations):
        @pl.when(j == gi * tiles_per_group)
        def _(d=d):
            per = ATT_UNITS // d
            for c in range(ATT_UNITS):
                r, q = divmod(c, per)
                if d == 1:
                    xs = x_ref[0, c * ATT_BLK:(c + 1) * ATT_BLK, :]
                else:
                    rows = pl.ds(r + q * ATT_BLK * d, ATT_BLK, stride=d)
                    xs = jnp.concatenate([xs_ref[lb, rows, :] for lb in range(nlb)], axis=1)
                h = _modnorm(xs, gam_ref[...], mod_ref[0, 1:2, :], mod_ref[0, 0:1, :])
                h_ref[c * ATT_BLK:(c + 1) * ATT_BLK, :] = h.astype(h_ref.dtype)

    w = w_ref[...]
    for rc in range(ATT_TILE // ROW_CHUNK):
        rows = slice(rc * ROW_CHUNK, (rc + 1) * ROW_CHUNK)
        res = _mm(h_ref[rows, :], w)
        o_ref[0, 0, 0, rows, :] = res[:, :PAIR_W].astype(o_ref.dtype)
        o_ref[0, 0, 1, rows, :] = res[:, PAIR_W:].astype(o_ref.dtype)


def _qkv_proj(x, mod, gamma, w_perm):
    bsz, s, d = x.shape
    dilations = tuple(dl for _, dl in A_CONFIGS)
    tiles_per_group = N_PAIRS * PAIR_W // QKV_TN
    kern = functools.partial(_qkv_kernel, dilations=dilations, tiles_per_group=tiles_per_group)
    return pl.pallas_call(
        kern,
        grid=(bsz, s // ATT_TILE, A_GROUPS * tiles_per_group),
        in_specs=[
            pl.BlockSpec((1, ATT_TILE, d), lambda b, i, j: (b, i, 0)),
            pl.BlockSpec((1, 6, d), lambda b, i, j: (b, 0, 0)),
            pl.BlockSpec((1, d), lambda b, i, j: (0, 0)),
            pl.BlockSpec((d, QKV_TN), lambda b, i, j: (0, j)),
        ],
        out_specs=pl.BlockSpec(
            (1, 1, 2, ATT_TILE, PAIR_W),
            lambda b, i, j: (j // tiles_per_group, b, j % tiles_per_group, i, 0)),
        out_shape=jax.ShapeDtypeStruct((A_GROUPS, bsz, N_PAIRS, s, PAIR_W), MXU_DTYPE),
        scratch_shapes=[pltpu.VMEM((ATT_TILE, d), MXU_DTYPE),
                        pltpu.VMEM((d // LANES, ATT_TILE, LANES), f32)],
        compiler_params=pltpu.CompilerParams(
            dimension_semantics=("parallel", "parallel", "arbitrary"),
            vmem_limit_bytes=VMEM_LIMIT),
        name="qkv_proj",
    )(x, mod, gamma, w_perm)


def _bias_kernel(tab_ref, bkt_ref, o_ref):
    col = pl.program_id(0) * A_HEADS + pl.program_id(1)
    bkt = bkt_ref[0]
    acc = jnp.full(bkt.shape, NEG_INF, f32)
    for k in range(N_BUCKETS):
        acc = jnp.where(bkt == k, tab_ref[k, col] * LOG2E, acc)
    o_ref[0, 0] = acc


def _t5_bucket(dist):
    max_exact = N_BUCKETS // 2
    n = jnp.maximum(dist, max_exact).astype(f32)
    large = max_exact + (jnp.log(n / max_exact) / math.log(MAX_DISTANCE / max_exact)
                         * (N_BUCKETS - max_exact)).astype(jnp.int32)
    large = jnp.minimum(large, N_BUCKETS - 1)
    return jnp.where(dist < max_exact, dist, large)


def _rel_bias(rel_bias):
    qi = jnp.arange(ATT_BLK)[:, None]
    ki = jnp.arange(2 * ATT_BLK)[None, :]
    steps = qi + ATT_BLK - ki
    band = (steps >= 0) & (steps <= ATT_BLK)
    bucket = jnp.stack([
        jnp.where(band, _t5_bucket(jnp.clip(steps, 0, ATT_BLK) * dl), -1)
        for _, dl in A_CONFIGS]).astype(jnp.int32)
    return pl.pallas_call(
        _bias_kernel,
        grid=(A_GROUPS, A_HEADS),
        in_specs=[
            pl.BlockSpec(memory_space=pltpu.SMEM),
            pl.BlockSpec((1, ATT_BLK, 2 * ATT_BLK), lambda g, h: (g, 0, 0)),
        ],
        out_specs=pl.BlockSpec((1, 1, ATT_BLK, 2 * ATT_BLK), lambda g, h: (g, h, 0, 0)),
        out_shape=jax.ShapeDtypeStruct((A_GROUPS, A_HEADS, ATT_BLK, 2 * ATT_BLK), f32),
        name="rel_bias",
    )(rel_bias, bucket)


def _attn_kernel(qkv_ref, bias_ref, out_ref, prev0, prev1, prev2, o1, l1, o2, l2, biasp):
    first = pl.program_id(2) == 0

    @pl.when(first)
    def _():
        prev0[...] = jnp.zeros_like(prev0)
        prev1[...] = jnp.zeros_like(prev1)
        prev2[...] = jnp.zeros_like(prev2)

    for g in range(A_GROUPS):
        for hh in range(2):
            biasp[g, hh, :, 0:ATT_BLK] = jnp.where(first, NEG_INF, bias_ref[g, hh, :, 0:ATT_BLK])
            biasp[g, hh, :, ATT_BLK:2 * ATT_BLK] = bias_ref[g, hh, :, ATT_BLK:2 * ATT_BLK]

    lane = lax.broadcasted_iota(jnp.int32, (1, LANES), 1)
    lo = lane < A_HEAD_DIM
    qmask = (jnp.where(lo, 1.0, 0.0).astype(MXU_DTYPE), jnp.where(lo, 0.0, 1.0).astype(MXU_DTYPE))

    def keys_values(g, rc, prev_ref, rp, lanes):
        if prev_ref is None:
            return qkv_ref[g, 0, 0, rp:rp + 2 * ATT_BLK, lanes]
        off = lanes.start - LANES
        return jnp.concatenate([prev_ref[rp:rp + ATT_BLK, off:off + LANES],
                                qkv_ref[g, 0, 0, rc:rc + ATT_BLK, lanes]], axis=0)

    def scores(g, rc, prev_ref, rp):
        q = qkv_ref[g, 0, 0, rc:rc + ATT_BLK, 0:LANES]
        k = keys_values(g, rc, prev_ref, rp, slice(LANES, 2 * LANES))
        bias = bias_ref if prev_ref is None else biasp
        return [_mm_nt(q * qmask[hh], k) + bias[g, hh] for hh in range(2)]

    def attend(g, rc, prev_ref, rp, s):
        v = keys_values(g, rc, prev_ref, rp, slice(2 * LANES, 3 * LANES))
        accs, dens, ms = [], [], []
        for sh in s:
            m = jnp.max(sh, axis=-1, keepdims=True)
            p = jnp.exp2(sh - m)
            dens.append(jnp.sum(p, axis=-1, keepdims=True))
            accs.append(_mm(p.astype(MXU_DTYPE), v))
            ms.append(m)
        den = jnp.where(lo, dens[0], dens[1])
        o = jnp.where(lo, accs[0], accs[1]) * (1.0 / den)
        lse = jnp.where(lo, ms[0], ms[1]) + jnp.log2(den)
        return o, jnp.broadcast_to(lse, o.shape)

    def store(o_ref, l_ref, rows):
        def post(o, lse):
            o_ref[rows, :] = o
            l_ref[rows, :] = lse
        return post

    def merge(rows):
        def post(o0, lse0):
            lse1, lse2 = l1[rows, :], l2[rows, :]
            mx = jnp.maximum(jnp.maximum(lse0, lse1), lse2)
            e0, e1, e2 = jnp.exp2(lse0 - mx), jnp.exp2(lse1 - mx), jnp.exp2(lse2 - mx)
            mix = (e0 * o0 + e1 * o1[rows, :] + e2 * o2[rows, :]) * (1.0 / (e0 + e1 + e2))
            out_ref[0, rows, :] = mix.astype(out_ref.dtype)
        return post

    units = []
    d2 = A_CONFIGS[2][1]
    for u in range(ATT_UNITS):
        units.append((2, u * ATT_BLK, prev2, u * ATT_BLK, store(o2, l2, pl.ds(u, ATT_BLK, stride=d2))))
    d1 = A_CONFIGS[1][1]
    per = ATT_UNITS // d1
    for u in range(ATT_UNITS):
        r, q = divmod(u, per)
        rc = u * ATT_BLK
        post = store(o1, l1, pl.ds(q * ATT_BLK * d1 + r, ATT_BLK, stride=d1))
        if q == 0:
            units.append((1, rc, prev1, (r * per + per - 1) * ATT_BLK, post))
        else:
            units.append((1, rc, None, rc - ATT_BLK, post))
    for u in range(ATT_UNITS):
        rc = u * ATT_BLK
        post = merge(slice(rc, rc + ATT_BLK))
        units.append((0, rc, prev0, 0, post) if u == 0 else (0, rc, None, rc - ATT_BLK, post))

    pending = [scores(*un[:4]) for un in units[:ATT_PIPELINE]]
    for idx, un in enumerate(units):
        if idx + ATT_PIPELINE < len(units):
            pending.append(scores(*units[idx + ATT_PIPELINE][:4]))
        o, lse = attend(*un[:4], pending.pop(0))
        un[4](o, lse)

    prev0[...] = qkv_ref[0, 0, 0, ATT_TILE - ATT_BLK:ATT_TILE, LANES:3 * LANES]
    prev1[...] = qkv_ref[1, 0, 0, :, LANES:3 * LANES]
    prev2[...] = qkv_ref[2, 0, 0, :, LANES:3 * LANES]


def _attention(qkv, bias):
    _, bsz, _, s, _ = qkv.shape
    return pl.pallas_call(
        _attn_kernel,
        grid=(bsz, N_PAIRS, s // ATT_TILE),
        in_specs=[
            pl.BlockSpec((A_GROUPS, 1, 1, ATT_TILE, PAIR_W), lambda b, p, i: (0, b, p, i, 0)),
            pl.BlockSpec((A_GROUPS, 2, ATT_BLK, 2 * ATT_BLK), lambda b, p, i: (0, p, 0, 0)),
        ],
        out_specs=pl.BlockSpec((1, ATT_TILE, LANES), lambda b, p, i: (b, i, p)),
        out_shape=jax.ShapeDtypeStruct((bsz, s, A_HEADS * A_HEAD_DIM), MXU_DTYPE),
        scratch_shapes=[
            pltpu.VMEM((ATT_BLK, 2 * LANES), MXU_DTYPE),
            pltpu.VMEM((ATT_TILE, 2 * LANES), MXU_DTYPE),
            pltpu.VMEM((ATT_TILE, 2 * LANES), MXU_DTYPE),
            pltpu.VMEM((ATT_TILE, LANES), f32),
            pltpu.VMEM((ATT_TILE, LANES), f32),
            pltpu.VMEM((ATT_TILE, LANES), f32),
            pltpu.VMEM((ATT_TILE, LANES), f32),
            pltpu.VMEM((A_GROUPS, 2, ATT_BLK, 2 * ATT_BLK), f32),
        ],
        compiler_params=pltpu.CompilerParams(
            dimension_semantics=("parallel", "parallel", "arbitrary"),
            vmem_limit_bytes=VMEM_LIMIT),
        name="dilated_attn",
    )(qkv, bias)


def _outproj_kernel(o_ref, w_ref, x_ref, mod_ref, out_ref):
    w = w_ref[...]
    gate = mod_ref[0, 2:3, :]
    for rc in range(OUT_TM // ROW_CHUNK):
        rows = slice(rc * ROW_CHUNK, (rc + 1) * ROW_CHUNK)
        out_ref[0, rows, :] = x_ref[0, rows, :] + gate * _mm(o_ref[0, rows, :], w)


def _out_proj(o, w, x, mod):
    bsz, s, d = x.shape
    k = o.shape[-1]
    return pl.pallas_call(
        _outproj_kernel,
        grid=(bsz, s // OUT_TM),
        in_specs=[
            pl.BlockSpec((1, OUT_TM, k), lambda b, i: (b, i, 0)),
            pl.BlockSpec((k, d), lambda b, i: (0, 0)),
            pl.BlockSpec((1, OUT_TM, d), lambda b, i: (b, i, 0)),
            pl.BlockSpec((1, 6, d), lambda b, i: (b, 0, 0)),
        ],
        out_specs=pl.BlockSpec((1, OUT_TM, d), lambda b, i: (b, i, 0)),
        out_shape=jax.ShapeDtypeStruct((bsz, s, d), f32),
        compiler_params=pltpu.CompilerParams(
            dimension_semantics=("parallel", "parallel"), vmem_limit_bytes=VMEM_LIMIT),
        name="out_proj",
    )(o, w, x, mod)


def _ffn_kernel(x_ref, halo_ref, mod_ref, gam_ref, wup_ref, cw_ref, cb_ref, wdn_ref, *rest, final):
    if final:
        gfin_ref, out_ref, h_ref, u0_ref, u1_ref, act_ref = rest
    else:
        out_ref, h_ref, u0_ref, u1_ref, act_ref = rest
    u_refs = (u0_ref, u1_ref)
    gam, scale, shift = gam_ref[...], mod_ref[0, 4:5, :], mod_ref[0, 3:4, :]

    halo = _modnorm(halo_ref[0], gam, scale, shift)
    halo = jnp.where(pl.program_id(1) == 0, 0.0, halo)
    h_ref[0:HALO, :] = halo.astype(h_ref.dtype)
    for rc in range(FFN_TM // ROW_CHUNK):
        rows = slice(rc * ROW_CHUNK, (rc + 1) * ROW_CHUNK)
        h = _modnorm(x_ref[0, rows, :], gam, scale, shift)
        h_ref[HALO + rc * ROW_CHUNK:HALO + (rc + 1) * ROW_CHUNK, :] = h.astype(h_ref.dtype)

    nchunk = wup_ref.shape[0]
    u_refs[0][...] = _mm(h_ref[...], wup_ref[0])
    for c in range(nchunk):
        u_ref = u_refs[c % 2]
        if c + 1 < nchunk:
            u_refs[(c + 1) % 2][...] = _mm(h_ref[...], wup_ref[c + 1])
        cw = cw_ref[c]
        cb = cb_ref[c]
        for rb in range(FFN_TM // FFN_ROWS):
            base = HALO + rb * FFN_ROWS
            v = cb
            for t in range(CONV_W):
                off = base - (CONV_W - 1) + t
                v = v + cw[t:t + 1, :] * u_ref[off:off + FFN_ROWS, :]
            a, b = v[:, :FFN_TF], v[:, FFN_TF:]
            act_ref[rb * FFN_ROWS:(rb + 1) * FFN_ROWS, c * FFN_TF:(c + 1) * FFN_TF] = (
                _silu(a) * b).astype(act_ref.dtype)

    gate = mod_ref[0, 5:6, :]
    for rc in range(FFN_TM // ROW_CHUNK):
        rows = slice(rc * ROW_CHUNK, (rc + 1) * ROW_CHUNK)
        y = x_ref[0, rows, :] + gate * _mm(act_ref[rows, :], wdn_ref[...])
        if final:
            ms = jnp.mean(y * y, axis=-1, keepdims=True)
            y = y * lax.rsqrt(ms + EPS) * gfin_ref[...]
        out_ref[0, rows, :] = y


def _conv_ffn(x, mod, gamma, wup_c, cw_c, cb_c, wdn_c, gamma_final=None):
    bsz, s, d = x.shape
    nchunk = wup_c.shape[0]
    final = gamma_final is not None
    tiles_per_halo = FFN_TM // HALO
    resident = dict(pipeline_mode=pl.Buffered(1))
    in_specs = [
        pl.BlockSpec((1, FFN_TM, d), lambda b, i: (b, i, 0)),
        pl.BlockSpec((1, HALO, d), lambda b, i: (b, jnp.maximum(i * tiles_per_halo - 1, 0), 0)),
        pl.BlockSpec((1, 6, d), lambda b, i: (b, 0, 0)),
        pl.BlockSpec((1, d), lambda b, i: (0, 0)),
        pl.BlockSpec((nchunk, d, 2 * FFN_TF), lambda b, i: (0, 0, 0), **resident),
        pl.BlockSpec((nchunk, CONV_W, 2 * FFN_TF), lambda b, i: (0, 0, 0)),
        pl.BlockSpec((nchunk, 1, 2 * FFN_TF), lambda b, i: (0, 0, 0)),
        pl.BlockSpec((nchunk * FFN_TF, d), lambda b, i: (0, 0), **resident),
    ]
    args = [x, x, mod, gamma, wup_c, cw_c, cb_c, wdn_c]
    if final:
        in_specs.append(pl.BlockSpec((1, d), lambda b, i: (0, 0)))
        args.append(gamma_final)
    return pl.pallas_call(
        functools.partial(_ffn_kernel, final=final),
        grid=(bsz, s // FFN_TM),
        in_specs=in_specs,
        out_specs=pl.BlockSpec((1, FFN_TM, d), lambda b, i: (b, i, 0)),
        out_shape=jax.ShapeDtypeStruct((bsz, s, d), f32),
        scratch_shapes=[
            pltpu.VMEM((FFN_TM + HALO, d), MXU_DTYPE),
            pltpu.VMEM((FFN_TM + HALO, 2 * FFN_TF), f32),
            pltpu.VMEM((FFN_TM + HALO, 2 * FFN_TF), f32),
            pltpu.VMEM((FFN_TM, nchunk * FFN_TF), MXU_DTYPE),
        ],
        compiler_params=pltpu.CompilerParams(
            dimension_semantics=("parallel", "parallel"), vmem_limit_bytes=VMEM_LIMIT),
        name="conv_ffn_final" if final else "conv_ffn",
    )(*args)


def _ffn_weights(w_up, conv_w, conv_b, w_down):
    d, two_ff = w_up.shape
    d_ff = two_ff // 2
    n = d_ff // FFN_TF

    def cols(t):
        lead = t.shape[:-1]
        t = t.reshape(lead + (2, n, FFN_TF))
        t = jnp.moveaxis(t, -2, 0)
        return t.reshape((n,) + lead + (2 * FFN_TF,))

    return (cols(w_up).astype(MXU_DTYPE), cols(conv_w), cols(conv_b[None, :]),
            w_down.astype(MXU_DTYPE))


def _gla_kernel(x_ref, mod_ref, gam_ref, win_ref, wglr_ref, wgate_ref, bgate_ref, gnorm_ref, wout_ref,
                out_ref, proj_ref, gk_ref, og_ref, state_ref):
    @pl.when(pl.program_id(1) == 0)
    def _():
        state_ref[...] = jnp.zeros_like(state_ref)

    h = _modnorm(x_ref[0], gam_ref[...], mod_ref[0, 1:2, :], mod_ref[0, 0:1, :]).astype(MXU_DTYPE)
    proj_ref[...] = _mm(h, win_ref[...])
    glr = _mm(h, wglr_ref[...]).astype(MXU_DTYPE)
    z = _mm(glr, wgate_ref[...]) + bgate_ref[...]
    gk_ref[...] = (jnp.minimum(z, 0.0) - jnp.log1p(jnp.exp(-jnp.abs(z)))) * (1.0 / B_TAU)

    ri = lax.broadcasted_iota(jnp.int32, (B_CHUNK, B_CHUNK), 0)
    ci = lax.broadcasted_iota(jnp.int32, (B_CHUNK, B_CHUNK), 1)
    causal = ci <= ri
    tri = causal.astype(MXU_DTYPE)
    gnorm = gnorm_ref[...]
    qscale = B_DK ** -0.5
    r_off = 2 * B_QK + B_V

    heads = [slice(hd * B_DK, (hd + 1) * B_DK) for hd in range(B_HEADS)]

    def local(c):
        rows = slice(c * B_CHUNK, (c + 1) * B_CHUNK)
        gk = gk_ref[rows, :]
        g_hi = gk.astype(MXU_DTYPE)
        rem = gk - g_hi.astype(f32)
        g_mid = rem.astype(MXU_DTYPE)
        g_lo = (rem - g_mid.astype(f32)).astype(MXU_DTYPE)
        bcum = _mm(tri, g_hi) + _mm(tri, g_mid) + _mm(tri, g_lo)
        blast = bcum[B_CHUNK - 1:B_CHUNK, :]
        q_t = ((proj_ref[rows, 0:B_QK] * qscale) * jnp.exp(bcum)).astype(MXU_DTYPE)
        k = proj_ref[rows, B_QK:2 * B_QK]
        k_t = (k * jnp.exp(-bcum)).astype(MXU_DTYPE)
        k_d = (k * jnp.exp(blast - bcum)).astype(MXU_DTYPE)
        v = [proj_ref[rows, 2 * B_QK + hd * B_DV:2 * B_QK + (hd + 1) * B_DV].astype(MXU_DTYPE)
             for hd in range(B_HEADS)]
        a = [jnp.where(causal, _mm_nt(q_t[:, ks], k_t[:, ks]), 0.0).astype(MXU_DTYPE) for ks in heads]
        kv = [_mm_tn(k_d[:, ks], v[hd]) for hd, ks in enumerate(heads)]
        o_intra = [_mm(a[hd], v[hd]) for hd in range(B_HEADS)]
        return q_t, blast, kv, o_intra

    def recur(c, loc, state):
        q_t, blast, kv, o_intra = loc
        rows = slice(c * B_CHUNK, (c + 1) * B_CHUNK)
        new_state = []
        for hd, ks in enumerate(heads):
            o = o_intra[hd] + _mm(q_t[:, ks], state[hd].astype(MXU_DTYPE))
            decay = jnp.exp(jnp.broadcast_to(blast[:, ks], (B_DK, B_DK)).T)
            new_state.append(jnp.concatenate([decay] * (B_DV // B_DK), axis=1) * state[hd] + kv[hd])
            o = o * lax.rsqrt(jnp.mean(o * o, axis=-1, keepdims=True) + EPS) * gnorm
            r = proj_ref[rows, r_off + hd * B_DV:r_off + (hd + 1) * B_DV]
            og_ref[rows, hd * B_DV:(hd + 1) * B_DV] = (o * _silu(r)).astype(og_ref.dtype)
        return new_state

    nchunk = GLA_TM // B_CHUNK
    state = [state_ref[hd] for hd in range(B_HEADS)]
    loc = local(0)
    for c in range(nchunk):
        nxt = local(c + 1) if c + 1 < nchunk else None
        state = recur(c, loc, state)
        loc = nxt
    for hd in range(B_HEADS):
        state_ref[hd] = state[hd]

    out_ref[0] = x_ref[0] + mod_ref[0, 2:3, :] * _mm(og_ref[...], wout_ref[...])


def _gla_layer(x, mod, gamma, w_in, w_gate, b_gate, g_norm, w_out):
    bsz, s, d = x.shape
    glr0 = 2 * B_QK + B_V
    w_main = jnp.concatenate([w_in[:, :glr0], w_in[:, glr0 + B_GATE_RANK:]], axis=1).astype(MXU_DTYPE)
    w_glr = jnp.pad(w_in[:, glr0:glr0 + B_GATE_RANK], ((0, 0), (0, LANES - B_GATE_RANK))).astype(MXU_DTYPE)
    w_gate_p = jnp.pad(w_gate, ((0, LANES - B_GATE_RANK), (0, 0))).astype(MXU_DTYPE)
    n_main = w_main.shape[1]
    const = lambda b, i: (0, 0)
    return pl.pallas_call(
        _gla_kernel,
        grid=(bsz, s // GLA_TM),
        in_specs=[
            pl.BlockSpec((1, GLA_TM, d), lambda b, i: (b, i, 0)),
            pl.BlockSpec((1, 6, d), lambda b, i: (b, 0, 0)),
            pl.BlockSpec((1, d), const),
            pl.BlockSpec((d, n_main), const),
            pl.BlockSpec((d, LANES), const),
            pl.BlockSpec((LANES, B_QK), const),
            pl.BlockSpec((1, B_QK), const),
            pl.BlockSpec((1, B_DV), const),
            pl.BlockSpec((B_V, d), const),
        ],
        out_specs=pl.BlockSpec((1, GLA_TM, d), lambda b, i: (b, i, 0)),
        out_shape=jax.ShapeDtypeStruct((bsz, s, d), f32),
        scratch_shapes=[
            pltpu.VMEM((GLA_TM, n_main), f32),
            pltpu.VMEM((GLA_TM, B_QK), f32),
            pltpu.VMEM((GLA_TM, B_V), MXU_DTYPE),
            pltpu.VMEM((B_HEADS, B_DK, B_DV), f32),
        ],
        compiler_params=pltpu.CompilerParams(
            dimension_semantics=("parallel", "arbitrary"), vmem_limit_bytes=VMEM_LIMIT),
        name="gla_layer",
    )(x, mod, gamma, w_main, w_glr, w_gate_p, b_gate[None, :], g_norm[None, :], w_out.astype(MXU_DTYPE))


def _qkv_weight(w_in):
    d = w_in.shape[0]
    w = w_in.reshape(d, A_GROUPS, 3, N_PAIRS, 2, A_HEAD_DIM)
    w = w * jnp.array([A_HEAD_DIM ** -0.5 * LOG2E, 1.0, 1.0], f32)[None, None, :, None, None, None]
    w = jnp.transpose(w, (0, 1, 3, 2, 4, 5))
    return w.reshape(d, A_GROUPS * N_PAIRS * PAIR_W).astype(MXU_DTYPE)


def kernel(x, c, w_in_a, w_out_a, rel_bias, w_in_b, w_gate_b, b_gate_b, gnorm_b, w_out_b, norm_mix, norm_ffn, w_ada, b_ada, w_up, conv_w, conv_b, w_down, norm_final):
    depth = w_ada.shape[0]
    mod = _adaln(c, w_ada, b_ada)
    bias = _rel_bias(rel_bias)
    for i in range(depth):
        gam_mix = norm_mix[i][None, :]
        j = i // 2
        if i % 2 == 0:
            qkv = _qkv_proj(x, mod[i], gam_mix, _qkv_weight(w_in_a[j]))
            o = _attention(qkv, bias)
            x = _out_proj(o, w_out_a[j].astype(MXU_DTYPE), x, mod[i])
        else:
            x = _gla_layer(x, mod[i], gam_mix, w_in_b[j], w_gate_b[j], b_gate_b[j], gnorm_b[j], w_out_b[j])
        last = i == depth - 1
        x = _conv_ffn(x, mod[i], norm_ffn[i][None, :],
                      *_ffn_weights(w_up[i], conv_w[i], conv_b[i], w_down[i]),
                      gamma_final=norm_final[None, :] if last else None)
    return x
```

```python
import functools
import math

import jax
import jax.numpy as jnp
from jax import lax
from jax.experimental import pallas as pl
from jax.experimental.pallas import tpu as pltpu

A_CONFIGS = ((128, 1), (512, 4), (2048, 16))
A_GROUPS = len(A_CONFIGS)
A_HEADS = 16
A_HEAD_DIM = 64
N_BUCKETS = 32
MAX_DISTANCE = 2048
B_HEADS = 4
B_DK = 128
B_DV = 256
B_QK = B_HEADS * B_DK
B_V = B_HEADS * B_DV
B_GATE_RANK = 16
B_TAU = 16.0
B_CHUNK = 64
CONV_W = 3
EPS = 1e-6
NEG_INF = -1e30
LOG2E = math.log2(math.e)

LANES = 128
MXU_DTYPE = jnp.bfloat16
ATT_BLK = 128
ATT_TILE = 2048
ATT_UNITS = ATT_TILE // ATT_BLK
ATT_PIPELINE = 3
PAIR_W = 3 * LANES
N_PAIRS = A_HEADS // 2
ROW_CHUNK = 512
PROBE_ROWS = 16
FFN_TM = 512
FFN_TF = 256
FFN_ROWS = 256
HALO = 8
OUT_TM = 1024
GLA_TM = 512
VMEM_LIMIT = 56 * 1024 * 1024

f32 = jnp.float32


def _mm(a, b):
    return jnp.dot(a, b, preferred_element_type=f32)


def _mm_nt(a, b):
    return lax.dot_general(a, b, (((1,), (1,)), ((), ())), preferred_element_type=f32)


def _mm_tn(a, b):
    return lax.dot_general(a, b, (((0,), (0,)), ((), ())), preferred_element_type=f32)


def _modnorm(x, gamma, scale, shift):
    ms = jnp.mean(x * x, axis=-1, keepdims=True)
    y = x * lax.rsqrt(ms + EPS) * gamma
    return y * (1.0 + scale) + shift


def _silu(x):
    return x * (1.0 / (1.0 + jnp.exp(-x)))


def _adaln_kernel(c_ref, w_ref, b_ref, o_ref):
    s = _silu(c_ref[...]).astype(MXU_DTYPE)
    o_ref[0] = _mm(s, w_ref[0].astype(MXU_DTYPE)) + b_ref[0]


def _adaln(c, w_ada, b_ada):
    depth, d, n = w_ada.shape
    bsz = c.shape[0]
    rows = 8 * pl.cdiv(bsz, 8)
    c_pad = jnp.pad(c, ((0, rows - bsz), (0, 0)))
    out = pl.pallas_call(
        _adaln_kernel,
        grid=(depth, n // d),
        in_specs=[
            pl.BlockSpec((rows, d), lambda l, j: (0, 0)),
            pl.BlockSpec((1, d, d), lambda l, j: (l, 0, j)),
            pl.BlockSpec((1, 1, d), lambda l, j: (l, 0, j)),
        ],
        out_specs=pl.BlockSpec((1, rows, d), lambda l, j: (l, 0, j)),
        out_shape=jax.ShapeDtypeStruct((depth, rows, n), f32),
        compiler_params=pltpu.CompilerParams(
            dimension_semantics=("parallel", "parallel"), vmem_limit_bytes=VMEM_LIMIT),
        name="adaln",
    )(c_pad, w_ada, b_ada.reshape(depth, 1, n))
    return out[:, :bsz].reshape(depth, bsz, n // d, d)


def _qkv_kernel(x_ref, mod_ref, gam_ref, wq_ref, wk_ref, wv_ref, o_ref, ha_ref, hb_ref, xs_ref, *,
                dilations, tiles_per_group):
    j = pl.program_id(2)
    g = j // tiles_per_group
    t = j % tiles_per_group
    nlb = xs_ref.shape[0]
    chunks_per_step = ATT_UNITS // tiles_per_group
    h_refs = [ha_ref, hb_ref]

    def norm(xs):
        return _modnorm(xs, gam_ref[...], mod_ref[0, 1:2, :], mod_ref[0, 0:1, :])

    @pl.when(j == 0)
    def _():
        for lb in range(nlb):
            xs_ref[lb] = x_ref[0, :, lb * LANES:(lb + 1) * LANES]
        for c in range(ATT_UNITS):
            rows = slice(c * ATT_BLK, (c + 1) * ATT_BLK)
            ha_ref[rows, :] = norm(x_ref[0, rows, :]).astype(ha_ref.dtype)

    def deinterleave_chunk(dst_ref, d, cc):
        per = ATT_UNITS // d
        c = t * chunks_per_step + cc
        start = c // per + (c % per) * (ATT_BLK * d)
        rows = pl.ds(start, ATT_BLK, stride=d)
        xs = jnp.concatenate([xs_ref[lb, rows, :] for lb in range(nlb)], axis=1)
        r0 = pl.multiple_of(c * ATT_BLK, ATT_BLK)
        dst_ref[pl.ds(r0, ATT_BLK), :] = norm(xs).astype(dst_ref.dtype)
        probe = dst_ref[pl.ds(r0, PROBE_ROWS), 0:LANES]
        return (probe != probe) & (probe == probe)

    def project_chunk(src_ref, rc, never):
        rows = slice(rc * ROW_CHUNK, (rc + 1) * ROW_CHUNK)
        lhs = src_ref[rows, :]
        for k, w_ref in enumerate((wq_ref, wk_ref, wv_ref)):
            res = _mm(lhs, w_ref[...]).astype(o_ref.dtype)
            lanes = slice(k * LANES, (k + 1) * LANES)
            o_ref[0, 0, 1, rows, lanes] = res[:, LANES:]
            if never is not None and k == 0:
                top = rc * ROW_CHUNK + PROBE_ROWS
                o_ref[0, 0, 0, rc * ROW_CHUNK:top, lanes] = jnp.where(
                    never, jnp.zeros_like(res[:PROBE_ROWS, :LANES]), res[:PROBE_ROWS, :LANES])
                o_ref[0, 0, 0, top:(rc + 1) * ROW_CHUNK, lanes] = res[PROBE_ROWS:, :LANES]
            else:
                o_ref[0, 0, 0, rows, lanes] = res[:, :LANES]

    assert dilations[0] == 1 and chunks_per_step == ATT_TILE // ROW_CHUNK
    for gi in range(len(dilations)):
        @pl.when(g == gi)
        def _(gi=gi):
            shares = [[0, 1]] + [[cc] for cc in range(2, chunks_per_step)] + [[]]
            for rc in range(chunks_per_step):
                never = None
                if gi + 1 < len(dilations):
                    for cc in shares[rc]:
                        probe = deinterleave_chunk(h_refs[(gi + 1) % 2], dilations[gi + 1], cc)
                        never = probe if never is None else never | probe
                project_chunk(h_refs[gi % 2], rc, never)


def _qkv_proj(x, mod, gamma, w):
    bsz, s, d = x.shape
    dilations = tuple(dl for _, dl in A_CONFIGS)
    tn = 2 * LANES
    tiles_per_group = A_HEADS * A_HEAD_DIM // tn
    kern = functools.partial(_qkv_kernel, dilations=dilations, tiles_per_group=tiles_per_group)

    def w_spec(k):
        return pl.BlockSpec(
            (d, tn), lambda b, i, j: (0, (j // tiles_per_group * 3 + k) * tiles_per_group + j % tiles_per_group))

    return pl.pallas_call(
        kern,
        grid=(bsz, s // ATT_TILE, A_GROUPS * tiles_per_group),
        in_specs=[
            pl.BlockSpec((1, ATT_TILE, d), lambda b, i, j: (b, i, 0)),
            pl.BlockSpec((1, 6, d), lambda b, i, j: (b, 0, 0)),
            pl.BlockSpec((1, d), lambda b, i, j: (0, 0)),
            w_spec(0), w_spec(1), w_spec(2),
        ],
        out_specs=pl.BlockSpec(
            (1, 1, 2, ATT_TILE, PAIR_W),
            lambda b, i, j: (j // tiles_per_group, b, j % tiles_per_group, i, 0)),
        out_shape=jax.ShapeDtypeStruct((A_GROUPS, bsz, N_PAIRS, s, PAIR_W), MXU_DTYPE),
        scratch_shapes=[pltpu.VMEM((ATT_TILE, d), MXU_DTYPE),
                        pltpu.VMEM((ATT_TILE, d), MXU_DTYPE),
                        pltpu.VMEM((d // LANES, ATT_TILE, LANES), f32)],
        compiler_params=pltpu.CompilerParams(
            dimension_semantics=("parallel", "parallel", "arbitrary"),
            vmem_limit_bytes=VMEM_LIMIT),
        name="qkv_proj",
    )(x, mod, gamma, w, w, w)


def _bias_kernel(tab_ref, bkt_ref, o_ref):
    col = pl.program_id(0) * A_HEADS + pl.program_id(1)
    bkt = bkt_ref[0]
    acc = jnp.full(bkt.shape, NEG_INF, f32)
    for k in range(N_BUCKETS):
        acc = jnp.where(bkt == k, tab_ref[k, col] * LOG2E, acc)
    o_ref[0, 0] = acc


def _t5_bucket(dist):
    max_exact = N_BUCKETS // 2
    n = jnp.maximum(dist, max_exact).astype(f32)
    large = max_exact + (jnp.log(n / max_exact) / math.log(MAX_DISTANCE / max_exact)
                         * (N_BUCKETS - max_exact)).astype(jnp.int32)
    large = jnp.minimum(large, N_BUCKETS - 1)
    return jnp.where(dist < max_exact, dist, large)


def _rel_bias(rel_bias):
    qi = jnp.arange(ATT_BLK)[:, None]
    ki = jnp.arange(2 * ATT_BLK)[None, :]
    steps = qi + ATT_BLK - ki
    band = (steps >= 0) & (steps <= ATT_BLK)
    bucket = jnp.stack([
        jnp.where(band, _t5_bucket(jnp.clip(steps, 0, ATT_BLK) * dl), -1)
        for _, dl in A_CONFIGS]).astype(jnp.int32)
    return pl.pallas_call(
        _bias_kernel,
        grid=(A_GROUPS, A_HEADS),
        in_specs=[
            pl.BlockSpec(memory_space=pltpu.SMEM),
            pl.BlockSpec((1, ATT_BLK, 2 * ATT_BLK), lambda g, h: (g, 0, 0)),
        ],
        out_specs=pl.BlockSpec((1, 1, ATT_BLK, 2 * ATT_BLK), lambda g, h: (g, h, 0, 0)),
        out_shape=jax.ShapeDtypeStruct((A_GROUPS, A_HEADS, ATT_BLK, 2 * ATT_BLK), f32),
        name="rel_bias",
    )(rel_bias, bucket)


def _attn_kernel(qkv_ref, bias_ref, out_ref, prev0, prev1, prev2, o1, l1, o2, l2, biasp):
    first = pl.program_id(2) == 0

    @pl.when(first)
    def _():
        prev0[...] = jnp.zeros_like(prev0)
        prev1[...] = jnp.zeros_like(prev1)
        prev2[...] = jnp.zeros_like(prev2)

    for g in range(A_GROUPS):
        for hh in range(2):
            biasp[g, hh, :, 0:ATT_BLK] = jnp.where(first, NEG_INF, bias_ref[g, hh, :, 0:ATT_BLK])
            biasp[g, hh, :, ATT_BLK:2 * ATT_BLK] = bias_ref[g, hh, :, ATT_BLK:2 * ATT_BLK]

    lane = lax.broadcasted_iota(jnp.int32, (1, LANES), 1)
    lo = lane < A_HEAD_DIM
    qmask = (jnp.where(lo, 1.0, 0.0).astype(MXU_DTYPE), jnp.where(lo, 0.0, 1.0).astype(MXU_DTYPE))

    def keys_values(g, rc, prev_ref, rp, lanes):
        if prev_ref is None:
            return qkv_ref[g, 0, 0, rp:rp + 2 * ATT_BLK, lanes]
        off = lanes.start - LANES
        return jnp.concatenate([prev_ref[rp:rp + ATT_BLK, off:off + LANES],
                                qkv_ref[g, 0, 0, rc:rc + ATT_BLK, lanes]], axis=0)

    def scores(g, rc, prev_ref, rp):
        q = qkv_ref[g, 0, 0, rc:rc + ATT_BLK, 0:LANES]
        k = keys_values(g, rc, prev_ref, rp, slice(LANES, 2 * LANES))
        bias = bias_ref if prev_ref is None else biasp
        return [_mm_nt(q * qmask[hh], k) + bias[g, hh] for hh in range(2)]

    def attend(g, rc, prev_ref, rp, s):
        v = keys_values(g, rc, prev_ref, rp, slice(2 * LANES, 3 * LANES))
        accs, dens, ms = [], [], []
        for sh in s:
            m = jnp.max(sh, axis=-1, keepdims=True)
            p = jnp.exp2(sh - m)
            dens.append(jnp.sum(p, axis=-1, keepdims=True))
            accs.append(_mm(p.astype(MXU_DTYPE), v))
            ms.append(m)
        den = jnp.where(lo, dens[0], dens[1])
        o = jnp.where(lo, accs[0], accs[1]) * (1.0 / den)
        lse = jnp.where(lo, ms[0], ms[1]) + jnp.log2(den)
        return o, jnp.broadcast_to(lse, o.shape)

    def store(o_ref, l_ref, rows):
        def post(o, lse):
            o_ref[rows, :] = o
            l_ref[rows, :] = lse
        return post

    def merge(rows):
        def post(o0, lse0):
            lse1, lse2 = l1[rows, :], l2[rows, :]
            mx = jnp.maximum(jnp.maximum(lse0, lse1), lse2)
            e0, e1, e2 = jnp.exp2(lse0 - mx), jnp.exp2(lse1 - mx), jnp.exp2(lse2 - mx)
            mix = (e0 * o0 + e1 * o1[rows, :] + e2 * o2[rows, :]) * (1.0 / (e0 + e1 + e2))
            out_ref[0, rows, :] = mix.astype(out_ref.dtype)
        return post

    units = []
    d2 = A_CONFIGS[2][1]
    for u in range(ATT_UNITS):
        units.append((2, u * ATT_BLK, prev2, u * ATT_BLK, store(o2, l2, pl.ds(u, ATT_BLK, stride=d2))))
    d1 = A_CONFIGS[1][1]
    per = ATT_UNITS // d1
    for u in range(ATT_UNITS):
        r, q = divmod(u, per)
        rc = u * ATT_BLK
        post = store(o1, l1, pl.ds(q * ATT_BLK * d1 + r, ATT_BLK, stride=d1))
        if q == 0:
            units.append((1, rc, prev1, (r * per + per - 1) * ATT_BLK, post))
        else:
            units.append((1, rc, None, rc - ATT_BLK, post))
    for u in range(ATT_UNITS):
        rc = u * ATT_BLK
        post = merge(slice(rc, rc + ATT_BLK))
        units.append((0, rc, prev0, 0, post) if u == 0 else (0, rc, None, rc - ATT_BLK, post))

    pending = [scores(*un[:4]) for un in units[:ATT_PIPELINE]]
    for idx, un in enumerate(units):
        if idx + ATT_PIPELINE < len(units):
            pending.append(scores(*units[idx + ATT_PIPELINE][:4]))
        o, lse = attend(*un[:4], pending.pop(0))
        un[4](o, lse)

    prev0[...] = qkv_ref[0, 0, 0, ATT_TILE - ATT_BLK:ATT_TILE, LANES:3 * LANES]
    prev1[...] = qkv_ref[1, 0, 0, :, LANES:3 * LANES]
    prev2[...] = qkv_ref[2, 0, 0, :, LANES:3 * LANES]


def _attention(qkv, bias):
    _, bsz, _, s, _ = qkv.shape
    return pl.pallas_call(
        _attn_kernel,
        grid=(bsz, N_PAIRS, s // ATT_TILE),
        in_specs=[
            pl.BlockSpec((A_GROUPS, 1, 1, ATT_TILE, PAIR_W), lambda b, p, i: (0, b, p, i, 0)),
            pl.BlockSpec((A_GROUPS, 2, ATT_BLK, 2 * ATT_BLK), lambda b, p, i: (0, p, 0, 0)),
        ],
        out_specs=pl.BlockSpec((1, ATT_TILE, LANES), lambda b, p, i: (b, i, p)),
        out_shape=jax.ShapeDtypeStruct((bsz, s, A_HEADS * A_HEAD_DIM), MXU_DTYPE),
        scratch_shapes=[
            pltpu.VMEM((ATT_BLK, 2 * LANES), MXU_DTYPE),
            pltpu.VMEM((ATT_TILE, 2 * LANES), MXU_DTYPE),
            pltpu.VMEM((ATT_TILE, 2 * LANES), MXU_DTYPE),
            pltpu.VMEM((ATT_TILE, LANES), f32),
            pltpu.VMEM((ATT_TILE, LANES), f32),
            pltpu.VMEM((ATT_TILE, LANES), f32),
            pltpu.VMEM((ATT_TILE, LANES), f32),
            pltpu.VMEM((A_GROUPS, 2, ATT_BLK, 2 * ATT_BLK), f32),
        ],
        compiler_params=pltpu.CompilerParams(
            dimension_semantics=("parallel", "parallel", "arbitrary"),
            vmem_limit_bytes=VMEM_LIMIT),
        name="dilated_attn",
    )(qkv, bias)


def _outproj_kernel(o_ref, w_ref, x_ref, mod_ref, out_ref):
    w = w_ref[...]
    gate = mod_ref[0, 2:3, :]
    for rc in range(OUT_TM // ROW_CHUNK):
        rows = slice(rc * ROW_CHUNK, (rc + 1) * ROW_CHUNK)
        out_ref[0, rows, :] = x_ref[0, rows, :] + gate * _mm(o_ref[0, rows, :], w)


def _out_proj(o, w, x, mod):
    bsz, s, d = x.shape
    k = o.shape[-1]
    return pl.pallas_call(
        _outproj_kernel,
        grid=(bsz, s // OUT_TM),
        in_specs=[
            pl.BlockSpec((1, OUT_TM, k), lambda b, i: (b, i, 0)),
            pl.BlockSpec((k, d), lambda b, i: (0, 0)),
            pl.BlockSpec((1, OUT_TM, d), lambda b, i: (b, i, 0)),
            pl.BlockSpec((1, 6, d), lambda b, i: (b, 0, 0)),
        ],
        out_specs=pl.BlockSpec((1, OUT_TM, d), lambda b, i: (b, i, 0)),
        out_shape=jax.ShapeDtypeStruct((bsz, s, d), f32),
        compiler_params=pltpu.CompilerParams(
            dimension_semantics=("parallel", "parallel"), vmem_limit_bytes=VMEM_LIMIT),
        name="out_proj",
    )(o, w, x, mod)


def _ffn_kernel(x_ref, halo_ref, mod_ref, gam_ref, wup_ref, cw_ref, cb_ref, wdn_ref, *rest, final):
    if final:
        gfin_ref, out_ref, h_ref, u0_ref, u1_ref, act_ref = rest
    else:
        out_ref, h_ref, u0_ref, u1_ref, act_ref = rest
    u_refs = (u0_ref, u1_ref)
    gam, scale, shift = gam_ref[...], mod_ref[0, 4:5, :], mod_ref[0, 3:4, :]

    halo = _modnorm(halo_ref[0], gam, scale, shift)
    halo = jnp.where(pl.program_id(1) == 0, 0.0, halo)
    h_ref[0:HALO, :] = halo.astype(h_ref.dtype)
    for rc in range(FFN_TM // ROW_CHUNK):
        rows = slice(rc * ROW_CHUNK, (rc + 1) * ROW_CHUNK)
        h = _modnorm(x_ref[0, rows, :], gam, scale, shift)
        h_ref[HALO + rc * ROW_CHUNK:HALO + (rc + 1) * ROW_CHUNK, :] = h.astype(h_ref.dtype)

    d_ff = wdn_ref.shape[0]
    nchunk = d_ff // FFN_TF

    def halves(c):
        return [slice(half * d_ff + c * FFN_TF, half * d_ff + (c + 1) * FFN_TF) for half in range(2)]

    def up(c):
        for half, cols in enumerate(halves(c)):
            u_refs[c % 2][:, half * FFN_TF:(half + 1) * FFN_TF] = _mm(h_ref[...], wup_ref[:, cols])

    up(0)
    for c in range(nchunk):
        u_ref = u_refs[c % 2]
        if c + 1 < nchunk:
            up(c + 1)
        for rb in range(FFN_TM // FFN_ROWS):
            base = HALO + rb * FFN_ROWS
            ab = []
            for half, cols in enumerate(halves(c)):
                v = cb_ref[:, cols]
                for t in range(CONV_W):
                    off = base - (CONV_W - 1) + t
                    v = v + cw_ref[t:t + 1, cols] * u_ref[off:off + FFN_ROWS, half * FFN_TF:(half + 1) * FFN_TF]
                ab.append(v)
            act_ref[rb * FFN_ROWS:(rb + 1) * FFN_ROWS, c * FFN_TF:(c + 1) * FFN_TF] = (
                _silu(ab[0]) * ab[1]).astype(act_ref.dtype)

    gate = mod_ref[0, 5:6, :]
    for rc in range(FFN_TM // ROW_CHUNK):
        rows = slice(rc * ROW_CHUNK, (rc + 1) * ROW_CHUNK)
        y = x_ref[0, rows, :] + gate * _mm(act_ref[rows, :], wdn_ref[...])
        if final:
            ms = jnp.mean(y * y, axis=-1, keepdims=True)
            y = y * lax.rsqrt(ms + EPS) * gfin_ref[...]
        out_ref[0, rows, :] = y


def _conv_ffn(x, mod, gamma, w_up, conv_w, conv_b, w_down, gamma_final=None):
    bsz, s, d = x.shape
    d_ff = w_down.shape[0]
    final = gamma_final is not None
    tiles_per_halo = FFN_TM // HALO
    resident = dict(pipeline_mode=pl.Buffered(1))
    in_specs = [
        pl.BlockSpec((1, FFN_TM, d), lambda b, i: (b, i, 0)),
        pl.BlockSpec((1, HALO, d), lambda b, i: (b, jnp.maximum(i * tiles_per_halo - 1, 0), 0)),
        pl.BlockSpec((1, 6, d), lambda b, i: (b, 0, 0)),
        pl.BlockSpec((1, d), lambda b, i: (0, 0)),
        pl.BlockSpec((d, 2 * d_ff), lambda b, i: (0, 0), **resident),
        pl.BlockSpec((CONV_W, 2 * d_ff), lambda b, i: (0, 0)),
        pl.BlockSpec((1, 2 * d_ff), lambda b, i: (0, 0)),
        pl.BlockSpec((d_ff, d), lambda b, i: (0, 0), **resident),
    ]
    args = [x, x, mod, gamma, w_up.astype(MXU_DTYPE), conv_w, conv_b[None, :], w_down.astype(MXU_DTYPE)]
    if final:
        in_specs.append(pl.BlockSpec((1, d), lambda b, i: (0, 0)))
        args.append(gamma_final)
    return pl.pallas_call(
        functools.partial(_ffn_kernel, final=final),
        grid=(bsz, s // FFN_TM),
        in_specs=in_specs,
        out_specs=pl.BlockSpec((1, FFN_TM, d), lambda b, i: (b, i, 0)),
        out_shape=jax.ShapeDtypeStruct((bsz, s, d), f32),
        scratch_shapes=[
            pltpu.VMEM((FFN_TM + HALO, d), MXU_DTYPE),
            pltpu.VMEM((FFN_TM + HALO, 2 * FFN_TF), f32),
            pltpu.VMEM((FFN_TM + HALO, 2 * FFN_TF), f32),
            pltpu.VMEM((FFN_TM, d_ff), MXU_DTYPE),
        ],
        compiler_params=pltpu.CompilerParams(
            dimension_semantics=("parallel", "parallel"), vmem_limit_bytes=VMEM_LIMIT),
        name="conv_ffn_final" if final else "conv_ffn",
    )(*args)


def _gla_kernel(x_ref, mod_ref, gam_ref, win_ref, wglr_ref, wgate_ref, bgate_ref, gnorm_ref, wout_ref,
                out_ref, proj_ref, gk_ref, og_ref, state_ref):
    @pl.when(pl.program_id(1) == 0)
    def _():
        state_ref[...] = jnp.zeros_like(state_ref)

    h = _modnorm(x_ref[0], gam_ref[...], mod_ref[0, 1:2, :], mod_ref[0, 0:1, :]).astype(MXU_DTYPE)
    proj_ref[...] = _mm(h, win_ref[...])
    glr = _mm(h, wglr_ref[...]).astype(MXU_DTYPE)
    z = _mm(glr, wgate_ref[...]) + bgate_ref[...]
    gk_ref[...] = (jnp.minimum(z, 0.0) - jnp.log1p(jnp.exp(-jnp.abs(z)))) * (1.0 / B_TAU)

    ri = lax.broadcasted_iota(jnp.int32, (B_CHUNK, B_CHUNK), 0)
    ci = lax.broadcasted_iota(jnp.int32, (B_CHUNK, B_CHUNK), 1)
    causal = ci <= ri
    tri = causal.astype(MXU_DTYPE)
    gnorm = gnorm_ref[...]
    qscale = B_DK ** -0.5
    r_off = 2 * B_QK + B_V

    heads = [slice(hd * B_DK, (hd + 1) * B_DK) for hd in range(B_HEADS)]

    def local(c):
        rows = slice(c * B_CHUNK, (c + 1) * B_CHUNK)
        gk = gk_ref[rows, :]
        g_hi = gk.astype(MXU_DTYPE)
        rem = gk - g_hi.astype(f32)
        g_mid = rem.astype(MXU_DTYPE)
        g_lo = (rem - g_mid.astype(f32)).astype(MXU_DTYPE)
        bcum = _mm(tri, g_hi) + _mm(tri, g_mid) + _mm(tri, g_lo)
        blast = bcum[B_CHUNK - 1:B_CHUNK, :]
        q_t = ((proj_ref[rows, 0:B_QK] * qscale) * jnp.exp(bcum)).astype(MXU_DTYPE)
        k = proj_ref[rows, B_QK:2 * B_QK]
        k_t = (k * jnp.exp(-bcum)).astype(MXU_DTYPE)
        k_d = (k * jnp.exp(blast - bcum)).astype(MXU_DTYPE)
        v = [proj_ref[rows, 2 * B_QK + hd * B_DV:2 * B_QK + (hd + 1) * B_DV].astype(MXU_DTYPE)
             for hd in range(B_HEADS)]
        a = [jnp.where(causal, _mm_nt(q_t[:, ks], k_t[:, ks]), 0.0).astype(MXU_DTYPE) for ks in heads]
        kv = [_mm_tn(k_d[:, ks], v[hd]) for hd, ks in enumerate(heads)]
        o_intra = [_mm(a[hd], v[hd]) for hd in range(B_HEADS)]
        return q_t, blast, kv, o_intra

    def recur(c, loc, state):
        q_t, blast, kv, o_intra = loc
        rows = slice(c * B_CHUNK, (c + 1) * B_CHUNK)
        new_state = []
        for hd, ks in enumerate(heads):
            o = o_intra[hd] + _mm(q_t[:, ks], state[hd].astype(MXU_DTYPE))
            decay = jnp.exp(jnp.broadcast_to(blast[:, ks], (B_DK, B_DK)).T)
            new_state.append(jnp.concatenate([decay] * (B_DV // B_DK), axis=1) * state[hd] + kv[hd])
            o = o * lax.rsqrt(jnp.mean(o * o, axis=-1, keepdims=True) + EPS) * gnorm
            r = proj_ref[rows, r_off + hd * B_DV:r_off + (hd + 1) * B_DV]
            og_ref[rows, hd * B_DV:(hd + 1) * B_DV] = (o * _silu(r)).astype(og_ref.dtype)
        return new_state

    nchunk = GLA_TM // B_CHUNK
    state = [state_ref[hd] for hd in range(B_HEADS)]
    loc = local(0)
    for c in range(nchunk):
        nxt = local(c + 1) if c + 1 < nchunk else None
        state = recur(c, loc, state)
        loc = nxt
    for hd in range(B_HEADS):
        state_ref[hd] = state[hd]

    out_ref[0] = x_ref[0] + mod_ref[0, 2:3, :] * _mm(og_ref[...], wout_ref[...])


def _gla_layer(x, mod, gamma, w_in, w_gate, b_gate, g_norm, w_out):
    bsz, s, d = x.shape
    glr0 = 2 * B_QK + B_V
    w_main = jnp.concatenate([w_in[:, :glr0], w_in[:, glr0 + B_GATE_RANK:]], axis=1).astype(MXU_DTYPE)
    w_glr = jnp.pad(w_in[:, glr0:glr0 + B_GATE_RANK], ((0, 0), (0, LANES - B_GATE_RANK))).astype(MXU_DTYPE)
    w_gate_p = jnp.pad(w_gate, ((0, LANES - B_GATE_RANK), (0, 0))).astype(MXU_DTYPE)
    n_main = w_main.shape[1]
    const = lambda b, i: (0, 0)
    return pl.pallas_call(
        _gla_kernel,
        grid=(bsz, s // GLA_TM),
        in_specs=[
            pl.BlockSpec((1, GLA_TM, d), lambda b, i: (b, i, 0)),
            pl.BlockSpec((1, 6, d), lambda b, i: (b, 0, 0)),
            pl.BlockSpec((1, d), const),
            pl.BlockSpec((d, n_main), const),
            pl.BlockSpec((d, LANES), const),
            pl.BlockSpec((LANES, B_QK), const),
            pl.BlockSpec((1, B_QK), const),
            pl.BlockSpec((1, B_DV), const),
            pl.BlockSpec((B_V, d), const),
        ],
        out_specs=pl.BlockSpec((1, GLA_TM, d), lambda b, i: (b, i, 0)),
        out_shape=jax.ShapeDtypeStruct((bsz, s, d), f32),
        scratch_shapes=[
            pltpu.VMEM((GLA_TM, n_main), f32),
            pltpu.VMEM((GLA_TM, B_QK), f32),
            pltpu.VMEM((GLA_TM, B_V), MXU_DTYPE),
            pltpu.VMEM((B_HEADS, B_DK, B_DV), f32),
        ],
        compiler_params=pltpu.CompilerParams(
            dimension_semantics=("parallel", "arbitrary"), vmem_limit_bytes=VMEM_LIMIT),
        name="gla_layer",
    )(x, mod, gamma, w_main, w_glr, w_gate_p, b_gate[None, :], g_norm[None, :], w_out.astype(MXU_DTYPE))


def _qkv_weight(w_in):
    width = A_HEADS * A_HEAD_DIM
    col_scale = jnp.tile(jnp.repeat(jnp.array([A_HEAD_DIM ** -0.5 * LOG2E, 1.0, 1.0], f32), width), A_GROUPS)
    return (w_in * col_scale[None, :]).astype(MXU_DTYPE)


def kernel(x, c, w_in_a, w_out_a, rel_bias, w_in_b, w_gate_b, b_gate_b, gnorm_b, w_out_b, norm_mix, norm_ffn, w_ada, b_ada, w_up, conv_w, conv_b, w_down, norm_final):
    depth = w_ada.shape[0]
    mod = _adaln(c, w_ada, b_ada)
    bias = _rel_bias(rel_bias)
    for i in range(depth):
        gam_mix = norm_mix[i][None, :]
        j = i // 2
        if i % 2 == 0:
            qkv = _qkv_proj(x, mod[i], gam_mix, _qkv_weight(w_in_a[j]))
            o = _attention(qkv, bias)
            x = _out_proj(o, w_out_a[j].astype(MXU_DTYPE), x, mod[i])
        else:
            x = _gla_layer(x, mod[i], gam_mix, w_in_b[j], w_gate_b[j], b_gate_b[j], gnorm_b[j], w_out_b[j])
        last = i == depth - 1
        x = _conv_ffn(x, mod[i], norm_ffn[i][None, :], w_up[i], conv_w[i], conv_b[i], w_down[i],
                      gamma_final=norm_final[None, :] if last else None)
    return x
```

```python
import functools
import math

import jax
import jax.numpy as jnp
from jax import lax
from jax.experimental import pallas as pl
from jax.experimental.pallas import tpu as pltpu

A_CONFIGS = ((128, 1), (512, 4), (2048, 16))
A_GROUPS = len(A_CONFIGS)
A_HEADS = 16
A_HEAD_DIM = 64
N_BUCKETS = 32
MAX_DISTANCE = 2048
B_HEADS = 4
B_DK = 128
B_DV = 256
B_QK = B_HEADS * B_DK
B_V = B_HEADS * B_DV
B_GATE_RANK = 16
B_TAU = 16.0
B_CHUNK = 64
CONV_W = 3
EPS = 1e-6
NEG_INF = -1e30
LOG2E = math.log2(math.e)

LANES = 128
MXU_DTYPE = jnp.bfloat16
ATT_BLK = 128
ATT_TILE = 2048
ATT_UNITS = ATT_TILE // ATT_BLK
ATT_PIPELINE = 3
PAIR_W = 3 * LANES
N_PAIRS = A_HEADS // 2
ROW_CHUNK = 512
PROBE_ROWS = 16
FFN_TM = 512
FFN_SUB = 1
FFN_NORM_AHEAD = 3
FFN_TF = 256
FFN_ROWS = 256
FFN_DOWN_GROUP = 2
HALO = 8
OUT_TM = 1024
GLA_TM = 512
GLA_SUB = 2
GLA_PROJ_COLS = 256
VMEM_LIMIT = 56 * 1024 * 1024

f32 = jnp.float32


def _mm(a, b):
    return jnp.dot(a, b, preferred_element_type=f32)


def _mm_nt(a, b):
    return lax.dot_general(a, b, (((1,), (1,)), ((), ())), preferred_element_type=f32)


def _mm_tn(a, b):
    return lax.dot_general(a, b, (((0,), (0,)), ((), ())), preferred_element_type=f32)


def _modnorm(x, gamma, scale, shift):
    ms = jnp.mean(x * x, axis=-1, keepdims=True)
    y = x * lax.rsqrt(ms + EPS) * gamma
    return y * (1.0 + scale) + shift


def _silu(x):
    return x * (1.0 / (1.0 + jnp.exp(-x)))


def _adaln_kernel(c_ref, w_ref, b_ref, o_ref):
    s = _silu(c_ref[...]).astype(MXU_DTYPE)
    o_ref[0] = _mm(s, w_ref[0].astype(MXU_DTYPE)) + b_ref[0]


def _adaln(c, w_ada, b_ada):
    depth, d, n = w_ada.shape
    bsz = c.shape[0]
    rows = 8 * pl.cdiv(bsz, 8)
    c_pad = jnp.pad(c, ((0, rows - bsz), (0, 0)))
    out = pl.pallas_call(
        _adaln_kernel,
        grid=(depth, n // d),
        in_specs=[
            pl.BlockSpec((rows, d), lambda l, j: (0, 0)),
            pl.BlockSpec((1, d, d), lambda l, j: (l, 0, j)),
            pl.BlockSpec((1, 1, d), lambda l, j: (l, 0, j)),
        ],
        out_specs=pl.BlockSpec((1, rows, d), lambda l, j: (l, 0, j)),
        out_shape=jax.ShapeDtypeStruct((depth, rows, n), f32),
        compiler_params=pltpu.CompilerParams(
            dimension_semantics=("parallel", "parallel"), vmem_limit_bytes=VMEM_LIMIT),
        name="adaln",
    )(c_pad, w_ada, b_ada.reshape(depth, 1, n))
    return out[:, :bsz].reshape(depth, bsz, n // d, d)


def _qkv_kernel(x_ref, mod_ref, gam_ref, wq_ref, wk_ref, wv_ref, o_ref, ha_ref, hb_ref, xs_ref, *,
                dilations, tiles_per_group):
    j = pl.program_id(2)
    g = j // tiles_per_group
    t = j % tiles_per_group
    nlb = xs_ref.shape[0]
    chunks_per_step = ATT_UNITS // tiles_per_group
    h_refs = [ha_ref, hb_ref]

    def norm(xs):
        return _modnorm(xs, gam_ref[...], mod_ref[0, 1:2, :], mod_ref[0, 0:1, :])

    @pl.when(j == 0)
    def _():
        for lb in range(nlb):
            xs_ref[lb] = x_ref[0, :, lb * LANES:(lb + 1) * LANES]
        for c in range(ATT_UNITS):
            rows = slice(c * ATT_BLK, (c + 1) * ATT_BLK)
            ha_ref[rows, :] = norm(x_ref[0, rows, :]).astype(ha_ref.dtype)

    def deinterleave_chunk(dst_ref, d, cc):
        per = ATT_UNITS // d
        c = t * chunks_per_step + cc
        start = c // per + (c % per) * (ATT_BLK * d)
        rows = pl.ds(start, ATT_BLK, stride=d)
        xs = jnp.concatenate([xs_ref[lb, rows, :] for lb in range(nlb)], axis=1)
        r0 = pl.multiple_of(c * ATT_BLK, ATT_BLK)
        dst_ref[pl.ds(r0, ATT_BLK), :] = norm(xs).astype(dst_ref.dtype)
        probe = dst_ref[pl.ds(r0, PROBE_ROWS), 0:LANES]
        return (probe != probe) & (probe == probe)

    def project_chunk(src_ref, rc, never):
        rows = slice(rc * ROW_CHUNK, (rc + 1) * ROW_CHUNK)
        lhs = src_ref[rows, :]
        for k, w_ref in enumerate((wq_ref, wk_ref, wv_ref)):
            res = _mm(lhs, w_ref[...]).astype(o_ref.dtype)
            lanes = slice(k * LANES, (k + 1) * LANES)
            o_ref[0, 0, 1, rows, lanes] = res[:, LANES:]
            if never is not None and k == 0:
                top = rc * ROW_CHUNK + PROBE_ROWS
                o_ref[0, 0, 0, rc * ROW_CHUNK:top, lanes] = jnp.where(
                    never, jnp.zeros_like(res[:PROBE_ROWS, :LANES]), res[:PROBE_ROWS, :LANES])
                o_ref[0, 0, 0, top:(rc + 1) * ROW_CHUNK, lanes] = res[PROBE_ROWS:, :LANES]
            else:
                o_ref[0, 0, 0, rows, lanes] = res[:, :LANES]

    assert dilations[0] == 1 and chunks_per_step == ATT_TILE // ROW_CHUNK
    for gi in range(len(dilations)):
        @pl.when(g == gi)
        def _(gi=gi):
            shares = [[0, 1]] + [[cc] for cc in range(2, chunks_per_step)] + [[]]
            for rc in range(chunks_per_step):
                never = None
                if gi + 1 < len(dilations):
                    for cc in shares[rc]:
                        probe = deinterleave_chunk(h_refs[(gi + 1) % 2], dilations[gi + 1], cc)
                        never = probe if never is None else never | probe
                project_chunk(h_refs[gi % 2], rc, never)


def _qkv_proj(x, mod, gamma, w):
    bsz, s, d = x.shape
    dilations = tuple(dl for _, dl in A_CONFIGS)
    tn = 2 * LANES
    tiles_per_group = A_HEADS * A_HEAD_DIM // tn
    kern = functools.partial(_qkv_kernel, dilations=dilations, tiles_per_group=tiles_per_group)

    def w_spec(k):
        return pl.BlockSpec(
            (d, tn), lambda b, i, j: (0, (j // tiles_per_group * 3 + k) * tiles_per_group + j % tiles_per_group))

    return pl.pallas_call(
        kern,
        grid=(bsz, s // ATT_TILE, A_GROUPS * tiles_per_group),
        in_specs=[
            pl.BlockSpec((1, ATT_TILE, d), lambda b, i, j: (b, i, 0)),
            pl.BlockSpec((1, 6, d), lambda b, i, j: (b, 0, 0)),
            pl.BlockSpec((1, d), lambda b, i, j: (0, 0)),
            w_spec(0), w_spec(1), w_spec(2),
        ],
        out_specs=pl.BlockSpec(
            (1, 1, 2, ATT_TILE, PAIR_W),
            lambda b, i, j: (j // tiles_per_group, b, j % tiles_per_group, i, 0)),
        out_shape=jax.ShapeDtypeStruct((A_GROUPS, bsz, N_PAIRS, s, PAIR_W), MXU_DTYPE),
        scratch_shapes=[pltpu.VMEM((ATT_TILE, d), MXU_DTYPE),
                        pltpu.VMEM((ATT_TILE, d), MXU_DTYPE),
                        pltpu.VMEM((d // LANES, ATT_TILE, LANES), f32)],
        compiler_params=pltpu.CompilerParams(
            dimension_semantics=("parallel", "parallel", "arbitrary"),
            vmem_limit_bytes=VMEM_LIMIT),
        name="qkv_proj",
    )(x, mod, gamma, w, w, w)


def _bias_kernel(tab_ref, bkt_ref, o_ref):
    col = pl.program_id(0) * A_HEADS + pl.program_id(1)
    bkt = bkt_ref[0]
    acc = jnp.full(bkt.shape, NEG_INF, f32)
    for k in range(N_BUCKETS):
        acc = jnp.where(bkt == k, tab_ref[k, col] * LOG2E, acc)
    o_ref[0, 0] = acc


def _t5_bucket(dist):
    max_exact = N_BUCKETS // 2
    n = jnp.maximum(dist, max_exact).astype(f32)
    large = max_exact + (jnp.log(n / max_exact) / math.log(MAX_DISTANCE / max_exact)
                         * (N_BUCKETS - max_exact)).astype(jnp.int32)
    large = jnp.minimum(large, N_BUCKETS - 1)
    return jnp.where(dist < max_exact, dist, large)


def _rel_bias(rel_bias):
    qi = jnp.arange(ATT_BLK)[:, None]
    ki = jnp.arange(2 * ATT_BLK)[None, :]
    steps = qi + ATT_BLK - ki
    band = (steps >= 0) & (steps <= ATT_BLK)
    bucket = jnp.stack([
        jnp.where(band, _t5_bucket(jnp.clip(steps, 0, ATT_BLK) * dl), -1)
        for _, dl in A_CONFIGS]).astype(jnp.int32)
    return pl.pallas_call(
        _bias_kernel,
        grid=(A_GROUPS, A_HEADS),
        in_specs=[
            pl.BlockSpec(memory_space=pltpu.SMEM),
            pl.BlockSpec((1, ATT_BLK, 2 * ATT_BLK), lambda g, h: (g, 0, 0)),
        ],
        out_specs=pl.BlockSpec((1, 1, ATT_BLK, 2 * ATT_BLK), lambda g, h: (g, h, 0, 0)),
        out_shape=jax.ShapeDtypeStruct((A_GROUPS, A_HEADS, ATT_BLK, 2 * ATT_BLK), f32),
        name="rel_bias",
    )(rel_bias, bucket)


def _attn_kernel(qkv_ref, bias_ref, out_ref, prev0, prev1, prev2, o1, l1, o2, l2, biasp):
    first = pl.program_id(2) == 0

    @pl.when(first)
    def _():
        prev0[...] = jnp.zeros_like(prev0)
        prev1[...] = jnp.zeros_like(prev1)
        prev2[...] = jnp.zeros_like(prev2)

    for g in range(A_GROUPS):
        for hh in range(2):
            biasp[g, hh, :, 0:ATT_BLK] = jnp.where(first, NEG_INF, bias_ref[g, hh, :, 0:ATT_BLK])
            biasp[g, hh, :, ATT_BLK:2 * ATT_BLK] = bias_ref[g, hh, :, ATT_BLK:2 * ATT_BLK]

    lane = lax.broadcasted_iota(jnp.int32, (1, LANES), 1)
    lo = lane < A_HEAD_DIM
    qmask = (jnp.where(lo, 1.0, 0.0).astype(MXU_DTYPE), jnp.where(lo, 0.0, 1.0).astype(MXU_DTYPE))

    def keys_values(g, rc, prev_ref, rp, lanes):
        if prev_ref is None:
            return qkv_ref[g, 0, 0, rp:rp + 2 * ATT_BLK, lanes]
        off = lanes.start - LANES
        return jnp.concatenate([prev_ref[rp:rp + ATT_BLK, off:off + LANES],
                                qkv_ref[g, 0, 0, rc:rc + ATT_BLK, lanes]], axis=0)

    def scores(g, rc, prev_ref, rp):
        q = qkv_ref[g, 0, 0, rc:rc + ATT_BLK, 0:LANES]
        k = keys_values(g, rc, prev_ref, rp, slice(LANES, 2 * LANES))
        bias = bias_ref if prev_ref is None else biasp
        return [_mm_nt(q * qmask[hh], k) + bias[g, hh] for hh in range(2)]

    def attend(g, rc, prev_ref, rp, s):
        v = keys_values(g, rc, prev_ref, rp, slice(2 * LANES, 3 * LANES))
        accs, dens, ms = [], [], []
        for sh in s:
            m = jnp.max(sh, axis=-1, keepdims=True)
            p = jnp.exp2(sh - m)
            dens.append(jnp.sum(p, axis=-1, keepdims=True))
            accs.append(_mm(p.astype(MXU_DTYPE), v))
            ms.append(m)
        den = jnp.where(lo, dens[0], dens[1])
        o = jnp.where(lo, accs[0], accs[1]) * (1.0 / den)
        lse = jnp.where(lo, ms[0], ms[1]) + jnp.log2(den)
        return o, jnp.broadcast_to(lse, o.shape)

    def store(o_ref, l_ref, rows):
        def post(o, lse):
            o_ref[rows, :] = o
            l_ref[rows, :] = lse
        return post

    def merge(rows):
        def post(o0, lse0):
            lse1, lse2 = l1[rows, :], l2[rows, :]
            mx = jnp.maximum(jnp.maximum(lse0, lse1), lse2)
            e0, e1, e2 = jnp.exp2(lse0 - mx), jnp.exp2(lse1 - mx), jnp.exp2(lse2 - mx)
            mix = (e0 * o0 + e1 * o1[rows, :] + e2 * o2[rows, :]) * (1.0 / (e0 + e1 + e2))
            out_ref[0, rows, :] = mix.astype(out_ref.dtype)
        return post

    units = []
    d2 = A_CONFIGS[2][1]
    for u in range(ATT_UNITS):
        units.append((2, u * ATT_BLK, prev2, u * ATT_BLK, store(o2, l2, pl.ds(u, ATT_BLK, stride=d2))))
    d1 = A_CONFIGS[1][1]
    per = ATT_UNITS // d1
    for u in range(ATT_UNITS):
        r, q = divmod(u, per)
        rc = u * ATT_BLK
        post = store(o1, l1, pl.ds(q * ATT_BLK * d1 + r, ATT_BLK, stride=d1))
        if q == 0:
            units.append((1, rc, prev1, (r * per + per - 1) * ATT_BLK, post))
        else:
            units.append((1, rc, None, rc - ATT_BLK, post))
    for u in range(ATT_UNITS):
        rc = u * ATT_BLK
        post = merge(slice(rc, rc + ATT_BLK))
        units.append((0, rc, prev0, 0, post) if u == 0 else (0, rc, None, rc - ATT_BLK, post))

    pending = [scores(*un[:4]) for un in units[:ATT_PIPELINE]]
    for idx, un in enumerate(units):
        if idx + ATT_PIPELINE < len(units):
            pending.append(scores(*units[idx + ATT_PIPELINE][:4]))
        o, lse = attend(*un[:4], pending.pop(0))
        un[4](o, lse)

    prev0[...] = qkv_ref[0, 0, 0, ATT_TILE - ATT_BLK:ATT_TILE, LANES:3 * LANES]
    prev1[...] = qkv_ref[1, 0, 0, :, LANES:3 * LANES]
    prev2[...] = qkv_ref[2, 0, 0, :, LANES:3 * LANES]


def _attention(qkv, bias):
    _, bsz, _, s, _ = qkv.shape
    return pl.pallas_call(
        _attn_kernel,
        grid=(bsz, N_PAIRS, s // ATT_TILE),
        in_specs=[
            pl.BlockSpec((A_GROUPS, 1, 1, ATT_TILE, PAIR_W), lambda b, p, i: (0, b, p, i, 0)),
            pl.BlockSpec((A_GROUPS, 2, ATT_BLK, 2 * ATT_BLK), lambda b, p, i: (0, p, 0, 0)),
        ],
        out_specs=pl.BlockSpec((1, ATT_TILE, LANES), lambda b, p, i: (b, i, p)),
        out_shape=jax.ShapeDtypeStruct((bsz, s, A_HEADS * A_HEAD_DIM), MXU_DTYPE),
        scratch_shapes=[
            pltpu.VMEM((ATT_BLK, 2 * LANES), MXU_DTYPE),
            pltpu.VMEM((ATT_TILE, 2 * LANES), MXU_DTYPE),
            pltpu.VMEM((ATT_TILE, 2 * LANES), MXU_DTYPE),
            pltpu.VMEM((ATT_TILE, LANES), f32),
            pltpu.VMEM((ATT_TILE, LANES), f32),
            pltpu.VMEM((ATT_TILE, LANES), f32),
            pltpu.VMEM((ATT_TILE, LANES), f32),
            pltpu.VMEM((A_GROUPS, 2, ATT_BLK, 2 * ATT_BLK), f32),
        ],
        compiler_params=pltpu.CompilerParams(
            dimension_semantics=("parallel", "parallel", "arbitrary"),
            vmem_limit_bytes=VMEM_LIMIT),
        name="dilated_attn",
    )(qkv, bias)


def _outproj_kernel(o_ref, w_ref, x_ref, mod_ref, out_ref):
    w = w_ref[...]
    gate = mod_ref[0, 2:3, :]
    for rc in range(OUT_TM // ROW_CHUNK):
        rows = slice(rc * ROW_CHUNK, (rc + 1) * ROW_CHUNK)
        out_ref[0, rows, :] = x_ref[0, rows, :] + gate * _mm(o_ref[0, rows, :], w)


def _out_proj(o, w, x, mod):
    bsz, s, d = x.shape
    k = o.shape[-1]
    return pl.pallas_call(
        _outproj_kernel,
        grid=(bsz, s // OUT_TM),
        in_specs=[
            pl.BlockSpec((1, OUT_TM, k), lambda b, i: (b, i, 0)),
            pl.BlockSpec((k, d), lambda b, i: (0, 0)),
            pl.BlockSpec((1, OUT_TM, d), lambda b, i: (b, i, 0)),
            pl.BlockSpec((1, 6, d), lambda b, i: (b, 0, 0)),
        ],
        out_specs=pl.BlockSpec((1, OUT_TM, d), lambda b, i: (b, i, 0)),
        out_shape=jax.ShapeDtypeStruct((bsz, s, d), f32),
        compiler_params=pltpu.CompilerParams(
            dimension_semantics=("parallel", "parallel"), vmem_limit_bytes=VMEM_LIMIT),
        name="out_proj",
    )(o, w, x, mod)


def _ffn_kernel(x_ref, halo_ref, mod_ref, gam_ref, wup_ref, cw_ref, cb_ref, wdn_ref, *rest, final):
    if final:
        gfin_ref, out_ref, h0_ref, h1_ref, u0_ref, u1_ref, act_ref, acc_ref = rest
    else:
        out_ref, h0_ref, h1_ref, u0_ref, u1_ref, act_ref, acc_ref = rest
    h_refs, u_refs = (h0_ref, h1_ref), (u0_ref, u1_ref)
    gam, scale, shift = gam_ref[...], mod_ref[0, 4:5, :], mod_ref[0, 3:4, :]
    d_ff = wdn_ref.shape[0]
    nchunk = d_ff // FFN_TF
    nsub = x_ref.shape[1] // FFN_TM

    def normalize(k):
        h_ref = h_refs[k % 2]
        if k == 0:
            halo = _modnorm(halo_ref[0], gam, scale, shift)
            halo = jnp.where(pl.program_id(1) == 0, 0.0, halo)
        else:
            halo = _modnorm(x_ref[0, k * FFN_TM - HALO:k * FFN_TM, :], gam, scale, shift)
        h_ref[0:HALO, :] = halo.astype(h_ref.dtype)
        h = _modnorm(x_ref[0, k * FFN_TM:(k + 1) * FFN_TM, :], gam, scale, shift)
        h_ref[HALO:HALO + FFN_TM, :] = h.astype(h_ref.dtype)

    def halves(c):
        return [slice(half * d_ff + c * FFN_TF, half * d_ff + (c + 1) * FFN_TF) for half in range(2)]

    def up(job):
        k, c = divmod(job, nchunk)
        for half, cols in enumerate(halves(c)):
            u_refs[job % 2][:, half * FFN_TF:(half + 1) * FFN_TF] = _mm(h_refs[k % 2][...], wup_ref[:, cols])

    def conv_act(job):
        c = job % nchunk
        u_ref = u_refs[job % 2]
        for rb in range(FFN_TM // FFN_ROWS):
            base = HALO + rb * FFN_ROWS
            ab = []
            for half, cols in enumerate(halves(c)):
                v = cb_ref[:, cols]
                for t in range(CONV_W):
                    off = base - (CONV_W - 1) + t
                    v = v + cw_ref[t:t + 1, cols] * u_ref[off:off + FFN_ROWS, half * FFN_TF:(half + 1) * FFN_TF]
                ab.append(v)
            act_ref[rb * FFN_ROWS:(rb + 1) * FFN_ROWS, c * FFN_TF:(c + 1) * FFN_TF] = (
                _silu(ab[0]) * ab[1]).astype(act_ref.dtype)

    def down(first, last):
        cols = slice(first * FFN_TF, last * FFN_TF)
        return _mm(act_ref[:, cols], wdn_ref[cols, :])

    def finish(k, start):
        ffn = down(start, nchunk)
        if start > 0:
            ffn = ffn + acc_ref[...]
        rows = slice(k * FFN_TM, (k + 1) * FFN_TM)
        y = x_ref[0, rows, :] + mod_ref[0, 5:6, :] * ffn
        if final:
            ms = jnp.mean(y * y, axis=-1, keepdims=True)
            y = y * lax.rsqrt(ms + EPS) * gfin_ref[...]
        out_ref[0, rows, :] = y

    normalize(0)
    up(0)
    start, pending = 0, None
    for job in range(nsub * nchunk):
        k, c = divmod(job, nchunk)
        if c == 0:
            start = 0
        if c == nchunk - FFN_NORM_AHEAD and k + 1 < nsub:
            normalize(k + 1)
        if job + 1 < nsub * nchunk:
            up(job + 1)
        if pending is not None:
            if pending[0] == 0:
                acc_ref[...] = down(*pending)
            else:
                acc_ref[...] += down(*pending)
            pending = None
        if c + 1 - start == FFN_DOWN_GROUP and c + 1 < nchunk:
            pending, start = (start, c + 1), c + 1
        conv_act(job)
        if c == nchunk - 1:
            finish(k, start)


def _conv_ffn(x, mod, gamma, w_up, conv_w, conv_b, w_down, gamma_final=None):
    bsz, s, d = x.shape
    d_ff = w_down.shape[0]
    final = gamma_final is not None
    rows = FFN_SUB * FFN_TM
    tiles_per_halo = rows // HALO
    resident = dict(pipeline_mode=pl.Buffered(1))
    in_specs = [
        pl.BlockSpec((1, rows, d), lambda b, i: (b, i, 0)),
        pl.BlockSpec((1, HALO, d), lambda b, i: (b, jnp.maximum(i * tiles_per_halo - 1, 0), 0)),
        pl.BlockSpec((1, 6, d), lambda b, i: (b, 0, 0)),
        pl.BlockSpec((1, d), lambda b, i: (0, 0)),
        pl.BlockSpec((d, 2 * d_ff), lambda b, i: (0, 0), **resident),
        pl.BlockSpec((CONV_W, 2 * d_ff), lambda b, i: (0, 0)),
        pl.BlockSpec((1, 2 * d_ff), lambda b, i: (0, 0)),
        pl.BlockSpec((d_ff, d), lambda b, i: (0, 0), **resident),
    ]
    args = [x, x, mod, gamma, w_up.astype(MXU_DTYPE), conv_w, conv_b[None, :], w_down.astype(MXU_DTYPE)]
    if final:
        in_specs.append(pl.BlockSpec((1, d), lambda b, i: (0, 0)))
        args.append(gamma_final)
    return pl.pallas_call(
        functools.partial(_ffn_kernel, final=final),
        grid=(bsz, s // rows),
        in_specs=in_specs,
        out_specs=pl.BlockSpec((1, rows, d), lambda b, i: (b, i, 0)),
        out_shape=jax.ShapeDtypeStruct((bsz, s, d), f32),
        scratch_shapes=[
            pltpu.VMEM((FFN_TM + HALO, d), MXU_DTYPE),
            pltpu.VMEM((FFN_TM + HALO, d), MXU_DTYPE),
            pltpu.VMEM((FFN_TM + HALO, 2 * FFN_TF), f32),
            pltpu.VMEM((FFN_TM + HALO, 2 * FFN_TF), f32),
            pltpu.VMEM((FFN_TM, d_ff), MXU_DTYPE),
            pltpu.VMEM((FFN_TM, d), f32),
        ],
        compiler_params=pltpu.CompilerParams(
            dimension_semantics=("parallel", "parallel"), vmem_limit_bytes=VMEM_LIMIT),
        name="conv_ffn_final" if final else "conv_ffn",
    )(*args)


def _gla_kernel(x_ref, mod_ref, gam_ref, win_ref, wglr_ref, wgate_ref, bgate_ref, gnorm_ref, wout_ref,
                out_ref, h_ref, proj0_ref, proj1_ref, gk0_ref, gk1_ref, og0_ref, og1_ref, state_ref):
    @pl.when(pl.program_id(1) == 0)
    def _():
        state_ref[...] = jnp.zeros_like(state_ref)

    proj_refs, gk_refs, og_refs = (proj0_ref, proj1_ref), (gk0_ref, gk1_ref), (og0_ref, og1_ref)
    nsub = x_ref.shape[1] // GLA_TM
    nchunk = GLA_TM // B_CHUNK
    n_main = win_ref.shape[1]

    ri = lax.broadcasted_iota(jnp.int32, (B_CHUNK, B_CHUNK), 0)
    ci = lax.broadcasted_iota(jnp.int32, (B_CHUNK, B_CHUNK), 1)
    causal = ci <= ri
    tri = causal.astype(MXU_DTYPE)
    gnorm = gnorm_ref[...]
    qscale = B_DK ** -0.5
    r_off = 2 * B_QK + B_V
    heads = [slice(hd * B_DK, (hd + 1) * B_DK) for hd in range(B_HEADS)]

    def project_steps(k):
        rows = slice(k * GLA_TM, (k + 1) * GLA_TM)

        def normalize():
            h = _modnorm(x_ref[0, rows, :], gam_ref[...], mod_ref[0, 1:2, :], mod_ref[0, 0:1, :])
            h_ref[...] = h.astype(h_ref.dtype)

        def piece(cols):
            def run():
                proj_refs[k % 2][:, cols] = _mm(h_ref[...], win_ref[:, cols])
            return run

        def gate():
            glr = _mm(h_ref[...], wglr_ref[...]).astype(MXU_DTYPE)
            z = _mm(glr, wgate_ref[...]) + bgate_ref[...]
            gk_refs[k % 2][...] = (jnp.minimum(z, 0.0) - jnp.log1p(jnp.exp(-jnp.abs(z)))) * (1.0 / B_TAU)

        pieces = [piece(slice(c0, c0 + GLA_PROJ_COLS)) for c0 in range(0, n_main, GLA_PROJ_COLS)]
        return [normalize] + pieces + [gate]

    def local(k, c):
        proj_ref, gk_ref = proj_refs[k % 2], gk_refs[k % 2]
        rows = slice(c * B_CHUNK, (c + 1) * B_CHUNK)
        gk = gk_ref[rows, :]
        g_hi = gk.astype(MXU_DTYPE)
        rem = gk - g_hi.astype(f32)
        g_mid = rem.astype(MXU_DTYPE)
        g_lo = (rem - g_mid.astype(f32)).astype(MXU_DTYPE)
        bcum = _mm(tri, g_hi) + _mm(tri, g_mid) + _mm(tri, g_lo)
        blast = bcum[B_CHUNK - 1:B_CHUNK, :]
        q_t = ((proj_ref[rows, 0:B_QK] * qscale) * jnp.exp(bcum)).astype(MXU_DTYPE)
        kk = proj_ref[rows, B_QK:2 * B_QK]
        k_t = (kk * jnp.exp(-bcum)).astype(MXU_DTYPE)
        k_d = (kk * jnp.exp(blast - bcum)).astype(MXU_DTYPE)
        v = [proj_ref[rows, 2 * B_QK + hd * B_DV:2 * B_QK + (hd + 1) * B_DV].astype(MXU_DTYPE)
             for hd in range(B_HEADS)]
        a = [jnp.where(causal, _mm_nt(q_t[:, ks], k_t[:, ks]), 0.0).astype(MXU_DTYPE) for ks in heads]
        kv = [_mm_tn(k_d[:, ks], v[hd]) for hd, ks in enumerate(heads)]
        o_intra = [_mm(a[hd], v[hd]) for hd in range(B_HEADS)]
        return q_t, blast, kv, o_intra

    def recur(k, c, loc, state):
        q_t, blast, kv, o_intra = loc
        rows = slice(c * B_CHUNK, (c + 1) * B_CHUNK)
        new_state = []
        for hd, ks in enumerate(heads):
            o = o_intra[hd] + _mm(q_t[:, ks], state[hd].astype(MXU_DTYPE))
            decay = jnp.exp(jnp.broadcast_to(blast[:, ks], (B_DK, B_DK)).T)
            new_state.append(jnp.concatenate([decay] * (B_DV // B_DK), axis=1) * state[hd] + kv[hd])
            o = o * lax.rsqrt(jnp.mean(o * o, axis=-1, keepdims=True) + EPS) * gnorm
            r = proj_refs[k % 2][rows, r_off + hd * B_DV:r_off + (hd + 1) * B_DV]
            og_refs[k % 2][rows, hd * B_DV:(hd + 1) * B_DV] = (o * _silu(r)).astype(og_refs[k % 2].dtype)
        return new_state

    for step in project_steps(0):
        step()
    state = [state_ref[hd] for hd in range(B_HEADS)]
    for k in range(nsub):
        ahead = project_steps(k + 1) if k + 1 < nsub else []
        loc = local(k, 0)
        for c in range(nchunk):
            nxt = local(k, c + 1) if c + 1 < nchunk else None
            state = recur(k, c, loc, state)
            loc = nxt
            take = -(-len(ahead) // (nchunk - c))
            for step in ahead[:take]:
                step()
            ahead = ahead[take:]
        rows = slice(k * GLA_TM, (k + 1) * GLA_TM)
        out_ref[0, rows, :] = x_ref[0, rows, :] + mod_ref[0, 2:3, :] * _mm(og_refs[k % 2][...], wout_ref[...])
    for hd in range(B_HEADS):
        state_ref[hd] = state[hd]


def _gla_layer(x, mod, gamma, w_in, w_gate, b_gate, g_norm, w_out):
    bsz, s, d = x.shape
    glr0 = 2 * B_QK + B_V
    w_main = jnp.concatenate([w_in[:, :glr0], w_in[:, glr0 + B_GATE_RANK:]], axis=1).astype(MXU_DTYPE)
    w_glr = jnp.pad(w_in[:, glr0:glr0 + B_GATE_RANK], ((0, 0), (0, LANES - B_GATE_RANK))).astype(MXU_DTYPE)
    w_gate_p = jnp.pad(w_gate, ((0, LANES - B_GATE_RANK), (0, 0))).astype(MXU_DTYPE)
    n_main = w_main.shape[1]
    rows = GLA_SUB * GLA_TM
    const = lambda b, i: (0, 0)
    resident = dict(pipeline_mode=pl.Buffered(1))
    return pl.pallas_call(
        _gla_kernel,
        grid=(bsz, s // rows),
        in_specs=[
            pl.BlockSpec((1, rows, d), lambda b, i: (b, i, 0)),
            pl.BlockSpec((1, 6, d), lambda b, i: (b, 0, 0)),
            pl.BlockSpec((1, d), const),
            pl.BlockSpec((d, n_main), const, **resident),
            pl.BlockSpec((d, LANES), const),
            pl.BlockSpec((LANES, B_QK), const),
            pl.BlockSpec((1, B_QK), const),
            pl.BlockSpec((1, B_DV), const),
            pl.BlockSpec((B_V, d), const, **resident),
        ],
        out_specs=pl.BlockSpec((1, rows, d), lambda b, i: (b, i, 0)),
        out_shape=jax.ShapeDtypeStruct((bsz, s, d), f32),
        scratch_shapes=[
            pltpu.VMEM((GLA_TM, d), MXU_DTYPE),
            pltpu.VMEM((GLA_TM, n_main), f32),
            pltpu.VMEM((GLA_TM, n_main), f32),
            pltpu.VMEM((GLA_TM, B_QK), f32),
            pltpu.VMEM((GLA_TM, B_QK), f32),
            pltpu.VMEM((GLA_TM, B_V), MXU_DTYPE),
            pltpu.VMEM((GLA_TM, B_V), MXU_DTYPE),
            pltpu.VMEM((B_HEADS, B_DK, B_DV), f32),
        ],
        compiler_params=pltpu.CompilerParams(
            dimension_semantics=("parallel", "arbitrary"), vmem_limit_bytes=VMEM_LIMIT),
        name="gla_layer",
    )(x, mod, gamma, w_main, w_glr, w_gate_p, b_gate[None, :], g_norm[None, :], w_out.astype(MXU_DTYPE))


def _qkv_weight(w_in):
    width = A_HEADS * A_HEAD_DIM
    col_scale = jnp.tile(jnp.repeat(jnp.array([A_HEAD_DIM ** -0.5 * LOG2E, 1.0, 1.0], f32), width), A_GROUPS)
    return (w_in * col_scale[None, :]).astype(MXU_DTYPE)


def kernel(x, c, w_in_a, w_out_a, rel_bias, w_in_b, w_gate_b, b_gate_b, gnorm_b, w_out_b, norm_mix, norm_ffn, w_ada, b_ada, w_up, conv_w, conv_b, w_down, norm_final):
    depth = w_ada.shape[0]
    mod = _adaln(c, w_ada, b_ada)
    bias = _rel_bias(rel_bias)
    for i in range(depth):
        gam_mix = norm_mix[i][None, :]
        j = i // 2
        if i % 2 == 0:
            qkv = _qkv_proj(x, mod[i], gam_mix, _qkv_weight(w_in_a[j]))
            o = _attention(qkv, bias)
            x = _out_proj(o, w_out_a[j].astype(MXU_DTYPE), x, mod[i])
        else:
            x = _gla_layer(x, mod[i], gam_mix, w_in_b[j], w_gate_b[j], b_gate_b[j], gnorm_b[j], w_out_b[j])
        last = i == depth - 1
        x = _conv_ffn(x, mod[i], norm_ffn[i][None, :], w_up[i], conv_w[i], conv_b[i], w_down[i],
                      gamma_final=norm_final[None, :] if last else None)
    return x
```

```python
import functools
import math

import jax
import jax.numpy as jnp
from jax import lax
from jax.experimental import pallas as pl
from jax.experimental.pallas import tpu as pltpu

A_CONFIGS = ((128, 1), (512, 4), (2048, 16))
A_GROUPS = len(A_CONFIGS)
A_HEADS = 16
A_HEAD_DIM = 64
N_BUCKETS = 32
MAX_DISTANCE = 2048
B_HEADS = 4
B_DK = 128
B_DV = 256
B_QK = B_HEADS * B_DK
B_V = B_HEADS * B_DV
B_GATE_RANK = 16
B_TAU = 16.0
B_CHUNK = 64
CONV_W = 3
EPS = 1e-6
NEG_INF = -1e30
LOG2E = math.log2(math.e)

LANES = 128
MXU_DTYPE = jnp.bfloat16
ATT_BLK = 128
ATT_TILE = 2048
ATT_UNITS = ATT_TILE // ATT_BLK
ATT_PIPELINE = 3
PAIR_W = 3 * LANES
N_PAIRS = A_HEADS // 2
ROW_CHUNK = 512
PROBE_ROWS = 16
FFN_TM = 512
FFN_SUB = 1
FFN_NORM_AHEAD = 3
FFN_TF = 256
FFN_ROWS = 256
FFN_DOWN_GROUP = 2
HALO = 8
OUT_TM = 1024
GLA_TM = 512
GLA_SUB = 2
GLA_PROJ_COLS = 256
VMEM_LIMIT = 56 * 1024 * 1024

f32 = jnp.float32


def _mm(a, b):
    return jnp.dot(a, b, preferred_element_type=f32)


def _mm_nt(a, b):
    return lax.dot_general(a, b, (((1,), (1,)), ((), ())), preferred_element_type=f32)


def _mm_tn(a, b):
    return lax.dot_general(a, b, (((0,), (0,)), ((), ())), preferred_element_type=f32)


def _modnorm(x, gamma, scale, shift):
    ms = jnp.mean(x * x, axis=-1, keepdims=True)
    y = x * lax.rsqrt(ms + EPS) * gamma
    return y * (1.0 + scale) + shift


def _silu(x):
    return x * (1.0 / (1.0 + jnp.exp(-x)))


def _adaln_kernel(c_ref, w_ref, b_ref, o_ref):
    s = _silu(c_ref[...]).astype(MXU_DTYPE)
    o_ref[0] = _mm(s, w_ref[0].astype(MXU_DTYPE)) + b_ref[0]


def _adaln(c, w_ada, b_ada):
    depth, d, n = w_ada.shape
    bsz = c.shape[0]
    rows = 8 * pl.cdiv(bsz, 8)
    c_pad = jnp.pad(c, ((0, rows - bsz), (0, 0)))
    out = pl.pallas_call(
        _adaln_kernel,
        grid=(depth, n // d),
        in_specs=[
            pl.BlockSpec((rows, d), lambda l, j: (0, 0)),
            pl.BlockSpec((1, d, d), lambda l, j: (l, 0, j)),
            pl.BlockSpec((1, 1, d), lambda l, j: (l, 0, j)),
        ],
        out_specs=pl.BlockSpec((1, rows, d), lambda l, j: (l, 0, j)),
        out_shape=jax.ShapeDtypeStruct((depth, rows, n), f32),
        compiler_params=pltpu.CompilerParams(
            dimension_semantics=("parallel", "parallel"), vmem_limit_bytes=VMEM_LIMIT),
        name="adaln",
    )(c_pad, w_ada, b_ada.reshape(depth, 1, n))
    return out[:, :bsz].reshape(depth, bsz, n // d, d)


def _qkv_kernel(x_ref, mod_ref, gam_ref, wq_ref, wk_ref, wv_ref, o_ref, ha_ref, hb_ref, sa_ref, sb_ref, *,
                dilations, tiles_per_group):
    j = pl.program_id(2)
    g = j // tiles_per_group
    t = j % tiles_per_group
    nlb = sa_ref.shape[0]
    chunks_per_step = ATT_UNITS // tiles_per_group
    h_refs, stage_refs = [ha_ref, hb_ref], [sa_ref, sb_ref]

    @pl.when(j == 0)
    def _():
        for c in range(ATT_UNITS):
            rows = slice(c * ATT_BLK, (c + 1) * ATT_BLK)
            hn = _modnorm(x_ref[0, rows, :], gam_ref[...], mod_ref[0, 1:2, :], mod_ref[0, 0:1, :])
            ha_ref[rows, :] = hn.astype(ha_ref.dtype)
            for lb in range(nlb):
                sa_ref[lb, rows, :] = hn[:, lb * LANES:(lb + 1) * LANES]

    def deinterleave_chunk(gi, cc):
        d, dp = dilations[gi], dilations[gi - 1]
        rel = d // dp
        per = ATT_UNITS // d
        c = t * chunks_per_step + cc
        r, q = c // per, c % per
        start = (r % dp) * (ATT_TILE // dp) + q * (ATT_BLK * rel) + r // dp
        rows = pl.ds(start, ATT_BLK, stride=rel)
        r0 = pl.multiple_of(c * ATT_BLK, ATT_BLK)
        parts = [stage_refs[(gi - 1) % 2][lb, rows, :] for lb in range(nlb)]
        if gi + 1 < len(dilations):
            for lb in range(nlb):
                stage_refs[gi % 2][lb, pl.ds(r0, ATT_BLK), :] = parts[lb]
        dst_ref = h_refs[gi % 2]
        dst_ref[pl.ds(r0, ATT_BLK), :] = jnp.concatenate(parts, axis=1).astype(dst_ref.dtype)
        probe = dst_ref[pl.ds(r0, PROBE_ROWS), 0:LANES]
        return (probe != probe) & (probe == probe)

    def project_chunk(src_ref, rc, never):
        rows = slice(rc * ROW_CHUNK, (rc + 1) * ROW_CHUNK)
        lhs = src_ref[rows, :]
        for k, w_ref in enumerate((wq_ref, wk_ref, wv_ref)):
            res = _mm(lhs, w_ref[...]).astype(o_ref.dtype)
            lanes = slice(k * LANES, (k + 1) * LANES)
            o_ref[0, 0, 1, rows, lanes] = res[:, LANES:]
            if never is not None and k == 0:
                top = rc * ROW_CHUNK + PROBE_ROWS
                o_ref[0, 0, 0, rc * ROW_CHUNK:top, lanes] = jnp.where(
                    never, jnp.zeros_like(res[:PROBE_ROWS, :LANES]), res[:PROBE_ROWS, :LANES])
                o_ref[0, 0, 0, top:(rc + 1) * ROW_CHUNK, lanes] = res[PROBE_ROWS:, :LANES]
            else:
                o_ref[0, 0, 0, rows, lanes] = res[:, :LANES]

    assert dilations[0] == 1 and chunks_per_step == ATT_TILE // ROW_CHUNK
    assert all(d % dp == 0 for dp, d in zip(dilations, dilations[1:]))
    for gi in range(len(dilations)):
        @pl.when(g == gi)
        def _(gi=gi):
            shares = [[0, 1]] + [[cc] for cc in range(2, chunks_per_step)] + [[]]
            for rc in range(chunks_per_step):
                never = None
                if gi + 1 < len(dilations):
                    for cc in shares[rc]:
                        probe = deinterleave_chunk(gi + 1, cc)
                        never = probe if never is None else never | probe
                project_chunk(h_refs[gi % 2], rc, never)


def _qkv_proj(x, mod, gamma, w):
    bsz, s, d = x.shape
    dilations = tuple(dl for _, dl in A_CONFIGS)
    tn = 2 * LANES
    tiles_per_group = A_HEADS * A_HEAD_DIM // tn
    kern = functools.partial(_qkv_kernel, dilations=dilations, tiles_per_group=tiles_per_group)

    def w_spec(k):
        return pl.BlockSpec(
            (d, tn), lambda b, i, j: (0, (j // tiles_per_group * 3 + k) * tiles_per_group + j % tiles_per_group))

    return pl.pallas_call(
        kern,
        grid=(bsz, s // ATT_TILE, A_GROUPS * tiles_per_group),
        in_specs=[
            pl.BlockSpec((1, ATT_TILE, d), lambda b, i, j: (b, i, 0)),
            pl.BlockSpec((1, 6, d), lambda b, i, j: (b, 0, 0)),
            pl.BlockSpec((1, d), lambda b, i, j: (0, 0)),
            w_spec(0), w_spec(1), w_spec(2),
        ],
        out_specs=pl.BlockSpec(
            (1, 1, 2, ATT_TILE, PAIR_W),
            lambda b, i, j: (j // tiles_per_group, b, j % tiles_per_group, i, 0)),
        out_shape=jax.ShapeDtypeStruct((A_GROUPS, bsz, N_PAIRS, s, PAIR_W), MXU_DTYPE),
        scratch_shapes=[pltpu.VMEM((ATT_TILE, d), MXU_DTYPE),
                        pltpu.VMEM((ATT_TILE, d), MXU_DTYPE),
                        pltpu.VMEM((d // LANES, ATT_TILE, LANES), f32),
                        pltpu.VMEM((d // LANES, ATT_TILE, LANES), f32)],
        compiler_params=pltpu.CompilerParams(
            dimension_semantics=("parallel", "parallel", "arbitrary"),
            vmem_limit_bytes=VMEM_LIMIT),
        name="qkv_proj",
    )(x, mod, gamma, w, w, w)


def _bias_kernel(tab_ref, bkt_ref, o_ref):
    col = pl.program_id(0) * A_HEADS + pl.program_id(1)
    bkt = bkt_ref[0]
    acc = jnp.full(bkt.shape, NEG_INF, f32)
    for k in range(N_BUCKETS):
        acc = jnp.where(bkt == k, tab_ref[k, col] * LOG2E, acc)
    o_ref[0, 0] = acc


def _t5_bucket(dist):
    max_exact = N_BUCKETS // 2
    n = jnp.maximum(dist, max_exact).astype(f32)
    large = max_exact + (jnp.log(n / max_exact) / math.log(MAX_DISTANCE / max_exact)
                         * (N_BUCKETS - max_exact)).astype(jnp.int32)
    large = jnp.minimum(large, N_BUCKETS - 1)
    return jnp.where(dist < max_exact, dist, large)


def _rel_bias(rel_bias):
    qi = jnp.arange(ATT_BLK)[:, None]
    ki = jnp.arange(2 * ATT_BLK)[None, :]
    steps = qi + ATT_BLK - ki
    band = (steps >= 0) & (steps <= ATT_BLK)
    bucket = jnp.stack([
        jnp.where(band, _t5_bucket(jnp.clip(steps, 0, ATT_BLK) * dl), -1)
        for _, dl in A_CONFIGS]).astype(jnp.int32)
    return pl.pallas_call(
        _bias_kernel,
        grid=(A_GROUPS, A_HEADS),
        in_specs=[
            pl.BlockSpec(memory_space=pltpu.SMEM),
            pl.BlockSpec((1, ATT_BLK, 2 * ATT_BLK), lambda g, h: (g, 0, 0)),
        ],
        out_specs=pl.BlockSpec((1, 1, ATT_BLK, 2 * ATT_BLK), lambda g, h: (g, h, 0, 0)),
        out_shape=jax.ShapeDtypeStruct((A_GROUPS, A_HEADS, ATT_BLK, 2 * ATT_BLK), f32),
        name="rel_bias",
    )(rel_bias, bucket)


def _attn_kernel(qkv_ref, bias_ref, out_ref, prev0, prev1, prev2, o1, l1, o2, l2, biasp):
    first = pl.program_id(2) == 0

    @pl.when(first)
    def _():
        prev0[...] = jnp.zeros_like(prev0)
        prev1[...] = jnp.zeros_like(prev1)
        prev2[...] = jnp.zeros_like(prev2)

    for g in range(A_GROUPS):
        for hh in range(2):
            biasp[g, hh, :, 0:ATT_BLK] = jnp.where(first, NEG_INF, bias_ref[g, hh, :, 0:ATT_BLK])
            biasp[g, hh, :, ATT_BLK:2 * ATT_BLK] = bias_ref[g, hh, :, ATT_BLK:2 * ATT_BLK]

    lane = lax.broadcasted_iota(jnp.int32, (1, LANES), 1)
    lo = lane < A_HEAD_DIM
    qmask = (jnp.where(lo, 1.0, 0.0).astype(MXU_DTYPE), jnp.where(lo, 0.0, 1.0).astype(MXU_DTYPE))

    def keys_values(g, rc, prev_ref, rp, lanes):
        if prev_ref is None:
            return qkv_ref[g, 0, 0, rp:rp + 2 * ATT_BLK, lanes]
        off = lanes.start - LANES
        return jnp.concatenate([prev_ref[rp:rp + ATT_BLK, off:off + LANES],
                                qkv_ref[g, 0, 0, rc:rc + ATT_BLK, lanes]], axis=0)

    def scores(g, rc, prev_ref, rp):
        q = qkv_ref[g, 0, 0, rc:rc + ATT_BLK, 0:LANES]
        k = keys_values(g, rc, prev_ref, rp, slice(LANES, 2 * LANES))
        bias = bias_ref if prev_ref is None else biasp
        return [_mm_nt(q * qmask[hh], k) + bias[g, hh] for hh in range(2)]

    def attend(g, rc, prev_ref, rp, s):
        v = keys_values(g, rc, prev_ref, rp, slice(2 * LANES, 3 * LANES))
        accs, dens, ms = [], [], []
        for sh in s:
            m = jnp.max(sh, axis=-1, keepdims=True)
            p = jnp.exp2(sh - m)
            dens.append(jnp.sum(p, axis=-1, keepdims=True))
            accs.append(_mm(p.astype(MXU_DTYPE), v))
            ms.append(m)
        den = jnp.where(lo, dens[0], dens[1])
        o = jnp.where(lo, accs[0], accs[1]) * (1.0 / den)
        lse = jnp.where(lo, ms[0], ms[1]) + jnp.log2(den)
        return o, jnp.broadcast_to(lse, o.shape)

    def store(o_ref, l_ref, rows):
        def post(o, lse):
            o_ref[rows, :] = o
            l_ref[rows, :] = lse
        return post

    def merge(rows):
        def post(o0, lse0):
            lse1, lse2 = l1[rows, :], l2[rows, :]
            mx = jnp.maximum(jnp.maximum(lse0, lse1), lse2)
            e0, e1, e2 = jnp.exp2(lse0 - mx), jnp.exp2(lse1 - mx), jnp.exp2(lse2 - mx)
            mix = (e0 * o0 + e1 * o1[rows, :] + e2 * o2[rows, :]) * (1.0 / (e0 + e1 + e2))
            out_ref[0, rows, :] = mix.astype(out_ref.dtype)
        return post

    units = []
    d2 = A_CONFIGS[2][1]
    for u in range(ATT_UNITS):
        units.append((2, u * ATT_BLK, prev2, u * ATT_BLK, store(o2, l2, pl.ds(u, ATT_BLK, stride=d2))))
    d1 = A_CONFIGS[1][1]
    per = ATT_UNITS // d1
    for u in range(ATT_UNITS):
        r, q = divmod(u, per)
        rc = u * ATT_BLK
        post = store(o1, l1, pl.ds(q * ATT_BLK * d1 + r, ATT_BLK, stride=d1))
        if q == 0:
            units.append((1, rc, prev1, (r * per + per - 1) * ATT_BLK, post))
        else:
            units.append((1, rc, None, rc - ATT_BLK, post))
    for u in range(ATT_UNITS):
        rc = u * ATT_BLK
        post = merge(slice(rc, rc + ATT_BLK))
        units.append((0, rc, prev0, 0, post) if u == 0 else (0, rc, None, rc - ATT_BLK, post))

    pending = [scores(*un[:4]) for un in units[:ATT_PIPELINE]]
    for idx, un in enumerate(units):
        if idx + ATT_PIPELINE < len(units):
            pending.append(scores(*units[idx + ATT_PIPELINE][:4]))
        o, lse = attend(*un[:4], pending.pop(0))
        un[4](o, lse)

    prev0[...] = qkv_ref[0, 0, 0, ATT_TILE - ATT_BLK:ATT_TILE, LANES:3 * LANES]
    prev1[...] = qkv_ref[1, 0, 0, :, LANES:3 * LANES]
    prev2[...] = qkv_ref[2, 0, 0, :, LANES:3 * LANES]


def _attention(qkv, bias):
    _, bsz, _, s, _ = qkv.shape
    return pl.pallas_call(
        _attn_kernel,
        grid=(bsz, N_PAIRS, s // ATT_TILE),
        in_specs=[
            pl.BlockSpec((A_GROUPS, 1, 1, ATT_TILE, PAIR_W), lambda b, p, i: (0, b, p, i, 0)),
            pl.BlockSpec((A_GROUPS, 2, ATT_BLK, 2 * ATT_BLK), lambda b, p, i: (0, p, 0, 0)),
        ],
        out_specs=pl.BlockSpec((1, ATT_TILE, LANES), lambda b, p, i: (b, i, p)),
        out_shape=jax.ShapeDtypeStruct((bsz, s, A_HEADS * A_HEAD_DIM), MXU_DTYPE),
        scratch_shapes=[
            pltpu.VMEM((ATT_BLK, 2 * LANES), MXU_DTYPE),
            pltpu.VMEM((ATT_TILE, 2 * LANES), MXU_DTYPE),
            pltpu.VMEM((ATT_TILE, 2 * LANES), MXU_DTYPE),
            pltpu.VMEM((ATT_TILE, LANES), f32),
            pltpu.VMEM((ATT_TILE, LANES), f32),
            pltpu.VMEM((ATT_TILE, LANES), f32),
            pltpu.VMEM((ATT_TILE, LANES), f32),
            pltpu.VMEM((A_GROUPS, 2, ATT_BLK, 2 * ATT_BLK), f32),
        ],
        compiler_params=pltpu.CompilerParams(
            dimension_semantics=("parallel", "parallel", "arbitrary"),
            vmem_limit_bytes=VMEM_LIMIT),
        name="dilated_attn",
    )(qkv, bias)


def _outproj_kernel(o_ref, w_ref, x_ref, mod_ref, out_ref):
    w = w_ref[...]
    gate = mod_ref[0, 2:3, :]
    for rc in range(OUT_TM // ROW_CHUNK):
        rows = slice(rc * ROW_CHUNK, (rc + 1) * ROW_CHUNK)
        out_ref[0, rows, :] = x_ref[0, rows, :] + gate * _mm(o_ref[0, rows, :], w)


def _out_proj(o, w, x, mod):
    bsz, s, d = x.shape
    k = o.shape[-1]
    return pl.pallas_call(
        _outproj_kernel,
        grid=(bsz, s // OUT_TM),
        in_specs=[
            pl.BlockSpec((1, OUT_TM, k), lambda b, i: (b, i, 0)),
            pl.BlockSpec((k, d), lambda b, i: (0, 0)),
            pl.BlockSpec((1, OUT_TM, d), lambda b, i: (b, i, 0)),
            pl.BlockSpec((1, 6, d), lambda b, i: (b, 0, 0)),
        ],
        out_specs=pl.BlockSpec((1, OUT_TM, d), lambda b, i: (b, i, 0)),
        out_shape=jax.ShapeDtypeStruct((bsz, s, d), f32),
        compiler_params=pltpu.CompilerParams(
            dimension_semantics=("parallel", "parallel"), vmem_limit_bytes=VMEM_LIMIT),
        name="out_proj",
    )(o, w, x, mod)


def _ffn_kernel(x_ref, halo_ref, mod_ref, gam_ref, wup_ref, cw_ref, cb_ref, wdn_ref, *rest, final):
    if final:
        gfin_ref, out_ref, h0_ref, h1_ref, u0_ref, u1_ref, act_ref, acc_ref = rest
    else:
        out_ref, h0_ref, h1_ref, u0_ref, u1_ref, act_ref, acc_ref = rest
    h_refs, u_refs = (h0_ref, h1_ref), (u0_ref, u1_ref)
    gam, scale, shift = gam_ref[...], mod_ref[0, 4:5, :], mod_ref[0, 3:4, :]
    d_ff = wdn_ref.shape[0]
    nchunk = d_ff // FFN_TF
    nsub = x_ref.shape[1] // FFN_TM

    def normalize(k):
        h_ref = h_refs[k % 2]
        if k == 0:
            halo = _modnorm(halo_ref[0], gam, scale, shift)
            halo = jnp.where(pl.program_id(1) == 0, 0.0, halo)
        else:
            halo = _modnorm(x_ref[0, k * FFN_TM - HALO:k * FFN_TM, :], gam, scale, shift)
        h_ref[0:HALO, :] = halo.astype(h_ref.dtype)
        h = _modnorm(x_ref[0, k * FFN_TM:(k + 1) * FFN_TM, :], gam, scale, shift)
        h_ref[HALO:HALO + FFN_TM, :] = h.astype(h_ref.dtype)

    def halves(c):
        return [slice(half * d_ff + c * FFN_TF, half * d_ff + (c + 1) * FFN_TF) for half in range(2)]

    def up(job):
        k, c = divmod(job, nchunk)
        for half, cols in enumerate(halves(c)):
            u_refs[job % 2][:, half * FFN_TF:(half + 1) * FFN_TF] = _mm(h_refs[k % 2][...], wup_ref[:, cols])

    def conv_act(job):
        c = job % nchunk
        u_ref = u_refs[job % 2]
        for rb in range(FFN_TM // FFN_ROWS):
            base = HALO + rb * FFN_ROWS
            ab = []
            for half, cols in enumerate(halves(c)):
                v = cb_ref[:, cols]
                for t in range(CONV_W):
                    off = base - (CONV_W - 1) + t
                    v = v + cw_ref[t:t + 1, cols] * u_ref[off:off + FFN_ROWS, half * FFN_TF:(half + 1) * FFN_TF]
                ab.append(v)
            act_ref[rb * FFN_ROWS:(rb + 1) * FFN_ROWS, c * FFN_TF:(c + 1) * FFN_TF] = (
                _silu(ab[0]) * ab[1]).astype(act_ref.dtype)

    def down(first, last):
        cols = slice(first * FFN_TF, last * FFN_TF)
        return _mm(act_ref[:, cols], wdn_ref[cols, :])

    def finish(k, start):
        ffn = down(start, nchunk)
        if start > 0:
            ffn = ffn + acc_ref[...]
        rows = slice(k * FFN_TM, (k + 1) * FFN_TM)
        y = x_ref[0, rows, :] + mod_ref[0, 5:6, :] * ffn
        if final:
            ms = jnp.mean(y * y, axis=-1, keepdims=True)
            y = y * lax.rsqrt(ms + EPS) * gfin_ref[...]
        out_ref[0, rows, :] = y

    normalize(0)
    up(0)
    start, pending = 0, None
    for job in range(nsub * nchunk):
        k, c = divmod(job, nchunk)
        if c == 0:
            start = 0
        if c == nchunk - FFN_NORM_AHEAD and k + 1 < nsub:
            normalize(k + 1)
        if job + 1 < nsub * nchunk:
            up(job + 1)
        if pending is not None:
            if pending[0] == 0:
                acc_ref[...] = down(*pending)
            else:
                acc_ref[...] += down(*pending)
            pending = None
        if c + 1 - start == FFN_DOWN_GROUP and c + 1 < nchunk:
            pending, start = (start, c + 1), c + 1
        conv_act(job)
        if c == nchunk - 1:
            finish(k, start)


def _conv_ffn(x, mod, gamma, w_up, conv_w, conv_b, w_down, gamma_final=None):
    bsz, s, d = x.shape
    d_ff = w_down.shape[0]
    final = gamma_final is not None
    rows = FFN_SUB * FFN_TM
    tiles_per_halo = rows // HALO
    resident = dict(pipeline_mode=pl.Buffered(1))
    in_specs = [
        pl.BlockSpec((1, rows, d), lambda b, i: (b, i, 0)),
        pl.BlockSpec((1, HALO, d), lambda b, i: (b, jnp.maximum(i * tiles_per_halo - 1, 0), 0)),
        pl.BlockSpec((1, 6, d), lambda b, i: (b, 0, 0)),
        pl.BlockSpec((1, d), lambda b, i: (0, 0)),
        pl.BlockSpec((d, 2 * d_ff), lambda b, i: (0, 0), **resident),
        pl.BlockSpec((CONV_W, 2 * d_ff), lambda b, i: (0, 0)),
        pl.BlockSpec((1, 2 * d_ff), lambda b, i: (0, 0)),
        pl.BlockSpec((d_ff, d), lambda b, i: (0, 0), **resident),
    ]
    args = [x, x, mod, gamma, w_up.astype(MXU_DTYPE), conv_w, conv_b[None, :], w_down.astype(MXU_DTYPE)]
    if final:
        in_specs.append(pl.BlockSpec((1, d), lambda b, i: (0, 0)))
        args.append(gamma_final)
    return pl.pallas_call(
        functools.partial(_ffn_kernel, final=final),
        grid=(bsz, s // rows),
        in_specs=in_specs,
        out_specs=pl.BlockSpec((1, rows, d), lambda b, i: (b, i, 0)),
        out_shape=jax.ShapeDtypeStruct((bsz, s, d), f32),
        scratch_shapes=[
            pltpu.VMEM((FFN_TM + HALO, d), MXU_DTYPE),
            pltpu.VMEM((FFN_TM + HALO, d), MXU_DTYPE),
            pltpu.VMEM((FFN_TM + HALO, 2 * FFN_TF), f32),
            pltpu.VMEM((FFN_TM + HALO, 2 * FFN_TF), f32),
            pltpu.VMEM((FFN_TM, d_ff), MXU_DTYPE),
            pltpu.VMEM((FFN_TM, d), f32),
        ],
        compiler_params=pltpu.CompilerParams(
            dimension_semantics=("parallel", "parallel"), vmem_limit_bytes=VMEM_LIMIT),
        name="conv_ffn_final" if final else "conv_ffn",
    )(*args)


def _gla_kernel(x_ref, mod_ref, gam_ref, win_ref, wglr_ref, wgate_ref, bgate_ref, gnorm_ref, wout_ref,
                out_ref, h_ref, proj0_ref, proj1_ref, gk0_ref, gk1_ref, og0_ref, og1_ref, state_ref):
    @pl.when(pl.program_id(1) == 0)
    def _():
        state_ref[...] = jnp.zeros_like(state_ref)

    proj_refs, gk_refs, og_refs = (proj0_ref, proj1_ref), (gk0_ref, gk1_ref), (og0_ref, og1_ref)
    nsub = x_ref.shape[1] // GLA_TM
    nchunk = GLA_TM // B_CHUNK
    n_main = win_ref.shape[1]

    ri = lax.broadcasted_iota(jnp.int32, (B_CHUNK, B_CHUNK), 0)
    ci = lax.broadcasted_iota(jnp.int32, (B_CHUNK, B_CHUNK), 1)
    causal = ci <= ri
    tri = causal.astype(MXU_DTYPE)
    gnorm = gnorm_ref[...]
    qscale = B_DK ** -0.5
    r_off = 2 * B_QK + B_V
    heads = [slice(hd * B_DK, (hd + 1) * B_DK) for hd in range(B_HEADS)]

    def project_steps(k):
        rows = slice(k * GLA_TM, (k + 1) * GLA_TM)

        def normalize():
            h = _modnorm(x_ref[0, rows, :], gam_ref[...], mod_ref[0, 1:2, :], mod_ref[0, 0:1, :])
            h_ref[...] = h.astype(h_ref.dtype)

        def piece(cols):
            def run():
                proj_refs[k % 2][:, cols] = _mm(h_ref[...], win_ref[:, cols])
            return run

        def gate():
            glr = _mm(h_ref[...], wglr_ref[...]).astype(MXU_DTYPE)
            z = _mm(glr, wgate_ref[...]) + bgate_ref[...]
            gk_refs[k % 2][...] = (jnp.minimum(z, 0.0) - jnp.log1p(jnp.exp(-jnp.abs(z)))) * (1.0 / B_TAU)

        pieces = [piece(slice(c0, c0 + GLA_PROJ_COLS)) for c0 in range(0, n_main, GLA_PROJ_COLS)]
        return [normalize] + pieces + [gate]

    def local(k, c):
        proj_ref, gk_ref = proj_refs[k % 2], gk_refs[k % 2]
        rows = slice(c * B_CHUNK, (c + 1) * B_CHUNK)
        gk = gk_ref[rows, :]
        g_hi = gk.astype(MXU_DTYPE)
        rem = gk - g_hi.astype(f32)
        g_mid = rem.astype(MXU_DTYPE)
        g_lo = (rem - g_mid.astype(f32)).astype(MXU_DTYPE)
        bcum = _mm(tri, g_hi) + _mm(tri, g_mid) + _mm(tri, g_lo)
        blast = bcum[B_CHUNK - 1:B_CHUNK, :]
        q_t = ((proj_ref[rows, 0:B_QK] * qscale) * jnp.exp(bcum)).astype(MXU_DTYPE)
        kk = proj_ref[rows, B_QK:2 * B_QK]
        k_t = (kk * jnp.exp(-bcum)).astype(MXU_DTYPE)
        k_d = (kk * jnp.exp(blast - bcum)).astype(MXU_DTYPE)
        v = [proj_ref[rows, 2 * B_QK + hd * B_DV:2 * B_QK + (hd + 1) * B_DV].astype(MXU_DTYPE)
             for hd in range(B_HEADS)]
        a = [jnp.where(causal, _mm_nt(q_t[:, ks], k_t[:, ks]), 0.0).astype(MXU_DTYPE) for ks in heads]
        kv = [_mm_tn(k_d[:, ks], v[hd]) for hd, ks in enumerate(heads)]
        o_intra = [_mm(a[hd], v[hd]) for hd in range(B_HEADS)]
        return q_t, blast, kv, o_intra

    def recur(k, c, loc, state):
        q_t, blast, kv, o_intra = loc
        rows = slice(c * B_CHUNK, (c + 1) * B_CHUNK)
        new_state = []
        for hd, ks in enumerate(heads):
            o = o_intra[hd] + _mm(q_t[:, ks], state[hd].astype(MXU_DTYPE))
            decay = jnp.exp(jnp.broadcast_to(blast[:, ks], (B_DK, B_DK)).T)
            new_state.append(jnp.concatenate([decay] * (B_DV // B_DK), axis=1) * state[hd] + kv[hd])
            o = o * lax.rsqrt(jnp.mean(o * o, axis=-1, keepdims=True) + EPS) * gnorm
            r = proj_refs[k % 2][rows, r_off + hd * B_DV:r_off + (hd + 1) * B_DV]
            og_refs[k % 2][rows, hd * B_DV:(hd + 1) * B_DV] = (o * _silu(r)).astype(og_refs[k % 2].dtype)
        return new_state

    for step in project_steps(0):
        step()
    state = [state_ref[hd] for hd in range(B_HEADS)]
    for k in range(nsub):
        ahead = project_steps(k + 1) if k + 1 < nsub else []
        loc = local(k, 0)
        for c in range(nchunk):
            nxt = local(k, c + 1) if c + 1 < nchunk else None
            state = recur(k, c, loc, state)
            loc = nxt
            take = -(-len(ahead) // (nchunk - c))
            for step in ahead[:take]:
                step()
            ahead = ahead[take:]
        rows = slice(k * GLA_TM, (k + 1) * GLA_TM)
        out_ref[0, rows, :] = x_ref[0, rows, :] + mod_ref[0, 2:3, :] * _mm(og_refs[k % 2][...], wout_ref[...])
    for hd in range(B_HEADS):
        state_ref[hd] = state[hd]


def _gla_layer(x, mod, gamma, w_in, w_gate, b_gate, g_norm, w_out):
    bsz, s, d = x.shape
    glr0 = 2 * B_QK + B_V
    w_main = jnp.concatenate([w_in[:, :glr0], w_in[:, glr0 + B_GATE_RANK:]], axis=1).astype(MXU_DTYPE)
    w_glr = jnp.pad(w_in[:, glr0:glr0 + B_GATE_RANK], ((0, 0), (0, LANES - B_GATE_RANK))).astype(MXU_DTYPE)
    w_gate_p = jnp.pad(w_gate, ((0, LANES - B_GATE_RANK), (0, 0))).astype(MXU_DTYPE)
    n_main = w_main.shape[1]
    rows = GLA_SUB * GLA_TM
    const = lambda b, i: (0, 0)
    resident = dict(pipeline_mode=pl.Buffered(1))
    return pl.pallas_call(
        _gla_kernel,
        grid=(bsz, s // rows),
        in_specs=[
            pl.BlockSpec((1, rows, d), lambda b, i: (b, i, 0)),
            pl.BlockSpec((1, 6, d), lambda b, i: (b, 0, 0)),
            pl.BlockSpec((1, d), const),
            pl.BlockSpec((d, n_main), const, **resident),
            pl.BlockSpec((d, LANES), const),
            pl.BlockSpec((LANES, B_QK), const),
            pl.BlockSpec((1, B_QK), const),
            pl.BlockSpec((1, B_DV), const),
            pl.BlockSpec((B_V, d), const, **resident),
        ],
        out_specs=pl.BlockSpec((1, rows, d), lambda b, i: (b, i, 0)),
        out_shape=jax.ShapeDtypeStruct((bsz, s, d), f32),
        scratch_shapes=[
            pltpu.VMEM((GLA_TM, d), MXU_DTYPE),
            pltpu.VMEM((GLA_TM, n_main), f32),
            pltpu.VMEM((GLA_TM, n_main), f32),
            pltpu.VMEM((GLA_TM, B_QK), f32),
            pltpu.VMEM((GLA_TM, B_QK), f32),
            pltpu.VMEM((GLA_TM, B_V), MXU_DTYPE),
            pltpu.VMEM((GLA_TM, B_V), MXU_DTYPE),
            pltpu.VMEM((B_HEADS, B_DK, B_DV), f32),
        ],
        compiler_params=pltpu.CompilerParams(
            dimension_semantics=("parallel", "arbitrary"), vmem_limit_bytes=VMEM_LIMIT),
        name="gla_layer",
    )(x, mod, gamma, w_main, w_glr, w_gate_p, b_gate[None, :], g_norm[None, :], w_out.astype(MXU_DTYPE))


def _qkv_weight(w_in):
    width = A_HEADS * A_HEAD_DIM
    col_scale = jnp.tile(jnp.repeat(jnp.array([A_HEAD_DIM ** -0.5 * LOG2E, 1.0, 1.0], f32), width), A_GROUPS)
    return (w_in * col_scale[None, :]).astype(MXU_DTYPE)


def kernel(x, c, w_in_a, w_out_a, rel_bias, w_in_b, w_gate_b, b_gate_b, gnorm_b, w_out_b, norm_mix, norm_ffn, w_ada, b_ada, w_up, conv_w, conv_b, w_down, norm_final):
    depth = w_ada.shape[0]
    mod = _adaln(c, w_ada, b_ada)
    bias = _rel_bias(rel_bias)
    for i in range(depth):
        gam_mix = norm_mix[i][None, :]
        j = i // 2
        if i % 2 == 0:
            qkv = _qkv_proj(x, mod[i], gam_mix, _qkv_weight(w_in_a[j]))
            o = _attention(qkv, bias)
            x = _out_proj(o, w_out_a[j].astype(MXU_DTYPE), x, mod[i])
        else:
            x = _gla_layer(x, mod[i], gam_mix, w_in_b[j], w_gate_b[j], b_gate_b[j], gnorm_b[j], w_out_b[j])
        last = i == depth - 1
        x = _conv_ffn(x, mod[i], norm_ffn[i][None, :], w_up[i], conv_w[i], conv_b[i], w_down[i],
                      gamma_final=norm_final[None, :] if last else None)
    return x
```

```python
import functools
import math

import jax
import jax.numpy as jnp
from jax import lax
from jax.experimental import pallas as pl
from jax.experimental.pallas import tpu as pltpu

A_CONFIGS = ((128, 1), (512, 4), (2048, 16))
A_GROUPS = len(A_CONFIGS)
A_HEADS = 16
A_HEAD_DIM = 64
N_BUCKETS = 32
MAX_DISTANCE = 2048
B_HEADS = 4
B_DK = 128
B_DV = 256
B_QK = B_HEADS * B_DK
B_V = B_HEADS * B_DV
B_GATE_RANK = 16
B_TAU = 16.0
B_CHUNK = 64
CONV_W = 3
EPS = 1e-6
NEG_INF = -1e30
LOG2E = math.log2(math.e)

LANES = 128
MXU_DTYPE = jnp.bfloat16
ATT_BLK = 128
ATT_TILE = 2048
ATT_UNITS = ATT_TILE // ATT_BLK
ATT_PIPELINE = 3
PAIR_W = 3 * LANES
N_PAIRS = A_HEADS // 2
ROW_CHUNK = 512
PROBE_ROWS = 16
FFN_TM = 512
FFN_SUB = 1
FFN_NORM_AHEAD = 3
FFN_TF = 256
FFN_ROWS = 256
FFN_DOWN_GROUP = 2
HALO = 8
O_HALO = 16
GLA_TM = 512
GLA_SUB = 2
GLA_PROJ_COLS = 256
VMEM_LIMIT = 56 * 1024 * 1024

f32 = jnp.float32


def _mm(a, b):
    return jnp.dot(a, b, preferred_element_type=f32)


def _mm_nt(a, b):
    return lax.dot_general(a, b, (((1,), (1,)), ((), ())), preferred_element_type=f32)


def _mm_tn(a, b):
    return lax.dot_general(a, b, (((0,), (0,)), ((), ())), preferred_element_type=f32)


def _modnorm(x, gamma, scale, shift):
    ms = jnp.mean(x * x, axis=-1, keepdims=True)
    y = x * lax.rsqrt(ms + EPS) * gamma
    return y * (1.0 + scale) + shift


def _silu(x):
    return x * (1.0 / (1.0 + jnp.exp(-x)))


def _adaln_kernel(c_ref, w_ref, b_ref, o_ref):
    s = _silu(c_ref[...]).astype(MXU_DTYPE)
    o_ref[0] = _mm(s, w_ref[0].astype(MXU_DTYPE)) + b_ref[0]


def _adaln(c, w_ada, b_ada):
    depth, d, n = w_ada.shape
    bsz = c.shape[0]
    rows = 8 * pl.cdiv(bsz, 8)
    c_pad = jnp.pad(c, ((0, rows - bsz), (0, 0)))
    out = pl.pallas_call(
        _adaln_kernel,
        grid=(depth, n // d),
        in_specs=[
            pl.BlockSpec((rows, d), lambda l, j: (0, 0)),
            pl.BlockSpec((1, d, d), lambda l, j: (l, 0, j)),
            pl.BlockSpec((1, 1, d), lambda l, j: (l, 0, j)),
        ],
        out_specs=pl.BlockSpec((1, rows, d), lambda l, j: (l, 0, j)),
        out_shape=jax.ShapeDtypeStruct((depth, rows, n), f32),
        compiler_params=pltpu.CompilerParams(
            dimension_semantics=("parallel", "parallel"), vmem_limit_bytes=VMEM_LIMIT),
        name="adaln",
    )(c_pad, w_ada, b_ada.reshape(depth, 1, n))
    return out[:, :bsz].reshape(depth, bsz, n // d, d)


def _qkv_kernel(x_ref, mod_ref, gam_ref, wq_ref, wk_ref, wv_ref, o_ref, ha_ref, hb_ref, sa_ref, sb_ref, *,
                dilations, tiles_per_group):
    j = pl.program_id(2)
    g = j // tiles_per_group
    t = j % tiles_per_group
    nlb = sa_ref.shape[0]
    chunks_per_step = ATT_UNITS // tiles_per_group
    h_refs, stage_refs = [ha_ref, hb_ref], [sa_ref, sb_ref]

    @pl.when(j == 0)
    def _():
        for c in range(ATT_UNITS):
            rows = slice(c * ATT_BLK, (c + 1) * ATT_BLK)
            hn = _modnorm(x_ref[0, rows, :], gam_ref[...], mod_ref[0, 1:2, :], mod_ref[0, 0:1, :])
            ha_ref[rows, :] = hn.astype(ha_ref.dtype)
            for lb in range(nlb):
                sa_ref[lb, rows, :] = hn[:, lb * LANES:(lb + 1) * LANES]

    def deinterleave_chunk(gi, cc):
        d, dp = dilations[gi], dilations[gi - 1]
        rel = d // dp
        per = ATT_UNITS // d
        c = t * chunks_per_step + cc
        r, q = c // per, c % per
        start = (r % dp) * (ATT_TILE // dp) + q * (ATT_BLK * rel) + r // dp
        rows = pl.ds(start, ATT_BLK, stride=rel)
        r0 = pl.multiple_of(c * ATT_BLK, ATT_BLK)
        parts = [stage_refs[(gi - 1) % 2][lb, rows, :] for lb in range(nlb)]
        if gi + 1 < len(dilations):
            for lb in range(nlb):
                stage_refs[gi % 2][lb, pl.ds(r0, ATT_BLK), :] = parts[lb]
        dst_ref = h_refs[gi % 2]
        dst_ref[pl.ds(r0, ATT_BLK), :] = jnp.concatenate(parts, axis=1).astype(dst_ref.dtype)
        probe = dst_ref[pl.ds(r0, PROBE_ROWS), 0:LANES]
        return (probe != probe) & (probe == probe)

    def project_chunk(src_ref, rc, never):
        rows = slice(rc * ROW_CHUNK, (rc + 1) * ROW_CHUNK)
        lhs = src_ref[rows, :]
        for k, w_ref in enumerate((wq_ref, wk_ref, wv_ref)):
            res = _mm(lhs, w_ref[...]).astype(o_ref.dtype)
            lanes = slice(k * LANES, (k + 1) * LANES)
            o_ref[0, 0, 1, rows, lanes] = res[:, LANES:]
            if never is not None and k == 0:
                top = rc * ROW_CHUNK + PROBE_ROWS
                o_ref[0, 0, 0, rc * ROW_CHUNK:top, lanes] = jnp.where(
                    never, jnp.zeros_like(res[:PROBE_ROWS, :LANES]), res[:PROBE_ROWS, :LANES])
                o_ref[0, 0, 0, top:(rc + 1) * ROW_CHUNK, lanes] = res[PROBE_ROWS:, :LANES]
            else:
                o_ref[0, 0, 0, rows, lanes] = res[:, :LANES]

    assert dilations[0] == 1 and chunks_per_step == ATT_TILE // ROW_CHUNK
    assert all(d % dp == 0 for dp, d in zip(dilations, dilations[1:]))
    for gi in range(len(dilations)):
        @pl.when(g == gi)
        def _(gi=gi):
            shares = [[0, 1]] + [[cc] for cc in range(2, chunks_per_step)] + [[]]
            for rc in range(chunks_per_step):
                never = None
                if gi + 1 < len(dilations):
                    for cc in shares[rc]:
                        probe = deinterleave_chunk(gi + 1, cc)
                        never = probe if never is None else never | probe
                project_chunk(h_refs[gi % 2], rc, never)


def _qkv_proj(x, mod, gamma, w):
    bsz, s, d = x.shape
    dilations = tuple(dl for _, dl in A_CONFIGS)
    tn = 2 * LANES
    tiles_per_group = A_HEADS * A_HEAD_DIM // tn
    kern = functools.partial(_qkv_kernel, dilations=dilations, tiles_per_group=tiles_per_group)

    def w_spec(k):
        return pl.BlockSpec(
            (d, tn), lambda b, i, j: (0, (j // tiles_per_group * 3 + k) * tiles_per_group + j % tiles_per_group))

    return pl.pallas_call(
        kern,
        grid=(bsz, s // ATT_TILE, A_GROUPS * tiles_per_group),
        in_specs=[
            pl.BlockSpec((1, ATT_TILE, d), lambda b, i, j: (b, i, 0)),
            pl.BlockSpec((1, 6, d), lambda b, i, j: (b, 0, 0)),
            pl.BlockSpec((1, d), lambda b, i, j: (0, 0)),
            w_spec(0), w_spec(1), w_spec(2),
        ],
        out_specs=pl.BlockSpec(
            (1, 1, 2, ATT_TILE, PAIR_W),
            lambda b, i, j: (j // tiles_per_group, b, j % tiles_per_group, i, 0)),
        out_shape=jax.ShapeDtypeStruct((A_GROUPS, bsz, N_PAIRS, s, PAIR_W), MXU_DTYPE),
        scratch_shapes=[pltpu.VMEM((ATT_TILE, d), MXU_DTYPE),
                        pltpu.VMEM((ATT_TILE, d), MXU_DTYPE),
                        pltpu.VMEM((d // LANES, ATT_TILE, LANES), f32),
                        pltpu.VMEM((d // LANES, ATT_TILE, LANES), f32)],
        compiler_params=pltpu.CompilerParams(
            dimension_semantics=("parallel", "parallel", "arbitrary"),
            vmem_limit_bytes=VMEM_LIMIT),
        name="qkv_proj",
    )(x, mod, gamma, w, w, w)


def _bias_kernel(tab_ref, bkt_ref, o_ref):
    bkt = bkt_ref[0]
    hits = [bkt == k for k in range(N_BUCKETS)]
    for h in range(A_HEADS):
        col = pl.program_id(0) * A_HEADS + h
        acc = jnp.full(bkt.shape, NEG_INF, f32)
        for k in range(N_BUCKETS):
            acc = jnp.where(hits[k], tab_ref[k, col] * LOG2E, acc)
        o_ref[0, h] = acc


def _t5_bucket(dist):
    max_exact = N_BUCKETS // 2
    n = jnp.maximum(dist, max_exact).astype(f32)
    large = max_exact + (jnp.log(n / max_exact) / math.log(MAX_DISTANCE / max_exact)
                         * (N_BUCKETS - max_exact)).astype(jnp.int32)
    large = jnp.minimum(large, N_BUCKETS - 1)
    return jnp.where(dist < max_exact, dist, large)


def _rel_bias(rel_bias):
    qi = jnp.arange(ATT_BLK)[:, None]
    ki = jnp.arange(2 * ATT_BLK)[None, :]
    steps = qi + ATT_BLK - ki
    band = (steps >= 0) & (steps <= ATT_BLK)
    bucket = jnp.stack([
        jnp.where(band, _t5_bucket(jnp.clip(steps, 0, ATT_BLK) * dl), -1)
        for _, dl in A_CONFIGS]).astype(jnp.int32)
    return pl.pallas_call(
        _bias_kernel,
        grid=(A_GROUPS,),
        in_specs=[
            pl.BlockSpec(memory_space=pltpu.SMEM),
            pl.BlockSpec((1, ATT_BLK, 2 * ATT_BLK), lambda g: (g, 0, 0)),
        ],
        out_specs=pl.BlockSpec((1, A_HEADS, ATT_BLK, 2 * ATT_BLK), lambda g: (g, 0, 0, 0)),
        out_shape=jax.ShapeDtypeStruct((A_GROUPS, A_HEADS, ATT_BLK, 2 * ATT_BLK), f32),
        name="rel_bias",
    )(rel_bias, bucket)


def _attn_kernel(qkv_ref, bias_ref, out_ref, prev0, prev1, prev2, o1, l1, o2, l2, biasp):
    first = pl.program_id(2) == 0

    @pl.when(first)
    def _():
        prev0[...] = jnp.zeros_like(prev0)
        prev1[...] = jnp.zeros_like(prev1)
        prev2[...] = jnp.zeros_like(prev2)

    for g in range(A_GROUPS):
        for hh in range(2):
            biasp[g, hh, :, 0:ATT_BLK] = jnp.where(first, NEG_INF, bias_ref[g, hh, :, 0:ATT_BLK])
            biasp[g, hh, :, ATT_BLK:2 * ATT_BLK] = bias_ref[g, hh, :, ATT_BLK:2 * ATT_BLK]

    lane = lax.broadcasted_iota(jnp.int32, (1, LANES), 1)
    lo = lane < A_HEAD_DIM
    qmask = (jnp.where(lo, 1.0, 0.0).astype(MXU_DTYPE), jnp.where(lo, 0.0, 1.0).astype(MXU_DTYPE))

    def keys_values(g, rc, prev_ref, rp, lanes):
        if prev_ref is None:
            return qkv_ref[g, 0, 0, rp:rp + 2 * ATT_BLK, lanes]
        off = lanes.start - LANES
        return jnp.concatenate([prev_ref[rp:rp + ATT_BLK, off:off + LANES],
                                qkv_ref[g, 0, 0, rc:rc + ATT_BLK, lanes]], axis=0)

    def scores(g, rc, prev_ref, rp):
        q = qkv_ref[g, 0, 0, rc:rc + ATT_BLK, 0:LANES]
        k = keys_values(g, rc, prev_ref, rp, slice(LANES, 2 * LANES))
        bias = bias_ref if prev_ref is None else biasp
        return [_mm_nt(q * qmask[hh], k) + bias[g, hh] for hh in range(2)]

    def attend(g, rc, prev_ref, rp, s):
        v = keys_values(g, rc, prev_ref, rp, slice(2 * LANES, 3 * LANES))
        accs, dens, ms = [], [], []
        for sh in s:
            m = jnp.max(sh, axis=-1, keepdims=True)
            p = jnp.exp2(sh - m)
            dens.append(jnp.sum(p, axis=-1, keepdims=True))
            accs.append(_mm(p.astype(MXU_DTYPE), v))
            ms.append(m)
        den = jnp.where(lo, dens[0], dens[1])
        o = jnp.where(lo, accs[0], accs[1]) * (1.0 / den)
        lse = jnp.where(lo, ms[0], ms[1]) + jnp.log2(den)
        return o, jnp.broadcast_to(lse, o.shape)

    def store(o_ref, l_ref, rows):
        def post(o, lse):
            o_ref[rows, :] = o
            l_ref[rows, :] = lse
        return post

    def merge(rows):
        def post(o0, lse0):
            lse1, lse2 = l1[rows, :], l2[rows, :]
            mx = jnp.maximum(jnp.maximum(lse0, lse1), lse2)
            e0, e1, e2 = jnp.exp2(lse0 - mx), jnp.exp2(lse1 - mx), jnp.exp2(lse2 - mx)
            mix = (e0 * o0 + e1 * o1[rows, :] + e2 * o2[rows, :]) * (1.0 / (e0 + e1 + e2))
            out_ref[0, rows, :] = mix.astype(out_ref.dtype)
        return post

    units = []
    d2 = A_CONFIGS[2][1]
    for u in range(ATT_UNITS):
        units.append((2, u * ATT_BLK, prev2, u * ATT_BLK, store(o2, l2, pl.ds(u, ATT_BLK, stride=d2))))
    d1 = A_CONFIGS[1][1]
    per = ATT_UNITS // d1
    for u in range(ATT_UNITS):
        r, q = divmod(u, per)
        rc = u * ATT_BLK
        post = store(o1, l1, pl.ds(q * ATT_BLK * d1 + r, ATT_BLK, stride=d1))
        if q == 0:
            units.append((1, rc, prev1, (r * per + per - 1) * ATT_BLK, post))
        else:
            units.append((1, rc, None, rc - ATT_BLK, post))
    for u in range(ATT_UNITS):
        rc = u * ATT_BLK
        post = merge(slice(rc, rc + ATT_BLK))
        units.append((0, rc, prev0, 0, post) if u == 0 else (0, rc, None, rc - ATT_BLK, post))

    pending = [scores(*un[:4]) for un in units[:ATT_PIPELINE]]
    for idx, un in enumerate(units):
        if idx + ATT_PIPELINE < len(units):
            pending.append(scores(*units[idx + ATT_PIPELINE][:4]))
        o, lse = attend(*un[:4], pending.pop(0))
        un[4](o, lse)

    prev0[...] = qkv_ref[0, 0, 0, ATT_TILE - ATT_BLK:ATT_TILE, LANES:3 * LANES]
    prev1[...] = qkv_ref[1, 0, 0, :, LANES:3 * LANES]
    prev2[...] = qkv_ref[2, 0, 0, :, LANES:3 * LANES]


def _attention(qkv, bias):
    _, bsz, _, s, _ = qkv.shape
    return pl.pallas_call(
        _attn_kernel,
        grid=(bsz, N_PAIRS, s // ATT_TILE),
        in_specs=[
            pl.BlockSpec((A_GROUPS, 1, 1, ATT_TILE, PAIR_W), lambda b, p, i: (0, b, p, i, 0)),
            pl.BlockSpec((A_GROUPS, 2, ATT_BLK, 2 * ATT_BLK), lambda b, p, i: (0, p, 0, 0)),
        ],
        out_specs=pl.BlockSpec((1, ATT_TILE, LANES), lambda b, p, i: (b, i, p)),
        out_shape=jax.ShapeDtypeStruct((bsz, s, A_HEADS * A_HEAD_DIM), MXU_DTYPE),
        scratch_shapes=[
            pltpu.VMEM((ATT_BLK, 2 * LANES), MXU_DTYPE),
            pltpu.VMEM((ATT_TILE, 2 * LANES), MXU_DTYPE),
            pltpu.VMEM((ATT_TILE, 2 * LANES), MXU_DTYPE),
            pltpu.VMEM((ATT_TILE, LANES), f32),
            pltpu.VMEM((ATT_TILE, LANES), f32),
            pltpu.VMEM((ATT_TILE, LANES), f32),
            pltpu.VMEM((ATT_TILE, LANES), f32),
            pltpu.VMEM((A_GROUPS, 2, ATT_BLK, 2 * ATT_BLK), f32),
        ],
        compiler_params=pltpu.CompilerParams(
            dimension_semantics=("parallel", "parallel", "arbitrary"),
            vmem_limit_bytes=VMEM_LIMIT),
        name="dilated_attn",
    )(qkv, bias)


def _ffn_kernel(*refs, final, mixer):
    refs = list(refs)
    x_ref, halo_ref = refs[:2]
    del refs[:2]
    if mixer:
        o_ref, ohalo_ref, wout_ref = refs[:3]
        del refs[:3]
    mod_ref, gam_ref, wup_ref, cw_ref, cb_ref, wdn_ref = refs[:6]
    del refs[:6]
    if final:
        gfin_ref = refs.pop(0)
    out_ref, h0_ref, h1_ref, u0_ref, u1_ref, act_ref, acc_ref = refs[:7]
    if mixer:
        x1_ref = refs[7]
    h_refs, u_refs = (h0_ref, h1_ref), (u0_ref, u1_ref)
    gam, scale, shift = gam_ref[...], mod_ref[0, 4:5, :], mod_ref[0, 3:4, :]
    d_ff = wdn_ref.shape[0]
    nchunk = d_ff // FFN_TF
    nsub = x_ref.shape[1] // FFN_TM

    def stream(k):
        rows = slice(k * FFN_TM, (k + 1) * FFN_TM)
        if not mixer:
            return x_ref[0, rows, :]
        x1 = x_ref[0, rows, :] + mod_ref[0, 2:3, :] * _mm(o_ref[0, rows, :], wout_ref[...])
        x1_ref[rows, :] = x1
        return x1

    def stream_halo(k):
        if k > 0:
            src = x1_ref if mixer else x_ref.at[0]
            return src[k * FFN_TM - HALO:k * FFN_TM, :]
        if not mixer:
            return halo_ref[0]
        y = _mm(ohalo_ref[0], wout_ref[...])[O_HALO - HALO:, :]
        return halo_ref[0] + mod_ref[0, 2:3, :] * y

    def normalize(k):
        h_ref = h_refs[k % 2]
        halo = _modnorm(stream_halo(k), gam, scale, shift)
        if k == 0:
            halo = jnp.where(pl.program_id(1) == 0, 0.0, halo)
        h_ref[0:HALO, :] = halo.astype(h_ref.dtype)
        h_ref[HALO:HALO + FFN_TM, :] = _modnorm(stream(k), gam, scale, shift).astype(h_ref.dtype)

    def halves(c):
        return [slice(half * d_ff + c * FFN_TF, half * d_ff + (c + 1) * FFN_TF) for half in range(2)]

    def up(job):
        k, c = divmod(job, nchunk)
        for half, cols in enumerate(halves(c)):
            u_refs[job % 2][:, half * FFN_TF:(half + 1) * FFN_TF] = _mm(h_refs[k % 2][...], wup_ref[:, cols])

    def conv_act(job):
        c = job % nchunk
        u_ref = u_refs[job % 2]
        for rb in range(FFN_TM // FFN_ROWS):
            base = HALO + rb * FFN_ROWS
            ab = []
            for half, cols in enumerate(halves(c)):
                v = cb_ref[:, cols]
                for t in range(CONV_W):
                    off = base - (CONV_W - 1) + t
                    v = v + cw_ref[t:t + 1, cols] * u_ref[off:off + FFN_ROWS, half * FFN_TF:(half + 1) * FFN_TF]
                ab.append(v)
            act_ref[rb * FFN_ROWS:(rb + 1) * FFN_ROWS, c * FFN_TF:(c + 1) * FFN_TF] = (
                _silu(ab[0]) * ab[1]).astype(act_ref.dtype)

    def down(first, last):
        cols = slice(first * FFN_TF, last * FFN_TF)
        return _mm(act_ref[:, cols], wdn_ref[cols, :])

    def finish(k, start):
        ffn = down(start, nchunk)
        if start > 0:
            ffn = ffn + acc_ref[...]
        rows = slice(k * FFN_TM, (k + 1) * FFN_TM)
        y = (x1_ref[rows, :] if mixer else x_ref[0, rows, :]) + mod_ref[0, 5:6, :] * ffn
        if final:
            ms = jnp.mean(y * y, axis=-1, keepdims=True)
            y = y * lax.rsqrt(ms + EPS) * gfin_ref[...]
        out_ref[0, rows, :] = y

    normalize(0)
    up(0)
    start, pending = 0, None
    for job in range(nsub * nchunk):
        k, c = divmod(job, nchunk)
        if c == 0:
            start = 0
        if c == nchunk - FFN_NORM_AHEAD and k + 1 < nsub:
            normalize(k + 1)
        if job + 1 < nsub * nchunk:
            up(job + 1)
        if pending is not None:
            if pending[0] == 0:
                acc_ref[...] = down(*pending)
            else:
                acc_ref[...] += down(*pending)
            pending = None
        if c + 1 - start == FFN_DOWN_GROUP and c + 1 < nchunk:
            pending, start = (start, c + 1), c + 1
        conv_act(job)
        if c == nchunk - 1:
            finish(k, start)


def _conv_ffn(x, mod, gamma, w_up, conv_w, conv_b, w_down, gamma_final=None, mixer=None):
    bsz, s, d = x.shape
    d_ff = w_down.shape[0]
    final = gamma_final is not None
    rows = FFN_SUB * FFN_TM
    tiles_per_halo = rows // HALO
    resident = dict(pipeline_mode=pl.Buffered(1))
    in_specs = [
        pl.BlockSpec((1, rows, d), lambda b, i: (b, i, 0)),
        pl.BlockSpec((1, HALO, d), lambda b, i: (b, jnp.maximum(i * tiles_per_halo - 1, 0), 0)),
    ]
    args = [x, x]
    if mixer is not None:
        o, w_out = mixer
        k = o.shape[-1]
        in_specs += [
            pl.BlockSpec((1, rows, k), lambda b, i: (b, i, 0)),
            pl.BlockSpec((1, O_HALO, k), lambda b, i: (b, jnp.maximum(i * (rows // O_HALO) - 1, 0), 0)),
            pl.BlockSpec((k, d), lambda b, i: (0, 0), **resident),
        ]
        args += [o, o, w_out.astype(MXU_DTYPE)]
    in_specs += [
        pl.BlockSpec((1, 6, d), lambda b, i: (b, 0, 0)),
        pl.BlockSpec((1, d), lambda b, i: (0, 0)),
        pl.BlockSpec((d, 2 * d_ff), lambda b, i: (0, 0), **resident),
        pl.BlockSpec((CONV_W, 2 * d_ff), lambda b, i: (0, 0)),
        pl.BlockSpec((1, 2 * d_ff), lambda b, i: (0, 0)),
        pl.BlockSpec((d_ff, d), lambda b, i: (0, 0), **resident),
    ]
    args += [mod, gamma, w_up.astype(MXU_DTYPE), conv_w, conv_b[None, :], w_down.astype(MXU_DTYPE)]
    if final:
        in_specs.append(pl.BlockSpec((1, d), lambda b, i: (0, 0)))
        args.append(gamma_final)
    return pl.pallas_call(
        functools.partial(_ffn_kernel, final=final, mixer=mixer is not None),
        grid=(bsz, s // rows),
        in_specs=in_specs,
        out_specs=pl.BlockSpec((1, rows, d), lambda b, i: (b, i, 0)),
        out_shape=jax.ShapeDtypeStruct((bsz, s, d), f32),
        scratch_shapes=[
            pltpu.VMEM((FFN_TM + HALO, d), MXU_DTYPE),
            pltpu.VMEM((FFN_TM + HALO, d), MXU_DTYPE),
            pltpu.VMEM((FFN_TM + HALO, 2 * FFN_TF), f32),
            pltpu.VMEM((FFN_TM + HALO, 2 * FFN_TF), f32),
            pltpu.VMEM((FFN_TM, d_ff), MXU_DTYPE),
            pltpu.VMEM((FFN_TM, d), f32),
        ] + ([pltpu.VMEM((rows, d), f32)] if mixer is not None else []),
        compiler_params=pltpu.CompilerParams(
            dimension_semantics=("parallel", "parallel"), vmem_limit_bytes=VMEM_LIMIT),
        name="conv_ffn_final" if final else "conv_ffn",
    )(*args)


def _gla_kernel(x_ref, mod_ref, gam_ref, win_ref, wglr_ref, wgate_ref, bgate_ref, gnorm_ref, wout_ref,
                out_ref, h_ref, proj0_ref, proj1_ref, gk0_ref, gk1_ref, og0_ref, og1_ref, state_ref):
    @pl.when(pl.program_id(1) == 0)
    def _():
        state_ref[...] = jnp.zeros_like(state_ref)

    proj_refs, gk_refs, og_refs = (proj0_ref, proj1_ref), (gk0_ref, gk1_ref), (og0_ref, og1_ref)
    nsub = x_ref.shape[1] // GLA_TM
    nchunk = GLA_TM // B_CHUNK
    n_main = win_ref.shape[1]

    ri = lax.broadcasted_iota(jnp.int32, (B_CHUNK, B_CHUNK), 0)
    ci = lax.broadcasted_iota(jnp.int32, (B_CHUNK, B_CHUNK), 1)
    causal = ci <= ri
    tri = causal.astype(MXU_DTYPE)
    gnorm = gnorm_ref[...]
    qscale = B_DK ** -0.5
    r_off = 2 * B_QK + B_V
    heads = [slice(hd * B_DK, (hd + 1) * B_DK) for hd in range(B_HEADS)]

    def project_steps(k):
        rows = slice(k * GLA_TM, (k + 1) * GLA_TM)

        def normalize():
            h = _modnorm(x_ref[0, rows, :], gam_ref[...], mod_ref[0, 1:2, :], mod_ref[0, 0:1, :])
            h_ref[...] = h.astype(h_ref.dtype)

        def piece(cols):
            def run():
                proj_refs[k % 2][:, cols] = _mm(h_ref[...], win_ref[:, cols])
            return run

        def gate():
            glr = _mm(h_ref[...], wglr_ref[...]).astype(MXU_DTYPE)
            z = _mm(glr, wgate_ref[...]) + bgate_ref[...]
            gk_refs[k % 2][...] = (jnp.minimum(z, 0.0) - jnp.log1p(jnp.exp(-jnp.abs(z)))) * (1.0 / B_TAU)

        pieces = [piece(slice(c0, c0 + GLA_PROJ_COLS)) for c0 in range(0, n_main, GLA_PROJ_COLS)]
        return [normalize] + pieces + [gate]

    def local(k, c):
        proj_ref, gk_ref = proj_refs[k % 2], gk_refs[k % 2]
        rows = slice(c * B_CHUNK, (c + 1) * B_CHUNK)
        gk = gk_ref[rows, :]
        g_hi = gk.astype(MXU_DTYPE)
        rem = gk - g_hi.astype(f32)
        g_mid = rem.astype(MXU_DTYPE)
        g_lo = (rem - g_mid.astype(f32)).astype(MXU_DTYPE)
        bcum = _mm(tri, g_hi) + _mm(tri, g_mid) + _mm(tri, g_lo)
        blast = bcum[B_CHUNK - 1:B_CHUNK, :]
        q_t = ((proj_ref[rows, 0:B_QK] * qscale) * jnp.exp(bcum)).astype(MXU_DTYPE)
        kk = proj_ref[rows, B_QK:2 * B_QK]
        k_t = (kk * jnp.exp(-bcum)).astype(MXU_DTYPE)
        k_d = (kk * jnp.exp(blast - bcum)).astype(MXU_DTYPE)
        v = [proj_ref[rows, 2 * B_QK + hd * B_DV:2 * B_QK + (hd + 1) * B_DV].astype(MXU_DTYPE)
             for hd in range(B_HEADS)]
        a = [jnp.where(causal, _mm_nt(q_t[:, ks], k_t[:, ks]), 0.0).astype(MXU_DTYPE) for ks in heads]
        kv = [_mm_tn(k_d[:, ks], v[hd]) for hd, ks in enumerate(heads)]
        o_intra = [_mm(a[hd], v[hd]) for hd in range(B_HEADS)]
        return q_t, blast, kv, o_intra

    def recur(k, c, loc, state):
        q_t, blast, kv, o_intra = loc
        rows = slice(c * B_CHUNK, (c + 1) * B_CHUNK)
        new_state = []
        for hd, ks in enumerate(heads):
            o = o_intra[hd] + _mm(q_t[:, ks], state[hd].astype(MXU_DTYPE))
            decay = jnp.exp(jnp.broadcast_to(blast[:, ks], (B_DK, B_DK)).T)
            new_state.append(jnp.concatenate([decay] * (B_DV // B_DK), axis=1) * state[hd] + kv[hd])
            o = o * lax.rsqrt(jnp.mean(o * o, axis=-1, keepdims=True) + EPS) * gnorm
            r = proj_refs[k % 2][rows, r_off + hd * B_DV:r_off + (hd + 1) * B_DV]
            og_refs[k % 2][rows, hd * B_DV:(hd + 1) * B_DV] = (o * _silu(r)).astype(og_refs[k % 2].dtype)
        return new_state

    for step in project_steps(0):
        step()
    state = [state_ref[hd] for hd in range(B_HEADS)]
    for k in range(nsub):
        ahead = project_steps(k + 1) if k + 1 < nsub else []
        loc = local(k, 0)
        for c in range(nchunk):
            nxt = local(k, c + 1) if c + 1 < nchunk else None
            state = recur(k, c, loc, state)
            loc = nxt
            take = -(-len(ahead) // (nchunk - c))
            for step in ahead[:take]:
                step()
            ahead = ahead[take:]
        rows = slice(k * GLA_TM, (k + 1) * GLA_TM)
        out_ref[0, rows, :] = x_ref[0, rows, :] + mod_ref[0, 2:3, :] * _mm(og_refs[k % 2][...], wout_ref[...])
    for hd in range(B_HEADS):
        state_ref[hd] = state[hd]


def _gla_layer(x, mod, gamma, w_in, w_gate, b_gate, g_norm, w_out):
    bsz, s, d = x.shape
    glr0 = 2 * B_QK + B_V
    w_main = jnp.concatenate([w_in[:, :glr0], w_in[:, glr0 + B_GATE_RANK:]], axis=1).astype(MXU_DTYPE)
    w_glr = jnp.pad(w_in[:, glr0:glr0 + B_GATE_RANK], ((0, 0), (0, LANES - B_GATE_RANK))).astype(MXU_DTYPE)
    w_gate_p = jnp.pad(w_gate, ((0, LANES - B_GATE_RANK), (0, 0))).astype(MXU_DTYPE)
    n_main = w_main.shape[1]
    rows = GLA_SUB * GLA_TM
    const = lambda b, i: (0, 0)
    resident = dict(pipeline_mode=pl.Buffered(1))
    return pl.pallas_call(
        _gla_kernel,
        grid=(bsz, s // rows),
        in_specs=[
            pl.BlockSpec((1, rows, d), lambda b, i: (b, i, 0)),
            pl.BlockSpec((1, 6, d), lambda b, i: (b, 0, 0)),
            pl.BlockSpec((1, d), const),
            pl.BlockSpec((d, n_main), const, **resident),
            pl.BlockSpec((d, LANES), const),
            pl.BlockSpec((LANES, B_QK), const),
            pl.BlockSpec((1, B_QK), const),
            pl.BlockSpec((1, B_DV), const),
            pl.BlockSpec((B_V, d), const, **resident),
        ],
        out_specs=pl.BlockSpec((1, rows, d), lambda b, i: (b, i, 0)),
        out_shape=jax.ShapeDtypeStruct((bsz, s, d), f32),
        scratch_shapes=[
            pltpu.VMEM((GLA_TM, d), MXU_DTYPE),
            pltpu.VMEM((GLA_TM, n_main), f32),
            pltpu.VMEM((GLA_TM, n_main), f32),
            pltpu.VMEM((GLA_TM, B_QK), f32),
            pltpu.VMEM((GLA_TM, B_QK), f32),
            pltpu.VMEM((GLA_TM, B_V), MXU_DTYPE),
            pltpu.VMEM((GLA_TM, B_V), MXU_DTYPE),
            pltpu.VMEM((B_HEADS, B_DK, B_DV), f32),
        ],
        compiler_params=pltpu.CompilerParams(
            dimension_semantics=("parallel", "arbitrary"), vmem_limit_bytes=VMEM_LIMIT),
        name="gla_layer",
    )(x, mod, gamma, w_main, w_glr, w_gate_p, b_gate[None, :], g_norm[None, :], w_out.astype(MXU_DTYPE))


def _qkv_weight(w_in):
    width = A_HEADS * A_HEAD_DIM
    col_scale = jnp.tile(jnp.repeat(jnp.array([A_HEAD_DIM ** -0.5 * LOG2E, 1.0, 1.0], f32), width), A_GROUPS)
    return (w_in * col_scale[None, :]).astype(MXU_DTYPE)


def kernel(x, c, w_in_a, w_out_a, rel_bias, w_in_b, w_gate_b, b_gate_b, gnorm_b, w_out_b, norm_mix, norm_ffn, w_ada, b_ada, w_up, conv_w, conv_b, w_down, norm_final):
    depth = w_ada.shape[0]
    mod = _adaln(c, w_ada, b_ada)
    bias = _rel_bias(rel_bias)
    for i in range(depth):
        gam_mix = norm_mix[i][None, :]
        j = i // 2
        mixer = None
        if i % 2 == 0:
            qkv = _qkv_proj(x, mod[i], gam_mix, _qkv_weight(w_in_a[j]))
            mixer = (_attention(qkv, bias), w_out_a[j])
        else:
            x = _gla_layer(x, mod[i], gam_mix, w_in_b[j], w_gate_b[j], b_gate_b[j], gnorm_b[j], w_out_b[j])
        last = i == depth - 1
        x = _conv_ffn(x, mod[i], norm_ffn[i][None, :], w_up[i], conv_w[i], conv_b[i], w_down[i],
                      gamma_final=norm_final[None, :] if last else None, mixer=mixer)
    return x
```

```python
import functools
import math

import jax
import jax.numpy as jnp
from jax import lax
from jax.experimental import pallas as pl
from jax.experimental.pallas import tpu as pltpu

A_CONFIGS = ((128, 1), (512, 4), (2048, 16))
A_GROUPS = len(A_CONFIGS)
A_HEADS = 16
A_HEAD_DIM = 64
N_BUCKETS = 32
MAX_DISTANCE = 2048
B_HEADS = 4
B_DK = 128
B_DV = 256
B_QK = B_HEADS * B_DK
B_V = B_HEADS * B_DV
B_GATE_RANK = 16
B_TAU = 16.0
B_CHUNK = 64
CONV_W = 3
EPS = 1e-6
NEG_INF = -1e30
LOG2E = math.log2(math.e)

LANES = 128
MXU_DTYPE = jnp.bfloat16
ATT_BLK = 128
ATT_TILE = 2048
ATT_UNITS = ATT_TILE // ATT_BLK
ATT_PIPELINE = 3
PAIR_W = 3 * LANES
N_PAIRS = A_HEADS // 2
QKV_ROWS = 1024
PROBE_ROWS = 16
FFN_TM = 512
FFN_SUB = 1
FFN_NORM_AHEAD = 3
FFN_TF = 256
FFN_ROWS = 256
FFN_DOWN_GROUP = 2
HALO = 8
O_HALO = 16
GLA_TM = 512
GLA_SUB = 2
GLA_PROJ_COLS = 256
VMEM_LIMIT = 56 * 1024 * 1024

f32 = jnp.float32


def _mm(a, b):
    return jnp.dot(a, b, preferred_element_type=f32)


def _mm_nt(a, b):
    return lax.dot_general(a, b, (((1,), (1,)), ((), ())), preferred_element_type=f32)


def _mm_tn(a, b):
    return lax.dot_general(a, b, (((0,), (0,)), ((), ())), preferred_element_type=f32)


def _modnorm(x, gamma, scale, shift):
    ms = jnp.mean(x * x, axis=-1, keepdims=True)
    y = x * lax.rsqrt(ms + EPS) * gamma
    return y * (1.0 + scale) + shift


def _silu(x):
    return x * (1.0 / (1.0 + jnp.exp(-x)))


def _adaln_kernel(c_ref, w_ref, b_ref, o_ref):
    s = _silu(c_ref[...]).astype(MXU_DTYPE)
    o_ref[0] = _mm(s, w_ref[0].astype(MXU_DTYPE)) + b_ref[0]


def _adaln(c, w_ada, b_ada):
    depth, d, n = w_ada.shape
    bsz = c.shape[0]
    rows = 8 * pl.cdiv(bsz, 8)
    c_pad = jnp.pad(c, ((0, rows - bsz), (0, 0)))
    out = pl.pallas_call(
        _adaln_kernel,
        grid=(depth, n // d),
        in_specs=[
            pl.BlockSpec((rows, d), lambda l, j: (0, 0)),
            pl.BlockSpec((1, d, d), lambda l, j: (l, 0, j)),
            pl.BlockSpec((1, 1, d), lambda l, j: (l, 0, j)),
        ],
        out_specs=pl.BlockSpec((1, rows, d), lambda l, j: (l, 0, j)),
        out_shape=jax.ShapeDtypeStruct((depth, rows, n), f32),
        compiler_params=pltpu.CompilerParams(
            dimension_semantics=("parallel", "parallel"), vmem_limit_bytes=VMEM_LIMIT),
        name="adaln",
    )(c_pad, w_ada, b_ada.reshape(depth, 1, n))
    return out[:, :bsz].reshape(depth, bsz, n // d, d)


def _qkv_kernel(x_ref, mod_ref, gam_ref, wq_ref, wk_ref, wv_ref, o_ref, ha_ref, hb_ref, sa_ref, sb_ref, *,
                dilations, tiles_per_group):
    j = pl.program_id(2)
    g = j // tiles_per_group
    t = j % tiles_per_group
    nlb = sa_ref.shape[0]
    chunks_per_step = ATT_UNITS // tiles_per_group
    h_refs, stage_refs = [ha_ref, hb_ref], [sa_ref, sb_ref]

    @pl.when(j == 0)
    def _():
        for c in range(ATT_UNITS):
            rows = slice(c * ATT_BLK, (c + 1) * ATT_BLK)
            hn = _modnorm(x_ref[0, rows, :], gam_ref[...], mod_ref[0, 1:2, :], mod_ref[0, 0:1, :])
            ha_ref[rows, :] = hn.astype(ha_ref.dtype)
            for lb in range(nlb):
                sa_ref[lb, rows, :] = hn[:, lb * LANES:(lb + 1) * LANES]

    def deinterleave_chunk(gi, cc):
        d, dp = dilations[gi], dilations[gi - 1]
        rel = d // dp
        per = ATT_UNITS // d
        c = t * chunks_per_step + cc
        r, q = c // per, c % per
        start = (r % dp) * (ATT_TILE // dp) + q * (ATT_BLK * rel) + r // dp
        rows = pl.ds(start, ATT_BLK, stride=rel)
        r0 = pl.multiple_of(c * ATT_BLK, ATT_BLK)
        parts = [stage_refs[(gi - 1) % 2][lb, rows, :] for lb in range(nlb)]
        if gi + 1 < len(dilations):
            for lb in range(nlb):
                stage_refs[gi % 2][lb, pl.ds(r0, ATT_BLK), :] = parts[lb]
        dst_ref = h_refs[gi % 2]
        dst_ref[pl.ds(r0, ATT_BLK), :] = jnp.concatenate(parts, axis=1).astype(dst_ref.dtype)
        probe = dst_ref[pl.ds(r0, PROBE_ROWS), 0:LANES]
        return (probe != probe) & (probe == probe)

    def project_chunk(src_ref, rc, never):
        rows = slice(rc * QKV_ROWS, (rc + 1) * QKV_ROWS)
        lhs = src_ref[rows, :]
        for k, w_ref in enumerate((wq_ref, wk_ref, wv_ref)):
            res = _mm(lhs, w_ref[...]).astype(o_ref.dtype)
            lanes = slice(k * LANES, (k + 1) * LANES)
            o_ref[0, 0, 1, rows, lanes] = res[:, LANES:]
            if never is not None and k == 0:
                top = rc * QKV_ROWS + PROBE_ROWS
                o_ref[0, 0, 0, rc * QKV_ROWS:top, lanes] = jnp.where(
                    never, jnp.zeros_like(res[:PROBE_ROWS, :LANES]), res[:PROBE_ROWS, :LANES])
                o_ref[0, 0, 0, top:(rc + 1) * QKV_ROWS, lanes] = res[PROBE_ROWS:, :LANES]
            else:
                o_ref[0, 0, 0, rows, lanes] = res[:, :LANES]

    assert dilations[0] == 1
    assert all(d % dp == 0 for dp, d in zip(dilations, dilations[1:]))
    n_mm = ATT_TILE // QKV_ROWS
    for gi in range(len(dilations)):
        @pl.when(g == gi)
        def _(gi=gi):
            slots = max(n_mm - 1, 1)
            shares = [list(range(chunks_per_step))[sl::slots] for sl in range(slots)] + [[]]
            for rc in range(n_mm):
                never = None
                if gi + 1 < len(dilations):
                    for cc in shares[rc]:
                        probe = deinterleave_chunk(gi + 1, cc)
                        never = probe if never is None else never | probe
                project_chunk(h_refs[gi % 2], rc, never)


def _qkv_proj(x, mod, gamma, w):
    bsz, s, d = x.shape
    dilations = tuple(dl for _, dl in A_CONFIGS)
    tn = 2 * LANES
    tiles_per_group = A_HEADS * A_HEAD_DIM // tn
    kern = functools.partial(_qkv_kernel, dilations=dilations, tiles_per_group=tiles_per_group)

    def w_spec(k):
        return pl.BlockSpec(
            (d, tn), lambda b, i, j: (0, (j // tiles_per_group * 3 + k) * tiles_per_group + j % tiles_per_group))

    return pl.pallas_call(
        kern,
        grid=(bsz, s // ATT_TILE, A_GROUPS * tiles_per_group),
        in_specs=[
            pl.BlockSpec((1, ATT_TILE, d), lambda b, i, j: (b, i, 0)),
            pl.BlockSpec((1, 6, d), lambda b, i, j: (b, 0, 0)),
            pl.BlockSpec((1, d), lambda b, i, j: (0, 0)),
            w_spec(0), w_spec(1), w_spec(2),
        ],
        out_specs=pl.BlockSpec(
            (1, 1, 2, ATT_TILE, PAIR_W),
            lambda b, i, j: (j // tiles_per_group, b, j % tiles_per_group, i, 0)),
        out_shape=jax.ShapeDtypeStruct((A_GROUPS, bsz, N_PAIRS, s, PAIR_W), MXU_DTYPE),
        scratch_shapes=[pltpu.VMEM((ATT_TILE, d), MXU_DTYPE),
                        pltpu.VMEM((ATT_TILE, d), MXU_DTYPE),
                        pltpu.VMEM((d // LANES, ATT_TILE, LANES), f32),
                        pltpu.VMEM((d // LANES, ATT_TILE, LANES), f32)],
        compiler_params=pltpu.CompilerParams(
            dimension_semantics=("parallel", "parallel", "arbitrary"),
            vmem_limit_bytes=VMEM_LIMIT),
        name="qkv_proj",
    )(x, mod, gamma, w, w, w)


def _bias_kernel(tab_ref, bkt_ref, o_ref):
    bkt = bkt_ref[0]
    hits = [bkt == k for k in range(N_BUCKETS)]
    for h in range(A_HEADS):
        col = pl.program_id(0) * A_HEADS + h
        acc = jnp.full(bkt.shape, NEG_INF, f32)
        for k in range(N_BUCKETS):
            acc = jnp.where(hits[k], tab_ref[k, col] * LOG2E, acc)
        o_ref[0, h] = acc


def _t5_bucket(dist):
    max_exact = N_BUCKETS // 2
    n = jnp.maximum(dist, max_exact).astype(f32)
    large = max_exact + (jnp.log(n / max_exact) / math.log(MAX_DISTANCE / max_exact)
                         * (N_BUCKETS - max_exact)).astype(jnp.int32)
    large = jnp.minimum(large, N_BUCKETS - 1)
    return jnp.where(dist < max_exact, dist, large)


def _rel_bias(rel_bias):
    qi = jnp.arange(ATT_BLK)[:, None]
    ki = jnp.arange(2 * ATT_BLK)[None, :]
    steps = qi + ATT_BLK - ki
    band = (steps >= 0) & (steps <= ATT_BLK)
    bucket = jnp.stack([
        jnp.where(band, _t5_bucket(jnp.clip(steps, 0, ATT_BLK) * dl), -1)
        for _, dl in A_CONFIGS]).astype(jnp.int32)
    return pl.pallas_call(
        _bias_kernel,
        grid=(A_GROUPS,),
        in_specs=[
            pl.BlockSpec(memory_space=pltpu.SMEM),
            pl.BlockSpec((1, ATT_BLK, 2 * ATT_BLK), lambda g: (g, 0, 0)),
        ],
        out_specs=pl.BlockSpec((1, A_HEADS, ATT_BLK, 2 * ATT_BLK), lambda g: (g, 0, 0, 0)),
        out_shape=jax.ShapeDtypeStruct((A_GROUPS, A_HEADS, ATT_BLK, 2 * ATT_BLK), f32),
        name="rel_bias",
    )(rel_bias, bucket)


def _attn_kernel(qkv_ref, bias_ref, out_ref, prev0, prev1, prev2, o1, l1, o2, l2, biasp):
    first = pl.program_id(2) == 0

    @pl.when(first)
    def _():
        prev0[...] = jnp.zeros_like(prev0)
        prev1[...] = jnp.zeros_like(prev1)
        prev2[...] = jnp.zeros_like(prev2)

    for g in range(A_GROUPS):
        for hh in range(2):
            biasp[g, hh, :, 0:ATT_BLK] = jnp.where(first, NEG_INF, bias_ref[g, hh, :, 0:ATT_BLK])
            biasp[g, hh, :, ATT_BLK:2 * ATT_BLK] = bias_ref[g, hh, :, ATT_BLK:2 * ATT_BLK]

    lane = lax.broadcasted_iota(jnp.int32, (1, LANES), 1)
    lo = lane < A_HEAD_DIM
    qmask = (jnp.where(lo, 1.0, 0.0).astype(MXU_DTYPE), jnp.where(lo, 0.0, 1.0).astype(MXU_DTYPE))

    def keys_values(g, rc, prev_ref, rp, lanes):
        if prev_ref is None:
            return qkv_ref[g, 0, 0, rp:rp + 2 * ATT_BLK, lanes]
        off = lanes.start - LANES
        return jnp.concatenate([prev_ref[rp:rp + ATT_BLK, off:off + LANES],
                                qkv_ref[g, 0, 0, rc:rc + ATT_BLK, lanes]], axis=0)

    def scores(g, rc, prev_ref, rp):
        q = qkv_ref[g, 0, 0, rc:rc + ATT_BLK, 0:LANES]
        k = keys_values(g, rc, prev_ref, rp, slice(LANES, 2 * LANES))
        bias = bias_ref if prev_ref is None else biasp
        return [_mm_nt(q * qmask[hh], k) + bias[g, hh] for hh in range(2)]

    def attend(g, rc, prev_ref, rp, s):
        v = keys_values(g, rc, prev_ref, rp, slice(2 * LANES, 3 * LANES))
        accs, dens, ms = [], [], []
        for sh in s:
            m = jnp.max(sh, axis=-1, keepdims=True)
            p = jnp.exp2(sh - m)
            dens.append(jnp.sum(p, axis=-1, keepdims=True))
            accs.append(_mm(p.astype(MXU_DTYPE), v))
            ms.append(m)
        den = jnp.where(lo, dens[0], dens[1])
        o = jnp.where(lo, accs[0], accs[1]) * (1.0 / den)
        lse = jnp.where(lo, ms[0], ms[1]) + jnp.log2(den)
        return o, jnp.broadcast_to(lse, o.shape)

    def store(o_ref, l_ref, rows):
        def post(o, lse):
            o_ref[rows, :] = o
            l_ref[rows, :] = lse
        return post

    def merge(rows):
        def post(o0, lse0):
            lse1, lse2 = l1[rows, :], l2[rows, :]
            mx = jnp.maximum(jnp.maximum(lse0, lse1), lse2)
            e0, e1, e2 = jnp.exp2(lse0 - mx), jnp.exp2(lse1 - mx), jnp.exp2(lse2 - mx)
            mix = (e0 * o0 + e1 * o1[rows, :] + e2 * o2[rows, :]) * (1.0 / (e0 + e1 + e2))
            out_ref[0, rows, :] = mix.astype(out_ref.dtype)
        return post

    units = []
    d2 = A_CONFIGS[2][1]
    for u in range(ATT_UNITS):
        units.append((2, u * ATT_BLK, prev2, u * ATT_BLK, store(o2, l2, pl.ds(u, ATT_BLK, stride=d2))))
    d1 = A_CONFIGS[1][1]
    per = ATT_UNITS // d1
    for u in range(ATT_UNITS):
        r, q = divmod(u, per)
        rc = u * ATT_BLK
        post = store(o1, l1, pl.ds(q * ATT_BLK * d1 + r, ATT_BLK, stride=d1))
        if q == 0:
            units.append((1, rc, prev1, (r * per + per - 1) * ATT_BLK, post))
        else:
            units.append((1, rc, None, rc - ATT_BLK, post))
    for u in range(ATT_UNITS):
        rc = u * ATT_BLK
        post = merge(slice(rc, rc + ATT_BLK))
        units.append((0, rc, prev0, 0, post) if u == 0 else (0, rc, None, rc - ATT_BLK, post))

    pending = [scores(*un[:4]) for un in units[:ATT_PIPELINE]]
    for idx, un in enumerate(units):
        if idx + ATT_PIPELINE < len(units):
            pending.append(scores(*units[idx + ATT_PIPELINE][:4]))
        o, lse = attend(*un[:4], pending.pop(0))
        un[4](o, lse)

    prev0[...] = qkv_ref[0, 0, 0, ATT_TILE - ATT_BLK:ATT_TILE, LANES:3 * LANES]
    prev1[...] = qkv_ref[1, 0, 0, :, LANES:3 * LANES]
    prev2[...] = qkv_ref[2, 0, 0, :, LANES:3 * LANES]


def _attention(qkv, bias):
    _, bsz, _, s, _ = qkv.shape
    return pl.pallas_call(
        _attn_kernel,
        grid=(bsz, N_PAIRS, s // ATT_TILE),
        in_specs=[
            pl.BlockSpec((A_GROUPS, 1, 1, ATT_TILE, PAIR_W), lambda b, p, i: (0, b, p, i, 0)),
            pl.BlockSpec((A_GROUPS, 2, ATT_BLK, 2 * ATT_BLK), lambda b, p, i: (0, p, 0, 0)),
        ],
        out_specs=pl.BlockSpec((1, ATT_TILE, LANES), lambda b, p, i: (b, i, p)),
        out_shape=jax.ShapeDtypeStruct((bsz, s, A_HEADS * A_HEAD_DIM), MXU_DTYPE),
        scratch_shapes=[
            pltpu.VMEM((ATT_BLK, 2 * LANES), MXU_DTYPE),
            pltpu.VMEM((ATT_TILE, 2 * LANES), MXU_DTYPE),
            pltpu.VMEM((ATT_TILE, 2 * LANES), MXU_DTYPE),
            pltpu.VMEM((ATT_TILE, LANES), f32),
            pltpu.VMEM((ATT_TILE, LANES), f32),
            pltpu.VMEM((ATT_TILE, LANES), f32),
            pltpu.VMEM((ATT_TILE, LANES), f32),
            pltpu.VMEM((A_GROUPS, 2, ATT_BLK, 2 * ATT_BLK), f32),
        ],
        compiler_params=pltpu.CompilerParams(
            dimension_semantics=("parallel", "parallel", "arbitrary"),
            vmem_limit_bytes=VMEM_LIMIT),
        name="dilated_attn",
    )(qkv, bias)


def _ffn_kernel(*refs, final, mixer):
    refs = list(refs)
    x_ref, halo_ref = refs[:2]
    del refs[:2]
    if mixer:
        o_ref, ohalo_ref, wout_ref = refs[:3]
        del refs[:3]
    mod_ref, gam_ref, wup_ref, cw_ref, cb_ref, wdn_ref = refs[:6]
    del refs[:6]
    if final:
        gfin_ref = refs.pop(0)
    out_ref, h0_ref, h1_ref, u0_ref, u1_ref, act_ref, acc_ref = refs[:7]
    if mixer:
        x1_ref = refs[7]
    h_refs, u_refs = (h0_ref, h1_ref), (u0_ref, u1_ref)
    gam, scale, shift = gam_ref[...], mod_ref[0, 4:5, :], mod_ref[0, 3:4, :]
    d_ff = wdn_ref.shape[0]
    nchunk = d_ff // FFN_TF
    nsub = x_ref.shape[1] // FFN_TM

    def stream(k):
        rows = slice(k * FFN_TM, (k + 1) * FFN_TM)
        if not mixer:
            return x_ref[0, rows, :]
        x1 = x_ref[0, rows, :] + mod_ref[0, 2:3, :] * _mm(o_ref[0, rows, :], wout_ref[...])
        x1_ref[rows, :] = x1
        return x1

    def stream_halo(k):
        if k > 0:
            src = x1_ref if mixer else x_ref.at[0]
            return src[k * FFN_TM - HALO:k * FFN_TM, :]
        if not mixer:
            return halo_ref[0]
        y = _mm(ohalo_ref[0], wout_ref[...])[O_HALO - HALO:, :]
        return halo_ref[0] + mod_ref[0, 2:3, :] * y

    def normalize(k):
        h_ref = h_refs[k % 2]
        halo = _modnorm(stream_halo(k), gam, scale, shift)
        if k == 0:
            halo = jnp.where(pl.program_id(1) == 0, 0.0, halo)
        h_ref[0:HALO, :] = halo.astype(h_ref.dtype)
        h_ref[HALO:HALO + FFN_TM, :] = _modnorm(stream(k), gam, scale, shift).astype(h_ref.dtype)

    def halves(c):
        return [slice(half * d_ff + c * FFN_TF, half * d_ff + (c + 1) * FFN_TF) for half in range(2)]

    def up(job):
        k, c = divmod(job, nchunk)
        for half, cols in enumerate(halves(c)):
            u_refs[job % 2][:, half * FFN_TF:(half + 1) * FFN_TF] = _mm(h_refs[k % 2][...], wup_ref[:, cols])

    def conv_act(job):
        c = job % nchunk
        u_ref = u_refs[job % 2]
        for rb in range(FFN_TM // FFN_ROWS):
            base = HALO + rb * FFN_ROWS
            ab = []
            for half, cols in enumerate(halves(c)):
                v = cb_ref[:, cols]
                for t in range(CONV_W):
                    off = base - (CONV_W - 1) + t
                    v = v + cw_ref[t:t + 1, cols] * u_ref[off:off + FFN_ROWS, half * FFN_TF:(half + 1) * FFN_TF]
                ab.append(v)
            act_ref[rb * FFN_ROWS:(rb + 1) * FFN_ROWS, c * FFN_TF:(c + 1) * FFN_TF] = (
                _silu(ab[0]) * ab[1]).astype(act_ref.dtype)

    def down(first, last):
        cols = slice(first * FFN_TF, last * FFN_TF)
        return _mm(act_ref[:, cols], wdn_ref[cols, :])

    def finish(k, start):
        ffn = down(start, nchunk)
        if start > 0:
            ffn = ffn + acc_ref[...]
        rows = slice(k * FFN_TM, (k + 1) * FFN_TM)
        y = (x1_ref[rows, :] if mixer else x_ref[0, rows, :]) + mod_ref[0, 5:6, :] * ffn
        if final:
            ms = jnp.mean(y * y, axis=-1, keepdims=True)
            y = y * lax.rsqrt(ms + EPS) * gfin_ref[...]
        out_ref[0, rows, :] = y

    normalize(0)
    up(0)
    start, pending = 0, None
    for job in range(nsub * nchunk):
        k, c = divmod(job, nchunk)
        if c == 0:
            start = 0
        if c == nchunk - FFN_NORM_AHEAD and k + 1 < nsub:
            normalize(k + 1)
        if job + 1 < nsub * nchunk:
            up(job + 1)
        if pending is not None:
            if pending[0] == 0:
                acc_ref[...] = down(*pending)
            else:
                acc_ref[...] += down(*pending)
            pending = None
        if c + 1 - start == FFN_DOWN_GROUP and c + 1 < nchunk:
            pending, start = (start, c + 1), c + 1
        conv_act(job)
        if c == nchunk - 1:
            finish(k, start)


def _conv_ffn(x, mod, gamma, w_up, conv_w, conv_b, w_down, gamma_final=None, mixer=None):
    bsz, s, d = x.shape
    d_ff = w_down.shape[0]
    final = gamma_final is not None
    rows = FFN_SUB * FFN_TM
    tiles_per_halo = rows // HALO
    resident = dict(pipeline_mode=pl.Buffered(1))
    in_specs = [
        pl.BlockSpec((1, rows, d), lambda b, i: (b, i, 0)),
        pl.BlockSpec((1, HALO, d), lambda b, i: (b, jnp.maximum(i * tiles_per_halo - 1, 0), 0)),
    ]
    args = [x, x]
    if mixer is not None:
        o, w_out = mixer
        k = o.shape[-1]
        in_specs += [
            pl.BlockSpec((1, rows, k), lambda b, i: (b, i, 0)),
            pl.BlockSpec((1, O_HALO, k), lambda b, i: (b, jnp.maximum(i * (rows // O_HALO) - 1, 0), 0)),
            pl.BlockSpec((k, d), lambda b, i: (0, 0), **resident),
        ]
        args += [o, o, w_out.astype(MXU_DTYPE)]
    in_specs += [
        pl.BlockSpec((1, 6, d), lambda b, i: (b, 0, 0)),
        pl.BlockSpec((1, d), lambda b, i: (0, 0)),
        pl.BlockSpec((d, 2 * d_ff), lambda b, i: (0, 0), **resident),
        pl.BlockSpec((CONV_W, 2 * d_ff), lambda b, i: (0, 0)),
        pl.BlockSpec((1, 2 * d_ff), lambda b, i: (0, 0)),
        pl.BlockSpec((d_ff, d), lambda b, i: (0, 0), **resident),
    ]
    args += [mod, gamma, w_up.astype(MXU_DTYPE), conv_w, conv_b[None, :], w_down.astype(MXU_DTYPE)]
    if final:
        in_specs.append(pl.BlockSpec((1, d), lambda b, i: (0, 0)))
        args.append(gamma_final)
    return pl.pallas_call(
        functools.partial(_ffn_kernel, final=final, mixer=mixer is not None),
        grid=(bsz, s // rows),
        in_specs=in_specs,
        out_specs=pl.BlockSpec((1, rows, d), lambda b, i: (b, i, 0)),
        out_shape=jax.ShapeDtypeStruct((bsz, s, d), f32),
        scratch_shapes=[
            pltpu.VMEM((FFN_TM + HALO, d), MXU_DTYPE),
            pltpu.VMEM((FFN_TM + HALO, d), MXU_DTYPE),
            pltpu.VMEM((FFN_TM + HALO, 2 * FFN_TF), f32),
            pltpu.VMEM((FFN_TM + HALO, 2 * FFN_TF), f32),
            pltpu.VMEM((FFN_TM, d_ff), MXU_DTYPE),
            pltpu.VMEM((FFN_TM, d), f32),
        ] + ([pltpu.VMEM((rows, d), f32)] if mixer is not None else []),
        compiler_params=pltpu.CompilerParams(
            dimension_semantics=("parallel", "parallel"), vmem_limit_bytes=VMEM_LIMIT),
        name="conv_ffn_final" if final else "conv_ffn",
    )(*args)


def _gla_kernel(x_ref, mod_ref, gam_ref, win_ref, wglr_ref, wgate_ref, bgate_ref, gnorm_ref, wout_ref,
                out_ref, h_ref, proj0_ref, proj1_ref, gk0_ref, gk1_ref, og0_ref, og1_ref, state_ref):
    @pl.when(pl.program_id(1) == 0)
    def _():
        state_ref[...] = jnp.zeros_like(state_ref)

    proj_refs, gk_refs, og_refs = (proj0_ref, proj1_ref), (gk0_ref, gk1_ref), (og0_ref, og1_ref)
    nsub = x_ref.shape[1] // GLA_TM
    nchunk = GLA_TM // B_CHUNK
    n_main = win_ref.shape[1]

    ri = lax.broadcasted_iota(jnp.int32, (B_CHUNK, B_CHUNK), 0)
    ci = lax.broadcasted_iota(jnp.int32, (B_CHUNK, B_CHUNK), 1)
    causal = ci <= ri
    tri = causal.astype(MXU_DTYPE)
    gnorm = gnorm_ref[...]
    qscale = B_DK ** -0.5
    r_off = 2 * B_QK + B_V
    heads = [slice(hd * B_DK, (hd + 1) * B_DK) for hd in range(B_HEADS)]

    def project_steps(k):
        rows = slice(k * GLA_TM, (k + 1) * GLA_TM)

        def normalize():
            h = _modnorm(x_ref[0, rows, :], gam_ref[...], mod_ref[0, 1:2, :], mod_ref[0, 0:1, :])
            h_ref[...] = h.astype(h_ref.dtype)

        def piece(cols):
            def run():
                proj_refs[k % 2][:, cols] = _mm(h_ref[...], win_ref[:, cols])
            return run

        def gate():
            glr = _mm(h_ref[...], wglr_ref[...]).astype(MXU_DTYPE)
            z = _mm(glr, wgate_ref[...]) + bgate_ref[...]
            gk_refs[k % 2][...] = (jnp.minimum(z, 0.0) - jnp.log1p(jnp.exp(-jnp.abs(z)))) * (1.0 / B_TAU)

        pieces = [piece(slice(c0, c0 + GLA_PROJ_COLS)) for c0 in range(0, n_main, GLA_PROJ_COLS)]
        return [normalize] + pieces + [gate]

    def local(k, c):
        proj_ref, gk_ref = proj_refs[k % 2], gk_refs[k % 2]
        rows = slice(c * B_CHUNK, (c + 1) * B_CHUNK)
        gk = gk_ref[rows, :]
        g_hi = gk.astype(MXU_DTYPE)
        rem = gk - g_hi.astype(f32)
        g_mid = rem.astype(MXU_DTYPE)
        g_lo = (rem - g_mid.astype(f32)).astype(MXU_DTYPE)
        bcum = _mm(tri, g_hi) + _mm(tri, g_mid) + _mm(tri, g_lo)
        blast = bcum[B_CHUNK - 1:B_CHUNK, :]
        q_t = ((proj_ref[rows, 0:B_QK] * qscale) * jnp.exp(bcum)).astype(MXU_DTYPE)
        kk = proj_ref[rows, B_QK:2 * B_QK]
        k_t = (kk * jnp.exp(-bcum)).astype(MXU_DTYPE)
        k_d = (kk * jnp.exp(blast - bcum)).astype(MXU_DTYPE)
        v = [proj_ref[rows, 2 * B_QK + hd * B_DV:2 * B_QK + (hd + 1) * B_DV].astype(MXU_DTYPE)
             for hd in range(B_HEADS)]
        a = [jnp.where(causal, _mm_nt(q_t[:, ks], k_t[:, ks]), 0.0).astype(MXU_DTYPE) for ks in heads]
        kv = [_mm_tn(k_d[:, ks], v[hd]) for hd, ks in enumerate(heads)]
        o_intra = [_mm(a[hd], v[hd]) for hd in range(B_HEADS)]
        return q_t, blast, kv, o_intra

    def recur(k, c, loc, state):
        q_t, blast, kv, o_intra = loc
        rows = slice(c * B_CHUNK, (c + 1) * B_CHUNK)
        new_state = []
        for hd, ks in enumerate(heads):
            o = o_intra[hd] + _mm(q_t[:, ks], state[hd].astype(MXU_DTYPE))
            decay = jnp.exp(jnp.broadcast_to(blast[:, ks], (B_DK, B_DK)).T)
            new_state.append(jnp.concatenate([decay] * (B_DV // B_DK), axis=1) * state[hd] + kv[hd])
            o = o * lax.rsqrt(jnp.mean(o * o, axis=-1, keepdims=True) + EPS) * gnorm
            r = proj_refs[k % 2][rows, r_off + hd * B_DV:r_off + (hd + 1) * B_DV]
            og_refs[k % 2][rows, hd * B_DV:(hd + 1) * B_DV] = (o * _silu(r)).astype(og_refs[k % 2].dtype)
        return new_state

    for step in project_steps(0):
        step()
    state = [state_ref[hd] for hd in range(B_HEADS)]
    for k in range(nsub):
        ahead = project_steps(k + 1) if k + 1 < nsub else []
        loc = local(k, 0)
        for c in range(nchunk):
            nxt = local(k, c + 1) if c + 1 < nchunk else None
            state = recur(k, c, loc, state)
            loc = nxt
            take = -(-len(ahead) // (nchunk - c))
            for step in ahead[:take]:
                step()
            ahead = ahead[take:]
        rows = slice(k * GLA_TM, (k + 1) * GLA_TM)
        out_ref[0, rows, :] = x_ref[0, rows, :] + mod_ref[0, 2:3, :] * _mm(og_refs[k % 2][...], wout_ref[...])
    for hd in range(B_HEADS):
        state_ref[hd] = state[hd]


def _gla_layer(x, mod, gamma, w_in, w_gate, b_gate, g_norm, w_out):
    bsz, s, d = x.shape
    glr0 = 2 * B_QK + B_V
    w_main = jnp.concatenate([w_in[:, :glr0], w_in[:, glr0 + B_GATE_RANK:]], axis=1).astype(MXU_DTYPE)
    w_glr = jnp.pad(w_in[:, glr0:glr0 + B_GATE_RANK], ((0, 0), (0, LANES - B_GATE_RANK))).astype(MXU_DTYPE)
    w_gate_p = jnp.pad(w_gate, ((0, LANES - B_GATE_RANK), (0, 0))).astype(MXU_DTYPE)
    n_main = w_main.shape[1]
    rows = GLA_SUB * GLA_TM
    const = lambda b, i: (0, 0)
    resident = dict(pipeline_mode=pl.Buffered(1))
    return pl.pallas_call(
        _gla_kernel,
        grid=(bsz, s // rows),
        in_specs=[
            pl.BlockSpec((1, rows, d), lambda b, i: (b, i, 0)),
            pl.BlockSpec((1, 6, d), lambda b, i: (b, 0, 0)),
            pl.BlockSpec((1, d), const),
            pl.BlockSpec((d, n_main), const, **resident),
            pl.BlockSpec((d, LANES), const),
            pl.BlockSpec((LANES, B_QK), const),
            pl.BlockSpec((1, B_QK), const),
            pl.BlockSpec((1, B_DV), const),
            pl.BlockSpec((B_V, d), const, **resident),
        ],
        out_specs=pl.BlockSpec((1, rows, d), lambda b, i: (b, i, 0)),
        out_shape=jax.ShapeDtypeStruct((bsz, s, d), f32),
        scratch_shapes=[
            pltpu.VMEM((GLA_TM, d), MXU_DTYPE),
            pltpu.VMEM((GLA_TM, n_main), f32),
            pltpu.VMEM((GLA_TM, n_main), f32),
            pltpu.VMEM((GLA_TM, B_QK), f32),
            pltpu.VMEM((GLA_TM, B_QK), f32),
            pltpu.VMEM((GLA_TM, B_V), MXU_DTYPE),
            pltpu.VMEM((GLA_TM, B_V), MXU_DTYPE),
            pltpu.VMEM((B_HEADS, B_DK, B_DV), f32),
        ],
        compiler_params=pltpu.CompilerParams(
            dimension_semantics=("parallel", "arbitrary"), vmem_limit_bytes=VMEM_LIMIT),
        name="gla_layer",
    )(x, mod, gamma, w_main, w_glr, w_gate_p, b_gate[None, :], g_norm[None, :], w_out.astype(MXU_DTYPE))


def _qkv_weight(w_in):
    width = A_HEADS * A_HEAD_DIM
    col_scale = jnp.tile(jnp.repeat(jnp.array([A_HEAD_DIM ** -0.5 * LOG2E, 1.0, 1.0], f32), width), A_GROUPS)
    return (w_in * col_scale[None, :]).astype(MXU_DTYPE)


def kernel(x, c, w_in_a, w_out_a, rel_bias, w_in_b, w_gate_b, b_gate_b, gnorm_b, w_out_b, norm_mix, norm_ffn, w_ada, b_ada, w_up, conv_w, conv_b, w_down, norm_final):
    depth = w_ada.shape[0]
    mod = _adaln(c, w_ada, b_ada)
    bias = _rel_bias(rel_bias)
    for i in range(depth):
        gam_mix = norm_mix[i][None, :]
        j = i // 2
        mixer = None
        if i % 2 == 0:
            qkv = _qkv_proj(x, mod[i], gam_mix, _qkv_weight(w_in_a[j]))
            mixer = (_attention(qkv, bias), w_out_a[j])
        else:
            x = _gla_layer(x, mod[i], gam_mix, w_in_b[j], w_gate_b[j], b_gate_b[j], gnorm_b[j], w_out_b[j])
        last = i == depth - 1
        x = _conv_ffn(x, mod[i], norm_ffn[i][None, :], w_up[i], conv_w[i], conv_b[i], w_down[i],
                      gamma_final=norm_final[None, :] if last else None, mixer=mixer)
    return x
```

```python
import functools
import math

import jax
import jax.numpy as jnp
from jax import lax
from jax.experimental import pallas as pl
from jax.experimental.pallas import tpu as pltpu

A_CONFIGS = ((128, 1), (512, 4), (2048, 16))
A_GROUPS = len(A_CONFIGS)
A_HEADS = 16
A_HEAD_DIM = 64
N_BUCKETS = 32
MAX_DISTANCE = 2048
B_HEADS = 4
B_DK = 128
B_DV = 256
B_QK = B_HEADS * B_DK
B_V = B_HEADS * B_DV
B_GATE_RANK = 16
B_TAU = 16.0
B_CHUNK = 64
CONV_W = 3
EPS = 1e-6
NEG_INF = -1e30
LOG2E = math.log2(math.e)

LANES = 128
MXU_DTYPE = jnp.bfloat16
ATT_BLK = 128
ATT_TILE = 2048
ATT_UNITS = ATT_TILE // ATT_BLK
ATT_PIPELINE = 3
PAIR_W = 3 * LANES
N_PAIRS = A_HEADS // 2
QKV_ROWS = 2048
PROBE_ROWS = 16
FFN_TM = 512
FFN_SUB = 1
FFN_NORM_AHEAD = 3
FFN_TF = 256
FFN_ROWS = 256
FFN_DOWN_GROUP = 2
HALO = 8
O_HALO = 16
GLA_TM = 512
GLA_SUB = 2
GLA_PROJ_COLS = 256
VMEM_LIMIT = 56 * 1024 * 1024

f32 = jnp.float32


def _mm(a, b):
    return jnp.dot(a, b, preferred_element_type=f32)


def _mm_nt(a, b):
    return lax.dot_general(a, b, (((1,), (1,)), ((), ())), preferred_element_type=f32)


def _mm_tn(a, b):
    return lax.dot_general(a, b, (((0,), (0,)), ((), ())), preferred_element_type=f32)


def _modnorm(x, gamma, scale, shift):
    ms = jnp.mean(x * x, axis=-1, keepdims=True)
    y = x * lax.rsqrt(ms + EPS) * gamma
    return y * (1.0 + scale) + shift


def _silu(x):
    return x * (1.0 / (1.0 + jnp.exp(-x)))


def _adaln_kernel(c_ref, w_ref, b_ref, o_ref):
    s = _silu(c_ref[...]).astype(MXU_DTYPE)
    o_ref[0] = _mm(s, w_ref[0].astype(MXU_DTYPE)) + b_ref[0]


def _adaln(c, w_ada, b_ada):
    depth, d, n = w_ada.shape
    bsz = c.shape[0]
    rows = 8 * pl.cdiv(bsz, 8)
    c_pad = jnp.pad(c, ((0, rows - bsz), (0, 0)))
    out = pl.pallas_call(
        _adaln_kernel,
        grid=(depth, n // d),
        in_specs=[
            pl.BlockSpec((rows, d), lambda l, j: (0, 0)),
            pl.BlockSpec((1, d, d), lambda l, j: (l, 0, j)),
            pl.BlockSpec((1, 1, d), lambda l, j: (l, 0, j)),
        ],
        out_specs=pl.BlockSpec((1, rows, d), lambda l, j: (l, 0, j)),
        out_shape=jax.ShapeDtypeStruct((depth, rows, n), f32),
        compiler_params=pltpu.CompilerParams(
            dimension_semantics=("parallel", "parallel"), vmem_limit_bytes=VMEM_LIMIT),
        name="adaln",
    )(c_pad, w_ada, b_ada.reshape(depth, 1, n))
    return out[:, :bsz].reshape(depth, bsz, n // d, d)


def _qkv_kernel(x_ref, mod_ref, gam_ref, wq_ref, wk_ref, wv_ref, o_ref, ha_ref, hb_ref, sa_ref, sb_ref, *,
                dilations, tiles_per_group):
    j = pl.program_id(2)
    g = j // tiles_per_group
    t = j % tiles_per_group
    nlb = sa_ref.shape[0]
    chunks_per_step = ATT_UNITS // tiles_per_group
    h_refs, stage_refs = [ha_ref, hb_ref], [sa_ref, sb_ref]

    @pl.when(j == 0)
    def _():
        for c in range(ATT_UNITS):
            rows = slice(c * ATT_BLK, (c + 1) * ATT_BLK)
            hn = _modnorm(x_ref[0, rows, :], gam_ref[...], mod_ref[0, 1:2, :], mod_ref[0, 0:1, :])
            ha_ref[rows, :] = hn.astype(ha_ref.dtype)
            for lb in range(nlb):
                sa_ref[lb, rows, :] = hn[:, lb * LANES:(lb + 1) * LANES]

    def deinterleave_chunk(gi, cc):
        d, dp = dilations[gi], dilations[gi - 1]
        rel = d // dp
        per = ATT_UNITS // d
        c = t * chunks_per_step + cc
        r, q = c // per, c % per
        start = (r % dp) * (ATT_TILE // dp) + q * (ATT_BLK * rel) + r // dp
        rows = pl.ds(start, ATT_BLK, stride=rel)
        r0 = pl.multiple_of(c * ATT_BLK, ATT_BLK)
        parts = [stage_refs[(gi - 1) % 2][lb, rows, :] for lb in range(nlb)]
        if gi + 1 < len(dilations):
            for lb in range(nlb):
                stage_refs[gi % 2][lb, pl.ds(r0, ATT_BLK), :] = parts[lb]
        dst_ref = h_refs[gi % 2]
        dst_ref[pl.ds(r0, ATT_BLK), :] = jnp.concatenate(parts, axis=1).astype(dst_ref.dtype)
        probe = dst_ref[pl.ds(r0, PROBE_ROWS), 0:LANES]
        return (probe != probe) & (probe == probe)

    def project_chunk(src_ref, rc, never):
        rows = slice(rc * QKV_ROWS, (rc + 1) * QKV_ROWS)
        lhs = src_ref[rows, :]
        for k, w_ref in enumerate((wq_ref, wk_ref, wv_ref)):
            res = _mm(lhs, w_ref[...]).astype(o_ref.dtype)
            lanes = slice(k * LANES, (k + 1) * LANES)
            o_ref[0, 0, 1, rows, lanes] = res[:, LANES:]
            if never is not None and k == 0:
                top = rc * QKV_ROWS + PROBE_ROWS
                o_ref[0, 0, 0, rc * QKV_ROWS:top, lanes] = jnp.where(
                    never, jnp.zeros_like(res[:PROBE_ROWS, :LANES]), res[:PROBE_ROWS, :LANES])
                o_ref[0, 0, 0, top:(rc + 1) * QKV_ROWS, lanes] = res[PROBE_ROWS:, :LANES]
            else:
                o_ref[0, 0, 0, rows, lanes] = res[:, :LANES]

    assert dilations[0] == 1
    assert all(d % dp == 0 for dp, d in zip(dilations, dilations[1:]))
    n_mm = ATT_TILE // QKV_ROWS
    for gi in range(len(dilations)):
        @pl.when(g == gi)
        def _(gi=gi):
            slots = max(n_mm - 1, 1)
            shares = [list(range(chunks_per_step))[sl::slots] for sl in range(slots)] + [[]]
            for rc in range(n_mm):
                never = None
                if gi + 1 < len(dilations):
                    for cc in shares[rc]:
                        probe = deinterleave_chunk(gi + 1, cc)
                        never = probe if never is None else never | probe
                project_chunk(h_refs[gi % 2], rc, never)


def _qkv_proj(x, mod, gamma, w):
    bsz, s, d = x.shape
    dilations = tuple(dl for _, dl in A_CONFIGS)
    tn = 2 * LANES
    tiles_per_group = A_HEADS * A_HEAD_DIM // tn
    kern = functools.partial(_qkv_kernel, dilations=dilations, tiles_per_group=tiles_per_group)

    def w_spec(k):
        return pl.BlockSpec(
            (d, tn), lambda b, i, j: (0, (j // tiles_per_group * 3 + k) * tiles_per_group + j % tiles_per_group))

    return pl.pallas_call(
        kern,
        grid=(bsz, s // ATT_TILE, A_GROUPS * tiles_per_group),
        in_specs=[
            pl.BlockSpec((1, ATT_TILE, d), lambda b, i, j: (b, i, 0)),
            pl.BlockSpec((1, 6, d), lambda b, i, j: (b, 0, 0)),
            pl.BlockSpec((1, d), lambda b, i, j: (0, 0)),
            w_spec(0), w_spec(1), w_spec(2),
        ],
        out_specs=pl.BlockSpec(
            (1, 1, 2, ATT_TILE, PAIR_W),
            lambda b, i, j: (j // tiles_per_group, b, j % tiles_per_group, i, 0)),
        out_shape=jax.ShapeDtypeStruct((A_GROUPS, bsz, N_PAIRS, s, PAIR_W), MXU_DTYPE),
        scratch_shapes=[pltpu.VMEM((ATT_TILE, d), MXU_DTYPE),
                        pltpu.VMEM((ATT_TILE, d), MXU_DTYPE),
                        pltpu.VMEM((d // LANES, ATT_TILE, LANES), f32),
                        pltpu.VMEM((d // LANES, ATT_TILE, LANES), f32)],
        compiler_params=pltpu.CompilerParams(
            dimension_semantics=("parallel", "parallel", "arbitrary"),
            vmem_limit_bytes=VMEM_LIMIT),
        name="qkv_proj",
    )(x, mod, gamma, w, w, w)


def _bias_kernel(tab_ref, bkt_ref, o_ref):
    bkt = bkt_ref[0]
    hits = [bkt == k for k in range(N_BUCKETS)]
    for h in range(A_HEADS):
        col = pl.program_id(0) * A_HEADS + h
        acc = jnp.full(bkt.shape, NEG_INF, f32)
        for k in range(N_BUCKETS):
            acc = jnp.where(hits[k], tab_ref[k, col] * LOG2E, acc)
        o_ref[0, h] = acc


def _t5_bucket(dist):
    max_exact = N_BUCKETS // 2
    n = jnp.maximum(dist, max_exact).astype(f32)
    large = max_exact + (jnp.log(n / max_exact) / math.log(MAX_DISTANCE / max_exact)
                         * (N_BUCKETS - max_exact)).astype(jnp.int32)
    large = jnp.minimum(large, N_BUCKETS - 1)
    return jnp.where(dist < max_exact, dist, large)


def _rel_bias(rel_bias):
    qi = jnp.arange(ATT_BLK)[:, None]
    ki = jnp.arange(2 * ATT_BLK)[None, :]
    steps = qi + ATT_BLK - ki
    band = (steps >= 0) & (steps <= ATT_BLK)
    bucket = jnp.stack([
        jnp.where(band, _t5_bucket(jnp.clip(steps, 0, ATT_BLK) * dl), -1)
        for _, dl in A_CONFIGS]).astype(jnp.int32)
    return pl.pallas_call(
        _bias_kernel,
        grid=(A_GROUPS,),
        in_specs=[
            pl.BlockSpec(memory_space=pltpu.SMEM),
            pl.BlockSpec((1, ATT_BLK, 2 * ATT_BLK), lambda g: (g, 0, 0)),
        ],
        out_specs=pl.BlockSpec((1, A_HEADS, ATT_BLK, 2 * ATT_BLK), lambda g: (g, 0, 0, 0)),
        out_shape=jax.ShapeDtypeStruct((A_GROUPS, A_HEADS, ATT_BLK, 2 * ATT_BLK), f32),
        name="rel_bias",
    )(rel_bias, bucket)


def _attn_kernel(qkv_ref, bias_ref, out_ref, prev0, prev1, prev2, o1, l1, o2, l2, biasp):
    first = pl.program_id(2) == 0

    @pl.when(first)
    def _():
        prev0[...] = jnp.zeros_like(prev0)
        prev1[...] = jnp.zeros_like(prev1)
        prev2[...] = jnp.zeros_like(prev2)

    for g in range(A_GROUPS):
        for hh in range(2):
            biasp[g, hh, :, 0:ATT_BLK] = jnp.where(first, NEG_INF, bias_ref[g, hh, :, 0:ATT_BLK])
            biasp[g, hh, :, ATT_BLK:2 * ATT_BLK] = bias_ref[g, hh, :, ATT_BLK:2 * ATT_BLK]

    lane = lax.broadcasted_iota(jnp.int32, (1, LANES), 1)
    lo = lane < A_HEAD_DIM
    qmask = (jnp.where(lo, 1.0, 0.0).astype(MXU_DTYPE), jnp.where(lo, 0.0, 1.0).astype(MXU_DTYPE))

    def keys_values(g, rc, prev_ref, rp, lanes):
        if prev_ref is None:
            return qkv_ref[g, 0, 0, rp:rp + 2 * ATT_BLK, lanes]
        off = lanes.start - LANES
        return jnp.concatenate([prev_ref[rp:rp + ATT_BLK, off:off + LANES],
                                qkv_ref[g, 0, 0, rc:rc + ATT_BLK, lanes]], axis=0)

    def scores(g, rc, prev_ref, rp):
        q = qkv_ref[g, 0, 0, rc:rc + ATT_BLK, 0:LANES]
        k = keys_values(g, rc, prev_ref, rp, slice(LANES, 2 * LANES))
        bias = bias_ref if prev_ref is None else biasp
        return [_mm_nt(q * qmask[hh], k) + bias[g, hh] for hh in range(2)]

    def attend(g, rc, prev_ref, rp, s):
        v = keys_values(g, rc, prev_ref, rp, slice(2 * LANES, 3 * LANES))
        accs, dens, ms = [], [], []
        for sh in s:
            m = jnp.max(sh, axis=-1, keepdims=True)
            p = jnp.exp2(sh - m)
            dens.append(jnp.sum(p, axis=-1, keepdims=True))
            accs.append(_mm(p.astype(MXU_DTYPE), v))
            ms.append(m)
        den = jnp.where(lo, dens[0], dens[1])
        o = jnp.where(lo, accs[0], accs[1]) * (1.0 / den)
        lse = jnp.where(lo, ms[0], ms[1]) + jnp.log2(den)
        return o, jnp.broadcast_to(lse, o.shape)

    def store(o_ref, l_ref, rows):
        def post(o, lse):
            o_ref[rows, :] = o
            l_ref[rows, :] = lse
        return post

    def merge(rows):
        def post(o0, lse0):
            lse1, lse2 = l1[rows, :], l2[rows, :]
            mx = jnp.maximum(jnp.maximum(lse0, lse1), lse2)
            e0, e1, e2 = jnp.exp2(lse0 - mx), jnp.exp2(lse1 - mx), jnp.exp2(lse2 - mx)
            mix = (e0 * o0 + e1 * o1[rows, :] + e2 * o2[rows, :]) * (1.0 / (e0 + e1 + e2))
            out_ref[0, rows, :] = mix.astype(out_ref.dtype)
        return post

    units = []
    d2 = A_CONFIGS[2][1]
    for u in range(ATT_UNITS):
        units.append((2, u * ATT_BLK, prev2, u * ATT_BLK, store(o2, l2, pl.ds(u, ATT_BLK, stride=d2))))
    d1 = A_CONFIGS[1][1]
    per = ATT_UNITS // d1
    for u in range(ATT_UNITS):
        r, q = divmod(u, per)
        rc = u * ATT_BLK
        post = store(o1, l1, pl.ds(q * ATT_BLK * d1 + r, ATT_BLK, stride=d1))
        if q == 0:
            units.append((1, rc, prev1, (r * per + per - 1) * ATT_BLK, post))
        else:
            units.append((1, rc, None, rc - ATT_BLK, post))
    for u in range(ATT_UNITS):
        rc = u * ATT_BLK
        post = merge(slice(rc, rc + ATT_BLK))
        units.append((0, rc, prev0, 0, post) if u == 0 else (0, rc, None, rc - ATT_BLK, post))

    pending = [scores(*un[:4]) for un in units[:ATT_PIPELINE]]
    for idx, un in enumerate(units):
        if idx + ATT_PIPELINE < len(units):
            pending.append(scores(*units[idx + ATT_PIPELINE][:4]))
        o, lse = attend(*un[:4], pending.pop(0))
        un[4](o, lse)

    prev0[...] = qkv_ref[0, 0, 0, ATT_TILE - ATT_BLK:ATT_TILE, LANES:3 * LANES]
    prev1[...] = qkv_ref[1, 0, 0, :, LANES:3 * LANES]
    prev2[...] = qkv_ref[2, 0, 0, :, LANES:3 * LANES]


def _attention(qkv, bias):
    _, bsz, _, s, _ = qkv.shape
    return pl.pallas_call(
        _attn_kernel,
        grid=(bsz, N_PAIRS, s // ATT_TILE),
        in_specs=[
            pl.BlockSpec((A_GROUPS, 1, 1, ATT_TILE, PAIR_W), lambda b, p, i: (0, b, p, i, 0)),
            pl.BlockSpec((A_GROUPS, 2, ATT_BLK, 2 * ATT_BLK), lambda b, p, i: (0, p, 0, 0)),
        ],
        out_specs=pl.BlockSpec((1, ATT_TILE, LANES), lambda b, p, i: (b, i, p)),
        out_shape=jax.ShapeDtypeStruct((bsz, s, A_HEADS * A_HEAD_DIM), MXU_DTYPE),
        scratch_shapes=[
            pltpu.VMEM((ATT_BLK, 2 * LANES), MXU_DTYPE),
            pltpu.VMEM((ATT_TILE, 2 * LANES), MXU_DTYPE),
            pltpu.VMEM((ATT_TILE, 2 * LANES), MXU_DTYPE),
            pltpu.VMEM((ATT_TILE, LANES), f32),
            pltpu.VMEM((ATT_TILE, LANES), f32),
            pltpu.VMEM((ATT_TILE, LANES), f32),
            pltpu.VMEM((ATT_TILE, LANES), f32),
            pltpu.VMEM((A_GROUPS, 2, ATT_BLK, 2 * ATT_BLK), f32),
        ],
        compiler_params=pltpu.CompilerParams(
            dimension_semantics=("parallel", "parallel", "arbitrary"),
            vmem_limit_bytes=VMEM_LIMIT),
        name="dilated_attn",
    )(qkv, bias)


def _ffn_kernel(*refs, final, mixer):
    refs = list(refs)
    x_ref, halo_ref = refs[:2]
    del refs[:2]
    if mixer:
        o_ref, ohalo_ref, wout_ref = refs[:3]
        del refs[:3]
    mod_ref, gam_ref, wup_ref, cw_ref, cb_ref, wdn_ref = refs[:6]
    del refs[:6]
    if final:
        gfin_ref = refs.pop(0)
    out_ref, h0_ref, h1_ref, u0_ref, u1_ref, act_ref, acc_ref = refs[:7]
    if mixer:
        x1_ref = refs[7]
    h_refs, u_refs = (h0_ref, h1_ref), (u0_ref, u1_ref)
    gam, scale, shift = gam_ref[...], mod_ref[0, 4:5, :], mod_ref[0, 3:4, :]
    d_ff = wdn_ref.shape[0]
    nchunk = d_ff // FFN_TF
    nsub = x_ref.shape[1] // FFN_TM

    def stream(k):
        rows = slice(k * FFN_TM, (k + 1) * FFN_TM)
        if not mixer:
            return x_ref[0, rows, :]
        x1 = x_ref[0, rows, :] + mod_ref[0, 2:3, :] * _mm(o_ref[0, rows, :], wout_ref[...])
        x1_ref[rows, :] = x1
        return x1

    def stream_halo(k):
        if k > 0:
            src = x1_ref if mixer else x_ref.at[0]
            return src[k * FFN_TM - HALO:k * FFN_TM, :]
        if not mixer:
            return halo_ref[0]
        y = _mm(ohalo_ref[0], wout_ref[...])[O_HALO - HALO:, :]
        return halo_ref[0] + mod_ref[0, 2:3, :] * y

    def normalize(k):
        h_ref = h_refs[k % 2]
        halo = _modnorm(stream_halo(k), gam, scale, shift)
        if k == 0:
            halo = jnp.where(pl.program_id(1) == 0, 0.0, halo)
        h_ref[0:HALO, :] = halo.astype(h_ref.dtype)
        h_ref[HALO:HALO + FFN_TM, :] = _modnorm(stream(k), gam, scale, shift).astype(h_ref.dtype)

    def halves(c):
        return [slice(half * d_ff + c * FFN_TF, half * d_ff + (c + 1) * FFN_TF) for half in range(2)]

    def up(job):
        k, c = divmod(job, nchunk)
        for half, cols in enumerate(halves(c)):
            u_refs[job % 2][:, half * FFN_TF:(half + 1) * FFN_TF] = _mm(h_refs[k % 2][...], wup_ref[:, cols])

    def conv_act(job):
        c = job % nchunk
        u_ref = u_refs[job % 2]
        for rb in range(FFN_TM // FFN_ROWS):
            base = HALO + rb * FFN_ROWS
            ab = []
            for half, cols in enumerate(halves(c)):
                v = cb_ref[:, cols]
                for t in range(CONV_W):
                    off = base - (CONV_W - 1) + t
                    v = v + cw_ref[t:t + 1, cols] * u_ref[off:off + FFN_ROWS, half * FFN_TF:(half + 1) * FFN_TF]
                ab.append(v)
            act_ref[rb * FFN_ROWS:(rb + 1) * FFN_ROWS, c * FFN_TF:(c + 1) * FFN_TF] = (
                _silu(ab[0]) * ab[1]).astype(act_ref.dtype)

    def down(first, last):
        cols = slice(first * FFN_TF, last * FFN_TF)
        return _mm(act_ref[:, cols], wdn_ref[cols, :])

    def finish(k, start):
        ffn = down(start, nchunk)
        if start > 0:
            ffn = ffn + acc_ref[...]
        rows = slice(k * FFN_TM, (k + 1) * FFN_TM)
        y = (x1_ref[rows, :] if mixer else x_ref[0, rows, :]) + mod_ref[0, 5:6, :] * ffn
        if final:
            ms = jnp.mean(y * y, axis=-1, keepdims=True)
            y = y * lax.rsqrt(ms + EPS) * gfin_ref[...]
        out_ref[0, rows, :] = y

    normalize(0)
    up(0)
    start, pending = 0, None
    for job in range(nsub * nchunk):
        k, c = divmod(job, nchunk)
        if c == 0:
            start = 0
        if c == nchunk - FFN_NORM_AHEAD and k + 1 < nsub:
            normalize(k + 1)
        if job + 1 < nsub * nchunk:
            up(job + 1)
        if pending is not None:
            if pending[0] == 0:
                acc_ref[...] = down(*pending)
            else:
                acc_ref[...] += down(*pending)
            pending = None
        if c + 1 - start == FFN_DOWN_GROUP and c + 1 < nchunk:
            pending, start = (start, c + 1), c + 1
        conv_act(job)
        if c == nchunk - 1:
            finish(k, start)


def _conv_ffn(x, mod, gamma, w_up, conv_w, conv_b, w_down, gamma_final=None, mixer=None):
    bsz, s, d = x.shape
    d_ff = w_down.shape[0]
    final = gamma_final is not None
    rows = FFN_SUB * FFN_TM
    tiles_per_halo = rows // HALO
    resident = dict(pipeline_mode=pl.Buffered(1))
    in_specs = [
        pl.BlockSpec((1, rows, d), lambda b, i: (b, i, 0)),
        pl.BlockSpec((1, HALO, d), lambda b, i: (b, jnp.maximum(i * tiles_per_halo - 1, 0), 0)),
    ]
    args = [x, x]
    if mixer is not None:
        o, w_out = mixer
        k = o.shape[-1]
        in_specs += [
            pl.BlockSpec((1, rows, k), lambda b, i: (b, i, 0)),
            pl.BlockSpec((1, O_HALO, k), lambda b, i: (b, jnp.maximum(i * (rows // O_HALO) - 1, 0), 0)),
            pl.BlockSpec((k, d), lambda b, i: (0, 0), **resident),
        ]
        args += [o, o, w_out.astype(MXU_DTYPE)]
    in_specs += [
        pl.BlockSpec((1, 6, d), lambda b, i: (b, 0, 0)),
        pl.BlockSpec((1, d), lambda b, i: (0, 0)),
        pl.BlockSpec((d, 2 * d_ff), lambda b, i: (0, 0), **resident),
        pl.BlockSpec((CONV_W, 2 * d_ff), lambda b, i: (0, 0)),
        pl.BlockSpec((1, 2 * d_ff), lambda b, i: (0, 0)),
        pl.BlockSpec((d_ff, d), lambda b, i: (0, 0), **resident),
    ]
    args += [mod, gamma, w_up.astype(MXU_DTYPE), conv_w, conv_b[None, :], w_down.astype(MXU_DTYPE)]
    if final:
        in_specs.append(pl.BlockSpec((1, d), lambda b, i: (0, 0)))
        args.append(gamma_final)
    return pl.pallas_call(
        functools.partial(_ffn_kernel, final=final, mixer=mixer is not None),
        grid=(bsz, s // rows),
        in_specs=in_specs,
        out_specs=pl.BlockSpec((1, rows, d), lambda b, i: (b, i, 0)),
        out_shape=jax.ShapeDtypeStruct((bsz, s, d), f32),
        scratch_shapes=[
            pltpu.VMEM((FFN_TM + HALO, d), MXU_DTYPE),
            pltpu.VMEM((FFN_TM + HALO, d), MXU_DTYPE),
            pltpu.VMEM((FFN_TM + HALO, 2 * FFN_TF), f32),
            pltpu.VMEM((FFN_TM + HALO, 2 * FFN_TF), f32),
            pltpu.VMEM((FFN_TM, d_ff), MXU_DTYPE),
            pltpu.VMEM((FFN_TM, d), f32),
        ] + ([pltpu.VMEM((rows, d), f32)] if mixer is not None else []),
        compiler_params=pltpu.CompilerParams(
            dimension_semantics=("parallel", "parallel"), vmem_limit_bytes=VMEM_LIMIT),
        name="conv_ffn_final" if final else "conv_ffn",
    )(*args)


def _gla_kernel(x_ref, mod_ref, gam_ref, win_ref, wglr_ref, wgate_ref, bgate_ref, gnorm_ref, wout_ref,
                out_ref, h_ref, proj0_ref, proj1_ref, gk0_ref, gk1_ref, og0_ref, og1_ref, state_ref):
    @pl.when(pl.program_id(1) == 0)
    def _():
        state_ref[...] = jnp.zeros_like(state_ref)

    proj_refs, gk_refs, og_refs = (proj0_ref, proj1_ref), (gk0_ref, gk1_ref), (og0_ref, og1_ref)
    nsub = x_ref.shape[1] // GLA_TM
    nchunk = GLA_TM // B_CHUNK
    n_main = win_ref.shape[1]

    ri = lax.broadcasted_iota(jnp.int32, (B_CHUNK, B_CHUNK), 0)
    ci = lax.broadcasted_iota(jnp.int32, (B_CHUNK, B_CHUNK), 1)
    causal = ci <= ri
    tri = causal.astype(MXU_DTYPE)
    gnorm = gnorm_ref[...]
    qscale = B_DK ** -0.5
    r_off = 2 * B_QK + B_V
    heads = [slice(hd * B_DK, (hd + 1) * B_DK) for hd in range(B_HEADS)]

    def project_steps(k):
        rows = slice(k * GLA_TM, (k + 1) * GLA_TM)

        def normalize():
            h = _modnorm(x_ref[0, rows, :], gam_ref[...], mod_ref[0, 1:2, :], mod_ref[0, 0:1, :])
            h_ref[...] = h.astype(h_ref.dtype)

        def piece(cols):
            def run():
                proj_refs[k % 2][:, cols] = _mm(h_ref[...], win_ref[:, cols])
            return run

        def gate():
            glr = _mm(h_ref[...], wglr_ref[...]).astype(MXU_DTYPE)
            z = _mm(glr, wgate_ref[...]) + bgate_ref[...]
            gk_refs[k % 2][...] = (jnp.minimum(z, 0.0) - jnp.log1p(jnp.exp(-jnp.abs(z)))) * (1.0 / B_TAU)

        pieces = [piece(slice(c0, c0 + GLA_PROJ_COLS)) for c0 in range(0, n_main, GLA_PROJ_COLS)]
        return [normalize] + pieces + [gate]

    def local(k, c):
        proj_ref, gk_ref = proj_refs[k % 2], gk_refs[k % 2]
        rows = slice(c * B_CHUNK, (c + 1) * B_CHUNK)
        gk = gk_ref[rows, :]
        g_hi = gk.astype(MXU_DTYPE)
        rem = gk - g_hi.astype(f32)
        g_mid = rem.astype(MXU_DTYPE)
        g_lo = (rem - g_mid.astype(f32)).astype(MXU_DTYPE)
        bcum = _mm(tri, g_hi) + _mm(tri, g_mid) + _mm(tri, g_lo)
        blast = bcum[B_CHUNK - 1:B_CHUNK, :]
        q_t = ((proj_ref[rows, 0:B_QK] * qscale) * jnp.exp(bcum)).astype(MXU_DTYPE)
        kk = proj_ref[rows, B_QK:2 * B_QK]
        k_t = (kk * jnp.exp(-bcum)).astype(MXU_DTYPE)
        k_d = (kk * jnp.exp(blast - bcum)).astype(MXU_DTYPE)
        v = [proj_ref[rows, 2 * B_QK + hd * B_DV:2 * B_QK + (hd + 1) * B_DV].astype(MXU_DTYPE)
             for hd in range(B_HEADS)]
        a = [jnp.where(causal, _mm_nt(q_t[:, ks], k_t[:, ks]), 0.0).astype(MXU_DTYPE) for ks in heads]
        kv = [_mm_tn(k_d[:, ks], v[hd]) for hd, ks in enumerate(heads)]
        o_intra = [_mm(a[hd], v[hd]) for hd in range(B_HEADS)]
        return q_t, blast, kv, o_intra

    def recur(k, c, loc, state):
        q_t, blast, kv, o_intra = loc
        rows = slice(c * B_CHUNK, (c + 1) * B_CHUNK)
        new_state = []
        for hd, ks in enumerate(heads):
            o = o_intra[hd] + _mm(q_t[:, ks], state[hd].astype(MXU_DTYPE))
            decay = jnp.exp(jnp.broadcast_to(blast[:, ks], (B_DK, B_DK)).T)
            new_state.append(jnp.concatenate([decay] * (B_DV // B_DK), axis=1) * state[hd] + kv[hd])
            o = o * lax.rsqrt(jnp.mean(o * o, axis=-1, keepdims=True) + EPS) * gnorm
            r = proj_refs[k % 2][rows, r_off + hd * B_DV:r_off + (hd + 1) * B_DV]
            og_refs[k % 2][rows, hd * B_DV:(hd + 1) * B_DV] = (o * _silu(r)).astype(og_refs[k % 2].dtype)
        return new_state

    for step in project_steps(0):
        step()
    state = [state_ref[hd] for hd in range(B_HEADS)]
    for k in range(nsub):
        ahead = project_steps(k + 1) if k + 1 < nsub else []
        loc = local(k, 0)
        for c in range(nchunk):
            nxt = local(k, c + 1) if c + 1 < nchunk else None
            state = recur(k, c, loc, state)
            loc = nxt
            take = -(-len(ahead) // (nchunk - c))
            for step in ahead[:take]:
                step()
            ahead = ahead[take:]
        rows = slice(k * GLA_TM, (k + 1) * GLA_TM)
        out_ref[0, rows, :] = x_ref[0, rows, :] + mod_ref[0, 2:3, :] * _mm(og_refs[k % 2][...], wout_ref[...])
    for hd in range(B_HEADS):
        state_ref[hd] = state[hd]


def _gla_layer(x, mod, gamma, w_in, w_gate, b_gate, g_norm, w_out):
    bsz, s, d = x.shape
    glr0 = 2 * B_QK + B_V
    w_main = jnp.concatenate([w_in[:, :glr0], w_in[:, glr0 + B_GATE_RANK:]], axis=1).astype(MXU_DTYPE)
    w_glr = jnp.pad(w_in[:, glr0:glr0 + B_GATE_RANK], ((0, 0), (0, LANES - B_GATE_RANK))).astype(MXU_DTYPE)
    w_gate_p = jnp.pad(w_gate, ((0, LANES - B_GATE_RANK), (0, 0))).astype(MXU_DTYPE)
    n_main = w_main.shape[1]
    rows = GLA_SUB * GLA_TM
    const = lambda b, i: (0, 0)
    resident = dict(pipeline_mode=pl.Buffered(1))
    return pl.pallas_call(
        _gla_kernel,
        grid=(bsz, s // rows),
        in_specs=[
            pl.BlockSpec((1, rows, d), lambda b, i: (b, i, 0)),
            pl.BlockSpec((1, 6, d), lambda b, i: (b, 0, 0)),
            pl.BlockSpec((1, d), const),
            pl.BlockSpec((d, n_main), const, **resident),
            pl.BlockSpec((d, LANES), const),
            pl.BlockSpec((LANES, B_QK), const),
            pl.BlockSpec((1, B_QK), const),
            pl.BlockSpec((1, B_DV), const),
            pl.BlockSpec((B_V, d), const, **resident),
        ],
        out_specs=pl.BlockSpec((1, rows, d), lambda b, i: (b, i, 0)),
        out_shape=jax.ShapeDtypeStruct((bsz, s, d), f32),
        scratch_shapes=[
            pltpu.VMEM((GLA_TM, d), MXU_DTYPE),
            pltpu.VMEM((GLA_TM, n_main), f32),
            pltpu.VMEM((GLA_TM, n_main), f32),
            pltpu.VMEM((GLA_TM, B_QK), f32),
            pltpu.VMEM((GLA_TM, B_QK), f32),
            pltpu.VMEM((GLA_TM, B_V), MXU_DTYPE),
            pltpu.VMEM((GLA_TM, B_V), MXU_DTYPE),
            pltpu.VMEM((B_HEADS, B_DK, B_DV), f32),
        ],
        compiler_params=pltpu.CompilerParams(
            dimension_semantics=("parallel", "arbitrary"), vmem_limit_bytes=VMEM_LIMIT),
        name="gla_layer",
    )(x, mod, gamma, w_main, w_glr, w_gate_p, b_gate[None, :], g_norm[None, :], w_out.astype(MXU_DTYPE))


def _qkv_weight(w_in):
    width = A_HEADS * A_HEAD_DIM
    col_scale = jnp.tile(jnp.repeat(jnp.array([A_HEAD_DIM ** -0.5 * LOG2E, 1.0, 1.0], f32), width), A_GROUPS)
    return (w_in * col_scale[None, :]).astype(MXU_DTYPE)


def kernel(x, c, w_in_a, w_out_a, rel_bias, w_in_b, w_gate_b, b_gate_b, gnorm_b, w_out_b, norm_mix, norm_ffn, w_ada, b_ada, w_up, conv_w, conv_b, w_down, norm_final):
    depth = w_ada.shape[0]
    mod = _adaln(c, w_ada, b_ada)
    bias = _rel_bias(rel_bias)
    for i in range(depth):
        gam_mix = norm_mix[i][None, :]
        j = i // 2
        mixer = None
        if i % 2 == 0:
            qkv = _qkv_proj(x, mod[i], gam_mix, _qkv_weight(w_in_a[j]))
            mixer = (_attention(qkv, bias), w_out_a[j])
        else:
            x = _gla_layer(x, mod[i], gam_mix, w_in_b[j], w_gate_b[j], b_gate_b[j], gnorm_b[j], w_out_b[j])
        last = i == depth - 1
        x = _conv_ffn(x, mod[i], norm_ffn[i][None, :], w_up[i], conv_w[i], conv_b[i], w_down[i],
                      gamma_final=norm_final[None, :] if last else None, mixer=mixer)
    return x
```

```python
import functools
import math

import jax
import jax.numpy as jnp
from jax import lax
from jax.experimental import pallas as pl
from jax.experimental.pallas import tpu as pltpu

A_CONFIGS = ((128, 1), (512, 4), (2048, 16))
A_GROUPS = len(A_CONFIGS)
A_HEADS = 16
A_HEAD_DIM = 64
N_BUCKETS = 32
MAX_DISTANCE = 2048
B_HEADS = 4
B_DK = 128
B_DV = 256
B_QK = B_HEADS * B_DK
B_V = B_HEADS * B_DV
B_GATE_RANK = 16
B_TAU = 16.0
B_CHUNK = 64
CONV_W = 3
EPS = 1e-6
NEG_INF = -1e30
LOG2E = math.log2(math.e)

LANES = 128
MXU_DTYPE = jnp.bfloat16
ATT_BLK = 128
ATT_TILE = 2048
ATT_UNITS = ATT_TILE // ATT_BLK
ATT_PIPELINE = 3
PAIR_W = 3 * LANES
N_PAIRS = A_HEADS // 2
QKV_ROWS = 1024
PROBE_ROWS = 16
FFN_TM = 512
FFN_SUB = 1
FFN_NORM_AHEAD = 3
FFN_TF = 256
FFN_ROWS = 256
FFN_DOWN_GROUP = 2
HALO = 8
O_HALO = 16
GLA_TM = 512
GLA_SUB = 2
GLA_PROJ_COLS = 256
VMEM_LIMIT = 56 * 1024 * 1024

f32 = jnp.float32


def _mm(a, b):
    return jnp.dot(a, b, preferred_element_type=f32)


def _mm_nt(a, b):
    return lax.dot_general(a, b, (((1,), (1,)), ((), ())), preferred_element_type=f32)


def _mm_tn(a, b):
    return lax.dot_general(a, b, (((0,), (0,)), ((), ())), preferred_element_type=f32)


def _modnorm(x, gamma, scale, shift):
    ms = jnp.mean(x * x, axis=-1, keepdims=True)
    y = x * lax.rsqrt(ms + EPS) * gamma
    return y * (1.0 + scale) + shift


def _silu(x):
    return x * (1.0 / (1.0 + jnp.exp(-x)))


def _adaln_kernel(c_ref, w_ref, b_ref, o_ref):
    s = _silu(c_ref[...]).astype(MXU_DTYPE)
    o_ref[0] = _mm(s, w_ref[0].astype(MXU_DTYPE)) + b_ref[0]


def _adaln(c, w_ada, b_ada):
    depth, d, n = w_ada.shape
    bsz = c.shape[0]
    rows = 8 * pl.cdiv(bsz, 8)
    c_pad = jnp.pad(c, ((0, rows - bsz), (0, 0)))
    out = pl.pallas_call(
        _adaln_kernel,
        grid=(depth, n // d),
        in_specs=[
            pl.BlockSpec((rows, d), lambda l, j: (0, 0)),
            pl.BlockSpec((1, d, d), lambda l, j: (l, 0, j)),
            pl.BlockSpec((1, 1, d), lambda l, j: (l, 0, j)),
        ],
        out_specs=pl.BlockSpec((1, rows, d), lambda l, j: (l, 0, j)),
        out_shape=jax.ShapeDtypeStruct((depth, rows, n), f32),
        compiler_params=pltpu.CompilerParams(
            dimension_semantics=("parallel", "parallel"), vmem_limit_bytes=VMEM_LIMIT),
        name="adaln",
    )(c_pad, w_ada, b_ada.reshape(depth, 1, n))
    return out[:, :bsz].reshape(depth, bsz, n // d, d)


def _qkv_kernel(x_ref, mod_ref, gam_ref, wq_ref, wk_ref, wv_ref, o_ref, ha_ref, hb_ref, sa_ref, sb_ref, *,
                dilations, tiles_per_group):
    j = pl.program_id(2)
    g = j // tiles_per_group
    t = j % tiles_per_group
    nlb = sa_ref.shape[0]
    chunks_per_step = ATT_UNITS // tiles_per_group
    h_refs, stage_refs = [ha_ref, hb_ref], [sa_ref, sb_ref]

    @pl.when(j == 0)
    def _():
        for c in range(ATT_UNITS):
            rows = slice(c * ATT_BLK, (c + 1) * ATT_BLK)
            hn = _modnorm(x_ref[0, rows, :], gam_ref[...], mod_ref[0, 1:2, :], mod_ref[0, 0:1, :])
            ha_ref[rows, :] = hn.astype(ha_ref.dtype)
            for lb in range(nlb):
                sa_ref[lb, rows, :] = hn[:, lb * LANES:(lb + 1) * LANES]

    def deinterleave_chunk(gi, cc):
        d, dp = dilations[gi], dilations[gi - 1]
        rel = d // dp
        per = ATT_UNITS // d
        c = t * chunks_per_step + cc
        r, q = c // per, c % per
        start = (r % dp) * (ATT_TILE // dp) + q * (ATT_BLK * rel) + r // dp
        rows = pl.ds(start, ATT_BLK, stride=rel)
        r0 = pl.multiple_of(c * ATT_BLK, ATT_BLK)
        parts = [stage_refs[(gi - 1) % 2][lb, rows, :] for lb in range(nlb)]
        if gi + 1 < len(dilations):
            for lb in range(nlb):
                stage_refs[gi % 2][lb, pl.ds(r0, ATT_BLK), :] = parts[lb]
        dst_ref = h_refs[gi % 2]
        dst_ref[pl.ds(r0, ATT_BLK), :] = jnp.concatenate(parts, axis=1).astype(dst_ref.dtype)
        probe = dst_ref[pl.ds(r0, PROBE_ROWS), 0:LANES]
        return (probe != probe) & (probe == probe)

    def project_chunk(src_ref, rc, never):
        rows = slice(rc * QKV_ROWS, (rc + 1) * QKV_ROWS)
        lhs = src_ref[rows, :]
        for k, w_ref in enumerate((wq_ref, wk_ref, wv_ref)):
            res = _mm(lhs, w_ref[...]).astype(o_ref.dtype)
            lanes = slice(k * LANES, (k + 1) * LANES)
            o_ref[0, 0, 1, rows, lanes] = res[:, LANES:]
            if never is not None and k == 0:
                top = rc * QKV_ROWS + PROBE_ROWS
                o_ref[0, 0, 0, rc * QKV_ROWS:top, lanes] = jnp.where(
                    never, jnp.zeros_like(res[:PROBE_ROWS, :LANES]), res[:PROBE_ROWS, :LANES])
                o_ref[0, 0, 0, top:(rc + 1) * QKV_ROWS, lanes] = res[PROBE_ROWS:, :LANES]
            else:
                o_ref[0, 0, 0, rows, lanes] = res[:, :LANES]

    assert dilations[0] == 1
    assert all(d % dp == 0 for dp, d in zip(dilations, dilations[1:]))
    n_mm = ATT_TILE // QKV_ROWS
    for gi in range(len(dilations)):
        @pl.when(g == gi)
        def _(gi=gi):
            slots = max(n_mm - 1, 1)
            shares = [list(range(chunks_per_step))[sl::slots] for sl in range(slots)] + [[]]
            for rc in range(n_mm):
                never = None
                if gi + 1 < len(dilations):
                    for cc in shares[rc]:
                        probe = deinterleave_chunk(gi + 1, cc)
                        never = probe if never is None else never | probe
                project_chunk(h_refs[gi % 2], rc, never)


def _qkv_proj(x, mod, gamma, w):
    bsz, s, d = x.shape
    dilations = tuple(dl for _, dl in A_CONFIGS)
    tn = 2 * LANES
    tiles_per_group = A_HEADS * A_HEAD_DIM // tn
    kern = functools.partial(_qkv_kernel, dilations=dilations, tiles_per_group=tiles_per_group)

    def w_spec(k):
        return pl.BlockSpec(
            (d, tn), lambda b, i, j: (0, (j // tiles_per_group * 3 + k) * tiles_per_group + j % tiles_per_group))

    return pl.pallas_call(
        kern,
        grid=(bsz, s // ATT_TILE, A_GROUPS * tiles_per_group),
        in_specs=[
            pl.BlockSpec((1, ATT_TILE, d), lambda b, i, j: (b, i, 0)),
            pl.BlockSpec((1, 6, d), lambda b, i, j: (b, 0, 0)),
            pl.BlockSpec((1, d), lambda b, i, j: (0, 0)),
            w_spec(0), w_spec(1), w_spec(2),
        ],
        out_specs=pl.BlockSpec(
            (1, 1, 2, ATT_TILE, PAIR_W),
            lambda b, i, j: (j // tiles_per_group, b, j % tiles_per_group, i, 0)),
        out_shape=jax.ShapeDtypeStruct((A_GROUPS, bsz, N_PAIRS, s, PAIR_W), MXU_DTYPE),
        scratch_shapes=[pltpu.VMEM((ATT_TILE, d), MXU_DTYPE),
                        pltpu.VMEM((ATT_TILE, d), MXU_DTYPE),
                        pltpu.VMEM((d // LANES, ATT_TILE, LANES), f32),
                        pltpu.VMEM((d // LANES, ATT_TILE, LANES), f32)],
        compiler_params=pltpu.CompilerParams(
            dimension_semantics=("parallel", "parallel", "arbitrary"),
            vmem_limit_bytes=VMEM_LIMIT),
        name="qkv_proj",
    )(x, mod, gamma, w, w, w)


def _bias_kernel(tab_ref, bkt_ref, o_ref):
    bkt = bkt_ref[0]
    hits = [bkt == k for k in range(N_BUCKETS)]
    for h in range(A_HEADS):
        col = pl.program_id(0) * A_HEADS + h
        acc = jnp.full(bkt.shape, NEG_INF, f32)
        for k in range(N_BUCKETS):
            acc = jnp.where(hits[k], tab_ref[k, col] * LOG2E, acc)
        o_ref[0, h] = acc


def _t5_bucket(dist):
    max_exact = N_BUCKETS // 2
    n = jnp.maximum(dist, max_exact).astype(f32)
    large = max_exact + (jnp.log(n / max_exact) / math.log(MAX_DISTANCE / max_exact)
                         * (N_BUCKETS - max_exact)).astype(jnp.int32)
    large = jnp.minimum(large, N_BUCKETS - 1)
    return jnp.where(dist < max_exact, dist, large)


def _rel_bias(rel_bias):
    qi = jnp.arange(ATT_BLK)[:, None]
    ki = jnp.arange(2 * ATT_BLK)[None, :]
    steps = qi + ATT_BLK - ki
    band = (steps >= 0) & (steps <= ATT_BLK)
    bucket = jnp.stack([
        jnp.where(band, _t5_bucket(jnp.clip(steps, 0, ATT_BLK) * dl), -1)
        for _, dl in A_CONFIGS]).astype(jnp.int32)
    return pl.pallas_call(
        _bias_kernel,
        grid=(A_GROUPS,),
        in_specs=[
            pl.BlockSpec(memory_space=pltpu.SMEM),
            pl.BlockSpec((1, ATT_BLK, 2 * ATT_BLK), lambda g: (g, 0, 0)),
        ],
        out_specs=pl.BlockSpec((1, A_HEADS, ATT_BLK, 2 * ATT_BLK), lambda g: (g, 0, 0, 0)),
        out_shape=jax.ShapeDtypeStruct((A_GROUPS, A_HEADS, ATT_BLK, 2 * ATT_BLK), f32),
        name="rel_bias",
    )(rel_bias, bucket)


def _attn_kernel(qkv_ref, bias_ref, out_ref, prev0, prev1, prev2, o1, l1, n1, o2, l2, n2, biasp):
    first = pl.program_id(2) == 0

    @pl.when(first)
    def _():
        prev0[...] = jnp.zeros_like(prev0)
        prev1[...] = jnp.zeros_like(prev1)
        prev2[...] = jnp.zeros_like(prev2)

    for g in range(A_GROUPS):
        for hh in range(2):
            biasp[g, hh, :, 0:ATT_BLK] = jnp.where(first, NEG_INF, bias_ref[g, hh, :, 0:ATT_BLK])
            biasp[g, hh, :, ATT_BLK:2 * ATT_BLK] = bias_ref[g, hh, :, ATT_BLK:2 * ATT_BLK]

    lane = lax.broadcasted_iota(jnp.int32, (1, LANES), 1)
    lo = lane < A_HEAD_DIM
    qmask = (jnp.where(lo, 1.0, 0.0).astype(MXU_DTYPE), jnp.where(lo, 0.0, 1.0).astype(MXU_DTYPE))

    def keys_values(g, rc, prev_ref, rp, lanes):
        if prev_ref is None:
            return qkv_ref[g, 0, 0, rp:rp + 2 * ATT_BLK, lanes]
        off = lanes.start - LANES
        return jnp.concatenate([prev_ref[rp:rp + ATT_BLK, off:off + LANES],
                                qkv_ref[g, 0, 0, rc:rc + ATT_BLK, lanes]], axis=0)

    def scores(g, rc, prev_ref, rp):
        q = qkv_ref[g, 0, 0, rc:rc + ATT_BLK, 0:LANES]
        k = keys_values(g, rc, prev_ref, rp, slice(LANES, 2 * LANES))
        bias = bias_ref if prev_ref is None else biasp
        return [_mm_nt(q * qmask[hh], k) + bias[g, hh] for hh in range(2)]

    def attend(g, rc, prev_ref, rp, s):
        v = keys_values(g, rc, prev_ref, rp, slice(2 * LANES, 3 * LANES))
        accs, dens, ms = [], [], []
        for sh in s:
            m = jnp.max(sh, axis=-1, keepdims=True)
            p = jnp.exp2(sh - m)
            dens.append(jnp.sum(p, axis=-1, keepdims=True))
            accs.append(_mm(p.astype(MXU_DTYPE), v))
            ms.append(m)
        acc = jnp.where(lo, accs[0], accs[1])
        return (acc, jnp.broadcast_to(jnp.where(lo, ms[0], ms[1]), acc.shape),
                jnp.broadcast_to(jnp.where(lo, dens[0], dens[1]), acc.shape))

    def store(a_ref, m_ref, d_ref, rows):
        def post(acc, m, den):
            a_ref[rows, :] = acc
            m_ref[rows, :] = m
            d_ref[rows, :] = den
        return post

    def merge(rows):
        def post(acc0, m0, den0):
            m1, m2 = l1[rows, :], l2[rows, :]
            mx = jnp.maximum(jnp.maximum(m0, m1), m2)
            e0, e1, e2 = jnp.exp2(m0 - mx), jnp.exp2(m1 - mx), jnp.exp2(m2 - mx)
            num = e0 * acc0 + e1 * o1[rows, :] + e2 * o2[rows, :]
            den = e0 * den0 + e1 * n1[rows, :] + e2 * n2[rows, :]
            out_ref[0, rows, :] = (num * (1.0 / den)).astype(out_ref.dtype)
        return post

    units = []
    d2 = A_CONFIGS[2][1]
    for u in range(ATT_UNITS):
        units.append((2, u * ATT_BLK, prev2, u * ATT_BLK, store(o2, l2, n2, pl.ds(u, ATT_BLK, stride=d2))))
    d1 = A_CONFIGS[1][1]
    per = ATT_UNITS // d1
    for u in range(ATT_UNITS):
        r, q = divmod(u, per)
        rc = u * ATT_BLK
        post = store(o1, l1, n1, pl.ds(q * ATT_BLK * d1 + r, ATT_BLK, stride=d1))
        if q == 0:
            units.append((1, rc, prev1, (r * per + per - 1) * ATT_BLK, post))
        else:
            units.append((1, rc, None, rc - ATT_BLK, post))
    for u in range(ATT_UNITS):
        rc = u * ATT_BLK
        post = merge(slice(rc, rc + ATT_BLK))
        units.append((0, rc, prev0, 0, post) if u == 0 else (0, rc, None, rc - ATT_BLK, post))

    pending = [scores(*un[:4]) for un in units[:ATT_PIPELINE]]
    for idx, un in enumerate(units):
        if idx + ATT_PIPELINE < len(units):
            pending.append(scores(*units[idx + ATT_PIPELINE][:4]))
        un[4](*attend(*un[:4], pending.pop(0)))

    prev0[...] = qkv_ref[0, 0, 0, ATT_TILE - ATT_BLK:ATT_TILE, LANES:3 * LANES]
    prev1[...] = qkv_ref[1, 0, 0, :, LANES:3 * LANES]
    prev2[...] = qkv_ref[2, 0, 0, :, LANES:3 * LANES]


def _attention(qkv, bias):
    _, bsz, _, s, _ = qkv.shape
    return pl.pallas_call(
        _attn_kernel,
        grid=(bsz, N_PAIRS, s // ATT_TILE),
        in_specs=[
            pl.BlockSpec((A_GROUPS, 1, 1, ATT_TILE, PAIR_W), lambda b, p, i: (0, b, p, i, 0)),
            pl.BlockSpec((A_GROUPS, 2, ATT_BLK, 2 * ATT_BLK), lambda b, p, i: (0, p, 0, 0)),
        ],
        out_specs=pl.BlockSpec((1, ATT_TILE, LANES), lambda b, p, i: (b, i, p)),
        out_shape=jax.ShapeDtypeStruct((bsz, s, A_HEADS * A_HEAD_DIM), MXU_DTYPE),
        scratch_shapes=[
            pltpu.VMEM((ATT_BLK, 2 * LANES), MXU_DTYPE),
            pltpu.VMEM((ATT_TILE, 2 * LANES), MXU_DTYPE),
            pltpu.VMEM((ATT_TILE, 2 * LANES), MXU_DTYPE),
            pltpu.VMEM((ATT_TILE, LANES), f32),
            pltpu.VMEM((ATT_TILE, LANES), f32),
            pltpu.VMEM((ATT_TILE, LANES), f32),
            pltpu.VMEM((ATT_TILE, LANES), f32),
            pltpu.VMEM((ATT_TILE, LANES), f32),
            pltpu.VMEM((ATT_TILE, LANES), f32),
            pltpu.VMEM((A_GROUPS, 2, ATT_BLK, 2 * ATT_BLK), f32),
        ],
        compiler_params=pltpu.CompilerParams(
            dimension_semantics=("parallel", "parallel", "arbitrary"),
            vmem_limit_bytes=VMEM_LIMIT),
        name="dilated_attn",
    )(qkv, bias)


def _ffn_kernel(*refs, final, mixer):
    refs = list(refs)
    x_ref, halo_ref = refs[:2]
    del refs[:2]
    if mixer:
        o_ref, ohalo_ref, wout_ref = refs[:3]
        del refs[:3]
    mod_ref, gam_ref, wup_ref, cw_ref, cb_ref, wdn_ref = refs[:6]
    del refs[:6]
    if final:
        gfin_ref = refs.pop(0)
    out_ref, h0_ref, h1_ref, u0_ref, u1_ref, act_ref, acc_ref = refs[:7]
    if mixer:
        x1_ref = refs[7]
    h_refs, u_refs = (h0_ref, h1_ref), (u0_ref, u1_ref)
    gam, scale, shift = gam_ref[...], mod_ref[0, 4:5, :], mod_ref[0, 3:4, :]
    d_ff = wdn_ref.shape[0]
    nchunk = d_ff // FFN_TF
    nsub = x_ref.shape[1] // FFN_TM

    def stream(k):
        rows = slice(k * FFN_TM, (k + 1) * FFN_TM)
        if not mixer:
            return x_ref[0, rows, :]
        x1 = x_ref[0, rows, :] + mod_ref[0, 2:3, :] * _mm(o_ref[0, rows, :], wout_ref[...])
        x1_ref[rows, :] = x1
        return x1

    def stream_halo(k):
        if k > 0:
            src = x1_ref if mixer else x_ref.at[0]
            return src[k * FFN_TM - HALO:k * FFN_TM, :]
        if not mixer:
            return halo_ref[0]
        y = _mm(ohalo_ref[0], wout_ref[...])[O_HALO - HALO:, :]
        return halo_ref[0] + mod_ref[0, 2:3, :] * y

    def normalize(k):
        h_ref = h_refs[k % 2]
        halo = _modnorm(stream_halo(k), gam, scale, shift)
        if k == 0:
            halo = jnp.where(pl.program_id(1) == 0, 0.0, halo)
        h_ref[0:HALO, :] = halo.astype(h_ref.dtype)
        h_ref[HALO:HALO + FFN_TM, :] = _modnorm(stream(k), gam, scale, shift).astype(h_ref.dtype)

    def halves(c):
        return [slice(half * d_ff + c * FFN_TF, half * d_ff + (c + 1) * FFN_TF) for half in range(2)]

    def up(job):
        k, c = divmod(job, nchunk)
        for half, cols in enumerate(halves(c)):
            u_refs[job % 2][:, half * FFN_TF:(half + 1) * FFN_TF] = _mm(h_refs[k % 2][...], wup_ref[:, cols])

    def conv_act(job):
        c = job % nchunk
        u_ref = u_refs[job % 2]
        for rb in range(FFN_TM // FFN_ROWS):
            base = HALO + rb * FFN_ROWS
            ab = []
            for half, cols in enumerate(halves(c)):
                v = cb_ref[:, cols]
                for t in range(CONV_W):
                    off = base - (CONV_W - 1) + t
                    v = v + cw_ref[t:t + 1, cols] * u_ref[off:off + FFN_ROWS, half * FFN_TF:(half + 1) * FFN_TF]
                ab.append(v)
            act_ref[rb * FFN_ROWS:(rb + 1) * FFN_ROWS, c * FFN_TF:(c + 1) * FFN_TF] = (
                _silu(ab[0]) * ab[1]).astype(act_ref.dtype)

    def down(first, last):
        cols = slice(first * FFN_TF, last * FFN_TF)
        return _mm(act_ref[:, cols], wdn_ref[cols, :])

    def finish(k, start):
        ffn = down(start, nchunk)
        if start > 0:
            ffn = ffn + acc_ref[...]
        rows = slice(k * FFN_TM, (k + 1) * FFN_TM)
        y = (x1_ref[rows, :] if mixer else x_ref[0, rows, :]) + mod_ref[0, 5:6, :] * ffn
        if final:
            ms = jnp.mean(y * y, axis=-1, keepdims=True)
            y = y * lax.rsqrt(ms + EPS) * gfin_ref[...]
        out_ref[0, rows, :] = y

    normalize(0)
    up(0)
    start, pending = 0, None
    for job in range(nsub * nchunk):
        k, c = divmod(job, nchunk)
        if c == 0:
            start = 0
        if c == nchunk - FFN_NORM_AHEAD and k + 1 < nsub:
            normalize(k + 1)
        if job + 1 < nsub * nchunk:
            up(job + 1)
        if pending is not None:
            if pending[0] == 0:
                acc_ref[...] = down(*pending)
            else:
                acc_ref[...] += down(*pending)
            pending = None
        if c + 1 - start == FFN_DOWN_GROUP and c + 1 < nchunk:
            pending, start = (start, c + 1), c + 1
        conv_act(job)
        if c == nchunk - 1:
            finish(k, start)


def _conv_ffn(x, mod, gamma, w_up, conv_w, conv_b, w_down, gamma_final=None, mixer=None):
    bsz, s, d = x.shape
    d_ff = w_down.shape[0]
    final = gamma_final is not None
    rows = FFN_SUB * FFN_TM
    tiles_per_halo = rows // HALO
    resident = dict(pipeline_mode=pl.Buffered(1))
    in_specs = [
        pl.BlockSpec((1, rows, d), lambda b, i: (b, i, 0)),
        pl.BlockSpec((1, HALO, d), lambda b, i: (b, jnp.maximum(i * tiles_per_halo - 1, 0), 0)),
    ]
    args = [x, x]
    if mixer is not None:
        o, w_out = mixer
        k = o.shape[-1]
        in_specs += [
            pl.BlockSpec((1, rows, k), lambda b, i: (b, i, 0)),
            pl.BlockSpec((1, O_HALO, k), lambda b, i: (b, jnp.maximum(i * (rows // O_HALO) - 1, 0), 0)),
            pl.BlockSpec((k, d), lambda b, i: (0, 0), **resident),
        ]
        args += [o, o, w_out.astype(MXU_DTYPE)]
    in_specs += [
        pl.BlockSpec((1, 6, d), lambda b, i: (b, 0, 0)),
        pl.BlockSpec((1, d), lambda b, i: (0, 0)),
        pl.BlockSpec((d, 2 * d_ff), lambda b, i: (0, 0), **resident),
        pl.BlockSpec((CONV_W, 2 * d_ff), lambda b, i: (0, 0)),
        pl.BlockSpec((1, 2 * d_ff), lambda b, i: (0, 0)),
        pl.BlockSpec((d_ff, d), lambda b, i: (0, 0), **resident),
    ]
    args += [mod, gamma, w_up.astype(MXU_DTYPE), conv_w, conv_b[None, :], w_down.astype(MXU_DTYPE)]
    if final:
        in_specs.append(pl.BlockSpec((1, d), lambda b, i: (0, 0)))
        args.append(gamma_final)
    return pl.pallas_call(
        functools.partial(_ffn_kernel, final=final, mixer=mixer is not None),
        grid=(bsz, s // rows),
        in_specs=in_specs,
        out_specs=pl.BlockSpec((1, rows, d), lambda b, i: (b, i, 0)),
        out_shape=jax.ShapeDtypeStruct((bsz, s, d), f32),
        scratch_shapes=[
            pltpu.VMEM((FFN_TM + HALO, d), MXU_DTYPE),
            pltpu.VMEM((FFN_TM + HALO, d), MXU_DTYPE),
            pltpu.VMEM((FFN_TM + HALO, 2 * FFN_TF), f32),
            pltpu.VMEM((FFN_TM + HALO, 2 * FFN_TF), f32),
            pltpu.VMEM((FFN_TM, d_ff), MXU_DTYPE),
            pltpu.VMEM((FFN_TM, d), f32),
        ] + ([pltpu.VMEM((rows, d), f32)] if mixer is not None else []),
        compiler_params=pltpu.CompilerParams(
            dimension_semantics=("parallel", "parallel"), vmem_limit_bytes=VMEM_LIMIT),
        name="conv_ffn_final" if final else "conv_ffn",
    )(*args)


def _gla_kernel(x_ref, mod_ref, gam_ref, win_ref, wglr_ref, wgate_ref, bgate_ref, gnorm_ref, wout_ref,
                out_ref, h_ref, proj0_ref, proj1_ref, gk0_ref, gk1_ref, og0_ref, og1_ref, state_ref):
    @pl.when(pl.program_id(1) == 0)
    def _():
        state_ref[...] = jnp.zeros_like(state_ref)

    proj_refs, gk_refs, og_refs = (proj0_ref, proj1_ref), (gk0_ref, gk1_ref), (og0_ref, og1_ref)
    nsub = x_ref.shape[1] // GLA_TM
    nchunk = GLA_TM // B_CHUNK
    n_main = win_ref.shape[1]

    ri = lax.broadcasted_iota(jnp.int32, (B_CHUNK, B_CHUNK), 0)
    ci = lax.broadcasted_iota(jnp.int32, (B_CHUNK, B_CHUNK), 1)
    causal = ci <= ri
    tri = causal.astype(MXU_DTYPE)
    gnorm = gnorm_ref[...]
    qscale = B_DK ** -0.5
    r_off = 2 * B_QK + B_V
    heads = [slice(hd * B_DK, (hd + 1) * B_DK) for hd in range(B_HEADS)]

    def project_steps(k):
        rows = slice(k * GLA_TM, (k + 1) * GLA_TM)

        def normalize():
            h = _modnorm(x_ref[0, rows, :], gam_ref[...], mod_ref[0, 1:2, :], mod_ref[0, 0:1, :])
            h_ref[...] = h.astype(h_ref.dtype)

        def piece(cols):
            def run():
                proj_refs[k % 2][:, cols] = _mm(h_ref[...], win_ref[:, cols])
            return run

        def gate():
            glr = _mm(h_ref[...], wglr_ref[...]).astype(MXU_DTYPE)
            z = _mm(glr, wgate_ref[...]) + bgate_ref[...]
            gk_refs[k % 2][...] = (jnp.minimum(z, 0.0) - jnp.log1p(jnp.exp(-jnp.abs(z)))) * (1.0 / B_TAU)

        pieces = [piece(slice(c0, c0 + GLA_PROJ_COLS)) for c0 in range(0, n_main, GLA_PROJ_COLS)]
        return [normalize] + pieces + [gate]

    def local(k, c):
        proj_ref, gk_ref = proj_refs[k % 2], gk_refs[k % 2]
        rows = slice(c * B_CHUNK, (c + 1) * B_CHUNK)
        gk = gk_ref[rows, :]
        g_hi = gk.astype(MXU_DTYPE)
        rem = gk - g_hi.astype(f32)
        g_mid = rem.astype(MXU_DTYPE)
        g_lo = (rem - g_mid.astype(f32)).astype(MXU_DTYPE)
        bcum = _mm(tri, g_hi) + _mm(tri, g_mid) + _mm(tri, g_lo)
        blast = bcum[B_CHUNK - 1:B_CHUNK, :]
        q_t = ((proj_ref[rows, 0:B_QK] * qscale) * jnp.exp(bcum)).astype(MXU_DTYPE)
        kk = proj_ref[rows, B_QK:2 * B_QK]
        k_t = (kk * jnp.exp(-bcum)).astype(MXU_DTYPE)
        k_d = (kk * jnp.exp(blast - bcum)).astype(MXU_DTYPE)
        v = [proj_ref[rows, 2 * B_QK + hd * B_DV:2 * B_QK + (hd + 1) * B_DV].astype(MXU_DTYPE)
             for hd in range(B_HEADS)]
        a = [jnp.where(causal, _mm_nt(q_t[:, ks], k_t[:, ks]), 0.0).astype(MXU_DTYPE) for ks in heads]
        kv = [_mm_tn(k_d[:, ks], v[hd]) for hd, ks in enumerate(heads)]
        o_intra = [_mm(a[hd], v[hd]) for hd in range(B_HEADS)]
        return q_t, blast, kv, o_intra

    def recur(k, c, loc, state):
        q_t, blast, kv, o_intra = loc
        rows = slice(c * B_CHUNK, (c + 1) * B_CHUNK)
        new_state = []
        for hd, ks in enumerate(heads):
            o = o_intra[hd] + _mm(q_t[:, ks], state[hd].astype(MXU_DTYPE))
            decay = jnp.exp(jnp.broadcast_to(blast[:, ks], (B_DK, B_DK)).T)
            new_state.append(jnp.concatenate([decay] * (B_DV // B_DK), axis=1) * state[hd] + kv[hd])
            o = o * lax.rsqrt(jnp.mean(o * o, axis=-1, keepdims=True) + EPS) * gnorm
            r = proj_refs[k % 2][rows, r_off + hd * B_DV:r_off + (hd + 1) * B_DV]
            og_refs[k % 2][rows, hd * B_DV:(hd + 1) * B_DV] = (o * _silu(r)).astype(og_refs[k % 2].dtype)
        return new_state

    for step in project_steps(0):
        step()
    state = [state_ref[hd] for hd in range(B_HEADS)]
    for k in range(nsub):
        ahead = project_steps(k + 1) if k + 1 < nsub else []
        loc = local(k, 0)
        for c in range(nchunk):
            nxt = local(k, c + 1) if c + 1 < nchunk else None
            state = recur(k, c, loc, state)
            loc = nxt
            take = -(-len(ahead) // (nchunk - c))
            for step in ahead[:take]:
                step()
            ahead = ahead[take:]
        rows = slice(k * GLA_TM, (k + 1) * GLA_TM)
        out_ref[0, rows, :] = x_ref[0, rows, :] + mod_ref[0, 2:3, :] * _mm(og_refs[k % 2][...], wout_ref[...])
    for hd in range(B_HEADS):
        state_ref[hd] = state[hd]


def _gla_layer(x, mod, gamma, w_in, w_gate, b_gate, g_norm, w_out):
    bsz, s, d = x.shape
    glr0 = 2 * B_QK + B_V
    w_main = jnp.concatenate([w_in[:, :glr0], w_in[:, glr0 + B_GATE_RANK:]], axis=1).astype(MXU_DTYPE)
    w_glr = jnp.pad(w_in[:, glr0:glr0 + B_GATE_RANK], ((0, 0), (0, LANES - B_GATE_RANK))).astype(MXU_DTYPE)
    w_gate_p = jnp.pad(w_gate, ((0, LANES - B_GATE_RANK), (0, 0))).astype(MXU_DTYPE)
    n_main = w_main.shape[1]
    rows = GLA_SUB * GLA_TM
    const = lambda b, i: (0, 0)
    resident = dict(pipeline_mode=pl.Buffered(1))
    return pl.pallas_call(
        _gla_kernel,
        grid=(bsz, s // rows),
        in_specs=[
            pl.BlockSpec((1, rows, d), lambda b, i: (b, i, 0)),
            pl.BlockSpec((1, 6, d), lambda b, i: (b, 0, 0)),
            pl.BlockSpec((1, d), const),
            pl.BlockSpec((d, n_main), const, **resident),
            pl.BlockSpec((d, LANES), const),
            pl.BlockSpec((LANES, B_QK), const),
            pl.BlockSpec((1, B_QK), const),
            pl.BlockSpec((1, B_DV), const),
            pl.BlockSpec((B_V, d), const, **resident),
        ],
        out_specs=pl.BlockSpec((1, rows, d), lambda b, i: (b, i, 0)),
        out_shape=jax.ShapeDtypeStruct((bsz, s, d), f32),
        scratch_shapes=[
            pltpu.VMEM((GLA_TM, d), MXU_DTYPE),
            pltpu.VMEM((GLA_TM, n_main), f32),
            pltpu.VMEM((GLA_TM, n_main), f32),
            pltpu.VMEM((GLA_TM, B_QK), f32),
            pltpu.VMEM((GLA_TM, B_QK), f32),
            pltpu.VMEM((GLA_TM, B_V), MXU_DTYPE),
            pltpu.VMEM((GLA_TM, B_V), MXU_DTYPE),
            pltpu.VMEM((B_HEADS, B_DK, B_DV), f32),
        ],
        compiler_params=pltpu.CompilerParams(
            dimension_semantics=("parallel", "arbitrary"), vmem_limit_bytes=VMEM_LIMIT),
        name="gla_layer",
    )(x, mod, gamma, w_main, w_glr, w_gate_p, b_gate[None, :], g_norm[None, :], w_out.astype(MXU_DTYPE))


def _qkv_weight(w_in):
    width = A_HEADS * A_HEAD_DIM
    col_scale = jnp.tile(jnp.repeat(jnp.array([A_HEAD_DIM ** -0.5 * LOG2E, 1.0, 1.0], f32), width), A_GROUPS)
    return (w_in * col_scale[None, :]).astype(MXU_DTYPE)


def kernel(x, c, w_in_a, w_out_a, rel_bias, w_in_b, w_gate_b, b_gate_b, gnorm_b, w_out_b, norm_mix, norm_ffn, w_ada, b_ada, w_up, conv_w, conv_b, w_down, norm_final):
    depth = w_ada.shape[0]
    mod = _adaln(c, w_ada, b_ada)
    bias = _rel_bias(rel_bias)
    for i in range(depth):
        gam_mix = norm_mix[i][None, :]
        j = i // 2
        mixer = None
        if i % 2 == 0:
            qkv = _qkv_proj(x, mod[i], gam_mix, _qkv_weight(w_in_a[j]))
            mixer = (_attention(qkv, bias), w_out_a[j])
        else:
            x = _gla_layer(x, mod[i], gam_mix, w_in_b[j], w_gate_b[j], b_gate_b[j], gnorm_b[j], w_out_b[j])
        last = i == depth - 1
        x = _conv_ffn(x, mod[i], norm_ffn[i][None, :], w_up[i], conv_w[i], conv_b[i], w_down[i],
                      gamma_final=norm_final[None, :] if last else None, mixer=mixer)
    return x
```

```python
import functools
import math

import jax
import jax.numpy as jnp
from jax import lax
from jax.experimental import pallas as pl
from jax.experimental.pallas import tpu as pltpu

A_CONFIGS = ((128, 1), (512, 4), (2048, 16))
A_GROUPS = len(A_CONFIGS)
A_HEADS = 16
A_HEAD_DIM = 64
N_BUCKETS = 32
MAX_DISTANCE = 2048
B_HEADS = 4
B_DK = 128
B_DV = 256
B_QK = B_HEADS * B_DK
B_V = B_HEADS * B_DV
B_GATE_RANK = 16
B_TAU = 16.0
B_CHUNK = 64
CONV_W = 3
EPS = 1e-6
NEG_INF = -1e30
LOG2E = math.log2(math.e)

LANES = 128
MXU_DTYPE = jnp.bfloat16
ATT_BLK = 128
ATT_TILE = 2048
ATT_UNITS = ATT_TILE // ATT_BLK
ATT_PIPELINE = 3
PAIR_W = 3 * LANES
N_PAIRS = A_HEADS // 2
QKV_ROWS = 1024
PROBE_ROWS = 16
FFN_TM = 512
FFN_SUB = 1
FFN_NORM_AHEAD = 3
FFN_TF = 256
FFN_ROWS = 256
FFN_DOWN_GROUP = 2
HALO = 8
O_HALO = 16
GLA_TM = 512
GLA_SUB = 2
GLA_PROJ_COLS = 256
VMEM_LIMIT = 56 * 1024 * 1024

f32 = jnp.float32


def _mm(a, b):
    return jnp.dot(a, b, preferred_element_type=f32)


def _mm_nt(a, b):
    return lax.dot_general(a, b, (((1,), (1,)), ((), ())), preferred_element_type=f32)


def _mm_tn(a, b):
    return lax.dot_general(a, b, (((0,), (0,)), ((), ())), preferred_element_type=f32)


def _modnorm(x, gamma, scale, shift):
    ms = jnp.mean(x * x, axis=-1, keepdims=True)
    y = x * lax.rsqrt(ms + EPS) * gamma
    return y * (1.0 + scale) + shift


def _silu(x):
    return x * (1.0 / (1.0 + jnp.exp(-x)))


def _adaln_kernel(c_ref, w_ref, b_ref, o_ref):
    s = _silu(c_ref[...]).astype(MXU_DTYPE)
    o_ref[0] = _mm(s, w_ref[0].astype(MXU_DTYPE)) + b_ref[0]


def _adaln(c, w_ada, b_ada):
    depth, d, n = w_ada.shape
    bsz = c.shape[0]
    rows = 8 * pl.cdiv(bsz, 8)
    c_pad = jnp.pad(c, ((0, rows - bsz), (0, 0)))
    out = pl.pallas_call(
        _adaln_kernel,
        grid=(depth, n // d),
        in_specs=[
            pl.BlockSpec((rows, d), lambda l, j: (0, 0)),
            pl.BlockSpec((1, d, d), lambda l, j: (l, 0, j)),
            pl.BlockSpec((1, 1, d), lambda l, j: (l, 0, j)),
        ],
        out_specs=pl.BlockSpec((1, rows, d), lambda l, j: (l, 0, j)),
        out_shape=jax.ShapeDtypeStruct((depth, rows, n), f32),
        compiler_params=pltpu.CompilerParams(
            dimension_semantics=("parallel", "parallel"), vmem_limit_bytes=VMEM_LIMIT),
        name="adaln",
    )(c_pad, w_ada, b_ada.reshape(depth, 1, n))
    return out[:, :bsz].reshape(depth, bsz, n // d, d)


def _qkv_kernel(x_ref, mod_ref, gam_ref, wq_ref, wk_ref, wv_ref, o_ref, ha_ref, hb_ref, sa_ref, sb_ref, *,
                dilations, tiles_per_group):
    j = pl.program_id(2)
    g = j // tiles_per_group
    t = j % tiles_per_group
    nlb = sa_ref.shape[0]
    chunks_per_step = ATT_UNITS // tiles_per_group
    h_refs, stage_refs = [ha_ref, hb_ref], [sa_ref, sb_ref]

    @pl.when(j == 0)
    def _():
        for c in range(ATT_UNITS):
            rows = slice(c * ATT_BLK, (c + 1) * ATT_BLK)
            hn = _modnorm(x_ref[0, rows, :], gam_ref[...], mod_ref[0, 1:2, :], mod_ref[0, 0:1, :])
            ha_ref[rows, :] = hn.astype(ha_ref.dtype)
            for lb in range(nlb):
                sa_ref[lb, rows, :] = hn[:, lb * LANES:(lb + 1) * LANES]

    def deinterleave_chunk(gi, cc):
        d, dp = dilations[gi], dilations[gi - 1]
        rel = d // dp
        per = ATT_UNITS // d
        c = t * chunks_per_step + cc
        r, q = c // per, c % per
        start = (r % dp) * (ATT_TILE // dp) + q * (ATT_BLK * rel) + r // dp
        rows = pl.ds(start, ATT_BLK, stride=rel)
        r0 = pl.multiple_of(c * ATT_BLK, ATT_BLK)
        parts = [stage_refs[(gi - 1) % 2][lb, rows, :] for lb in range(nlb)]
        if gi + 1 < len(dilations):
            for lb in range(nlb):
                stage_refs[gi % 2][lb, pl.ds(r0, ATT_BLK), :] = parts[lb]
        dst_ref = h_refs[gi % 2]
        dst_ref[pl.ds(r0, ATT_BLK), :] = jnp.concatenate(parts, axis=1).astype(dst_ref.dtype)
        probe = dst_ref[pl.ds(r0, PROBE_ROWS), 0:LANES]
        return (probe != probe) & (probe == probe)

    def project_chunk(src_ref, rc, never):
        rows = slice(rc * QKV_ROWS, (rc + 1) * QKV_ROWS)
        lhs = src_ref[rows, :]
        for k, w_ref in enumerate((wq_ref, wk_ref, wv_ref)):
            res = _mm(lhs, w_ref[...]).astype(o_ref.dtype)
            lanes = slice(k * LANES, (k + 1) * LANES)
            o_ref[0, 0, 1, rows, lanes] = res[:, LANES:]
            if never is not None and k == 0:
                top = rc * QKV_ROWS + PROBE_ROWS
                o_ref[0, 0, 0, rc * QKV_ROWS:top, lanes] = jnp.where(
                    never, jnp.zeros_like(res[:PROBE_ROWS, :LANES]), res[:PROBE_ROWS, :LANES])
                o_ref[0, 0, 0, top:(rc + 1) * QKV_ROWS, lanes] = res[PROBE_ROWS:, :LANES]
            else:
                o_ref[0, 0, 0, rows, lanes] = res[:, :LANES]

    assert dilations[0] == 1
    assert all(d % dp == 0 for dp, d in zip(dilations, dilations[1:]))
    n_mm = ATT_TILE // QKV_ROWS
    for gi in range(len(dilations)):
        @pl.when(g == gi)
        def _(gi=gi):
            slots = max(n_mm - 1, 1)
            shares = [list(range(chunks_per_step))[sl::slots] for sl in range(slots)] + [[]]
            for rc in range(n_mm):
                never = None
                if gi + 1 < len(dilations):
                    for cc in shares[rc]:
                        probe = deinterleave_chunk(gi + 1, cc)
                        never = probe if never is None else never | probe
                project_chunk(h_refs[gi % 2], rc, never)


def _qkv_proj(x, mod, gamma, w):
    bsz, s, d = x.shape
    dilations = tuple(dl for _, dl in A_CONFIGS)
    tn = 2 * LANES
    tiles_per_group = A_HEADS * A_HEAD_DIM // tn
    kern = functools.partial(_qkv_kernel, dilations=dilations, tiles_per_group=tiles_per_group)

    def w_spec(k):
        return pl.BlockSpec(
            (d, tn), lambda b, i, j: (0, (j // tiles_per_group * 3 + k) * tiles_per_group + j % tiles_per_group))

    return pl.pallas_call(
        kern,
        grid=(bsz, s // ATT_TILE, A_GROUPS * tiles_per_group),
        in_specs=[
            pl.BlockSpec((1, ATT_TILE, d), lambda b, i, j: (b, i, 0)),
            pl.BlockSpec((1, 6, d), lambda b, i, j: (b, 0, 0)),
            pl.BlockSpec((1, d), lambda b, i, j: (0, 0)),
            w_spec(0), w_spec(1), w_spec(2),
        ],
        out_specs=pl.BlockSpec(
            (1, 1, 2, ATT_TILE, PAIR_W),
            lambda b, i, j: (j // tiles_per_group, b, j % tiles_per_group, i, 0)),
        out_shape=jax.ShapeDtypeStruct((A_GROUPS, bsz, N_PAIRS, s, PAIR_W), MXU_DTYPE),
        scratch_shapes=[pltpu.VMEM((ATT_TILE, d), MXU_DTYPE),
                        pltpu.VMEM((ATT_TILE, d), MXU_DTYPE),
                        pltpu.VMEM((d // LANES, ATT_TILE, LANES), f32),
                        pltpu.VMEM((d // LANES, ATT_TILE, LANES), f32)],
        compiler_params=pltpu.CompilerParams(
            dimension_semantics=("parallel", "parallel", "arbitrary"),
            vmem_limit_bytes=VMEM_LIMIT),
        name="qkv_proj",
    )(x, mod, gamma, w, w, w)


def _bias_kernel(tab_ref, bkt_ref, o_ref):
    bkt = bkt_ref[0]
    hits = [bkt == k for k in range(N_BUCKETS)]
    for h in range(A_HEADS):
        col = pl.program_id(0) * A_HEADS + h
        acc = jnp.full(bkt.shape, NEG_INF, f32)
        for k in range(N_BUCKETS):
            acc = jnp.where(hits[k], tab_ref[k, col] * LOG2E, acc)
        o_ref[0, h] = acc


def _t5_bucket(dist):
    max_exact = N_BUCKETS // 2
    n = jnp.maximum(dist, max_exact).astype(f32)
    large = max_exact + (jnp.log(n / max_exact) / math.log(MAX_DISTANCE / max_exact)
                         * (N_BUCKETS - max_exact)).astype(jnp.int32)
    large = jnp.minimum(large, N_BUCKETS - 1)
    return jnp.where(dist < max_exact, dist, large)


def _rel_bias(rel_bias):
    qi = jnp.arange(ATT_BLK)[:, None]
    ki = jnp.arange(2 * ATT_BLK)[None, :]
    steps = qi + ATT_BLK - ki
    band = (steps >= 0) & (steps <= ATT_BLK)
    bucket = jnp.stack([
        jnp.where(band, _t5_bucket(jnp.clip(steps, 0, ATT_BLK) * dl), -1)
        for _, dl in A_CONFIGS]).astype(jnp.int32)
    return pl.pallas_call(
        _bias_kernel,
        grid=(A_GROUPS,),
        in_specs=[
            pl.BlockSpec(memory_space=pltpu.SMEM),
            pl.BlockSpec((1, ATT_BLK, 2 * ATT_BLK), lambda g: (g, 0, 0)),
        ],
        out_specs=pl.BlockSpec((1, A_HEADS, ATT_BLK, 2 * ATT_BLK), lambda g: (g, 0, 0, 0)),
        out_shape=jax.ShapeDtypeStruct((A_GROUPS, A_HEADS, ATT_BLK, 2 * ATT_BLK), f32),
        name="rel_bias",
    )(rel_bias, bucket)


def _attn_kernel(qkv_ref, bias_ref, out_ref, prev0, prev1, prev2, o1, l1, n1, o2, l2, n2, biasp):
    first = pl.program_id(2) == 0

    @pl.when(first)
    def _():
        prev0[...] = jnp.zeros_like(prev0)
        prev1[...] = jnp.zeros_like(prev1)
        prev2[...] = jnp.zeros_like(prev2)

    for g in range(A_GROUPS):
        for hh in range(2):
            biasp[g, hh, :, 0:ATT_BLK] = jnp.where(first, NEG_INF, bias_ref[g, hh, :, 0:ATT_BLK])
            biasp[g, hh, :, ATT_BLK:2 * ATT_BLK] = bias_ref[g, hh, :, ATT_BLK:2 * ATT_BLK]

    lane = lax.broadcasted_iota(jnp.int32, (1, LANES), 1)
    lo = lane < A_HEAD_DIM
    qmask = (jnp.where(lo, 1.0, 0.0).astype(MXU_DTYPE), jnp.where(lo, 0.0, 1.0).astype(MXU_DTYPE))

    def keys_values(g, rc, prev_ref, rp, lanes):
        if prev_ref is None:
            return qkv_ref[g, 0, 0, rp:rp + 2 * ATT_BLK, lanes]
        off = lanes.start - LANES
        return jnp.concatenate([prev_ref[rp:rp + ATT_BLK, off:off + LANES],
                                qkv_ref[g, 0, 0, rc:rc + ATT_BLK, lanes]], axis=0)

    def scores(g, rc, prev_ref, rp):
        q = qkv_ref[g, 0, 0, rc:rc + ATT_BLK, 0:LANES]
        k = keys_values(g, rc, prev_ref, rp, slice(LANES, 2 * LANES))
        bias = bias_ref if prev_ref is None else biasp
        return [_mm_nt(q * qmask[hh], k) + bias[g, hh] for hh in range(2)]

    def attend(g, rc, prev_ref, rp, s):
        v = keys_values(g, rc, prev_ref, rp, slice(2 * LANES, 3 * LANES))
        v_ones = jnp.concatenate([v, jnp.ones_like(v)], axis=1)
        res, ms = [], []
        for sh in s:
            m = jnp.max(sh, axis=-1, keepdims=True)
            res.append(_mm(jnp.exp2(sh - m).astype(MXU_DTYPE), v_ones))
            ms.append(m)
        acc = jnp.where(lo, res[0][:, :LANES], res[1][:, :LANES])
        den = jnp.where(lo, res[0][:, LANES:], res[1][:, LANES:])
        return acc, jnp.broadcast_to(jnp.where(lo, ms[0], ms[1]), acc.shape), den

    def store(a_ref, m_ref, d_ref, rows):
        def post(acc, m, den):
            a_ref[rows, :] = acc
            m_ref[rows, :] = m
            d_ref[rows, :] = den
        return post

    def merge(rows):
        def post(acc0, m0, den0):
            m1, m2 = l1[rows, :], l2[rows, :]
            mx = jnp.maximum(jnp.maximum(m0, m1), m2)
            e0, e1, e2 = jnp.exp2(m0 - mx), jnp.exp2(m1 - mx), jnp.exp2(m2 - mx)
            num = e0 * acc0 + e1 * o1[rows, :] + e2 * o2[rows, :]
            den = e0 * den0 + e1 * n1[rows, :] + e2 * n2[rows, :]
            out_ref[0, rows, :] = (num * (1.0 / den)).astype(out_ref.dtype)
        return post

    units = []
    d2 = A_CONFIGS[2][1]
    for u in range(ATT_UNITS):
        units.append((2, u * ATT_BLK, prev2, u * ATT_BLK, store(o2, l2, n2, pl.ds(u, ATT_BLK, stride=d2))))
    d1 = A_CONFIGS[1][1]
    per = ATT_UNITS // d1
    for u in range(ATT_UNITS):
        r, q = divmod(u, per)
        rc = u * ATT_BLK
        post = store(o1, l1, n1, pl.ds(q * ATT_BLK * d1 + r, ATT_BLK, stride=d1))
        if q == 0:
            units.append((1, rc, prev1, (r * per + per - 1) * ATT_BLK, post))
        else:
            units.append((1, rc, None, rc - ATT_BLK, post))
    for u in range(ATT_UNITS):
        rc = u * ATT_BLK
        post = merge(slice(rc, rc + ATT_BLK))
        units.append((0, rc, prev0, 0, post) if u == 0 else (0, rc, None, rc - ATT_BLK, post))

    pending = [scores(*un[:4]) for un in units[:ATT_PIPELINE]]
    for idx, un in enumerate(units):
        if idx + ATT_PIPELINE < len(units):
            pending.append(scores(*units[idx + ATT_PIPELINE][:4]))
        un[4](*attend(*un[:4], pending.pop(0)))

    prev0[...] = qkv_ref[0, 0, 0, ATT_TILE - ATT_BLK:ATT_TILE, LANES:3 * LANES]
    prev1[...] = qkv_ref[1, 0, 0, :, LANES:3 * LANES]
    prev2[...] = qkv_ref[2, 0, 0, :, LANES:3 * LANES]


def _attention(qkv, bias):
    _, bsz, _, s, _ = qkv.shape
    return pl.pallas_call(
        _attn_kernel,
        grid=(bsz, N_PAIRS, s // ATT_TILE),
        in_specs=[
            pl.BlockSpec((A_GROUPS, 1, 1, ATT_TILE, PAIR_W), lambda b, p, i: (0, b, p, i, 0)),
            pl.BlockSpec((A_GROUPS, 2, ATT_BLK, 2 * ATT_BLK), lambda b, p, i: (0, p, 0, 0)),
        ],
        out_specs=pl.BlockSpec((1, ATT_TILE, LANES), lambda b, p, i: (b, i, p)),
        out_shape=jax.ShapeDtypeStruct((bsz, s, A_HEADS * A_HEAD_DIM), MXU_DTYPE),
        scratch_shapes=[
            pltpu.VMEM((ATT_BLK, 2 * LANES), MXU_DTYPE),
            pltpu.VMEM((ATT_TILE, 2 * LANES), MXU_DTYPE),
            pltpu.VMEM((ATT_TILE, 2 * LANES), MXU_DTYPE),
            pltpu.VMEM((ATT_TILE, LANES), f32),
            pltpu.VMEM((ATT_TILE, LANES), f32),
            pltpu.VMEM((ATT_TILE, LANES), f32),
            pltpu.VMEM((ATT_TILE, LANES), f32),
            pltpu.VMEM((ATT_TILE, LANES), f32),
            pltpu.VMEM((ATT_TILE, LANES), f32),
            pltpu.VMEM((A_GROUPS, 2, ATT_BLK, 2 * ATT_BLK), f32),
        ],
        compiler_params=pltpu.CompilerParams(
            dimension_semantics=("parallel", "parallel", "arbitrary"),
            vmem_limit_bytes=VMEM_LIMIT),
        name="dilated_attn",
    )(qkv, bias)


def _ffn_kernel(*refs, final, mixer):
    refs = list(refs)
    x_ref, halo_ref = refs[:2]
    del refs[:2]
    if mixer:
        o_ref, ohalo_ref, wout_ref = refs[:3]
        del refs[:3]
    mod_ref, gam_ref, wup_ref, cw_ref, cb_ref, wdn_ref = refs[:6]
    del refs[:6]
    if final:
        gfin_ref = refs.pop(0)
    out_ref, h0_ref, h1_ref, u0_ref, u1_ref, act_ref, acc_ref = refs[:7]
    if mixer:
        x1_ref = refs[7]
    h_refs, u_refs = (h0_ref, h1_ref), (u0_ref, u1_ref)
    gam, scale, shift = gam_ref[...], mod_ref[0, 4:5, :], mod_ref[0, 3:4, :]
    d_ff = wdn_ref.shape[0]
    nchunk = d_ff // FFN_TF
    nsub = x_ref.shape[1] // FFN_TM

    def stream(k):
        rows = slice(k * FFN_TM, (k + 1) * FFN_TM)
        if not mixer:
            return x_ref[0, rows, :]
        x1 = x_ref[0, rows, :] + mod_ref[0, 2:3, :] * _mm(o_ref[0, rows, :], wout_ref[...])
        x1_ref[rows, :] = x1
        return x1

    def stream_halo(k):
        if k > 0:
            src = x1_ref if mixer else x_ref.at[0]
            return src[k * FFN_TM - HALO:k * FFN_TM, :]
        if not mixer:
            return halo_ref[0]
        y = _mm(ohalo_ref[0], wout_ref[...])[O_HALO - HALO:, :]
        return halo_ref[0] + mod_ref[0, 2:3, :] * y

    def normalize(k):
        h_ref = h_refs[k % 2]
        halo = _modnorm(stream_halo(k), gam, scale, shift)
        if k == 0:
            halo = jnp.where(pl.program_id(1) == 0, 0.0, halo)
        h_ref[0:HALO, :] = halo.astype(h_ref.dtype)
        h_ref[HALO:HALO + FFN_TM, :] = _modnorm(stream(k), gam, scale, shift).astype(h_ref.dtype)

    def halves(c):
        return [slice(half * d_ff + c * FFN_TF, half * d_ff + (c + 1) * FFN_TF) for half in range(2)]

    def up(job):
        k, c = divmod(job, nchunk)
        for half, cols in enumerate(halves(c)):
            u_refs[job % 2][:, half * FFN_TF:(half + 1) * FFN_TF] = _mm(h_refs[k % 2][...], wup_ref[:, cols])

    def conv_act(job):
        c = job % nchunk
        u_ref = u_refs[job % 2]
        for rb in range(FFN_TM // FFN_ROWS):
            base = HALO + rb * FFN_ROWS
            ab = []
            for half, cols in enumerate(halves(c)):
                v = cb_ref[:, cols]
                for t in range(CONV_W):
                    off = base - (CONV_W - 1) + t
                    v = v + cw_ref[t:t + 1, cols] * u_ref[off:off + FFN_ROWS, half * FFN_TF:(half + 1) * FFN_TF]
                ab.append(v)
            act_ref[rb * FFN_ROWS:(rb + 1) * FFN_ROWS, c * FFN_TF:(c + 1) * FFN_TF] = (
                _silu(ab[0]) * ab[1]).astype(act_ref.dtype)

    def down(first, last):
        cols = slice(first * FFN_TF, last * FFN_TF)
        return _mm(act_ref[:, cols], wdn_ref[cols, :])

    def finish(k, start):
        ffn = down(start, nchunk)
        if start > 0:
            ffn = ffn + acc_ref[...]
        rows = slice(k * FFN_TM, (k + 1) * FFN_TM)
        y = (x1_ref[rows, :] if mixer else x_ref[0, rows, :]) + mod_ref[0, 5:6, :] * ffn
        if final:
            ms = jnp.mean(y * y, axis=-1, keepdims=True)
            y = y * lax.rsqrt(ms + EPS) * gfin_ref[...]
        out_ref[0, rows, :] = y

    normalize(0)
    up(0)
    start, pending = 0, None
    for job in range(nsub * nchunk):
        k, c = divmod(job, nchunk)
        if c == 0:
            start = 0
        if c == nchunk - FFN_NORM_AHEAD and k + 1 < nsub:
            normalize(k + 1)
        if job + 1 < nsub * nchunk:
            up(job + 1)
        if pending is not None:
            if pending[0] == 0:
                acc_ref[...] = down(*pending)
            else:
                acc_ref[...] += down(*pending)
            pending = None
        if c + 1 - start == FFN_DOWN_GROUP and c + 1 < nchunk:
            pending, start = (start, c + 1), c + 1
        conv_act(job)
        if c == nchunk - 1:
            finish(k, start)


def _conv_ffn(x, mod, gamma, w_up, conv_w, conv_b, w_down, gamma_final=None, mixer=None):
    bsz, s, d = x.shape
    d_ff = w_down.shape[0]
    final = gamma_final is not None
    rows = FFN_SUB * FFN_TM
    tiles_per_halo = rows // HALO
    resident = dict(pipeline_mode=pl.Buffered(1))
    in_specs = [
        pl.BlockSpec((1, rows, d), lambda b, i: (b, i, 0)),
        pl.BlockSpec((1, HALO, d), lambda b, i: (b, jnp.maximum(i * tiles_per_halo - 1, 0), 0)),
    ]
    args = [x, x]
    if mixer is not None:
        o, w_out = mixer
        k = o.shape[-1]
        in_specs += [
            pl.BlockSpec((1, rows, k), lambda b, i: (b, i, 0)),
            pl.BlockSpec((1, O_HALO, k), lambda b, i: (b, jnp.maximum(i * (rows // O_HALO) - 1, 0), 0)),
            pl.BlockSpec((k, d), lambda b, i: (0, 0), **resident),
        ]
        args += [o, o, w_out.astype(MXU_DTYPE)]
    in_specs += [
        pl.BlockSpec((1, 6, d), lambda b, i: (b, 0, 0)),
        pl.BlockSpec((1, d), lambda b, i: (0, 0)),
        pl.BlockSpec((d, 2 * d_ff), lambda b, i: (0, 0), **resident),
        pl.BlockSpec((CONV_W, 2 * d_ff), lambda b, i: (0, 0)),
        pl.BlockSpec((1, 2 * d_ff), lambda b, i: (0, 0)),
        pl.BlockSpec((d_ff, d), lambda b, i: (0, 0), **resident),
    ]
    args += [mod, gamma, w_up.astype(MXU_DTYPE), conv_w, conv_b[None, :], w_down.astype(MXU_DTYPE)]
    if final:
        in_specs.append(pl.BlockSpec((1, d), lambda b, i: (0, 0)))
        args.append(gamma_final)
    return pl.pallas_call(
        functools.partial(_ffn_kernel, final=final, mixer=mixer is not None),
        grid=(bsz, s // rows),
        in_specs=in_specs,
        out_specs=pl.BlockSpec((1, rows, d), lambda b, i: (b, i, 0)),
        out_shape=jax.ShapeDtypeStruct((bsz, s, d), f32),
        scratch_shapes=[
            pltpu.VMEM((FFN_TM + HALO, d), MXU_DTYPE),
            pltpu.VMEM((FFN_TM + HALO, d), MXU_DTYPE),
            pltpu.VMEM((FFN_TM + HALO, 2 * FFN_TF), f32),
            pltpu.VMEM((FFN_TM + HALO, 2 * FFN_TF), f32),
            pltpu.VMEM((FFN_TM, d_ff), MXU_DTYPE),
            pltpu.VMEM((FFN_TM, d), f32),
        ] + ([pltpu.VMEM((rows, d), f32)] if mixer is not None else []),
        compiler_params=pltpu.CompilerParams(
            dimension_semantics=("parallel", "parallel"), vmem_limit_bytes=VMEM_LIMIT),
        name="conv_ffn_final" if final else "conv_ffn",
    )(*args)


def _gla_kernel(x_ref, mod_ref, gam_ref, win_ref, wglr_ref, wgate_ref, bgate_ref, gnorm_ref, wout_ref,
                out_ref, h_ref, proj0_ref, proj1_ref, gk0_ref, gk1_ref, og0_ref, og1_ref, state_ref):
    @pl.when(pl.program_id(1) == 0)
    def _():
        state_ref[...] = jnp.zeros_like(state_ref)

    proj_refs, gk_refs, og_refs = (proj0_ref, proj1_ref), (gk0_ref, gk1_ref), (og0_ref, og1_ref)
    nsub = x_ref.shape[1] // GLA_TM
    nchunk = GLA_TM // B_CHUNK
    n_main = win_ref.shape[1]

    ri = lax.broadcasted_iota(jnp.int32, (B_CHUNK, B_CHUNK), 0)
    ci = lax.broadcasted_iota(jnp.int32, (B_CHUNK, B_CHUNK), 1)
    causal = ci <= ri
    tri = causal.astype(MXU_DTYPE)
    gnorm = gnorm_ref[...]
    qscale = B_DK ** -0.5
    r_off = 2 * B_QK + B_V
    heads = [slice(hd * B_DK, (hd + 1) * B_DK) for hd in range(B_HEADS)]

    def project_steps(k):
        rows = slice(k * GLA_TM, (k + 1) * GLA_TM)

        def normalize():
            h = _modnorm(x_ref[0, rows, :], gam_ref[...], mod_ref[0, 1:2, :], mod_ref[0, 0:1, :])
            h_ref[...] = h.astype(h_ref.dtype)

        def piece(cols):
            def run():
                proj_refs[k % 2][:, cols] = _mm(h_ref[...], win_ref[:, cols])
            return run

        def gate():
            glr = _mm(h_ref[...], wglr_ref[...]).astype(MXU_DTYPE)
            z = _mm(glr, wgate_ref[...]) + bgate_ref[...]
            gk_refs[k % 2][...] = (jnp.minimum(z, 0.0) - jnp.log1p(jnp.exp(-jnp.abs(z)))) * (1.0 / B_TAU)

        pieces = [piece(slice(c0, c0 + GLA_PROJ_COLS)) for c0 in range(0, n_main, GLA_PROJ_COLS)]
        return [normalize] + pieces + [gate]

    def local(k, c):
        proj_ref, gk_ref = proj_refs[k % 2], gk_refs[k % 2]
        rows = slice(c * B_CHUNK, (c + 1) * B_CHUNK)
        gk = gk_ref[rows, :]
        g_hi = gk.astype(MXU_DTYPE)
        rem = gk - g_hi.astype(f32)
        g_mid = rem.astype(MXU_DTYPE)
        g_lo = (rem - g_mid.astype(f32)).astype(MXU_DTYPE)
        bcum = _mm(tri, g_hi) + _mm(tri, g_mid) + _mm(tri, g_lo)
        blast = bcum[B_CHUNK - 1:B_CHUNK, :]
        q_t = ((proj_ref[rows, 0:B_QK] * qscale) * jnp.exp(bcum)).astype(MXU_DTYPE)
        kk = proj_ref[rows, B_QK:2 * B_QK]
        k_t = (kk * jnp.exp(-bcum)).astype(MXU_DTYPE)
        k_d = (kk * jnp.exp(blast - bcum)).astype(MXU_DTYPE)
        v = [proj_ref[rows, 2 * B_QK + hd * B_DV:2 * B_QK + (hd + 1) * B_DV].astype(MXU_DTYPE)
             for hd in range(B_HEADS)]
        a = [jnp.where(causal, _mm_nt(q_t[:, ks], k_t[:, ks]), 0.0).astype(MXU_DTYPE) for ks in heads]
        kv = [_mm_tn(k_d[:, ks], v[hd]) for hd, ks in enumerate(heads)]
        o_intra = [_mm(a[hd], v[hd]) for hd in range(B_HEADS)]
        return q_t, blast, kv, o_intra

    def recur(k, c, loc, state):
        q_t, blast, kv, o_intra = loc
        rows = slice(c * B_CHUNK, (c + 1) * B_CHUNK)
        new_state = []
        for hd, ks in enumerate(heads):
            o = o_intra[hd] + _mm(q_t[:, ks], state[hd].astype(MXU_DTYPE))
            decay = jnp.exp(jnp.broadcast_to(blast[:, ks], (B_DK, B_DK)).T)
            new_state.append(jnp.concatenate([decay] * (B_DV // B_DK), axis=1) * state[hd] + kv[hd])
            o = o * lax.rsqrt(jnp.mean(o * o, axis=-1, keepdims=True) + EPS) * gnorm
            r = proj_refs[k % 2][rows, r_off + hd * B_DV:r_off + (hd + 1) * B_DV]
            og_refs[k % 2][rows, hd * B_DV:(hd + 1) * B_DV] = (o * _silu(r)).astype(og_refs[k % 2].dtype)
        return new_state

    for step in project_steps(0):
        step()
    state = [state_ref[hd] for hd in range(B_HEADS)]
    for k in range(nsub):
        ahead = project_steps(k + 1) if k + 1 < nsub else []
        loc = local(k, 0)
        for c in range(nchunk):
            nxt = local(k, c + 1) if c + 1 < nchunk else None
            state = recur(k, c, loc, state)
            loc = nxt
            take = -(-len(ahead) // (nchunk - c))
            for step in ahead[:take]:
                step()
            ahead = ahead[take:]
        rows = slice(k * GLA_TM, (k + 1) * GLA_TM)
        out_ref[0, rows, :] = x_ref[0, rows, :] + mod_ref[0, 2:3, :] * _mm(og_refs[k % 2][...], wout_ref[...])
    for hd in range(B_HEADS):
        state_ref[hd] = state[hd]


def _gla_layer(x, mod, gamma, w_in, w_gate, b_gate, g_norm, w_out):
    bsz, s, d = x.shape
    glr0 = 2 * B_QK + B_V
    w_main = jnp.concatenate([w_in[:, :glr0], w_in[:, glr0 + B_GATE_RANK:]], axis=1).astype(MXU_DTYPE)
    w_glr = jnp.pad(w_in[:, glr0:glr0 + B_GATE_RANK], ((0, 0), (0, LANES - B_GATE_RANK))).astype(MXU_DTYPE)
    w_gate_p = jnp.pad(w_gate, ((0, LANES - B_GATE_RANK), (0, 0))).astype(MXU_DTYPE)
    n_main = w_main.shape[1]
    rows = GLA_SUB * GLA_TM
    const = lambda b, i: (0, 0)
    resident = dict(pipeline_mode=pl.Buffered(1))
    return pl.pallas_call(
        _gla_kernel,
        grid=(bsz, s // rows),
        in_specs=[
            pl.BlockSpec((1, rows, d), lambda b, i: (b, i, 0)),
            pl.BlockSpec((1, 6, d), lambda b, i: (b, 0, 0)),
            pl.BlockSpec((1, d), const),
            pl.BlockSpec((d, n_main), const, **resident),
            pl.BlockSpec((d, LANES), const),
            pl.BlockSpec((LANES, B_QK), const),
            pl.BlockSpec((1, B_QK), const),
            pl.BlockSpec((1, B_DV), const),
            pl.BlockSpec((B_V, d), const, **resident),
        ],
        out_specs=pl.BlockSpec((1, rows, d), lambda b, i: (b, i, 0)),
        out_shape=jax.ShapeDtypeStruct((bsz, s, d), f32),
        scratch_shapes=[
            pltpu.VMEM((GLA_TM, d), MXU_DTYPE),
            pltpu.VMEM((GLA_TM, n_main), f32),
            pltpu.VMEM((GLA_TM, n_main), f32),
            pltpu.VMEM((GLA_TM, B_QK), f32),
            pltpu.VMEM((GLA_TM, B_QK), f32),
            pltpu.VMEM((GLA_TM, B_V), MXU_DTYPE),
            pltpu.VMEM((GLA_TM, B_V), MXU_DTYPE),
            pltpu.VMEM((B_HEADS, B_DK, B_DV), f32),
        ],
        compiler_params=pltpu.CompilerParams(
            dimension_semantics=("parallel", "arbitrary"), vmem_limit_bytes=VMEM_LIMIT),
        name="gla_layer",
    )(x, mod, gamma, w_main, w_glr, w_gate_p, b_gate[None, :], g_norm[None, :], w_out.astype(MXU_DTYPE))


def _qkv_weight(w_in):
    width = A_HEADS * A_HEAD_DIM
    col_scale = jnp.tile(jnp.repeat(jnp.array([A_HEAD_DIM ** -0.5 * LOG2E, 1.0, 1.0], f32), width), A_GROUPS)
    return (w_in * col_scale[None, :]).astype(MXU_DTYPE)


def kernel(x, c, w_in_a, w_out_a, rel_bias, w_in_b, w_gate_b, b_gate_b, gnorm_b, w_out_b, norm_mix, norm_ffn, w_ada, b_ada, w_up, conv_w, conv_b, w_down, norm_final):
    depth = w_ada.shape[0]
    mod = _adaln(c, w_ada, b_ada)
    bias = _rel_bias(rel_bias)
    for i in range(depth):
        gam_mix = norm_mix[i][None, :]
        j = i // 2
        mixer = None
        if i % 2 == 0:
            qkv = _qkv_proj(x, mod[i], gam_mix, _qkv_weight(w_in_a[j]))
            mixer = (_attention(qkv, bias), w_out_a[j])
        else:
            x = _gla_layer(x, mod[i], gam_mix, w_in_b[j], w_gate_b[j], b_gate_b[j], gnorm_b[j], w_out_b[j])
        last = i == depth - 1
        x = _conv_ffn(x, mod[i], norm_ffn[i][None, :], w_up[i], conv_w[i], conv_b[i], w_down[i],
                      gamma_final=norm_final[None, :] if last else None, mixer=mixer)
    return x
```

```python
import functools
import math

import jax
import jax.numpy as jnp
from jax import lax
from jax.experimental import pallas as pl
from jax.experimental.pallas import tpu as pltpu

A_CONFIGS = ((128, 1), (512, 4), (2048, 16))
A_GROUPS = len(A_CONFIGS)
A_HEADS = 16
A_HEAD_DIM = 64
N_BUCKETS = 32
MAX_DISTANCE = 2048
B_HEADS = 4
B_DK = 128
B_DV = 256
B_QK = B_HEADS * B_DK
B_V = B_HEADS * B_DV
B_GATE_RANK = 16
B_TAU = 16.0
B_CHUNK = 64
CONV_W = 3
EPS = 1e-6
NEG_INF = -1e30
LOG2E = math.log2(math.e)

LANES = 128
MXU_DTYPE = jnp.bfloat16
ATT_BLK = 128
ATT_TILE = 2048
ATT_UNITS = ATT_TILE // ATT_BLK
ATT_PIPELINE = 3
PAIR_W = 3 * LANES
N_PAIRS = A_HEADS // 2
QKV_ROWS = 1024
PROBE_ROWS = 16
FFN_TM = 512
FFN_TF = 256
FFN_ROWS = 256
FFN_DOWN_GROUP = 2
HALO = 8
O_HALO = 16
GLA_TM = 512
GLA_SUB = 2
GLA_PROJ_COLS = 256
VMEM_LIMIT = 56 * 1024 * 1024

f32 = jnp.float32


def _mm(a, b):
    return jnp.dot(a, b, preferred_element_type=f32)


def _mm_nt(a, b):
    return lax.dot_general(a, b, (((1,), (1,)), ((), ())), preferred_element_type=f32)


def _mm_tn(a, b):
    return lax.dot_general(a, b, (((0,), (0,)), ((), ())), preferred_element_type=f32)


def _modnorm(x, gamma, scale, shift):
    ms = jnp.mean(x * x, axis=-1, keepdims=True)
    y = x * lax.rsqrt(ms + EPS) * gamma
    return y * (1.0 + scale) + shift


def _silu(x):
    return x * (1.0 / (1.0 + jnp.exp(-x)))


def _adaln_kernel(c_ref, w_ref, b_ref, o_ref):
    s = _silu(c_ref[...]).astype(MXU_DTYPE)
    o_ref[0] = _mm(s, w_ref[0].astype(MXU_DTYPE)) + b_ref[0]


def _adaln(c, w_ada, b_ada):
    depth, d, n = w_ada.shape
    bsz = c.shape[0]
    rows = 8 * pl.cdiv(bsz, 8)
    c_pad = jnp.pad(c, ((0, rows - bsz), (0, 0)))
    out = pl.pallas_call(
        _adaln_kernel,
        grid=(depth, n // d),
        in_specs=[
            pl.BlockSpec((rows, d), lambda l, j: (0, 0)),
            pl.BlockSpec((1, d, d), lambda l, j: (l, 0, j)),
            pl.BlockSpec((1, 1, d), lambda l, j: (l, 0, j)),
        ],
        out_specs=pl.BlockSpec((1, rows, d), lambda l, j: (l, 0, j)),
        out_shape=jax.ShapeDtypeStruct((depth, rows, n), f32),
        compiler_params=pltpu.CompilerParams(
            dimension_semantics=("parallel", "parallel"), vmem_limit_bytes=VMEM_LIMIT),
        name="adaln",
    )(c_pad, w_ada, b_ada.reshape(depth, 1, n))
    return out[:, :bsz].reshape(depth, bsz, n // d, d)


def _qkv_kernel(x_ref, mod_ref, gam_ref, wq_ref, wk_ref, wv_ref, o_ref, ha_ref, hb_ref, sa_ref, sb_ref, *,
                dilations, tiles_per_group):
    j = pl.program_id(2)
    g = j // tiles_per_group
    t = j % tiles_per_group
    nlb = sa_ref.shape[0]
    chunks_per_step = ATT_UNITS // tiles_per_group
    h_refs, stage_refs = [ha_ref, hb_ref], [sa_ref, sb_ref]

    @pl.when(j == 0)
    def _():
        for c in range(ATT_UNITS):
            rows = slice(c * ATT_BLK, (c + 1) * ATT_BLK)
            hn = _modnorm(x_ref[0, rows, :], gam_ref[...], mod_ref[0, 1:2, :], mod_ref[0, 0:1, :])
            ha_ref[rows, :] = hn.astype(ha_ref.dtype)
            for lb in range(nlb):
                sa_ref[lb, rows, :] = hn[:, lb * LANES:(lb + 1) * LANES]

    def deinterleave_chunk(gi, cc):
        d, dp = dilations[gi], dilations[gi - 1]
        rel = d // dp
        per = ATT_UNITS // d
        c = t * chunks_per_step + cc
        r, q = c // per, c % per
        start = (r % dp) * (ATT_TILE // dp) + q * (ATT_BLK * rel) + r // dp
        rows = pl.ds(start, ATT_BLK, stride=rel)
        r0 = pl.multiple_of(c * ATT_BLK, ATT_BLK)
        parts = [stage_refs[(gi - 1) % 2][lb, rows, :] for lb in range(nlb)]
        if gi + 1 < len(dilations):
            for lb in range(nlb):
                stage_refs[gi % 2][lb, pl.ds(r0, ATT_BLK), :] = parts[lb]
        dst_ref = h_refs[gi % 2]
        dst_ref[pl.ds(r0, ATT_BLK), :] = jnp.concatenate(parts, axis=1).astype(dst_ref.dtype)
        probe = dst_ref[pl.ds(r0, PROBE_ROWS), 0:LANES]
        return (probe != probe) & (probe == probe)

    def project_chunk(src_ref, rc, never):
        rows = slice(rc * QKV_ROWS, (rc + 1) * QKV_ROWS)
        lhs = src_ref[rows, :]
        for k, w_ref in enumerate((wq_ref, wk_ref, wv_ref)):
            res = _mm(lhs, w_ref[...]).astype(o_ref.dtype)
            lanes = slice(k * LANES, (k + 1) * LANES)
            o_ref[0, 0, 1, rows, lanes] = res[:, LANES:]
            if never is not None and k == 0:
                top = rc * QKV_ROWS + PROBE_ROWS
                o_ref[0, 0, 0, rc * QKV_ROWS:top, lanes] = jnp.where(
                    never, jnp.zeros_like(res[:PROBE_ROWS, :LANES]), res[:PROBE_ROWS, :LANES])
                o_ref[0, 0, 0, top:(rc + 1) * QKV_ROWS, lanes] = res[PROBE_ROWS:, :LANES]
            else:
                o_ref[0, 0, 0, rows, lanes] = res[:, :LANES]

    assert dilations[0] == 1
    assert all(d % dp == 0 for dp, d in zip(dilations, dilations[1:]))
    n_mm = ATT_TILE // QKV_ROWS
    for gi in range(len(dilations)):
        @pl.when(g == gi)
        def _(gi=gi):
            slots = max(n_mm - 1, 1)
            shares = [list(range(chunks_per_step))[sl::slots] for sl in range(slots)] + [[]]
            for rc in range(n_mm):
                never = None
                if gi + 1 < len(dilations):
                    for cc in shares[rc]:
                        probe = deinterleave_chunk(gi + 1, cc)
                        never = probe if never is None else never | probe
                project_chunk(h_refs[gi % 2], rc, never)


def _qkv_proj(x, mod, gamma, w):
    bsz, s, d = x.shape
    dilations = tuple(dl for _, dl in A_CONFIGS)
    tn = 2 * LANES
    tiles_per_group = A_HEADS * A_HEAD_DIM // tn
    kern = functools.partial(_qkv_kernel, dilations=dilations, tiles_per_group=tiles_per_group)

    def w_spec(k):
        return pl.BlockSpec(
            (d, tn), lambda b, i, j: (0, (j // tiles_per_group * 3 + k) * tiles_per_group + j % tiles_per_group))

    return pl.pallas_call(
        kern,
        grid=(bsz, s // ATT_TILE, A_GROUPS * tiles_per_group),
        in_specs=[
            pl.BlockSpec((1, ATT_TILE, d), lambda b, i, j: (b, i, 0)),
            pl.BlockSpec((1, 6, d), lambda b, i, j: (b, 0, 0)),
            pl.BlockSpec((1, d), lambda b, i, j: (0, 0)),
            w_spec(0), w_spec(1), w_spec(2),
        ],
        out_specs=pl.BlockSpec(
            (1, 1, 2, ATT_TILE, PAIR_W),
            lambda b, i, j: (j // tiles_per_group, b, j % tiles_per_group, i, 0)),
        out_shape=jax.ShapeDtypeStruct((A_GROUPS, bsz, N_PAIRS, s, PAIR_W), MXU_DTYPE),
        scratch_shapes=[pltpu.VMEM((ATT_TILE, d), MXU_DTYPE),
                        pltpu.VMEM((ATT_TILE, d), MXU_DTYPE),
                        pltpu.VMEM((d // LANES, ATT_TILE, LANES), f32),
                        pltpu.VMEM((d // LANES, ATT_TILE, LANES), f32)],
        compiler_params=pltpu.CompilerParams(
            dimension_semantics=("parallel", "parallel", "arbitrary"),
            vmem_limit_bytes=VMEM_LIMIT),
        name="qkv_proj",
    )(x, mod, gamma, w, w, w)


def _bias_kernel(tab_ref, bkt_ref, o_ref):
    bkt = bkt_ref[0]
    hits = [bkt == k for k in range(N_BUCKETS)]
    for h in range(A_HEADS):
        col = pl.program_id(0) * A_HEADS + h
        acc = jnp.full(bkt.shape, NEG_INF, f32)
        for k in range(N_BUCKETS):
            acc = jnp.where(hits[k], tab_ref[k, col] * LOG2E, acc)
        o_ref[0, h] = acc


def _t5_bucket(dist):
    max_exact = N_BUCKETS // 2
    n = jnp.maximum(dist, max_exact).astype(f32)
    large = max_exact + (jnp.log(n / max_exact) / math.log(MAX_DISTANCE / max_exact)
                         * (N_BUCKETS - max_exact)).astype(jnp.int32)
    large = jnp.minimum(large, N_BUCKETS - 1)
    return jnp.where(dist < max_exact, dist, large)


def _rel_bias(rel_bias):
    qi = jnp.arange(ATT_BLK)[:, None]
    ki = jnp.arange(2 * ATT_BLK)[None, :]
    steps = qi + ATT_BLK - ki
    band = (steps >= 0) & (steps <= ATT_BLK)
    bucket = jnp.stack([
        jnp.where(band, _t5_bucket(jnp.clip(steps, 0, ATT_BLK) * dl), -1)
        for _, dl in A_CONFIGS]).astype(jnp.int32)
    return pl.pallas_call(
        _bias_kernel,
        grid=(A_GROUPS,),
        in_specs=[
            pl.BlockSpec(memory_space=pltpu.SMEM),
            pl.BlockSpec((1, ATT_BLK, 2 * ATT_BLK), lambda g: (g, 0, 0)),
        ],
        out_specs=pl.BlockSpec((1, A_HEADS, ATT_BLK, 2 * ATT_BLK), lambda g: (g, 0, 0, 0)),
        out_shape=jax.ShapeDtypeStruct((A_GROUPS, A_HEADS, ATT_BLK, 2 * ATT_BLK), f32),
        name="rel_bias",
    )(rel_bias, bucket)


def _attn_kernel(qkv_ref, bias_ref, out_ref, prev0, prev1, prev2, o1, l1, n1, o2, l2, n2, biasp):
    first = pl.program_id(2) == 0

    @pl.when(first)
    def _():
        prev0[...] = jnp.zeros_like(prev0)
        prev1[...] = jnp.zeros_like(prev1)
        prev2[...] = jnp.zeros_like(prev2)

    for g in range(A_GROUPS):
        for hh in range(2):
            biasp[g, hh, :, 0:ATT_BLK] = jnp.where(first, NEG_INF, bias_ref[g, hh, :, 0:ATT_BLK])
            biasp[g, hh, :, ATT_BLK:2 * ATT_BLK] = bias_ref[g, hh, :, ATT_BLK:2 * ATT_BLK]

    lane = lax.broadcasted_iota(jnp.int32, (1, LANES), 1)
    lo = lane < A_HEAD_DIM
    qmask = (jnp.where(lo, 1.0, 0.0).astype(MXU_DTYPE), jnp.where(lo, 0.0, 1.0).astype(MXU_DTYPE))

    def keys_values(g, rc, prev_ref, rp, lanes):
        if prev_ref is None:
            return qkv_ref[g, 0, 0, rp:rp + 2 * ATT_BLK, lanes]
        off = lanes.start - LANES
        return jnp.concatenate([prev_ref[rp:rp + ATT_BLK, off:off + LANES],
                                qkv_ref[g, 0, 0, rc:rc + ATT_BLK, lanes]], axis=0)

    def scores(g, rc, prev_ref, rp):
        q = qkv_ref[g, 0, 0, rc:rc + ATT_BLK, 0:LANES]
        k = keys_values(g, rc, prev_ref, rp, slice(LANES, 2 * LANES))
        bias = bias_ref if prev_ref is None else biasp
        return [_mm_nt(q * qmask[hh], k) + bias[g, hh] for hh in range(2)]

    def attend(g, rc, prev_ref, rp, s):
        v = keys_values(g, rc, prev_ref, rp, slice(2 * LANES, 3 * LANES))
        accs, dens, ms = [], [], []
        for sh in s:
            m = jnp.max(sh, axis=-1, keepdims=True)
            p = jnp.exp2(sh - m)
            dens.append(jnp.sum(p, axis=-1, keepdims=True))
            accs.append(_mm(p.astype(MXU_DTYPE), v))
            ms.append(m)
        acc = jnp.where(lo, accs[0], accs[1])
        return (acc, jnp.broadcast_to(jnp.where(lo, ms[0], ms[1]), acc.shape),
                jnp.broadcast_to(jnp.where(lo, dens[0], dens[1]), acc.shape))

    def store(a_ref, m_ref, d_ref, rows):
        def post(acc, m, den):
            a_ref[rows, :] = acc
            m_ref[rows, :] = m
            d_ref[rows, :] = den
        return post

    def merge(rows):
        def post(acc0, m0, den0):
            m1, m2 = l1[rows, :], l2[rows, :]
            mx = jnp.maximum(jnp.maximum(m0, m1), m2)
            e0, e1, e2 = jnp.exp2(m0 - mx), jnp.exp2(m1 - mx), jnp.exp2(m2 - mx)
            num = e0 * acc0 + e1 * o1[rows, :] + e2 * o2[rows, :]
            den = e0 * den0 + e1 * n1[rows, :] + e2 * n2[rows, :]
            out_ref[0, rows, :] = (num * (1.0 / den)).astype(out_ref.dtype)
        return post

    units = []
    d2 = A_CONFIGS[2][1]
    for u in range(ATT_UNITS):
        units.append((2, u * ATT_BLK, prev2, u * ATT_BLK, store(o2, l2, n2, pl.ds(u, ATT_BLK, stride=d2))))
    d1 = A_CONFIGS[1][1]
    per = ATT_UNITS // d1
    for u in range(ATT_UNITS):
        r, q = divmod(u, per)
        rc = u * ATT_BLK
        post = store(o1, l1, n1, pl.ds(q * ATT_BLK * d1 + r, ATT_BLK, stride=d1))
        if q == 0:
            units.append((1, rc, prev1, (r * per + per - 1) * ATT_BLK, post))
        else:
            units.append((1, rc, None, rc - ATT_BLK, post))
    for u in range(ATT_UNITS):
        rc = u * ATT_BLK
        post = merge(slice(rc, rc + ATT_BLK))
        units.append((0, rc, prev0, 0, post) if u == 0 else (0, rc, None, rc - ATT_BLK, post))

    pending = [scores(*un[:4]) for un in units[:ATT_PIPELINE]]
    for idx, un in enumerate(units):
        if idx + ATT_PIPELINE < len(units):
            pending.append(scores(*units[idx + ATT_PIPELINE][:4]))
        un[4](*attend(*un[:4], pending.pop(0)))

    prev0[...] = qkv_ref[0, 0, 0, ATT_TILE - ATT_BLK:ATT_TILE, LANES:3 * LANES]
    prev1[...] = qkv_ref[1, 0, 0, :, LANES:3 * LANES]
    prev2[...] = qkv_ref[2, 0, 0, :, LANES:3 * LANES]


def _attention(qkv, bias):
    _, bsz, _, s, _ = qkv.shape
    return pl.pallas_call(
        _attn_kernel,
        grid=(bsz, N_PAIRS, s // ATT_TILE),
        in_specs=[
            pl.BlockSpec((A_GROUPS, 1, 1, ATT_TILE, PAIR_W), lambda b, p, i: (0, b, p, i, 0)),
            pl.BlockSpec((A_GROUPS, 2, ATT_BLK, 2 * ATT_BLK), lambda b, p, i: (0, p, 0, 0)),
        ],
        out_specs=pl.BlockSpec((1, ATT_TILE, LANES), lambda b, p, i: (b, i, p)),
        out_shape=jax.ShapeDtypeStruct((bsz, s, A_HEADS * A_HEAD_DIM), MXU_DTYPE),
        scratch_shapes=[
            pltpu.VMEM((ATT_BLK, 2 * LANES), MXU_DTYPE),
            pltpu.VMEM((ATT_TILE, 2 * LANES), MXU_DTYPE),
            pltpu.VMEM((ATT_TILE, 2 * LANES), MXU_DTYPE),
            pltpu.VMEM((ATT_TILE, LANES), f32),
            pltpu.VMEM((ATT_TILE, LANES), f32),
            pltpu.VMEM((ATT_TILE, LANES), f32),
            pltpu.VMEM((ATT_TILE, LANES), f32),
            pltpu.VMEM((ATT_TILE, LANES), f32),
            pltpu.VMEM((ATT_TILE, LANES), f32),
            pltpu.VMEM((A_GROUPS, 2, ATT_BLK, 2 * ATT_BLK), f32),
        ],
        compiler_params=pltpu.CompilerParams(
            dimension_semantics=("parallel", "parallel", "arbitrary"),
            vmem_limit_bytes=VMEM_LIMIT),
        name="dilated_attn",
    )(qkv, bias)


def _ffn_kernel(*refs, final, mixer):
    refs = list(refs)
    x_ref, halo_ref = refs[:2]
    del refs[:2]
    if mixer:
        o_ref, ohalo_ref, wout_ref = refs[:3]
        del refs[:3]
    mod_ref, gam_ref, wup_ref, cw_ref, cb_ref, wdn_ref = refs[:6]
    del refs[:6]
    if final:
        gfin_ref = refs.pop(0)
    out_ref, h_ref, u0_ref, u1_ref, act_ref, acc_ref = refs[:6]
    if mixer:
        x1_ref = refs[6]
    u_refs = (u0_ref, u1_ref)
    gam, scale, shift = gam_ref[...], mod_ref[0, 4:5, :], mod_ref[0, 3:4, :]
    d_ff = wdn_ref.shape[0]
    nchunk = d_ff // FFN_TF
    tm = x_ref.shape[1]

    if mixer:
        x1 = x_ref[0] + mod_ref[0, 2:3, :] * _mm(o_ref[0], wout_ref[...])
        x1_ref[...] = x1
        halo = halo_ref[0] + mod_ref[0, 2:3, :] * _mm(ohalo_ref[0], wout_ref[...])[O_HALO - HALO:, :]
    else:
        x1, halo = x_ref[0], halo_ref[0]
    halo = _modnorm(halo, gam, scale, shift)
    halo = jnp.where(pl.program_id(1) == 0, 0.0, halo)
    h_ref[0:HALO, :] = halo.astype(h_ref.dtype)
    h_ref[HALO:HALO + tm, :] = _modnorm(x1, gam, scale, shift).astype(h_ref.dtype)

    def halves(c):
        return [slice(half * d_ff + c * FFN_TF, half * d_ff + (c + 1) * FFN_TF) for half in range(2)]

    def up(c):
        for half, cols in enumerate(halves(c)):
            u_refs[c % 2][:, half * FFN_TF:(half + 1) * FFN_TF] = _mm(h_ref[...], wup_ref[:, cols])

    def conv_act(c):
        u_ref = u_refs[c % 2]
        for rb in range(tm // FFN_ROWS):
            base = HALO + rb * FFN_ROWS
            ab = []
            for half, cols in enumerate(halves(c)):
                v = cb_ref[:, cols]
                for t in range(CONV_W):
                    off = base - (CONV_W - 1) + t
                    v = v + cw_ref[t:t + 1, cols] * u_ref[off:off + FFN_ROWS, half * FFN_TF:(half + 1) * FFN_TF]
                ab.append(v)
            act_ref[rb * FFN_ROWS:(rb + 1) * FFN_ROWS, c * FFN_TF:(c + 1) * FFN_TF] = (
                _silu(ab[0]) * ab[1]).astype(act_ref.dtype)

    def down(first, last):
        cols = slice(first * FFN_TF, last * FFN_TF)
        return _mm(act_ref[:, cols], wdn_ref[cols, :])

    up(0)
    start, pending = 0, None
    for c in range(nchunk):
        if c + 1 < nchunk:
            up(c + 1)
        if pending is not None:
            if pending[0] == 0:
                acc_ref[...] = down(*pending)
            else:
                acc_ref[...] += down(*pending)
            pending = None
        if c + 1 - start == FFN_DOWN_GROUP and c + 1 < nchunk:
            pending, start = (start, c + 1), c + 1
        conv_act(c)

    ffn = down(start, nchunk)
    if start > 0:
        ffn = ffn + acc_ref[...]
    y = (x1_ref[...] if mixer else x_ref[0]) + mod_ref[0, 5:6, :] * ffn
    if final:
        ms = jnp.mean(y * y, axis=-1, keepdims=True)
        y = y * lax.rsqrt(ms + EPS) * gfin_ref[...]
    out_ref[0] = y


def _conv_ffn(x, mod, gamma, w_up, conv_w, conv_b, w_down, gamma_final=None, mixer=None):
    bsz, s, d = x.shape
    d_ff = w_down.shape[0]
    final = gamma_final is not None
    rows = FFN_TM
    tiles_per_halo = rows // HALO
    resident = dict(pipeline_mode=pl.Buffered(1))
    in_specs = [
        pl.BlockSpec((1, rows, d), lambda b, i: (b, i, 0)),
        pl.BlockSpec((1, HALO, d), lambda b, i: (b, jnp.maximum(i * tiles_per_halo - 1, 0), 0)),
    ]
    args = [x, x]
    if mixer is not None:
        o, w_out = mixer
        k = o.shape[-1]
        in_specs += [
            pl.BlockSpec((1, rows, k), lambda b, i: (b, i, 0)),
            pl.BlockSpec((1, O_HALO, k), lambda b, i: (b, jnp.maximum(i * (rows // O_HALO) - 1, 0), 0)),
            pl.BlockSpec((k, d), lambda b, i: (0, 0), **resident),
        ]
        args += [o, o, w_out.astype(MXU_DTYPE)]
    in_specs += [
        pl.BlockSpec((1, 6, d), lambda b, i: (b, 0, 0)),
        pl.BlockSpec((1, d), lambda b, i: (0, 0)),
        pl.BlockSpec((d, 2 * d_ff), lambda b, i: (0, 0), **resident),
        pl.BlockSpec((CONV_W, 2 * d_ff), lambda b, i: (0, 0)),
        pl.BlockSpec((1, 2 * d_ff), lambda b, i: (0, 0)),
        pl.BlockSpec((d_ff, d), lambda b, i: (0, 0), **resident),
    ]
    args += [mod, gamma, w_up.astype(MXU_DTYPE), conv_w, conv_b[None, :], w_down.astype(MXU_DTYPE)]
    if final:
        in_specs.append(pl.BlockSpec((1, d), lambda b, i: (0, 0)))
        args.append(gamma_final)
    return pl.pallas_call(
        functools.partial(_ffn_kernel, final=final, mixer=mixer is not None),
        grid=(bsz, s // rows),
        in_specs=in_specs,
        out_specs=pl.BlockSpec((1, rows, d), lambda b, i: (b, i, 0)),
        out_shape=jax.ShapeDtypeStruct((bsz, s, d), f32),
        scratch_shapes=[
            pltpu.VMEM((rows + HALO, d), MXU_DTYPE),
            pltpu.VMEM((rows + HALO, 2 * FFN_TF), f32),
            pltpu.VMEM((rows + HALO, 2 * FFN_TF), f32),
            pltpu.VMEM((rows, d_ff), MXU_DTYPE),
            pltpu.VMEM((rows, d), f32),
        ] + ([pltpu.VMEM((rows, d), f32)] if mixer is not None else []),
        compiler_params=pltpu.CompilerParams(
            dimension_semantics=("parallel", "parallel"), vmem_limit_bytes=VMEM_LIMIT),
        name="conv_ffn_final" if final else "conv_ffn",
    )(*args)


def _gla_kernel(x_ref, mod_ref, gam_ref, win_ref, wglr_ref, wgate_ref, bgate_ref, gnorm_ref, wout_ref,
                out_ref, h_ref, proj0_ref, proj1_ref, gk0_ref, gk1_ref, og0_ref, og1_ref, state_ref):
    @pl.when(pl.program_id(1) == 0)
    def _():
        state_ref[...] = jnp.zeros_like(state_ref)

    proj_refs, gk_refs, og_refs = (proj0_ref, proj1_ref), (gk0_ref, gk1_ref), (og0_ref, og1_ref)
    nsub = x_ref.shape[1] // GLA_TM
    nchunk = GLA_TM // B_CHUNK
    n_main = win_ref.shape[1]

    ri = lax.broadcasted_iota(jnp.int32, (B_CHUNK, B_CHUNK), 0)
    ci = lax.broadcasted_iota(jnp.int32, (B_CHUNK, B_CHUNK), 1)
    causal = ci <= ri
    tri = causal.astype(MXU_DTYPE)
    gnorm = gnorm_ref[...]
    qscale = B_DK ** -0.5
    r_off = 2 * B_QK + B_V
    heads = [slice(hd * B_DK, (hd + 1) * B_DK) for hd in range(B_HEADS)]

    def project_steps(k):
        rows = slice(k * GLA_TM, (k + 1) * GLA_TM)

        def normalize():
            h = _modnorm(x_ref[0, rows, :], gam_ref[...], mod_ref[0, 1:2, :], mod_ref[0, 0:1, :])
            h_ref[...] = h.astype(h_ref.dtype)

        def piece(cols):
            def run():
                proj_refs[k % 2][:, cols] = _mm(h_ref[...], win_ref[:, cols])
            return run

        def gate():
            glr = _mm(h_ref[...], wglr_ref[...]).astype(MXU_DTYPE)
            z = _mm(glr, wgate_ref[...]) + bgate_ref[...]
            gk_refs[k % 2][...] = (jnp.minimum(z, 0.0) - jnp.log1p(jnp.exp(-jnp.abs(z)))) * (1.0 / B_TAU)

        pieces = [piece(slice(c0, c0 + GLA_PROJ_COLS)) for c0 in range(0, n_main, GLA_PROJ_COLS)]
        return [normalize] + pieces + [gate]

    def local(k, c):
        proj_ref, gk_ref = proj_refs[k % 2], gk_refs[k % 2]
        rows = slice(c * B_CHUNK, (c + 1) * B_CHUNK)
        gk = gk_ref[rows, :]
        g_hi = gk.astype(MXU_DTYPE)
        rem = gk - g_hi.astype(f32)
        g_mid = rem.astype(MXU_DTYPE)
        g_lo = (rem - g_mid.astype(f32)).astype(MXU_DTYPE)
        bcum = _mm(tri, g_hi) + _mm(tri, g_mid) + _mm(tri, g_lo)
        blast = bcum[B_CHUNK - 1:B_CHUNK, :]
        q_t = ((proj_ref[rows, 0:B_QK] * qscale) * jnp.exp(bcum)).astype(MXU_DTYPE)
        kk = proj_ref[rows, B_QK:2 * B_QK]
        k_t = (kk * jnp.exp(-bcum)).astype(MXU_DTYPE)
        k_d = (kk * jnp.exp(blast - bcum)).astype(MXU_DTYPE)
        v = [proj_ref[rows, 2 * B_QK + hd * B_DV:2 * B_QK + (hd + 1) * B_DV].astype(MXU_DTYPE)
             for hd in range(B_HEADS)]
        a = [jnp.where(causal, _mm_nt(q_t[:, ks], k_t[:, ks]), 0.0).astype(MXU_DTYPE) for ks in heads]
        kv = [_mm_tn(k_d[:, ks], v[hd]) for hd, ks in enumerate(heads)]
        o_intra = [_mm(a[hd], v[hd]) for hd in range(B_HEADS)]
        return q_t, blast, kv, o_intra

    def recur(k, c, loc, state):
        q_t, blast, kv, o_intra = loc
        rows = slice(c * B_CHUNK, (c + 1) * B_CHUNK)
        new_state = []
        for hd, ks in enumerate(heads):
            o = o_intra[hd] + _mm(q_t[:, ks], state[hd].astype(MXU_DTYPE))
            decay = jnp.exp(jnp.broadcast_to(blast[:, ks], (B_DK, B_DK)).T)
            new_state.append(jnp.concatenate([decay] * (B_DV // B_DK), axis=1) * state[hd] + kv[hd])
            o = o * lax.rsqrt(jnp.mean(o * o, axis=-1, keepdims=True) + EPS) * gnorm
            r = proj_refs[k % 2][rows, r_off + hd * B_DV:r_off + (hd + 1) * B_DV]
            og_refs[k % 2][rows, hd * B_DV:(hd + 1) * B_DV] = (o * _silu(r)).astype(og_refs[k % 2].dtype)
        return new_state

    for step in project_steps(0):
        step()
    state = [state_ref[hd] for hd in range(B_HEADS)]
    for k in range(nsub):
        ahead = project_steps(k + 1) if k + 1 < nsub else []
        loc = local(k, 0)
        for c in range(nchunk):
            nxt = local(k, c + 1) if c + 1 < nchunk else None
            state = recur(k, c, loc, state)
            loc = nxt
            take = -(-len(ahead) // (nchunk - c))
            for step in ahead[:take]:
                step()
            ahead = ahead[take:]
        rows = slice(k * GLA_TM, (k + 1) * GLA_TM)
        out_ref[0, rows, :] = x_ref[0, rows, :] + mod_ref[0, 2:3, :] * _mm(og_refs[k % 2][...], wout_ref[...])
    for hd in range(B_HEADS):
        state_ref[hd] = state[hd]


def _gla_layer(x, mod, gamma, w_in, w_gate, b_gate, g_norm, w_out):
    bsz, s, d = x.shape
    glr0 = 2 * B_QK + B_V
    w_main = jnp.concatenate([w_in[:, :glr0], w_in[:, glr0 + B_GATE_RANK:]], axis=1).astype(MXU_DTYPE)
    w_glr = jnp.pad(w_in[:, glr0:glr0 + B_GATE_RANK], ((0, 0), (0, LANES - B_GATE_RANK))).astype(MXU_DTYPE)
    w_gate_p = jnp.pad(w_gate, ((0, LANES - B_GATE_RANK), (0, 0))).astype(MXU_DTYPE)
    n_main = w_main.shape[1]
    rows = GLA_SUB * GLA_TM
    const = lambda b, i: (0, 0)
    resident = dict(pipeline_mode=pl.Buffered(1))
    return pl.pallas_call(
        _gla_kernel,
        grid=(bsz, s // rows),
        in_specs=[
            pl.BlockSpec((1, rows, d), lambda b, i: (b, i, 0)),
            pl.BlockSpec((1, 6, d), lambda b, i: (b, 0, 0)),
            pl.BlockSpec((1, d), const),
            pl.BlockSpec((d, n_main), const, **resident),
            pl.BlockSpec((d, LANES), const),
            pl.BlockSpec((LANES, B_QK), const),
            pl.BlockSpec((1, B_QK), const),
            pl.BlockSpec((1, B_DV), const),
            pl.BlockSpec((B_V, d), const, **resident),
        ],
        out_specs=pl.BlockSpec((1, rows, d), lambda b, i: (b, i, 0)),
        out_shape=jax.ShapeDtypeStruct((bsz, s, d), f32),
        scratch_shapes=[
            pltpu.VMEM((GLA_TM, d), MXU_DTYPE),
            pltpu.VMEM((GLA_TM, n_main), f32),
            pltpu.VMEM((GLA_TM, n_main), f32),
            pltpu.VMEM((GLA_TM, B_QK), f32),
            pltpu.VMEM((GLA_TM, B_QK), f32),
            pltpu.VMEM((GLA_TM, B_V), MXU_DTYPE),
            pltpu.VMEM((GLA_TM, B_V), MXU_DTYPE),
            pltpu.VMEM((B_HEADS, B_DK, B_DV), f32),
        ],
        compiler_params=pltpu.CompilerParams(
            dimension_semantics=("parallel", "arbitrary"), vmem_limit_bytes=VMEM_LIMIT),
        name="gla_layer",
    )(x, mod, gamma, w_main, w_glr, w_gate_p, b_gate[None, :], g_norm[None, :], w_out.astype(MXU_DTYPE))


def _qkv_weight(w_in):
    width = A_HEADS * A_HEAD_DIM
    col_scale = jnp.tile(jnp.repeat(jnp.array([A_HEAD_DIM ** -0.5 * LOG2E, 1.0, 1.0], f32), width), A_GROUPS)
    return (w_in * col_scale[None, :]).astype(MXU_DTYPE)


def kernel(x, c, w_in_a, w_out_a, rel_bias, w_in_b, w_gate_b, b_gate_b, gnorm_b, w_out_b, norm_mix, norm_ffn, w_ada, b_ada, w_up, conv_w, conv_b, w_down, norm_final):
    depth = w_ada.shape[0]
    mod = _adaln(c, w_ada, b_ada)
    bias = _rel_bias(rel_bias)
    for i in range(depth):
        gam_mix = norm_mix[i][None, :]
        j = i // 2
        mixer = None
        if i % 2 == 0:
            qkv = _qkv_proj(x, mod[i], gam_mix, _qkv_weight(w_in_a[j]))
            mixer = (_attention(qkv, bias), w_out_a[j])
        else:
            x = _gla_layer(x, mod[i], gam_mix, w_in_b[j], w_gate_b[j], b_gate_b[j], gnorm_b[j], w_out_b[j])
        last = i == depth - 1
        x = _conv_ffn(x, mod[i], norm_ffn[i][None, :], w_up[i], conv_w[i], conv_b[i], w_down[i],
                      gamma_final=norm_final[None, :] if last else None, mixer=mixer)
    return x
```

```python
import functools
import math

import jax
import jax.numpy as jnp
from jax import lax
from jax.experimental import pallas as pl
from jax.experimental.pallas import tpu as pltpu

A_CONFIGS = ((128, 1), (512, 4), (2048, 16))
A_GROUPS = len(A_CONFIGS)
A_HEADS = 16
A_HEAD_DIM = 64
N_BUCKETS = 32
MAX_DISTANCE = 2048
B_HEADS = 4
B_DK = 128
B_DV = 256
B_QK = B_HEADS * B_DK
B_V = B_HEADS * B_DV
B_GATE_RANK = 16
B_TAU = 16.0
B_CHUNK = 64
CONV_W = 3
EPS = 1e-6
NEG_INF = -1e30
LOG2E = math.log2(math.e)

LANES = 128
SUBLANES = 8
MXU_DTYPE = jnp.bfloat16
ATT_BLK = 128
ATT_TILE = 2048
ATT_UNITS = ATT_TILE // ATT_BLK
ATT_PIPELINE = 3
PAIR_W = 3 * LANES
N_PAIRS = A_HEADS // 2
QKV_ROWS = 1024
PROBE_ROWS = 16
FFN_TM = 512
FFN_TF = 256
FFN_ROWS = 256
FFN_DOWN_GROUP = 2
HALO = 8
LEAD = (CONV_W - 1) * SUBLANES
O_HALO = 16
GLA_TM = 512
GLA_SUB = 2
GLA_PROJ_COLS = 256
VMEM_LIMIT = 56 * 1024 * 1024

f32 = jnp.float32


def _mm(a, b):
    return jnp.dot(a, b, preferred_element_type=f32)


def _mm_nt(a, b):
    return lax.dot_general(a, b, (((1,), (1,)), ((), ())), preferred_element_type=f32)


def _mm_tn(a, b):
    return lax.dot_general(a, b, (((0,), (0,)), ((), ())), preferred_element_type=f32)


def _modnorm(x, gamma, scale, shift):
    ms = jnp.mean(x * x, axis=-1, keepdims=True)
    y = x * lax.rsqrt(ms + EPS) * gamma
    return y * (1.0 + scale) + shift


def _silu(x):
    return x * (1.0 / (1.0 + jnp.exp(-x)))


def _regroup(t, groups):
    rows, d = t.shape
    return t.reshape(groups, rows // groups, d).swapaxes(0, 1).reshape(rows, d)


def _adaln_kernel(c_ref, w_ref, b_ref, o_ref):
    s = _silu(c_ref[...]).astype(MXU_DTYPE)
    o_ref[0] = _mm(s, w_ref[0].astype(MXU_DTYPE)) + b_ref[0]


def _adaln(c, w_ada, b_ada):
    depth, d, n = w_ada.shape
    bsz = c.shape[0]
    rows = 8 * pl.cdiv(bsz, 8)
    c_pad = jnp.pad(c, ((0, rows - bsz), (0, 0)))
    out = pl.pallas_call(
        _adaln_kernel,
        grid=(depth, n // d),
        in_specs=[
            pl.BlockSpec((rows, d), lambda l, j: (0, 0)),
            pl.BlockSpec((1, d, d), lambda l, j: (l, 0, j)),
            pl.BlockSpec((1, 1, d), lambda l, j: (l, 0, j)),
        ],
        out_specs=pl.BlockSpec((1, rows, d), lambda l, j: (l, 0, j)),
        out_shape=jax.ShapeDtypeStruct((depth, rows, n), f32),
        compiler_params=pltpu.CompilerParams(
            dimension_semantics=("parallel", "parallel"), vmem_limit_bytes=VMEM_LIMIT),
        name="adaln",
    )(c_pad, w_ada, b_ada.reshape(depth, 1, n))
    return out[:, :bsz].reshape(depth, bsz, n // d, d)


def _qkv_kernel(x_ref, mod_ref, gam_ref, wq_ref, wk_ref, wv_ref, o_ref, ha_ref, hb_ref, sa_ref, sb_ref, *,
                dilations, tiles_per_group):
    j = pl.program_id(2)
    g = j // tiles_per_group
    t = j % tiles_per_group
    nlb = sa_ref.shape[0]
    chunks_per_step = ATT_UNITS // tiles_per_group
    h_refs, stage_refs = [ha_ref, hb_ref], [sa_ref, sb_ref]

    @pl.when(j == 0)
    def _():
        for c in range(ATT_UNITS):
            rows = slice(c * ATT_BLK, (c + 1) * ATT_BLK)
            hn = _modnorm(x_ref[0, rows, :], gam_ref[...], mod_ref[0, 1:2, :], mod_ref[0, 0:1, :])
            ha_ref[rows, :] = hn.astype(ha_ref.dtype)
            for lb in range(nlb):
                sa_ref[lb, rows, :] = hn[:, lb * LANES:(lb + 1) * LANES]

    def deinterleave_chunk(gi, cc):
        d, dp = dilations[gi], dilations[gi - 1]
        rel = d // dp
        per = ATT_UNITS // d
        c = t * chunks_per_step + cc
        r, q = c // per, c % per
        start = (r % dp) * (ATT_TILE // dp) + q * (ATT_BLK * rel) + r // dp
        rows = pl.ds(start, ATT_BLK, stride=rel)
        r0 = pl.multiple_of(c * ATT_BLK, ATT_BLK)
        parts = [stage_refs[(gi - 1) % 2][lb, rows, :] for lb in range(nlb)]
        if gi + 1 < len(dilations):
            for lb in range(nlb):
                stage_refs[gi % 2][lb, pl.ds(r0, ATT_BLK), :] = parts[lb]
        dst_ref = h_refs[gi % 2]
        dst_ref[pl.ds(r0, ATT_BLK), :] = jnp.concatenate(parts, axis=1).astype(dst_ref.dtype)
        probe = dst_ref[pl.ds(r0, PROBE_ROWS), 0:LANES]
        return (probe != probe) & (probe == probe)

    def project_chunk(src_ref, rc, never):
        rows = slice(rc * QKV_ROWS, (rc + 1) * QKV_ROWS)
        lhs = src_ref[rows, :]
        for k, w_ref in enumerate((wq_ref, wk_ref, wv_ref)):
            res = _mm(lhs, w_ref[...]).astype(o_ref.dtype)
            lanes = slice(k * LANES, (k + 1) * LANES)
            o_ref[0, 0, 1, rows, lanes] = res[:, LANES:]
            if never is not None and k == 0:
                top = rc * QKV_ROWS + PROBE_ROWS
                o_ref[0, 0, 0, rc * QKV_ROWS:top, lanes] = jnp.where(
                    never, jnp.zeros_like(res[:PROBE_ROWS, :LANES]), res[:PROBE_ROWS, :LANES])
                o_ref[0, 0, 0, top:(rc + 1) * QKV_ROWS, lanes] = res[PROBE_ROWS:, :LANES]
            else:
                o_ref[0, 0, 0, rows, lanes] = res[:, :LANES]

    assert dilations[0] == 1
    assert all(d % dp == 0 for dp, d in zip(dilations, dilations[1:]))
    n_mm = ATT_TILE // QKV_ROWS
    for gi in range(len(dilations)):
        @pl.when(g == gi)
        def _(gi=gi):
            slots = max(n_mm - 1, 1)
            shares = [list(range(chunks_per_step))[sl::slots] for sl in range(slots)] + [[]]
            for rc in range(n_mm):
                never = None
                if gi + 1 < len(dilations):
                    for cc in shares[rc]:
                        probe = deinterleave_chunk(gi + 1, cc)
                        never = probe if never is None else never | probe
                project_chunk(h_refs[gi % 2], rc, never)


def _qkv_proj(x, mod, gamma, w):
    bsz, s, d = x.shape
    dilations = tuple(dl for _, dl in A_CONFIGS)
    tn = 2 * LANES
    tiles_per_group = A_HEADS * A_HEAD_DIM // tn
    kern = functools.partial(_qkv_kernel, dilations=dilations, tiles_per_group=tiles_per_group)

    def w_spec(k):
        return pl.BlockSpec(
            (d, tn), lambda b, i, j: (0, (j // tiles_per_group * 3 + k) * tiles_per_group + j % tiles_per_group))

    return pl.pallas_call(
        kern,
        grid=(bsz, s // ATT_TILE, A_GROUPS * tiles_per_group),
        in_specs=[
            pl.BlockSpec((1, ATT_TILE, d), lambda b, i, j: (b, i, 0)),
            pl.BlockSpec((1, 6, d), lambda b, i, j: (b, 0, 0)),
            pl.BlockSpec((1, d), lambda b, i, j: (0, 0)),
            w_spec(0), w_spec(1), w_spec(2),
        ],
        out_specs=pl.BlockSpec(
            (1, 1, 2, ATT_TILE, PAIR_W),
            lambda b, i, j: (j // tiles_per_group, b, j % tiles_per_group, i, 0)),
        out_shape=jax.ShapeDtypeStruct((A_GROUPS, bsz, N_PAIRS, s, PAIR_W), MXU_DTYPE),
        scratch_shapes=[pltpu.VMEM((ATT_TILE, d), MXU_DTYPE),
                        pltpu.VMEM((ATT_TILE, d), MXU_DTYPE),
                        pltpu.VMEM((d // LANES, ATT_TILE, LANES), f32),
                        pltpu.VMEM((d // LANES, ATT_TILE, LANES), f32)],
        compiler_params=pltpu.CompilerParams(
            dimension_semantics=("parallel", "parallel", "arbitrary"),
            vmem_limit_bytes=VMEM_LIMIT),
        name="qkv_proj",
    )(x, mod, gamma, w, w, w)


def _bias_kernel(tab_ref, bkt_ref, o_ref):
    bkt = bkt_ref[0]
    hits = [(bkt >= k) & (bkt < k + 1) for k in range(N_BUCKETS)]
    for h in range(A_HEADS):
        col = pl.program_id(0) * A_HEADS + h
        acc = jnp.full(bkt.shape, NEG_INF, f32)
        for k in range(N_BUCKETS):
            acc = jnp.where(hits[k], tab_ref[k, col] * LOG2E, acc)
        o_ref[0, h] = acc


def _t5_bucket(dist):
    max_exact = N_BUCKETS // 2
    n = jnp.maximum(dist, max_exact).astype(f32)
    large = max_exact + (jnp.log(n / max_exact) / math.log(MAX_DISTANCE / max_exact)
                         * (N_BUCKETS - max_exact)).astype(jnp.int32)
    large = jnp.minimum(large, N_BUCKETS - 1)
    return jnp.where(dist < max_exact, dist, large)


def _rel_bias(rel_bias):
    qi = jnp.arange(ATT_BLK)[:, None]
    ki = jnp.arange(2 * ATT_BLK)[None, :]
    steps = qi + ATT_BLK - ki
    band = (steps >= 0) & (steps <= ATT_BLK)
    bucket = jnp.stack([
        jnp.where(band, _t5_bucket(jnp.clip(steps, 0, ATT_BLK) * dl), -1)
        for _, dl in A_CONFIGS]).astype(jnp.int32)
    return pl.pallas_call(
        _bias_kernel,
        grid=(A_GROUPS,),
        in_specs=[
            pl.BlockSpec(memory_space=pltpu.SMEM),
            pl.BlockSpec((1, ATT_BLK, 2 * ATT_BLK), lambda g: (g, 0, 0)),
        ],
        out_specs=pl.BlockSpec((1, A_HEADS, ATT_BLK, 2 * ATT_BLK), lambda g: (g, 0, 0, 0)),
        out_shape=jax.ShapeDtypeStruct((A_GROUPS, A_HEADS, ATT_BLK, 2 * ATT_BLK), f32),
        name="rel_bias",
    )(rel_bias, bucket)


def _attn_kernel(qkv_ref, bias_ref, out_ref, prev0, prev1, prev2, o1, l1, n1, o2, l2, n2, biasp):
    first = pl.program_id(2) == 0

    @pl.when(first)
    def _():
        prev0[...] = jnp.zeros_like(prev0)
        prev1[...] = jnp.zeros_like(prev1)
        prev2[...] = jnp.zeros_like(prev2)

    for g in range(A_GROUPS):
        for hh in range(2):
            biasp[g, hh, :, 0:ATT_BLK] = jnp.where(first, NEG_INF, bias_ref[g, hh, :, 0:ATT_BLK])
            biasp[g, hh, :, ATT_BLK:2 * ATT_BLK] = bias_ref[g, hh, :, ATT_BLK:2 * ATT_BLK]

    lane = lax.broadcasted_iota(jnp.int32, (1, LANES), 1)
    lo = lane < A_HEAD_DIM
    qmask = (jnp.where(lo, 1.0, 0.0).astype(MXU_DTYPE), jnp.where(lo, 0.0, 1.0).astype(MXU_DTYPE))

    def keys_values(g, rc, prev_ref, rp, lanes):
        if prev_ref is None:
            return qkv_ref[g, 0, 0, rp:rp + 2 * ATT_BLK, lanes]
        off = lanes.start - LANES
        return jnp.concatenate([prev_ref[rp:rp + ATT_BLK, off:off + LANES],
                                qkv_ref[g, 0, 0, rc:rc + ATT_BLK, lanes]], axis=0)

    def scores(g, rc, prev_ref, rp):
        q = qkv_ref[g, 0, 0, rc:rc + ATT_BLK, 0:LANES]
        k = keys_values(g, rc, prev_ref, rp, slice(LANES, 2 * LANES))
        bias = bias_ref if prev_ref is None else biasp
        return [_mm_nt(q * qmask[hh], k) + bias[g, hh] for hh in range(2)]

    def attend(g, rc, prev_ref, rp, s):
        v = keys_values(g, rc, prev_ref, rp, slice(2 * LANES, 3 * LANES))
        accs, dens, ms = [], [], []
        for sh in s:
            m = jnp.max(sh, axis=-1, keepdims=True)
            p = jnp.exp2(sh - m)
            dens.append(jnp.sum(p, axis=-1, keepdims=True))
            accs.append(_mm(p.astype(MXU_DTYPE), v))
            ms.append(m)
        acc = jnp.where(lo, accs[0], accs[1])
        return (acc, jnp.broadcast_to(jnp.where(lo, ms[0], ms[1]), acc.shape),
                jnp.broadcast_to(jnp.where(lo, dens[0], dens[1]), acc.shape))

    def store(a_ref, m_ref, d_ref, rows):
        def post(acc, m, den):
            a_ref[rows, :] = acc
            m_ref[rows, :] = m
            d_ref[rows, :] = den
        return post

    def merge(rows):
        def post(acc0, m0, den0):
            m1, m2 = l1[rows, :], l2[rows, :]
            mx = jnp.maximum(jnp.maximum(m0, m1), m2)
            e0, e1, e2 = jnp.exp2(m0 - mx), jnp.exp2(m1 - mx), jnp.exp2(m2 - mx)
            num = e0 * acc0 + e1 * o1[rows, :] + e2 * o2[rows, :]
            den = e0 * den0 + e1 * n1[rows, :] + e2 * n2[rows, :]
            out_ref[0, rows, :] = (num * (1.0 / den)).astype(out_ref.dtype)
        return post

    units = []
    d2 = A_CONFIGS[2][1]
    for u in range(ATT_UNITS):
        units.append((2, u * ATT_BLK, prev2, u * ATT_BLK, store(o2, l2, n2, pl.ds(u, ATT_BLK, stride=d2))))
    d1 = A_CONFIGS[1][1]
    per = ATT_UNITS // d1
    for u in range(ATT_UNITS):
        r, q = divmod(u, per)
        rc = u * ATT_BLK
        post = store(o1, l1, n1, pl.ds(q * ATT_BLK * d1 + r, ATT_BLK, stride=d1))
        if q == 0:
            units.append((1, rc, prev1, (r * per + per - 1) * ATT_BLK, post))
        else:
            units.append((1, rc, None, rc - ATT_BLK, post))
    for u in range(ATT_UNITS):
        rc = u * ATT_BLK
        post = merge(slice(rc, rc + ATT_BLK))
        units.append((0, rc, prev0, 0, post) if u == 0 else (0, rc, None, rc - ATT_BLK, post))

    pending = [scores(*un[:4]) for un in units[:ATT_PIPELINE]]
    for idx, un in enumerate(units):
        if idx + ATT_PIPELINE < len(units):
            pending.append(scores(*units[idx + ATT_PIPELINE][:4]))
        un[4](*attend(*un[:4], pending.pop(0)))

    prev0[...] = qkv_ref[0, 0, 0, ATT_TILE - ATT_BLK:ATT_TILE, LANES:3 * LANES]
    prev1[...] = qkv_ref[1, 0, 0, :, LANES:3 * LANES]
    prev2[...] = qkv_ref[2, 0, 0, :, LANES:3 * LANES]


def _attention(qkv, bias):
    _, bsz, _, s, _ = qkv.shape
    return pl.pallas_call(
        _attn_kernel,
        grid=(bsz, N_PAIRS, s // ATT_TILE),
        in_specs=[
            pl.BlockSpec((A_GROUPS, 1, 1, ATT_TILE, PAIR_W), lambda b, p, i: (0, b, p, i, 0)),
            pl.BlockSpec((A_GROUPS, 2, ATT_BLK, 2 * ATT_BLK), lambda b, p, i: (0, p, 0, 0)),
        ],
        out_specs=pl.BlockSpec((1, ATT_TILE, LANES), lambda b, p, i: (b, i, p)),
        out_shape=jax.ShapeDtypeStruct((bsz, s, A_HEADS * A_HEAD_DIM), MXU_DTYPE),
        scratch_shapes=[
            pltpu.VMEM((ATT_BLK, 2 * LANES), MXU_DTYPE),
            pltpu.VMEM((ATT_TILE, 2 * LANES), MXU_DTYPE),
            pltpu.VMEM((ATT_TILE, 2 * LANES), MXU_DTYPE),
            pltpu.VMEM((ATT_TILE, LANES), f32),
            pltpu.VMEM((ATT_TILE, LANES), f32),
            pltpu.VMEM((ATT_TILE, LANES), f32),
            pltpu.VMEM((ATT_TILE, LANES), f32),
            pltpu.VMEM((ATT_TILE, LANES), f32),
            pltpu.VMEM((ATT_TILE, LANES), f32),
            pltpu.VMEM((A_GROUPS, 2, ATT_BLK, 2 * ATT_BLK), f32),
        ],
        compiler_params=pltpu.CompilerParams(
            dimension_semantics=("parallel", "parallel", "arbitrary"),
            vmem_limit_bytes=VMEM_LIMIT),
        name="dilated_attn",
    )(qkv, bias)


def _ffn_kernel(*refs, final, mixer):
    refs = list(refs)
    x_ref, halo_ref = refs[:2]
    del refs[:2]
    if mixer:
        o_ref, ohalo_ref, wout_ref = refs[:3]
        del refs[:3]
    mod_ref, gam_ref, wup_ref, cw_ref, cb_ref, wdn_ref = refs[:6]
    del refs[:6]
    if final:
        gfin_ref = refs.pop(0)
    out_ref, h_ref, u0_ref, u1_ref, act_ref, acc_ref = refs[:6]
    if mixer:
        x1_ref = refs[6]
    u_refs = (u0_ref, u1_ref)
    gam, scale, shift = gam_ref[...], mod_ref[0, 4:5, :], mod_ref[0, 3:4, :]
    d_ff = wdn_ref.shape[0]
    nchunk = d_ff // FFN_TF
    tm = x_ref.shape[1]

    if mixer:
        x1 = x_ref[0] + mod_ref[0, 2:3, :] * _mm(o_ref[0], wout_ref[...])
        x1_ref[...] = x1
        halo = halo_ref[0] + mod_ref[0, 2:3, :] * _mm(ohalo_ref[0], wout_ref[...])[O_HALO - HALO:, :]
    else:
        x1, halo = x_ref[0], halo_ref[0]
    halo = _modnorm(halo, gam, scale, shift)
    halo = jnp.where(pl.program_id(1) == 0, 0.0, halo)
    hp = _regroup(_modnorm(x1, gam, scale, shift), SUBLANES)
    for back in range(1, CONV_W):
        last = hp[tm - back * SUBLANES:tm - (back - 1) * SUBLANES, :]
        lead = jnp.concatenate([halo[HALO - back:HALO - back + 1, :], last[:SUBLANES - 1, :]], axis=0)
        h_ref[(CONV_W - 1 - back) * SUBLANES:(CONV_W - back) * SUBLANES, :] = lead.astype(h_ref.dtype)
    h_ref[LEAD:LEAD + tm, :] = hp.astype(h_ref.dtype)

    def halves(c):
        return [slice(half * d_ff + c * FFN_TF, half * d_ff + (c + 1) * FFN_TF) for half in range(2)]

    def up(c):
        for half, cols in enumerate(halves(c)):
            u_refs[c % 2][:, half * FFN_TF:(half + 1) * FFN_TF] = _mm(h_ref[...], wup_ref[:, cols])

    def conv_act(c):
        u_ref = u_refs[c % 2]
        for rb in range(tm // FFN_ROWS):
            ab = []
            for half, cols in enumerate(halves(c)):
                v = cb_ref[:, cols]
                for t in range(CONV_W):
                    off = rb * FFN_ROWS + t * SUBLANES
                    v = v + cw_ref[t:t + 1, cols] * u_ref[off:off + FFN_ROWS, half * FFN_TF:(half + 1) * FFN_TF]
                ab.append(v)
            act_ref[rb * FFN_ROWS:(rb + 1) * FFN_ROWS, c * FFN_TF:(c + 1) * FFN_TF] = (
                _silu(ab[0]) * ab[1]).astype(act_ref.dtype)

    def down(first, last):
        cols = slice(first * FFN_TF, last * FFN_TF)
        return _mm(act_ref[:, cols], wdn_ref[cols, :])

    up(0)
    start, pending = 0, None
    for c in range(nchunk):
        if c + 1 < nchunk:
            up(c + 1)
        if pending is not None:
            if pending[0] == 0:
                acc_ref[...] = down(*pending)
            else:
                acc_ref[...] += down(*pending)
            pending = None
        if c + 1 - start == FFN_DOWN_GROUP and c + 1 < nchunk:
            pending, start = (start, c + 1), c + 1
        conv_act(c)

    ffn = down(start, nchunk)
    if start > 0:
        ffn = ffn + acc_ref[...]
    ffn = _regroup(ffn, tm // SUBLANES)
    y = (x1_ref[...] if mixer else x_ref[0]) + mod_ref[0, 5:6, :] * ffn
    if final:
        ms = jnp.mean(y * y, axis=-1, keepdims=True)
        y = y * lax.rsqrt(ms + EPS) * gfin_ref[...]
    out_ref[0] = y


def _conv_ffn(x, mod, gamma, w_up, conv_w, conv_b, w_down, gamma_final=None, mixer=None):
    bsz, s, d = x.shape
    d_ff = w_down.shape[0]
    final = gamma_final is not None
    rows = FFN_TM
    tiles_per_halo = rows // HALO
    resident = dict(pipeline_mode=pl.Buffered(1))
    in_specs = [
        pl.BlockSpec((1, rows, d), lambda b, i: (b, i, 0)),
        pl.BlockSpec((1, HALO, d), lambda b, i: (b, jnp.maximum(i * tiles_per_halo - 1, 0), 0)),
    ]
    args = [x, x]
    if mixer is not None:
        o, w_out = mixer
        k = o.shape[-1]
        in_specs += [
            pl.BlockSpec((1, rows, k), lambda b, i: (b, i, 0)),
            pl.BlockSpec((1, O_HALO, k), lambda b, i: (b, jnp.maximum(i * (rows // O_HALO) - 1, 0), 0)),
            pl.BlockSpec((k, d), lambda b, i: (0, 0), **resident),
        ]
        args += [o, o, w_out.astype(MXU_DTYPE)]
    in_specs += [
        pl.BlockSpec((1, 6, d), lambda b, i: (b, 0, 0)),
        pl.BlockSpec((1, d), lambda b, i: (0, 0)),
        pl.BlockSpec((d, 2 * d_ff), lambda b, i: (0, 0), **resident),
        pl.BlockSpec((CONV_W, 2 * d_ff), lambda b, i: (0, 0)),
        pl.BlockSpec((1, 2 * d_ff), lambda b, i: (0, 0)),
        pl.BlockSpec((d_ff, d), lambda b, i: (0, 0), **resident),
    ]
    args += [mod, gamma, w_up.astype(MXU_DTYPE), conv_w, conv_b[None, :], w_down.astype(MXU_DTYPE)]
    if final:
        in_specs.append(pl.BlockSpec((1, d), lambda b, i: (0, 0)))
        args.append(gamma_final)
    return pl.pallas_call(
        functools.partial(_ffn_kernel, final=final, mixer=mixer is not None),
        grid=(bsz, s // rows),
        in_specs=in_specs,
        out_specs=pl.BlockSpec((1, rows, d), lambda b, i: (b, i, 0)),
        out_shape=jax.ShapeDtypeStruct((bsz, s, d), f32),
        scratch_shapes=[
            pltpu.VMEM((rows + LEAD, d), MXU_DTYPE),
            pltpu.VMEM((rows + LEAD, 2 * FFN_TF), f32),
            pltpu.VMEM((rows + LEAD, 2 * FFN_TF), f32),
            pltpu.VMEM((rows, d_ff), MXU_DTYPE),
            pltpu.VMEM((rows, d), f32),
        ] + ([pltpu.VMEM((rows, d), f32)] if mixer is not None else []),
        compiler_params=pltpu.CompilerParams(
            dimension_semantics=("parallel", "parallel"), vmem_limit_bytes=VMEM_LIMIT),
        name="conv_ffn_final" if final else "conv_ffn",
    )(*args)


def _gla_kernel(x_ref, mod_ref, gam_ref, win_ref, wglr_ref, wgate_ref, bgate_ref, gnorm_ref, wout_ref,
                out_ref, h_ref, proj0_ref, proj1_ref, gk0_ref, gk1_ref, og0_ref, og1_ref, state_ref):
    @pl.when(pl.program_id(1) == 0)
    def _():
        state_ref[...] = jnp.zeros_like(state_ref)

    proj_refs, gk_refs, og_refs = (proj0_ref, proj1_ref), (gk0_ref, gk1_ref), (og0_ref, og1_ref)
    nsub = x_ref.shape[1] // GLA_TM
    nchunk = GLA_TM // B_CHUNK
    n_main = win_ref.shape[1]

    ri = lax.broadcasted_iota(jnp.int32, (B_CHUNK, B_CHUNK), 0)
    ci = lax.broadcasted_iota(jnp.int32, (B_CHUNK, B_CHUNK), 1)
    causal = ci <= ri
    tri = causal.astype(MXU_DTYPE)
    gnorm = gnorm_ref[...]
    qscale = B_DK ** -0.5
    r_off = 2 * B_QK + B_V
    heads = [slice(hd * B_DK, (hd + 1) * B_DK) for hd in range(B_HEADS)]

    def project_steps(k):
        rows = slice(k * GLA_TM, (k + 1) * GLA_TM)

        def normalize():
            h = _modnorm(x_ref[0, rows, :], gam_ref[...], mod_ref[0, 1:2, :], mod_ref[0, 0:1, :])
            h_ref[...] = h.astype(h_ref.dtype)

        def piece(cols):
            def run():
                proj_refs[k % 2][:, cols] = _mm(h_ref[...], win_ref[:, cols])
            return run

        def gate():
            glr = _mm(h_ref[...], wglr_ref[...]).astype(MXU_DTYPE)
            z = _mm(glr, wgate_ref[...]) + bgate_ref[...]
            gk_refs[k % 2][...] = (jnp.minimum(z, 0.0) - jnp.log1p(jnp.exp(-jnp.abs(z)))) * (1.0 / B_TAU)

        pieces = [piece(slice(c0, c0 + GLA_PROJ_COLS)) for c0 in range(0, n_main, GLA_PROJ_COLS)]
        return [normalize] + pieces + [gate]

    def local(k, c):
        proj_ref, gk_ref = proj_refs[k % 2], gk_refs[k % 2]
        rows = slice(c * B_CHUNK, (c + 1) * B_CHUNK)
        gk = gk_ref[rows, :]
        g_hi = gk.astype(MXU_DTYPE)
        rem = gk - g_hi.astype(f32)
        g_mid = rem.astype(MXU_DTYPE)
        g_lo = (rem - g_mid.astype(f32)).astype(MXU_DTYPE)
        bcum = _mm(tri, g_hi) + _mm(tri, g_mid) + _mm(tri, g_lo)
        blast = bcum[B_CHUNK - 1:B_CHUNK, :]
        q_t = ((proj_ref[rows, 0:B_QK] * qscale) * jnp.exp(bcum)).astype(MXU_DTYPE)
        kk = proj_ref[rows, B_QK:2 * B_QK]
        k_t = (kk * jnp.exp(-bcum)).astype(MXU_DTYPE)
        k_d = (kk * jnp.exp(blast - bcum)).astype(MXU_DTYPE)
        v = [proj_ref[rows, 2 * B_QK + hd * B_DV:2 * B_QK + (hd + 1) * B_DV].astype(MXU_DTYPE)
             for hd in range(B_HEADS)]
        a = [jnp.where(causal, _mm_nt(q_t[:, ks], k_t[:, ks]), 0.0).astype(MXU_DTYPE) for ks in heads]
        kv = [_mm_tn(k_d[:, ks], v[hd]) for hd, ks in enumerate(heads)]
        o_intra = [_mm(a[hd], v[hd]) for hd in range(B_HEADS)]
        return q_t, blast, kv, o_intra

    def recur(k, c, loc, state):
        q_t, blast, kv, o_intra = loc
        rows = slice(c * B_CHUNK, (c + 1) * B_CHUNK)
        new_state = []
        for hd, ks in enumerate(heads):
            o = o_intra[hd] + _mm(q_t[:, ks], state[hd].astype(MXU_DTYPE))
            decay = jnp.exp(jnp.broadcast_to(blast[:, ks], (B_DK, B_DK)).T)
            new_state.append(jnp.concatenate([decay] * (B_DV // B_DK), axis=1) * state[hd] + kv[hd])
            o = o * lax.rsqrt(jnp.mean(o * o, axis=-1, keepdims=True) + EPS) * gnorm
            r = proj_refs[k % 2][rows, r_off + hd * B_DV:r_off + (hd + 1) * B_DV]
            og_refs[k % 2][rows, hd * B_DV:(hd + 1) * B_DV] = (o * _silu(r)).astype(og_refs[k % 2].dtype)
        return new_state

    for step in project_steps(0):
        step()
    state = [state_ref[hd] for hd in range(B_HEADS)]
    for k in range(nsub):
        ahead = project_steps(k + 1) if k + 1 < nsub else []
        loc = local(k, 0)
        for c in range(nchunk):
            nxt = local(k, c + 1) if c + 1 < nchunk else None
            state = recur(k, c, loc, state)
            loc = nxt
            take = -(-len(ahead) // (nchunk - c))
            for step in ahead[:take]:
                step()
            ahead = ahead[take:]
        rows = slice(k * GLA_TM, (k + 1) * GLA_TM)
        out_ref[0, rows, :] = x_ref[0, rows, :] + mod_ref[0, 2:3, :] * _mm(og_refs[k % 2][...], wout_ref[...])
    for hd in range(B_HEADS):
        state_ref[hd] = state[hd]


def _gla_layer(x, mod, gamma, w_in, w_gate, b_gate, g_norm, w_out):
    bsz, s, d = x.shape
    glr0 = 2 * B_QK + B_V
    w_main = jnp.concatenate([w_in[:, :glr0], w_in[:, glr0 + B_GATE_RANK:]], axis=1).astype(MXU_DTYPE)
    w_glr = jnp.pad(w_in[:, glr0:glr0 + B_GATE_RANK], ((0, 0), (0, LANES - B_GATE_RANK))).astype(MXU_DTYPE)
    w_gate_p = jnp.pad(w_gate, ((0, LANES - B_GATE_RANK), (0, 0))).astype(MXU_DTYPE)
    n_main = w_main.shape[1]
    rows = GLA_SUB * GLA_TM
    const = lambda b, i: (0, 0)
    resident = dict(pipeline_mode=pl.Buffered(1))
    return pl.pallas_call(
        _gla_kernel,
        grid=(bsz, s // rows),
        in_specs=[
            pl.BlockSpec((1, rows, d), lambda b, i: (b, i, 0)),
            pl.BlockSpec((1, 6, d), lambda b, i: (b, 0, 0)),
            pl.BlockSpec((1, d), const),
            pl.BlockSpec((d, n_main), const, **resident),
            pl.BlockSpec((d, LANES), const),
            pl.BlockSpec((LANES, B_QK), const),
            pl.BlockSpec((1, B_QK), const),
            pl.BlockSpec((1, B_DV), const),
            pl.BlockSpec((B_V, d), const, **resident),
        ],
        out_specs=pl.BlockSpec((1, rows, d), lambda b, i: (b, i, 0)),
        out_shape=jax.ShapeDtypeStruct((bsz, s, d), f32),
        scratch_shapes=[
            pltpu.VMEM((GLA_TM, d), MXU_DTYPE),
            pltpu.VMEM((GLA_TM, n_main), f32),
            pltpu.VMEM((GLA_TM, n_main), f32),
            pltpu.VMEM((GLA_TM, B_QK), f32),
            pltpu.VMEM((GLA_TM, B_QK), f32),
            pltpu.VMEM((GLA_TM, B_V), MXU_DTYPE),
            pltpu.VMEM((GLA_TM, B_V), MXU_DTYPE),
            pltpu.VMEM((B_HEADS, B_DK, B_DV), f32),
        ],
        compiler_params=pltpu.CompilerParams(
            dimension_semantics=("parallel", "arbitrary"), vmem_limit_bytes=VMEM_LIMIT),
        name="gla_layer",
    )(x, mod, gamma, w_main, w_glr, w_gate_p, b_gate[None, :], g_norm[None, :], w_out.astype(MXU_DTYPE))


def _qkv_weight(w_in):
    width = A_HEADS * A_HEAD_DIM
    col_scale = jnp.tile(jnp.repeat(jnp.array([A_HEAD_DIM ** -0.5 * LOG2E, 1.0, 1.0], f32), width), A_GROUPS)
    return (w_in * col_scale[None, :]).astype(MXU_DTYPE)


def kernel(x, c, w_in_a, w_out_a, rel_bias, w_in_b, w_gate_b, b_gate_b, gnorm_b, w_out_b, norm_mix, norm_ffn, w_ada, b_ada, w_up, conv_w, conv_b, w_down, norm_final):
    depth = w_ada.shape[0]
    mod = _adaln(c, w_ada, b_ada)
    bias = _rel_bias(rel_bias)
    for i in range(depth):
        gam_mix = norm_mix[i][None, :]
        j = i // 2
        mixer = None
        if i % 2 == 0:
            qkv = _qkv_proj(x, mod[i], gam_mix, _qkv_weight(w_in_a[j]))
            mixer = (_attention(qkv, bias), w_out_a[j])
        else:
            x = _gla_layer(x, mod[i], gam_mix, w_in_b[j], w_gate_b[j], b_gate_b[j], gnorm_b[j], w_out_b[j])
        last = i == depth - 1
        x = _conv_ffn(x, mod[i], norm_ffn[i][None, :], w_up[i], conv_w[i], conv_b[i], w_down[i],
                      gamma_final=norm_final[None, :] if last else None, mixer=mixer)
    return x
```

```python
import functools
import math

import jax
import jax.numpy as jnp
from jax import lax
from jax.experimental import pallas as pl
from jax.experimental.pallas import tpu as pltpu

A_CONFIGS = ((128, 1), (512, 4), (2048, 16))
A_GROUPS = len(A_CONFIGS)
A_HEADS = 16
A_HEAD_DIM = 64
N_BUCKETS = 32
MAX_DISTANCE = 2048
B_HEADS = 4
B_DK = 128
B_DV = 256
B_QK = B_HEADS * B_DK
B_V = B_HEADS * B_DV
B_GATE_RANK = 16
B_TAU = 16.0
B_CHUNK = 64
CONV_W = 3
EPS = 1e-6
NEG_INF = -1e30
LOG2E = math.log2(math.e)

LANES = 128
SUBLANES = 8
MXU_DTYPE = jnp.bfloat16
ATT_BLK = 128
ATT_TILE = 2048
ATT_UNITS = ATT_TILE // ATT_BLK
ATT_PIPELINE = 3
PAIR_W = 3 * LANES
N_PAIRS = A_HEADS // 2
QKV_ROWS = 1024
PROBE_ROWS = 2 * SUBLANES
FFN_TM = 512
FFN_TF = 256
FFN_ROWS = 256
FFN_DOWN_GROUP = 2
HALO = SUBLANES
LEAD = (CONV_W - 1) * SUBLANES
O_HALO = 2 * SUBLANES
GLA_TM = 512
GLA_SUB = 2
GLA_PROJ_COLS = 256
VMEM_LIMIT = 56 * 1024 * 1024

f32 = jnp.float32


def _mm(a, b):
    return jnp.dot(a, b, preferred_element_type=f32)


def _mm_nt(a, b):
    return lax.dot_general(a, b, (((1,), (1,)), ((), ())), preferred_element_type=f32)


def _mm_tn(a, b):
    return lax.dot_general(a, b, (((0,), (0,)), ((), ())), preferred_element_type=f32)


def _modnorm(x, gamma, scale, shift):
    ms = jnp.mean(x * x, axis=-1, keepdims=True)
    y = x * lax.rsqrt(ms + EPS) * gamma
    return y * (1.0 + scale) + shift


def _silu(x):
    return x * (1.0 / (1.0 + jnp.exp(-x)))


def _regroup(t, groups):
    rows, d = t.shape
    return t.reshape(groups, rows // groups, d).swapaxes(0, 1).reshape(rows, d)


def _adaln_kernel(c_ref, w_ref, b_ref, o_ref):
    s = _silu(c_ref[...]).astype(MXU_DTYPE)
    o_ref[0] = _mm(s, w_ref[0].astype(MXU_DTYPE)) + b_ref[0]


def _adaln(c, w_ada, b_ada):
    depth, d, n = w_ada.shape
    bsz = c.shape[0]
    rows = 8 * pl.cdiv(bsz, 8)
    c_pad = jnp.pad(c, ((0, rows - bsz), (0, 0)))
    out = pl.pallas_call(
        _adaln_kernel,
        grid=(depth, n // d),
        in_specs=[
            pl.BlockSpec((rows, d), lambda l, j: (0, 0)),
            pl.BlockSpec((1, d, d), lambda l, j: (l, 0, j)),
            pl.BlockSpec((1, 1, d), lambda l, j: (l, 0, j)),
        ],
        out_specs=pl.BlockSpec((1, rows, d), lambda l, j: (l, 0, j)),
        out_shape=jax.ShapeDtypeStruct((depth, rows, n), f32),
        compiler_params=pltpu.CompilerParams(
            dimension_semantics=("parallel", "parallel"), vmem_limit_bytes=VMEM_LIMIT),
        name="adaln",
    )(c_pad, w_ada, b_ada.reshape(depth, 1, n))
    return out[:, :bsz].reshape(depth, bsz, n // d, d)


def _qkv_kernel(x_ref, mod_ref, gam_ref, wq_ref, wk_ref, wv_ref, o_ref, ha_ref, hb_ref, sa_ref, sb_ref, *,
                dilations, tiles_per_group):
    j = pl.program_id(2)
    g = j // tiles_per_group
    t = j % tiles_per_group
    nlb = sa_ref.shape[0]
    chunks_per_step = ATT_UNITS // tiles_per_group
    h_refs, stage_refs = [ha_ref, hb_ref], [sa_ref, sb_ref]

    @pl.when(j == 0)
    def _():
        for c in range(ATT_UNITS):
            rows = slice(c * ATT_BLK, (c + 1) * ATT_BLK)
            hn = _modnorm(x_ref[0, rows, :], gam_ref[...], mod_ref[0, 1:2, :], mod_ref[0, 0:1, :])
            ha_ref[rows, :] = hn.astype(ha_ref.dtype)
            for lb in range(nlb):
                sa_ref[lb, rows, :] = hn[:, lb * LANES:(lb + 1) * LANES]

    def deinterleave_chunk(gi, cc):
        d, dp = dilations[gi], dilations[gi - 1]
        rel = d // dp
        per = ATT_UNITS // d
        c = t * chunks_per_step + cc
        r, q = c // per, c % per
        start = (r % dp) * (ATT_TILE // dp) + q * (ATT_BLK * rel) + r // dp
        rows = pl.ds(start, ATT_BLK, stride=rel)
        r0 = pl.multiple_of(c * ATT_BLK, ATT_BLK)
        parts = [stage_refs[(gi - 1) % 2][lb, rows, :] for lb in range(nlb)]
        if gi + 1 < len(dilations):
            for lb in range(nlb):
                stage_refs[gi % 2][lb, pl.ds(r0, ATT_BLK), :] = parts[lb]
        dst_ref = h_refs[gi % 2]
        dst_ref[pl.ds(r0, ATT_BLK), :] = jnp.concatenate(parts, axis=1).astype(dst_ref.dtype)
        probe = dst_ref[pl.ds(r0, PROBE_ROWS), 0:LANES]
        return (probe != probe) & (probe == probe)

    def project_chunk(src_ref, rc, never):
        rows = slice(rc * QKV_ROWS, (rc + 1) * QKV_ROWS)
        lhs = src_ref[rows, :]
        for k, w_ref in enumerate((wq_ref, wk_ref, wv_ref)):
            res = _mm(lhs, w_ref[...]).astype(o_ref.dtype)
            lanes = slice(k * LANES, (k + 1) * LANES)
            o_ref[0, 0, 1, rows, lanes] = res[:, LANES:]
            if never is not None and k == 0:
                top = rc * QKV_ROWS + PROBE_ROWS
                o_ref[0, 0, 0, rc * QKV_ROWS:top, lanes] = jnp.where(
                    never, jnp.zeros_like(res[:PROBE_ROWS, :LANES]), res[:PROBE_ROWS, :LANES])
                o_ref[0, 0, 0, top:(rc + 1) * QKV_ROWS, lanes] = res[PROBE_ROWS:, :LANES]
            else:
                o_ref[0, 0, 0, rows, lanes] = res[:, :LANES]

    assert dilations[0] == 1
    assert all(d % dp == 0 for dp, d in zip(dilations, dilations[1:]))
    n_mm = ATT_TILE // QKV_ROWS
    for gi in range(len(dilations)):
        @pl.when(g == gi)
        def _(gi=gi):
            slots = max(n_mm - 1, 1)
            shares = [list(range(chunks_per_step))[sl::slots] for sl in range(slots)] + [[]]
            for rc in range(n_mm):
                never = None
                if gi + 1 < len(dilations):
                    for cc in shares[rc]:
                        probe = deinterleave_chunk(gi + 1, cc)
                        never = probe if never is None else never | probe
                project_chunk(h_refs[gi % 2], rc, never)


def _qkv_proj(x, mod, gamma, w):
    bsz, s, d = x.shape
    dilations = tuple(dl for _, dl in A_CONFIGS)
    tn = 2 * LANES
    tiles_per_group = A_HEADS * A_HEAD_DIM // tn
    kern = functools.partial(_qkv_kernel, dilations=dilations, tiles_per_group=tiles_per_group)

    def w_spec(k):
        return pl.BlockSpec(
            (d, tn), lambda b, i, j: (0, (j // tiles_per_group * 3 + k) * tiles_per_group + j % tiles_per_group))

    return pl.pallas_call(
        kern,
        grid=(bsz, s // ATT_TILE, A_GROUPS * tiles_per_group),
        in_specs=[
            pl.BlockSpec((1, ATT_TILE, d), lambda b, i, j: (b, i, 0)),
            pl.BlockSpec((1, 6, d), lambda b, i, j: (b, 0, 0)),
            pl.BlockSpec((1, d), lambda b, i, j: (0, 0)),
            w_spec(0), w_spec(1), w_spec(2),
        ],
        out_specs=pl.BlockSpec(
            (1, 1, 2, ATT_TILE, PAIR_W),
            lambda b, i, j: (j // tiles_per_group, b, j % tiles_per_group, i, 0)),
        out_shape=jax.ShapeDtypeStruct((A_GROUPS, bsz, N_PAIRS, s, PAIR_W), MXU_DTYPE),
        scratch_shapes=[pltpu.VMEM((ATT_TILE, d), MXU_DTYPE),
                        pltpu.VMEM((ATT_TILE, d), MXU_DTYPE),
                        pltpu.VMEM((d // LANES, ATT_TILE, LANES), f32),
                        pltpu.VMEM((d // LANES, ATT_TILE, LANES), f32)],
        compiler_params=pltpu.CompilerParams(
            dimension_semantics=("parallel", "parallel", "arbitrary"),
            vmem_limit_bytes=VMEM_LIMIT),
        name="qkv_proj",
    )(x, mod, gamma, w, w, w)


def _bias_kernel(tab_ref, bkt_ref, o_ref):
    bkt = bkt_ref[0]
    hits = [(bkt >= k) & (bkt < k + 1) for k in range(N_BUCKETS)]
    for h in range(A_HEADS):
        col = pl.program_id(0) * A_HEADS + h
        acc = jnp.full(bkt.shape, NEG_INF, f32)
        for k in range(N_BUCKETS):
            acc = jnp.where(hits[k], tab_ref[k, col] * LOG2E, acc)
        o_ref[0, h] = acc


def _t5_bucket(dist):
    max_exact = N_BUCKETS // 2
    n = jnp.maximum(dist, max_exact).astype(f32)
    large = max_exact + (jnp.log(n / max_exact) / math.log(MAX_DISTANCE / max_exact)
                         * (N_BUCKETS - max_exact)).astype(jnp.int32)
    large = jnp.minimum(large, N_BUCKETS - 1)
    return jnp.where(dist < max_exact, dist, large)


def _rel_bias(rel_bias):
    qi = jnp.arange(ATT_BLK)[:, None]
    ki = jnp.arange(2 * ATT_BLK)[None, :]
    steps = qi + ATT_BLK - ki
    band = (steps >= 0) & (steps <= ATT_BLK)
    bucket = jnp.stack([
        jnp.where(band, _t5_bucket(jnp.clip(steps, 0, ATT_BLK) * dl), -1)
        for _, dl in A_CONFIGS]).astype(jnp.int32)
    return pl.pallas_call(
        _bias_kernel,
        grid=(A_GROUPS,),
        in_specs=[
            pl.BlockSpec(memory_space=pltpu.SMEM),
            pl.BlockSpec((1, ATT_BLK, 2 * ATT_BLK), lambda g: (g, 0, 0)),
        ],
        out_specs=pl.BlockSpec((1, A_HEADS, ATT_BLK, 2 * ATT_BLK), lambda g: (g, 0, 0, 0)),
        out_shape=jax.ShapeDtypeStruct((A_GROUPS, A_HEADS, ATT_BLK, 2 * ATT_BLK), f32),
        name="rel_bias",
    )(rel_bias, bucket)


def _attn_kernel(qkv_ref, bias_ref, out_ref, prev0, prev1, prev2, o1, l1, n1, o2, l2, n2, biasp):
    first = pl.program_id(2) == 0

    @pl.when(first)
    def _():
        prev0[...] = jnp.zeros_like(prev0)
        prev1[...] = jnp.zeros_like(prev1)
        prev2[...] = jnp.zeros_like(prev2)

    for g in range(A_GROUPS):
        for hh in range(2):
            biasp[g, hh, :, 0:ATT_BLK] = jnp.where(first, NEG_INF, bias_ref[g, hh, :, 0:ATT_BLK])
            biasp[g, hh, :, ATT_BLK:2 * ATT_BLK] = bias_ref[g, hh, :, ATT_BLK:2 * ATT_BLK]

    lane = lax.broadcasted_iota(jnp.int32, (1, LANES), 1)
    lo = lane < A_HEAD_DIM
    qmask = (jnp.where(lo, 1.0, 0.0).astype(MXU_DTYPE), jnp.where(lo, 0.0, 1.0).astype(MXU_DTYPE))

    def keys_values(g, rc, prev_ref, rp, lanes):
        if prev_ref is None:
            return qkv_ref[g, 0, 0, rp:rp + 2 * ATT_BLK, lanes]
        off = lanes.start - LANES
        return jnp.concatenate([prev_ref[rp:rp + ATT_BLK, off:off + LANES],
                                qkv_ref[g, 0, 0, rc:rc + ATT_BLK, lanes]], axis=0)

    def scores(g, rc, prev_ref, rp):
        q = qkv_ref[g, 0, 0, rc:rc + ATT_BLK, 0:LANES]
        k = keys_values(g, rc, prev_ref, rp, slice(LANES, 2 * LANES))
        bias = bias_ref if prev_ref is None else biasp
        return [_mm_nt(q * qmask[hh], k) + bias[g, hh] for hh in range(2)]

    def attend(g, rc, prev_ref, rp, s):
        v = keys_values(g, rc, prev_ref, rp, slice(2 * LANES, 3 * LANES))
        accs, dens, ms = [], [], []
        for sh in s:
            m = jnp.max(sh, axis=-1, keepdims=True)
            p = jnp.exp2(sh - m)
            dens.append(jnp.sum(p, axis=-1, keepdims=True))
            accs.append(_mm(p.astype(MXU_DTYPE), v))
            ms.append(m)
        acc = jnp.where(lo, accs[0], accs[1])
        return (acc, jnp.broadcast_to(jnp.where(lo, ms[0], ms[1]), acc.shape),
                jnp.broadcast_to(jnp.where(lo, dens[0], dens[1]), acc.shape))

    def store(a_ref, m_ref, d_ref, rows):
        def post(acc, m, den):
            a_ref[rows, :] = acc
            m_ref[rows, :] = m
            d_ref[rows, :] = den
        return post

    def merge(rows):
        def post(acc0, m0, den0):
            m1, m2 = l1[rows, :], l2[rows, :]
            mx = jnp.maximum(jnp.maximum(m0, m1), m2)
            e0, e1, e2 = jnp.exp2(m0 - mx), jnp.exp2(m1 - mx), jnp.exp2(m2 - mx)
            num = e0 * acc0 + e1 * o1[rows, :] + e2 * o2[rows, :]
            den = e0 * den0 + e1 * n1[rows, :] + e2 * n2[rows, :]
            out_ref[0, rows, :] = (num * (1.0 / den)).astype(out_ref.dtype)
        return post

    units = []
    d2 = A_CONFIGS[2][1]
    for u in range(ATT_UNITS):
        units.append((2, u * ATT_BLK, prev2, u * ATT_BLK, store(o2, l2, n2, pl.ds(u, ATT_BLK, stride=d2))))
    d1 = A_CONFIGS[1][1]
    per = ATT_UNITS // d1
    for u in range(ATT_UNITS):
        r, q = divmod(u, per)
        rc = u * ATT_BLK
        post = store(o1, l1, n1, pl.ds(q * ATT_BLK * d1 + r, ATT_BLK, stride=d1))
        if q == 0:
            units.append((1, rc, prev1, (r * per + per - 1) * ATT_BLK, post))
        else:
            units.append((1, rc, None, rc - ATT_BLK, post))
    for u in range(ATT_UNITS):
        rc = u * ATT_BLK
        post = merge(slice(rc, rc + ATT_BLK))
        units.append((0, rc, prev0, 0, post) if u == 0 else (0, rc, None, rc - ATT_BLK, post))

    pending = [scores(*un[:4]) for un in units[:ATT_PIPELINE]]
    for idx, un in enumerate(units):
        if idx + ATT_PIPELINE < len(units):
            pending.append(scores(*units[idx + ATT_PIPELINE][:4]))
        un[4](*attend(*un[:4], pending.pop(0)))

    prev0[...] = qkv_ref[0, 0, 0, ATT_TILE - ATT_BLK:ATT_TILE, LANES:3 * LANES]
    prev1[...] = qkv_ref[1, 0, 0, :, LANES:3 * LANES]
    prev2[...] = qkv_ref[2, 0, 0, :, LANES:3 * LANES]


def _attention(qkv, bias):
    _, bsz, _, s, _ = qkv.shape
    return pl.pallas_call(
        _attn_kernel,
        grid=(bsz, N_PAIRS, s // ATT_TILE),
        in_specs=[
            pl.BlockSpec((A_GROUPS, 1, 1, ATT_TILE, PAIR_W), lambda b, p, i: (0, b, p, i, 0)),
            pl.BlockSpec((A_GROUPS, 2, ATT_BLK, 2 * ATT_BLK), lambda b, p, i: (0, p, 0, 0)),
        ],
        out_specs=pl.BlockSpec((1, ATT_TILE, LANES), lambda b, p, i: (b, i, p)),
        out_shape=jax.ShapeDtypeStruct((bsz, s, A_HEADS * A_HEAD_DIM), MXU_DTYPE),
        scratch_shapes=[
            pltpu.VMEM((ATT_BLK, 2 * LANES), MXU_DTYPE),
            pltpu.VMEM((ATT_TILE, 2 * LANES), MXU_DTYPE),
            pltpu.VMEM((ATT_TILE, 2 * LANES), MXU_DTYPE),
            pltpu.VMEM((ATT_TILE, LANES), f32),
            pltpu.VMEM((ATT_TILE, LANES), f32),
            pltpu.VMEM((ATT_TILE, LANES), f32),
            pltpu.VMEM((ATT_TILE, LANES), f32),
            pltpu.VMEM((ATT_TILE, LANES), f32),
            pltpu.VMEM((ATT_TILE, LANES), f32),
            pltpu.VMEM((A_GROUPS, 2, ATT_BLK, 2 * ATT_BLK), f32),
        ],
        compiler_params=pltpu.CompilerParams(
            dimension_semantics=("parallel", "parallel", "arbitrary"),
            vmem_limit_bytes=VMEM_LIMIT),
        name="dilated_attn",
    )(qkv, bias)


def _ffn_kernel(*refs, final, mixer):
    refs = list(refs)
    x_ref, halo_ref = refs[:2]
    del refs[:2]
    if mixer:
        o_ref, ohalo_ref, wout_ref = refs[:3]
        del refs[:3]
    mod_ref, gam_ref, wup_ref, cw_ref, cb_ref, wdn_ref = refs[:6]
    del refs[:6]
    if final:
        gfin_ref = refs.pop(0)
    out_ref, h_ref, u0_ref, u1_ref, act_ref, acc_ref = refs[:6]
    if mixer:
        x1_ref = refs[6]
    u_refs = (u0_ref, u1_ref)
    gam, scale, shift = gam_ref[...], mod_ref[0, 4:5, :], mod_ref[0, 3:4, :]
    d_ff = wdn_ref.shape[0]
    nchunk = d_ff // FFN_TF
    tm = x_ref.shape[1]

    if mixer:
        x1 = x_ref[0] + mod_ref[0, 2:3, :] * _mm(o_ref[0], wout_ref[...])
        x1_ref[...] = x1
        halo = halo_ref[0] + mod_ref[0, 2:3, :] * _mm(ohalo_ref[0], wout_ref[...])[O_HALO - HALO:, :]
    else:
        x1, halo = x_ref[0], halo_ref[0]
    halo = _modnorm(halo, gam, scale, shift)
    halo = jnp.where(pl.program_id(1) == 0, 0.0, halo)
    hp = _regroup(_modnorm(x1, gam, scale, shift), SUBLANES)
    for back in range(1, CONV_W):
        last = hp[tm - back * SUBLANES:tm - (back - 1) * SUBLANES, :]
        lead = jnp.concatenate([halo[HALO - back:HALO - back + 1, :], last[:SUBLANES - 1, :]], axis=0)
        h_ref[(CONV_W - 1 - back) * SUBLANES:(CONV_W - back) * SUBLANES, :] = lead.astype(h_ref.dtype)
    h_ref[LEAD:LEAD + tm, :] = hp.astype(h_ref.dtype)

    def halves(c):
        return [slice(half * d_ff + c * FFN_TF, half * d_ff + (c + 1) * FFN_TF) for half in range(2)]

    def up(c):
        for half, cols in enumerate(halves(c)):
            u_refs[c % 2][:, half * FFN_TF:(half + 1) * FFN_TF] = _mm(h_ref[...], wup_ref[:, cols])

    def conv_act(c):
        u_ref = u_refs[c % 2]
        for rb in range(tm // FFN_ROWS):
            ab = []
            for half, cols in enumerate(halves(c)):
                v = cb_ref[:, cols]
                for t in range(CONV_W):
                    off = rb * FFN_ROWS + t * SUBLANES
                    v = v + cw_ref[t:t + 1, cols] * u_ref[off:off + FFN_ROWS, half * FFN_TF:(half + 1) * FFN_TF]
                ab.append(v)
            act_ref[rb * FFN_ROWS:(rb + 1) * FFN_ROWS, c * FFN_TF:(c + 1) * FFN_TF] = (
                _silu(ab[0]) * ab[1]).astype(act_ref.dtype)

    def down(first, last):
        cols = slice(first * FFN_TF, last * FFN_TF)
        return _mm(act_ref[:, cols], wdn_ref[cols, :])

    up(0)
    start, pending = 0, None
    for c in range(nchunk):
        if c + 1 < nchunk:
            up(c + 1)
        if pending is not None:
            if pending[0] == 0:
                acc_ref[...] = down(*pending)
            else:
                acc_ref[...] += down(*pending)
            pending = None
        if c + 1 - start == FFN_DOWN_GROUP and c + 1 < nchunk:
            pending, start = (start, c + 1), c + 1
        conv_act(c)

    ffn = down(start, nchunk)
    if start > 0:
        ffn = ffn + acc_ref[...]
    ffn = _regroup(ffn, tm // SUBLANES)
    y = (x1_ref[...] if mixer else x_ref[0]) + mod_ref[0, 5:6, :] * ffn
    if final:
        ms = jnp.mean(y * y, axis=-1, keepdims=True)
        y = y * lax.rsqrt(ms + EPS) * gfin_ref[...]
    out_ref[0] = y


def _conv_ffn(x, mod, gamma, w_up, conv_w, conv_b, w_down, gamma_final=None, mixer=None):
    bsz, s, d = x.shape
    d_ff = w_down.shape[0]
    final = gamma_final is not None
    rows = FFN_TM
    tiles_per_halo = rows // HALO
    resident = dict(pipeline_mode=pl.Buffered(1))
    in_specs = [
        pl.BlockSpec((1, rows, d), lambda b, i: (b, i, 0)),
        pl.BlockSpec((1, HALO, d), lambda b, i: (b, jnp.maximum(i * tiles_per_halo - 1, 0), 0)),
    ]
    args = [x, x]
    if mixer is not None:
        o, w_out = mixer
        k = o.shape[-1]
        in_specs += [
            pl.BlockSpec((1, rows, k), lambda b, i: (b, i, 0)),
            pl.BlockSpec((1, O_HALO, k), lambda b, i: (b, jnp.maximum(i * (rows // O_HALO) - 1, 0), 0)),
            pl.BlockSpec((k, d), lambda b, i: (0, 0), **resident),
        ]
        args += [o, o, w_out.astype(MXU_DTYPE)]
    in_specs += [
        pl.BlockSpec((1, 6, d), lambda b, i: (b, 0, 0)),
        pl.BlockSpec((1, d), lambda b, i: (0, 0)),
        pl.BlockSpec((d, 2 * d_ff), lambda b, i: (0, 0), **resident),
        pl.BlockSpec((CONV_W, 2 * d_ff), lambda b, i: (0, 0)),
        pl.BlockSpec((1, 2 * d_ff), lambda b, i: (0, 0)),
        pl.BlockSpec((d_ff, d), lambda b, i: (0, 0), **resident),
    ]
    args += [mod, gamma, w_up.astype(MXU_DTYPE), conv_w, conv_b[None, :], w_down.astype(MXU_DTYPE)]
    if final:
        in_specs.append(pl.BlockSpec((1, d), lambda b, i: (0, 0)))
        args.append(gamma_final)
    return pl.pallas_call(
        functools.partial(_ffn_kernel, final=final, mixer=mixer is not None),
        grid=(bsz, s // rows),
        in_specs=in_specs,
        out_specs=pl.BlockSpec((1, rows, d), lambda b, i: (b, i, 0)),
        out_shape=jax.ShapeDtypeStruct((bsz, s, d), f32),
        scratch_shapes=[
            pltpu.VMEM((rows + LEAD, d), MXU_DTYPE),
            pltpu.VMEM((rows + LEAD, 2 * FFN_TF), f32),
            pltpu.VMEM((rows + LEAD, 2 * FFN_TF), f32),
            pltpu.VMEM((rows, d_ff), MXU_DTYPE),
            pltpu.VMEM((rows, d), f32),
        ] + ([pltpu.VMEM((rows, d), f32)] if mixer is not None else []),
        compiler_params=pltpu.CompilerParams(
            dimension_semantics=("parallel", "parallel"), vmem_limit_bytes=VMEM_LIMIT),
        name="conv_ffn_final" if final else "conv_ffn",
    )(*args)


def _gla_kernel(x_ref, mod_ref, gam_ref, win_ref, wglr_ref, wgate_ref, bgate_ref, gnorm_ref, wout_ref,
                out_ref, h_ref, proj0_ref, proj1_ref, gk0_ref, gk1_ref, og0_ref, og1_ref, state_ref):
    @pl.when(pl.program_id(1) == 0)
    def _():
        state_ref[...] = jnp.zeros_like(state_ref)

    proj_refs, gk_refs, og_refs = (proj0_ref, proj1_ref), (gk0_ref, gk1_ref), (og0_ref, og1_ref)
    nsub = x_ref.shape[1] // GLA_TM
    nchunk = GLA_TM // B_CHUNK
    n_main = win_ref.shape[1]

    ri = lax.broadcasted_iota(jnp.int32, (B_CHUNK, B_CHUNK), 0)
    ci = lax.broadcasted_iota(jnp.int32, (B_CHUNK, B_CHUNK), 1)
    causal = ci <= ri
    tri = causal.astype(MXU_DTYPE)
    gnorm = gnorm_ref[...]
    qscale = B_DK ** -0.5
    r_off = 2 * B_QK + B_V
    heads = [slice(hd * B_DK, (hd + 1) * B_DK) for hd in range(B_HEADS)]

    def project_steps(k):
        rows = slice(k * GLA_TM, (k + 1) * GLA_TM)

        def normalize():
            h = _modnorm(x_ref[0, rows, :], gam_ref[...], mod_ref[0, 1:2, :], mod_ref[0, 0:1, :])
            h_ref[...] = h.astype(h_ref.dtype)

        def piece(cols):
            def run():
                proj_refs[k % 2][:, cols] = _mm(h_ref[...], win_ref[:, cols])
            return run

        def gate():
            glr = _mm(h_ref[...], wglr_ref[...]).astype(MXU_DTYPE)
            z = _mm(glr, wgate_ref[...]) + bgate_ref[...]
            gk_refs[k % 2][...] = (jnp.minimum(z, 0.0) - jnp.log1p(jnp.exp(-jnp.abs(z)))) * (1.0 / B_TAU)

        pieces = [piece(slice(c0, c0 + GLA_PROJ_COLS)) for c0 in range(0, n_main, GLA_PROJ_COLS)]
        return [normalize] + pieces + [gate]

    def local(k, c):
        proj_ref, gk_ref = proj_refs[k % 2], gk_refs[k % 2]
        rows = slice(c * B_CHUNK, (c + 1) * B_CHUNK)
        gk = gk_ref[rows, :]
        g_hi = gk.astype(MXU_DTYPE)
        rem = gk - g_hi.astype(f32)
        g_mid = rem.astype(MXU_DTYPE)
        g_lo = (rem - g_mid.astype(f32)).astype(MXU_DTYPE)
        bcum = _mm(tri, g_hi) + _mm(tri, g_mid) + _mm(tri, g_lo)
        blast = bcum[B_CHUNK - 1:B_CHUNK, :]
        q_t = ((proj_ref[rows, 0:B_QK] * qscale) * jnp.exp(bcum)).astype(MXU_DTYPE)
        kk = proj_ref[rows, B_QK:2 * B_QK]
        k_t = (kk * jnp.exp(-bcum)).astype(MXU_DTYPE)
        k_d = (kk * jnp.exp(blast - bcum)).astype(MXU_DTYPE)
        v = [proj_ref[rows, 2 * B_QK + hd * B_DV:2 * B_QK + (hd + 1) * B_DV].astype(MXU_DTYPE)
             for hd in range(B_HEADS)]
        a = [jnp.where(causal, _mm_nt(q_t[:, ks], k_t[:, ks]), 0.0).astype(MXU_DTYPE) for ks in heads]
        kv = [_mm_tn(k_d[:, ks], v[hd]) for hd, ks in enumerate(heads)]
        o_intra = [_mm(a[hd], v[hd]) for hd in range(B_HEADS)]
        return q_t, blast, kv, o_intra

    def recur(k, c, loc, state):
        q_t, blast, kv, o_intra = loc
        rows = slice(c * B_CHUNK, (c + 1) * B_CHUNK)
        new_state = []
        for hd, ks in enumerate(heads):
            o = o_intra[hd] + _mm(q_t[:, ks], state[hd].astype(MXU_DTYPE))
            decay = jnp.exp(jnp.broadcast_to(blast[:, ks], (B_DK, B_DK)).T)
            new_state.append(jnp.concatenate([decay] * (B_DV // B_DK), axis=1) * state[hd] + kv[hd])
            o = o * lax.rsqrt(jnp.mean(o * o, axis=-1, keepdims=True) + EPS) * gnorm
            r = proj_refs[k % 2][rows, r_off + hd * B_DV:r_off + (hd + 1) * B_DV]
            og_refs[k % 2][rows, hd * B_DV:(hd + 1) * B_DV] = (o * _silu(r)).astype(og_refs[k % 2].dtype)
        return new_state

    for step in project_steps(0):
        step()
    state = [state_ref[hd] for hd in range(B_HEADS)]
    for k in range(nsub):
        ahead = project_steps(k + 1) if k + 1 < nsub else []
        loc = local(k, 0)
        for c in range(nchunk):
            nxt = local(k, c + 1) if c + 1 < nchunk else None
            state = recur(k, c, loc, state)
            loc = nxt
            take = -(-len(ahead) // (nchunk - c))
            for step in ahead[:take]:
                step()
            ahead = ahead[take:]
        rows = slice(k * GLA_TM, (k + 1) * GLA_TM)
        out_ref[0, rows, :] = x_ref[0, rows, :] + mod_ref[0, 2:3, :] * _mm(og_refs[k % 2][...], wout_ref[...])
    for hd in range(B_HEADS):
        state_ref[hd] = state[hd]


def _gla_layer(x, mod, gamma, w_in, w_gate, b_gate, g_norm, w_out):
    bsz, s, d = x.shape
    glr0 = 2 * B_QK + B_V
    w_main = jnp.concatenate([w_in[:, :glr0], w_in[:, glr0 + B_GATE_RANK:]], axis=1).astype(MXU_DTYPE)
    w_glr = jnp.pad(w_in[:, glr0:glr0 + B_GATE_RANK], ((0, 0), (0, LANES - B_GATE_RANK))).astype(MXU_DTYPE)
    w_gate_p = jnp.pad(w_gate, ((0, LANES - B_GATE_RANK), (0, 0))).astype(MXU_DTYPE)
    n_main = w_main.shape[1]
    rows = GLA_SUB * GLA_TM
    const = lambda b, i: (0, 0)
    resident = dict(pipeline_mode=pl.Buffered(1))
    return pl.pallas_call(
        _gla_kernel,
        grid=(bsz, s // rows),
        in_specs=[
            pl.BlockSpec((1, rows, d), lambda b, i: (b, i, 0)),
            pl.BlockSpec((1, 6, d), lambda b, i: (b, 0, 0)),
            pl.BlockSpec((1, d), const),
            pl.BlockSpec((d, n_main), const, **resident),
            pl.BlockSpec((d, LANES), const),
            pl.BlockSpec((LANES, B_QK), const),
            pl.BlockSpec((1, B_QK), const),
            pl.BlockSpec((1, B_DV), const),
            pl.BlockSpec((B_V, d), const, **resident),
        ],
        out_specs=pl.BlockSpec((1, rows, d), lambda b, i: (b, i, 0)),
        out_shape=jax.ShapeDtypeStruct((bsz, s, d), f32),
        scratch_shapes=[
            pltpu.VMEM((GLA_TM, d), MXU_DTYPE),
            pltpu.VMEM((GLA_TM, n_main), f32),
            pltpu.VMEM((GLA_TM, n_main), f32),
            pltpu.VMEM((GLA_TM, B_QK), f32),
            pltpu.VMEM((GLA_TM, B_QK), f32),
            pltpu.VMEM((GLA_TM, B_V), MXU_DTYPE),
            pltpu.VMEM((GLA_TM, B_V), MXU_DTYPE),
            pltpu.VMEM((B_HEADS, B_DK, B_DV), f32),
        ],
        compiler_params=pltpu.CompilerParams(
            dimension_semantics=("parallel", "arbitrary"), vmem_limit_bytes=VMEM_LIMIT),
        name="gla_layer",
    )(x, mod, gamma, w_main, w_glr, w_gate_p, b_gate[None, :], g_norm[None, :], w_out.astype(MXU_DTYPE))


def _qkv_weight(w_in):
    width = A_HEADS * A_HEAD_DIM
    col_scale = jnp.tile(jnp.repeat(jnp.array([A_HEAD_DIM ** -0.5 * LOG2E, 1.0, 1.0], f32), width), A_GROUPS)
    return (w_in * col_scale[None, :]).astype(MXU_DTYPE)


def kernel(x, c, w_in_a, w_out_a, rel_bias, w_in_b, w_gate_b, b_gate_b, gnorm_b, w_out_b, norm_mix, norm_ffn, w_ada, b_ada, w_up, conv_w, conv_b, w_down, norm_final):
    depth = w_ada.shape[0]
    mod = _adaln(c, w_ada, b_ada)
    bias = _rel_bias(rel_bias)
    for i in range(depth):
        gam_mix = norm_mix[i][None, :]
        j = i // 2
        mixer = None
        if i % 2 == 0:
            qkv = _qkv_proj(x, mod[i], gam_mix, _qkv_weight(w_in_a[j]))
            mixer = (_attention(qkv, bias), w_out_a[j])
        else:
            x = _gla_layer(x, mod[i], gam_mix, w_in_b[j], w_gate_b[j], b_gate_b[j], gnorm_b[j], w_out_b[j])
        last = i == depth - 1
        x = _conv_ffn(x, mod[i], norm_ffn[i][None, :], w_up[i], conv_w[i], conv_b[i], w_down[i],
                      gamma_final=norm_final[None, :] if last else None, mixer=mixer)
    return x
```

```python
import functools
import math

import jax
import jax.numpy as jnp
from jax import lax
from jax.experimental import pallas as pl
from jax.experimental.pallas import tpu as pltpu

A_CONFIGS = ((128, 1), (512, 4), (2048, 16))
A_GROUPS = len(A_CONFIGS)
A_HEADS = 16
A_HEAD_DIM = 64
N_BUCKETS = 32
MAX_DISTANCE = 2048
B_HEADS = 4
B_DK = 128
B_DV = 256
B_QK = B_HEADS * B_DK
B_V = B_HEADS * B_DV
B_GATE_RANK = 16
B_TAU = 16.0
B_CHUNK = 64
CONV_W = 3
EPS = 1e-6
NEG_INF = -1e30
LOG2E = math.log2(math.e)

LANES = 128
SUBLANES = 8
MXU_DTYPE = jnp.bfloat16
ATT_BLK = 128
ATT_TILE = 2048
ATT_UNITS = ATT_TILE // ATT_BLK
ATT_PIPELINE = 3
PAIR_W = 3 * LANES
N_PAIRS = A_HEADS // 2
QKV_ROWS = 1024
PROBE_ROWS = 2 * SUBLANES
FFN_TM = 512
FFN_TF = 256
FFN_ROWS = 256
FFN_DOWN_GROUP = 2
HALO = SUBLANES
LEAD = (CONV_W - 1) * SUBLANES
O_HALO = 2 * SUBLANES
GLA_TM = 512
GLA_SUB = 2
GLA_PROJ_COLS = 256
VMEM_LIMIT = 56 * 1024 * 1024

f32 = jnp.float32


def _mm(a, b):
    return jnp.dot(a, b, preferred_element_type=f32)


def _mm_nt(a, b):
    return lax.dot_general(a, b, (((1,), (1,)), ((), ())), preferred_element_type=f32)


def _mm_tn(a, b):
    return lax.dot_general(a, b, (((0,), (0,)), ((), ())), preferred_element_type=f32)


def _modnorm(x, gamma, scale, shift):
    ms = jnp.mean(x * x, axis=-1, keepdims=True)
    y = x * lax.rsqrt(ms + EPS) * gamma
    return y * (1.0 + scale) + shift


def _silu(x):
    return x * (1.0 / (1.0 + jnp.exp(-x)))


def _regroup(t, groups):
    rows, d = t.shape
    return t.reshape(groups, rows // groups, d).swapaxes(0, 1).reshape(rows, d)


def _adaln_kernel(c_ref, w_ref, b_ref, o_ref):
    s = _silu(c_ref[...]).astype(MXU_DTYPE)
    o_ref[0] = _mm(s, w_ref[0].astype(MXU_DTYPE)) + b_ref[0]


def _adaln(c, w_ada, b_ada):
    depth, d, n = w_ada.shape
    bsz = c.shape[0]
    rows = 8 * pl.cdiv(bsz, 8)
    c_pad = jnp.pad(c, ((0, rows - bsz), (0, 0)))
    out = pl.pallas_call(
        _adaln_kernel,
        grid=(depth, n // d),
        in_specs=[
            pl.BlockSpec((rows, d), lambda l, j: (0, 0)),
            pl.BlockSpec((1, d, d), lambda l, j: (l, 0, j)),
            pl.BlockSpec((1, 1, d), lambda l, j: (l, 0, j)),
        ],
        out_specs=pl.BlockSpec((1, rows, d), lambda l, j: (l, 0, j)),
        out_shape=jax.ShapeDtypeStruct((depth, rows, n), f32),
        compiler_params=pltpu.CompilerParams(
            dimension_semantics=("parallel", "parallel"), vmem_limit_bytes=VMEM_LIMIT),
        name="adaln",
    )(c_pad, w_ada, b_ada.reshape(depth, 1, n))
    return out[:, :bsz].reshape(depth, bsz, n // d, d)


def _qkv_kernel(x_ref, mod_ref, gam_ref, wq_ref, wk_ref, wv_ref, o_ref, ha_ref, hb_ref, sa_ref, sb_ref, *,
                dilations, tiles_per_group):
    j = pl.program_id(2)
    g = j // tiles_per_group
    t = j % tiles_per_group
    nlb = sa_ref.shape[0]
    chunks_per_step = ATT_UNITS // tiles_per_group
    h_refs, stage_refs = [ha_ref, hb_ref], [sa_ref, sb_ref]

    @pl.when(j == 0)
    def _():
        for c in range(ATT_UNITS):
            rows = slice(c * ATT_BLK, (c + 1) * ATT_BLK)
            hn = _modnorm(x_ref[0, rows, :], gam_ref[...], mod_ref[0, 1:2, :], mod_ref[0, 0:1, :])
            ha_ref[rows, :] = hn.astype(ha_ref.dtype)
            for lb in range(nlb):
                sa_ref[lb, rows, :] = hn[:, lb * LANES:(lb + 1) * LANES]

    def deinterleave_chunk(gi, cc):
        d, dp = dilations[gi], dilations[gi - 1]
        rel = d // dp
        per = ATT_UNITS // d
        c = t * chunks_per_step + cc
        r, q = c // per, c % per
        start = (r % dp) * (ATT_TILE // dp) + q * (ATT_BLK * rel) + r // dp
        rows = pl.ds(start, ATT_BLK, stride=rel)
        r0 = pl.multiple_of(c * ATT_BLK, ATT_BLK)
        parts = [stage_refs[(gi - 1) % 2][lb, rows, :] for lb in range(nlb)]
        if gi + 1 < len(dilations):
            for lb in range(nlb):
                stage_refs[gi % 2][lb, pl.ds(r0, ATT_BLK), :] = parts[lb]
        dst_ref = h_refs[gi % 2]
        dst_ref[pl.ds(r0, ATT_BLK), :] = jnp.concatenate(parts, axis=1).astype(dst_ref.dtype)
        probe = dst_ref[pl.ds(r0, PROBE_ROWS), 0:LANES]
        return (probe != probe) & (probe == probe)

    def weights():
        q_scale = A_HEAD_DIM ** -0.5 * LOG2E
        return [(wq_ref[...] * q_scale).astype(MXU_DTYPE), wk_ref[...].astype(MXU_DTYPE),
                wv_ref[...].astype(MXU_DTYPE)]

    def project_chunk(src_ref, rc, never, ws):
        rows = slice(rc * QKV_ROWS, (rc + 1) * QKV_ROWS)
        lhs = src_ref[rows, :]
        for k, w in enumerate(ws):
            res = _mm(lhs, w).astype(o_ref.dtype)
            lanes = slice(k * LANES, (k + 1) * LANES)
            o_ref[0, 0, 1, rows, lanes] = res[:, LANES:]
            if never is not None and k == 0:
                top = rc * QKV_ROWS + PROBE_ROWS
                o_ref[0, 0, 0, rc * QKV_ROWS:top, lanes] = jnp.where(
                    never, jnp.zeros_like(res[:PROBE_ROWS, :LANES]), res[:PROBE_ROWS, :LANES])
                o_ref[0, 0, 0, top:(rc + 1) * QKV_ROWS, lanes] = res[PROBE_ROWS:, :LANES]
            else:
                o_ref[0, 0, 0, rows, lanes] = res[:, :LANES]

    assert dilations[0] == 1
    assert all(d % dp == 0 for dp, d in zip(dilations, dilations[1:]))
    n_mm = ATT_TILE // QKV_ROWS
    for gi in range(len(dilations)):
        @pl.when(g == gi)
        def _(gi=gi):
            slots = max(n_mm - 1, 1)
            shares = [list(range(chunks_per_step))[sl::slots] for sl in range(slots)] + [[]]
            ws = weights()
            for rc in range(n_mm):
                never = None
                if gi + 1 < len(dilations):
                    for cc in shares[rc]:
                        probe = deinterleave_chunk(gi + 1, cc)
                        never = probe if never is None else never | probe
                project_chunk(h_refs[gi % 2], rc, never, ws)


def _qkv_proj(x, mod, gamma, w):
    bsz, s, d = x.shape
    dilations = tuple(dl for _, dl in A_CONFIGS)
    tn = 2 * LANES
    tiles_per_group = A_HEADS * A_HEAD_DIM // tn
    kern = functools.partial(_qkv_kernel, dilations=dilations, tiles_per_group=tiles_per_group)

    def w_spec(k):
        return pl.BlockSpec(
            (d, tn), lambda b, i, j: (0, (j // tiles_per_group * 3 + k) * tiles_per_group + j % tiles_per_group))

    return pl.pallas_call(
        kern,
        grid=(bsz, s // ATT_TILE, A_GROUPS * tiles_per_group),
        in_specs=[
            pl.BlockSpec((1, ATT_TILE, d), lambda b, i, j: (b, i, 0)),
            pl.BlockSpec((1, 6, d), lambda b, i, j: (b, 0, 0)),
            pl.BlockSpec((1, d), lambda b, i, j: (0, 0)),
            w_spec(0), w_spec(1), w_spec(2),
        ],
        out_specs=pl.BlockSpec(
            (1, 1, 2, ATT_TILE, PAIR_W),
            lambda b, i, j: (j // tiles_per_group, b, j % tiles_per_group, i, 0)),
        out_shape=jax.ShapeDtypeStruct((A_GROUPS, bsz, N_PAIRS, s, PAIR_W), MXU_DTYPE),
        scratch_shapes=[pltpu.VMEM((ATT_TILE, d), MXU_DTYPE),
                        pltpu.VMEM((ATT_TILE, d), MXU_DTYPE),
                        pltpu.VMEM((d // LANES, ATT_TILE, LANES), f32),
                        pltpu.VMEM((d // LANES, ATT_TILE, LANES), f32)],
        compiler_params=pltpu.CompilerParams(
            dimension_semantics=("parallel", "parallel", "arbitrary"),
            vmem_limit_bytes=VMEM_LIMIT),
        name="qkv_proj",
    )(x, mod, gamma, w, w, w)


def _bias_kernel(tab_ref, bkt_ref, o_ref):
    bkt = bkt_ref[0]
    hits = [(bkt >= k) & (bkt < k + 1) for k in range(N_BUCKETS)]
    for h in range(A_HEADS):
        col = pl.program_id(0) * A_HEADS + h
        acc = jnp.full(bkt.shape, NEG_INF, f32)
        for k in range(N_BUCKETS):
            acc = jnp.where(hits[k], tab_ref[k, col] * LOG2E, acc)
        o_ref[0, h] = acc


def _t5_bucket(dist):
    max_exact = N_BUCKETS // 2
    n = jnp.maximum(dist, max_exact).astype(f32)
    large = max_exact + (jnp.log(n / max_exact) / math.log(MAX_DISTANCE / max_exact)
                         * (N_BUCKETS - max_exact)).astype(jnp.int32)
    large = jnp.minimum(large, N_BUCKETS - 1)
    return jnp.where(dist < max_exact, dist, large)


def _rel_bias(rel_bias):
    qi = jnp.arange(ATT_BLK)[:, None]
    ki = jnp.arange(2 * ATT_BLK)[None, :]
    steps = qi + ATT_BLK - ki
    band = (steps >= 0) & (steps <= ATT_BLK)
    bucket = jnp.stack([
        jnp.where(band, _t5_bucket(jnp.clip(steps, 0, ATT_BLK) * dl), -1)
        for _, dl in A_CONFIGS]).astype(jnp.int32)
    return pl.pallas_call(
        _bias_kernel,
        grid=(A_GROUPS,),
        in_specs=[
            pl.BlockSpec(memory_space=pltpu.SMEM),
            pl.BlockSpec((1, ATT_BLK, 2 * ATT_BLK), lambda g: (g, 0, 0)),
        ],
        out_specs=pl.BlockSpec((1, A_HEADS, ATT_BLK, 2 * ATT_BLK), lambda g: (g, 0, 0, 0)),
        out_shape=jax.ShapeDtypeStruct((A_GROUPS, A_HEADS, ATT_BLK, 2 * ATT_BLK), f32),
        name="rel_bias",
    )(rel_bias, bucket)


def _attn_kernel(qkv_ref, bias_ref, out_ref, prev0, prev1, prev2, o1, l1, n1, o2, l2, n2, biasp):
    first = pl.program_id(2) == 0

    @pl.when(first)
    def _():
        prev0[...] = jnp.zeros_like(prev0)
        prev1[...] = jnp.zeros_like(prev1)
        prev2[...] = jnp.zeros_like(prev2)

    for g in range(A_GROUPS):
        for hh in range(2):
            biasp[g, hh, :, 0:ATT_BLK] = jnp.where(first, NEG_INF, bias_ref[g, hh, :, 0:ATT_BLK])
            biasp[g, hh, :, ATT_BLK:2 * ATT_BLK] = bias_ref[g, hh, :, ATT_BLK:2 * ATT_BLK]

    lane = lax.broadcasted_iota(jnp.int32, (1, LANES), 1)
    lo = lane < A_HEAD_DIM
    qmask = (jnp.where(lo, 1.0, 0.0).astype(MXU_DTYPE), jnp.where(lo, 0.0, 1.0).astype(MXU_DTYPE))

    def keys_values(g, rc, prev_ref, rp, lanes):
        if prev_ref is None:
            return qkv_ref[g, 0, 0, rp:rp + 2 * ATT_BLK, lanes]
        off = lanes.start - LANES
        return jnp.concatenate([prev_ref[rp:rp + ATT_BLK, off:off + LANES],
                                qkv_ref[g, 0, 0, rc:rc + ATT_BLK, lanes]], axis=0)

    def scores(g, rc, prev_ref, rp):
        q = qkv_ref[g, 0, 0, rc:rc + ATT_BLK, 0:LANES]
        k = keys_values(g, rc, prev_ref, rp, slice(LANES, 2 * LANES))
        bias = bias_ref if prev_ref is None else biasp
        return [_mm_nt(q * qmask[hh], k) + bias[g, hh] for hh in range(2)]

    def attend(g, rc, prev_ref, rp, s):
        v = keys_values(g, rc, prev_ref, rp, slice(2 * LANES, 3 * LANES))
        accs, dens, ms = [], [], []
        for sh in s:
            m = jnp.max(sh, axis=-1, keepdims=True)
            p = jnp.exp2(sh - m)
            dens.append(jnp.sum(p, axis=-1, keepdims=True))
            accs.append(_mm(p.astype(MXU_DTYPE), v))
            ms.append(m)
        acc = jnp.where(lo, accs[0], accs[1])
        return (acc, jnp.broadcast_to(jnp.where(lo, ms[0], ms[1]), acc.shape),
                jnp.broadcast_to(jnp.where(lo, dens[0], dens[1]), acc.shape))

    def store(a_ref, m_ref, d_ref, rows):
        def post(acc, m, den):
            a_ref[rows, :] = acc
            m_ref[rows, :] = m
            d_ref[rows, :] = den
        return post

    def merge(rows):
        def post(acc0, m0, den0):
            m1, m2 = l1[rows, :], l2[rows, :]
            mx = jnp.maximum(jnp.maximum(m0, m1), m2)
            e0, e1, e2 = jnp.exp2(m0 - mx), jnp.exp2(m1 - mx), jnp.exp2(m2 - mx)
            num = e0 * acc0 + e1 * o1[rows, :] + e2 * o2[rows, :]
            den = e0 * den0 + e1 * n1[rows, :] + e2 * n2[rows, :]
            out_ref[0, rows, :] = (num * (1.0 / den)).astype(out_ref.dtype)
        return post

    units = []
    d2 = A_CONFIGS[2][1]
    for u in range(ATT_UNITS):
        units.append((2, u * ATT_BLK, prev2, u * ATT_BLK, store(o2, l2, n2, pl.ds(u, ATT_BLK, stride=d2))))
    d1 = A_CONFIGS[1][1]
    per = ATT_UNITS // d1
    for u in range(ATT_UNITS):
        r, q = divmod(u, per)
        rc = u * ATT_BLK
        post = store(o1, l1, n1, pl.ds(q * ATT_BLK * d1 + r, ATT_BLK, stride=d1))
        if q == 0:
            units.append((1, rc, prev1, (r * per + per - 1) * ATT_BLK, post))
        else:
            units.append((1, rc, None, rc - ATT_BLK, post))
    for u in range(ATT_UNITS):
        rc = u * ATT_BLK
        post = merge(slice(rc, rc + ATT_BLK))
        units.append((0, rc, prev0, 0, post) if u == 0 else (0, rc, None, rc - ATT_BLK, post))

    pending = [scores(*un[:4]) for un in units[:ATT_PIPELINE]]
    for idx, un in enumerate(units):
        if idx + ATT_PIPELINE < len(units):
            pending.append(scores(*units[idx + ATT_PIPELINE][:4]))
        un[4](*attend(*un[:4], pending.pop(0)))

    prev0[...] = qkv_ref[0, 0, 0, ATT_TILE - ATT_BLK:ATT_TILE, LANES:3 * LANES]
    prev1[...] = qkv_ref[1, 0, 0, :, LANES:3 * LANES]
    prev2[...] = qkv_ref[2, 0, 0, :, LANES:3 * LANES]


def _attention(qkv, bias):
    _, bsz, _, s, _ = qkv.shape
    return pl.pallas_call(
        _attn_kernel,
        grid=(bsz, N_PAIRS, s // ATT_TILE),
        in_specs=[
            pl.BlockSpec((A_GROUPS, 1, 1, ATT_TILE, PAIR_W), lambda b, p, i: (0, b, p, i, 0)),
            pl.BlockSpec((A_GROUPS, 2, ATT_BLK, 2 * ATT_BLK), lambda b, p, i: (0, p, 0, 0)),
        ],
        out_specs=pl.BlockSpec((1, ATT_TILE, LANES), lambda b, p, i: (b, i, p)),
        out_shape=jax.ShapeDtypeStruct((bsz, s, A_HEADS * A_HEAD_DIM), MXU_DTYPE),
        scratch_shapes=[
            pltpu.VMEM((ATT_BLK, 2 * LANES), MXU_DTYPE),
            pltpu.VMEM((ATT_TILE, 2 * LANES), MXU_DTYPE),
            pltpu.VMEM((ATT_TILE, 2 * LANES), MXU_DTYPE),
            pltpu.VMEM((ATT_TILE, LANES), f32),
            pltpu.VMEM((ATT_TILE, LANES), f32),
            pltpu.VMEM((ATT_TILE, LANES), f32),
            pltpu.VMEM((ATT_TILE, LANES), f32),
            pltpu.VMEM((ATT_TILE, LANES), f32),
            pltpu.VMEM((ATT_TILE, LANES), f32),
            pltpu.VMEM((A_GROUPS, 2, ATT_BLK, 2 * ATT_BLK), f32),
        ],
        compiler_params=pltpu.CompilerParams(
            dimension_semantics=("parallel", "parallel", "arbitrary"),
            vmem_limit_bytes=VMEM_LIMIT),
        name="dilated_attn",
    )(qkv, bias)


def _ffn_kernel(*refs, final, mixer):
    refs = list(refs)
    x_ref, halo_ref = refs[:2]
    del refs[:2]
    if mixer:
        o_ref, ohalo_ref, wout_ref = refs[:3]
        del refs[:3]
    mod_ref, gam_ref, wup_ref, cw_ref, cb_ref, wdn_ref = refs[:6]
    del refs[:6]
    if final:
        gfin_ref = refs.pop(0)
    out_ref, h_ref, u0_ref, u1_ref, act_ref, acc_ref = refs[:6]
    if mixer:
        x1_ref = refs[6]
    u_refs = (u0_ref, u1_ref)
    gam, scale, shift = gam_ref[...], mod_ref[0, 4:5, :], mod_ref[0, 3:4, :]
    d_ff = wdn_ref.shape[0]
    nchunk = d_ff // FFN_TF
    tm = x_ref.shape[1]

    if mixer:
        x1 = x_ref[0] + mod_ref[0, 2:3, :] * _mm(o_ref[0], wout_ref[...])
        x1_ref[...] = x1
        halo = halo_ref[0] + mod_ref[0, 2:3, :] * _mm(ohalo_ref[0], wout_ref[...])[O_HALO - HALO:, :]
    else:
        x1, halo = x_ref[0], halo_ref[0]
    halo = _modnorm(halo, gam, scale, shift)
    halo = jnp.where(pl.program_id(1) == 0, 0.0, halo)
    hp = _regroup(_modnorm(x1, gam, scale, shift), SUBLANES)
    for back in range(1, CONV_W):
        last = hp[tm - back * SUBLANES:tm - (back - 1) * SUBLANES, :]
        lead = jnp.concatenate([halo[HALO - back:HALO - back + 1, :], last[:SUBLANES - 1, :]], axis=0)
        h_ref[(CONV_W - 1 - back) * SUBLANES:(CONV_W - back) * SUBLANES, :] = lead.astype(h_ref.dtype)
    h_ref[LEAD:LEAD + tm, :] = hp.astype(h_ref.dtype)

    def halves(c):
        return [slice(half * d_ff + c * FFN_TF, half * d_ff + (c + 1) * FFN_TF) for half in range(2)]

    def up(c):
        for half, cols in enumerate(halves(c)):
            u_refs[c % 2][:, half * FFN_TF:(half + 1) * FFN_TF] = _mm(h_ref[...], wup_ref[:, cols])

    def conv_act(c):
        u_ref = u_refs[c % 2]
        for rb in range(tm // FFN_ROWS):
            ab = []
            for half, cols in enumerate(halves(c)):
                v = cb_ref[:, cols]
                for t in range(CONV_W):
                    off = rb * FFN_ROWS + t * SUBLANES
                    v = v + cw_ref[t:t + 1, cols] * u_ref[off:off + FFN_ROWS, half * FFN_TF:(half + 1) * FFN_TF]
                ab.append(v)
            act_ref[rb * FFN_ROWS:(rb + 1) * FFN_ROWS, c * FFN_TF:(c + 1) * FFN_TF] = (
                _silu(ab[0]) * ab[1]).astype(act_ref.dtype)

    def down(first, last):
        cols = slice(first * FFN_TF, last * FFN_TF)
        return _mm(act_ref[:, cols], wdn_ref[cols, :])

    up(0)
    start, pending = 0, None
    for c in range(nchunk):
        if c + 1 < nchunk:
            up(c + 1)
        if pending is not None:
            if pending[0] == 0:
                acc_ref[...] = down(*pending)
            else:
                acc_ref[...] += down(*pending)
            pending = None
        if c + 1 - start == FFN_DOWN_GROUP and c + 1 < nchunk:
            pending, start = (start, c + 1), c + 1
        conv_act(c)

    ffn = down(start, nchunk)
    if start > 0:
        ffn = ffn + acc_ref[...]
    ffn = _regroup(ffn, tm // SUBLANES)
    y = (x1_ref[...] if mixer else x_ref[0]) + mod_ref[0, 5:6, :] * ffn
    if final:
        ms = jnp.mean(y * y, axis=-1, keepdims=True)
        y = y * lax.rsqrt(ms + EPS) * gfin_ref[...]
    out_ref[0] = y


def _conv_ffn(x, mod, gamma, w_up, conv_w, conv_b, w_down, gamma_final=None, mixer=None):
    bsz, s, d = x.shape
    d_ff = w_down.shape[0]
    final = gamma_final is not None
    rows = FFN_TM
    tiles_per_halo = rows // HALO
    resident = dict(pipeline_mode=pl.Buffered(1))
    in_specs = [
        pl.BlockSpec((1, rows, d), lambda b, i: (b, i, 0)),
        pl.BlockSpec((1, HALO, d), lambda b, i: (b, jnp.maximum(i * tiles_per_halo - 1, 0), 0)),
    ]
    args = [x, x]
    if mixer is not None:
        o, w_out = mixer
        k = o.shape[-1]
        in_specs += [
            pl.BlockSpec((1, rows, k), lambda b, i: (b, i, 0)),
            pl.BlockSpec((1, O_HALO, k), lambda b, i: (b, jnp.maximum(i * (rows // O_HALO) - 1, 0), 0)),
            pl.BlockSpec((k, d), lambda b, i: (0, 0), **resident),
        ]
        args += [o, o, w_out.astype(MXU_DTYPE)]
    in_specs += [
        pl.BlockSpec((1, 6, d), lambda b, i: (b, 0, 0)),
        pl.BlockSpec((1, d), lambda b, i: (0, 0)),
        pl.BlockSpec((d, 2 * d_ff), lambda b, i: (0, 0), **resident),
        pl.BlockSpec((CONV_W, 2 * d_ff), lambda b, i: (0, 0)),
        pl.BlockSpec((1, 2 * d_ff), lambda b, i: (0, 0)),
        pl.BlockSpec((d_ff, d), lambda b, i: (0, 0), **resident),
    ]
    args += [mod, gamma, w_up.astype(MXU_DTYPE), conv_w, conv_b[None, :], w_down.astype(MXU_DTYPE)]
    if final:
        in_specs.append(pl.BlockSpec((1, d), lambda b, i: (0, 0)))
        args.append(gamma_final)
    return pl.pallas_call(
        functools.partial(_ffn_kernel, final=final, mixer=mixer is not None),
        grid=(bsz, s // rows),
        in_specs=in_specs,
        out_specs=pl.BlockSpec((1, rows, d), lambda b, i: (b, i, 0)),
        out_shape=jax.ShapeDtypeStruct((bsz, s, d), f32),
        scratch_shapes=[
            pltpu.VMEM((rows + LEAD, d), MXU_DTYPE),
            pltpu.VMEM((rows + LEAD, 2 * FFN_TF), f32),
            pltpu.VMEM((rows + LEAD, 2 * FFN_TF), f32),
            pltpu.VMEM((rows, d_ff), MXU_DTYPE),
            pltpu.VMEM((rows, d), f32),
        ] + ([pltpu.VMEM((rows, d), f32)] if mixer is not None else []),
        compiler_params=pltpu.CompilerParams(
            dimension_semantics=("parallel", "parallel"), vmem_limit_bytes=VMEM_LIMIT),
        name="conv_ffn_final" if final else "conv_ffn",
    )(*args)


def _gla_kernel(x_ref, mod_ref, gam_ref, win_ref, wglr_ref, wgate_ref, bgate_ref, gnorm_ref, wout_ref,
                out_ref, h_ref, proj0_ref, proj1_ref, gk0_ref, gk1_ref, og0_ref, og1_ref, state_ref):
    @pl.when(pl.program_id(1) == 0)
    def _():
        state_ref[...] = jnp.zeros_like(state_ref)

    proj_refs, gk_refs, og_refs = (proj0_ref, proj1_ref), (gk0_ref, gk1_ref), (og0_ref, og1_ref)
    nsub = x_ref.shape[1] // GLA_TM
    nchunk = GLA_TM // B_CHUNK
    n_main = win_ref.shape[1]

    ri = lax.broadcasted_iota(jnp.int32, (B_CHUNK, B_CHUNK), 0)
    ci = lax.broadcasted_iota(jnp.int32, (B_CHUNK, B_CHUNK), 1)
    causal = ci <= ri
    tri = causal.astype(MXU_DTYPE)
    gnorm = gnorm_ref[...]
    qscale = B_DK ** -0.5
    r_off = 2 * B_QK + B_V
    heads = [slice(hd * B_DK, (hd + 1) * B_DK) for hd in range(B_HEADS)]

    def project_steps(k):
        rows = slice(k * GLA_TM, (k + 1) * GLA_TM)

        def normalize():
            h = _modnorm(x_ref[0, rows, :], gam_ref[...], mod_ref[0, 1:2, :], mod_ref[0, 0:1, :])
            h_ref[...] = h.astype(h_ref.dtype)

        def piece(cols):
            def run():
                proj_refs[k % 2][:, cols] = _mm(h_ref[...], win_ref[:, cols])
            return run

        def gate():
            glr = _mm(h_ref[...], wglr_ref[...]).astype(MXU_DTYPE)
            z = _mm(glr, wgate_ref[...]) + bgate_ref[...]
            gk_refs[k % 2][...] = (jnp.minimum(z, 0.0) - jnp.log1p(jnp.exp(-jnp.abs(z)))) * (1.0 / B_TAU)

        pieces = [piece(slice(c0, c0 + GLA_PROJ_COLS)) for c0 in range(0, n_main, GLA_PROJ_COLS)]
        return [normalize] + pieces + [gate]

    def local(k, c):
        proj_ref, gk_ref = proj_refs[k % 2], gk_refs[k % 2]
        rows = slice(c * B_CHUNK, (c + 1) * B_CHUNK)
        gk = gk_ref[rows, :]
        g_hi = gk.astype(MXU_DTYPE)
        rem = gk - g_hi.astype(f32)
        g_mid = rem.astype(MXU_DTYPE)
        g_lo = (rem - g_mid.astype(f32)).astype(MXU_DTYPE)
        bcum = _mm(tri, g_hi) + _mm(tri, g_mid) + _mm(tri, g_lo)
        blast = bcum[B_CHUNK - 1:B_CHUNK, :]
        q_t = ((proj_ref[rows, 0:B_QK] * qscale) * jnp.exp(bcum)).astype(MXU_DTYPE)
        kk = proj_ref[rows, B_QK:2 * B_QK]
        k_t = (kk * jnp.exp(-bcum)).astype(MXU_DTYPE)
        k_d = (kk * jnp.exp(blast - bcum)).astype(MXU_DTYPE)
        v = [proj_ref[rows, 2 * B_QK + hd * B_DV:2 * B_QK + (hd + 1) * B_DV].astype(MXU_DTYPE)
             for hd in range(B_HEADS)]
        a = [jnp.where(causal, _mm_nt(q_t[:, ks], k_t[:, ks]), 0.0).astype(MXU_DTYPE) for ks in heads]
        kv = [_mm_tn(k_d[:, ks], v[hd]) for hd, ks in enumerate(heads)]
        o_intra = [_mm(a[hd], v[hd]) for hd in range(B_HEADS)]
        return q_t, blast, kv, o_intra

    def recur(k, c, loc, state):
        q_t, blast, kv, o_intra = loc
        rows = slice(c * B_CHUNK, (c + 1) * B_CHUNK)
        new_state = []
        for hd, ks in enumerate(heads):
            o = o_intra[hd] + _mm(q_t[:, ks], state[hd].astype(MXU_DTYPE))
            decay = jnp.exp(jnp.broadcast_to(blast[:, ks], (B_DK, B_DK)).T)
            new_state.append(jnp.concatenate([decay] * (B_DV // B_DK), axis=1) * state[hd] + kv[hd])
            o = o * lax.rsqrt(jnp.mean(o * o, axis=-1, keepdims=True) + EPS) * gnorm
            r = proj_refs[k % 2][rows, r_off + hd * B_DV:r_off + (hd + 1) * B_DV]
            og_refs[k % 2][rows, hd * B_DV:(hd + 1) * B_DV] = (o * _silu(r)).astype(og_refs[k % 2].dtype)
        return new_state

    for step in project_steps(0):
        step()
    state = [state_ref[hd] for hd in range(B_HEADS)]
    for k in range(nsub):
        ahead = project_steps(k + 1) if k + 1 < nsub else []
        loc = local(k, 0)
        for c in range(nchunk):
            nxt = local(k, c + 1) if c + 1 < nchunk else None
            state = recur(k, c, loc, state)
            loc = nxt
            take = -(-len(ahead) // (nchunk - c))
            for step in ahead[:take]:
                step()
            ahead = ahead[take:]
        rows = slice(k * GLA_TM, (k + 1) * GLA_TM)
        out_ref[0, rows, :] = x_ref[0, rows, :] + mod_ref[0, 2:3, :] * _mm(og_refs[k % 2][...], wout_ref[...])
    for hd in range(B_HEADS):
        state_ref[hd] = state[hd]


def _gla_layer(x, mod, gamma, w_in, w_gate, b_gate, g_norm, w_out):
    bsz, s, d = x.shape
    glr0 = 2 * B_QK + B_V
    w_main = jnp.concatenate([w_in[:, :glr0], w_in[:, glr0 + B_GATE_RANK:]], axis=1).astype(MXU_DTYPE)
    w_glr = jnp.pad(w_in[:, glr0:glr0 + B_GATE_RANK], ((0, 0), (0, LANES - B_GATE_RANK))).astype(MXU_DTYPE)
    w_gate_p = jnp.pad(w_gate, ((0, LANES - B_GATE_RANK), (0, 0))).astype(MXU_DTYPE)
    n_main = w_main.shape[1]
    rows = GLA_SUB * GLA_TM
    const = lambda b, i: (0, 0)
    resident = dict(pipeline_mode=pl.Buffered(1))
    return pl.pallas_call(
        _gla_kernel,
        grid=(bsz, s // rows),
        in_specs=[
            pl.BlockSpec((1, rows, d), lambda b, i: (b, i, 0)),
            pl.BlockSpec((1, 6, d), lambda b, i: (b, 0, 0)),
            pl.BlockSpec((1, d), const),
            pl.BlockSpec((d, n_main), const, **resident),
            pl.BlockSpec((d, LANES), const),
            pl.BlockSpec((LANES, B_QK), const),
            pl.BlockSpec((1, B_QK), const),
            pl.BlockSpec((1, B_DV), const),
            pl.BlockSpec((B_V, d), const, **resident),
        ],
        out_specs=pl.BlockSpec((1, rows, d), lambda b, i: (b, i, 0)),
        out_shape=jax.ShapeDtypeStruct((bsz, s, d), f32),
        scratch_shapes=[
            pltpu.VMEM((GLA_TM, d), MXU_DTYPE),
            pltpu.VMEM((GLA_TM, n_main), f32),
            pltpu.VMEM((GLA_TM, n_main), f32),
            pltpu.VMEM((GLA_TM, B_QK), f32),
            pltpu.VMEM((GLA_TM, B_QK), f32),
            pltpu.VMEM((GLA_TM, B_V), MXU_DTYPE),
            pltpu.VMEM((GLA_TM, B_V), MXU_DTYPE),
            pltpu.VMEM((B_HEADS, B_DK, B_DV), f32),
        ],
        compiler_params=pltpu.CompilerParams(
            dimension_semantics=("parallel", "arbitrary"), vmem_limit_bytes=VMEM_LIMIT),
        name="gla_layer",
    )(x, mod, gamma, w_main, w_glr, w_gate_p, b_gate[None, :], g_norm[None, :], w_out.astype(MXU_DTYPE))


def kernel(x, c, w_in_a, w_out_a, rel_bias, w_in_b, w_gate_b, b_gate_b, gnorm_b, w_out_b, norm_mix, norm_ffn, w_ada, b_ada, w_up, conv_w, conv_b, w_down, norm_final):
    depth = w_ada.shape[0]
    mod = _adaln(c, w_ada, b_ada)
    bias = _rel_bias(rel_bias)
    for i in range(depth):
        gam_mix = norm_mix[i][None, :]
        j = i // 2
        mixer = None
        if i % 2 == 0:
            qkv = _qkv_proj(x, mod[i], gam_mix, w_in_a[j])
            mixer = (_attention(qkv, bias), w_out_a[j])
        else:
            x = _gla_layer(x, mod[i], gam_mix, w_in_b[j], w_gate_b[j], b_gate_b[j], gnorm_b[j], w_out_b[j])
        last = i == depth - 1
        x = _conv_ffn(x, mod[i], norm_ffn[i][None, :], w_up[i], conv_w[i], conv_b[i], w_down[i],
                      gamma_final=norm_final[None, :] if last else None, mixer=mixer)
    return x
```

```python
import functools
import math

import jax
import jax.numpy as jnp
from jax import lax
from jax.experimental import pallas as pl
from jax.experimental.pallas import tpu as pltpu

A_CONFIGS = ((128, 1), (512, 4), (2048, 16))
A_GROUPS = len(A_CONFIGS)
A_HEADS = 16
A_HEAD_DIM = 64
N_BUCKETS = 32
MAX_DISTANCE = 2048
B_HEADS = 4
B_DK = 128
B_DV = 256
B_QK = B_HEADS * B_DK
B_V = B_HEADS * B_DV
B_GATE_RANK = 16
B_TAU = 16.0
B_CHUNK = 64
CONV_W = 3
EPS = 1e-6
NEG_INF = -1e30
LOG2E = math.log2(math.e)

LANES = 128
SUBLANES = 8
MXU_DTYPE = jnp.bfloat16
ATT_BLK = 128
ATT_TILE = 2048
ATT_UNITS = ATT_TILE // ATT_BLK
ATT_PIPELINE = 3
PAIR_W = 3 * LANES
N_PAIRS = A_HEADS // 2
QKV_ROWS = 1024
PROBE_ROWS = 2 * SUBLANES
FFN_TM = 512
FFN_TF = 256
FFN_ROWS = 256
FFN_DOWN_GROUP = 10
HALO = SUBLANES
LEAD = (CONV_W - 1) * SUBLANES
O_HALO = 2 * SUBLANES
GLA_TM = 512
GLA_SUB = 2
GLA_PROJ_COLS = 256
VMEM_LIMIT = 56 * 1024 * 1024

f32 = jnp.float32


def _mm(a, b):
    return jnp.dot(a, b, preferred_element_type=f32)


def _mm_nt(a, b):
    return lax.dot_general(a, b, (((1,), (1,)), ((), ())), preferred_element_type=f32)


def _mm_tn(a, b):
    return lax.dot_general(a, b, (((0,), (0,)), ((), ())), preferred_element_type=f32)


def _modnorm(x, gamma, scale, shift):
    ms = jnp.mean(x * x, axis=-1, keepdims=True)
    y = x * lax.rsqrt(ms + EPS) * gamma
    return y * (1.0 + scale) + shift


def _silu(x):
    return x * (1.0 / (1.0 + jnp.exp(-x)))


def _regroup(t, groups):
    rows, d = t.shape
    return t.reshape(groups, rows // groups, d).swapaxes(0, 1).reshape(rows, d)


def _adaln_kernel(c_ref, w_ref, b_ref, o_ref):
    s = _silu(c_ref[...]).astype(MXU_DTYPE)
    o_ref[0] = _mm(s, w_ref[0].astype(MXU_DTYPE)) + b_ref[0]


def _adaln(c, w_ada, b_ada):
    depth, d, n = w_ada.shape
    bsz = c.shape[0]
    rows = 8 * pl.cdiv(bsz, 8)
    c_pad = jnp.pad(c, ((0, rows - bsz), (0, 0)))
    out = pl.pallas_call(
        _adaln_kernel,
        grid=(depth, n // d),
        in_specs=[
            pl.BlockSpec((rows, d), lambda l, j: (0, 0)),
            pl.BlockSpec((1, d, d), lambda l, j: (l, 0, j)),
            pl.BlockSpec((1, 1, d), lambda l, j: (l, 0, j)),
        ],
        out_specs=pl.BlockSpec((1, rows, d), lambda l, j: (l, 0, j)),
        out_shape=jax.ShapeDtypeStruct((depth, rows, n), f32),
        compiler_params=pltpu.CompilerParams(
            dimension_semantics=("parallel", "parallel"), vmem_limit_bytes=VMEM_LIMIT),
        name="adaln",
    )(c_pad, w_ada, b_ada.reshape(depth, 1, n))
    return out[:, :bsz].reshape(depth, bsz, n // d, d)


def _qkv_kernel(x_ref, mod_ref, gam_ref, wq_ref, wk_ref, wv_ref, o_ref, ha_ref, hb_ref, sa_ref, sb_ref, *,
                dilations, tiles_per_group):
    j = pl.program_id(2)
    g = j // tiles_per_group
    t = j % tiles_per_group
    nlb = sa_ref.shape[0]
    chunks_per_step = ATT_UNITS // tiles_per_group
    h_refs, stage_refs = [ha_ref, hb_ref], [sa_ref, sb_ref]

    @pl.when(j == 0)
    def _():
        for c in range(ATT_UNITS):
            rows = slice(c * ATT_BLK, (c + 1) * ATT_BLK)
            hn = _modnorm(x_ref[0, rows, :], gam_ref[...], mod_ref[0, 1:2, :], mod_ref[0, 0:1, :])
            ha_ref[rows, :] = hn.astype(ha_ref.dtype)
            for lb in range(nlb):
                sa_ref[lb, rows, :] = hn[:, lb * LANES:(lb + 1) * LANES]

    def deinterleave_chunk(gi, cc):
        d, dp = dilations[gi], dilations[gi - 1]
        rel = d // dp
        per = ATT_UNITS // d
        c = t * chunks_per_step + cc
        r, q = c // per, c % per
        start = (r % dp) * (ATT_TILE // dp) + q * (ATT_BLK * rel) + r // dp
        rows = pl.ds(start, ATT_BLK, stride=rel)
        r0 = pl.multiple_of(c * ATT_BLK, ATT_BLK)
        parts = [stage_refs[(gi - 1) % 2][lb, rows, :] for lb in range(nlb)]
        if gi + 1 < len(dilations):
            for lb in range(nlb):
                stage_refs[gi % 2][lb, pl.ds(r0, ATT_BLK), :] = parts[lb]
        dst_ref = h_refs[gi % 2]
        dst_ref[pl.ds(r0, ATT_BLK), :] = jnp.concatenate(parts, axis=1).astype(dst_ref.dtype)
        probe = dst_ref[pl.ds(r0, PROBE_ROWS), 0:LANES]
        return (probe != probe) & (probe == probe)

    def project_chunk(src_ref, rc, never):
        rows = slice(rc * QKV_ROWS, (rc + 1) * QKV_ROWS)
        lhs = src_ref[rows, :]
        for k, w_ref in enumerate((wq_ref, wk_ref, wv_ref)):
            res = _mm(lhs, w_ref[...]).astype(o_ref.dtype)
            lanes = slice(k * LANES, (k + 1) * LANES)
            o_ref[0, 0, 1, rows, lanes] = res[:, LANES:]
            if never is not None and k == 0:
                top = rc * QKV_ROWS + PROBE_ROWS
                o_ref[0, 0, 0, rc * QKV_ROWS:top, lanes] = jnp.where(
                    never, jnp.zeros_like(res[:PROBE_ROWS, :LANES]), res[:PROBE_ROWS, :LANES])
                o_ref[0, 0, 0, top:(rc + 1) * QKV_ROWS, lanes] = res[PROBE_ROWS:, :LANES]
            else:
                o_ref[0, 0, 0, rows, lanes] = res[:, :LANES]

    assert dilations[0] == 1
    assert all(d % dp == 0 for dp, d in zip(dilations, dilations[1:]))
    n_mm = ATT_TILE // QKV_ROWS
    for gi in range(len(dilations)):
        @pl.when(g == gi)
        def _(gi=gi):
            slots = max(n_mm - 1, 1)
            shares = [list(range(chunks_per_step))[sl::slots] for sl in range(slots)] + [[]]
            for rc in range(n_mm):
                never = None
                if gi + 1 < len(dilations):
                    for cc in shares[rc]:
                        probe = deinterleave_chunk(gi + 1, cc)
                        never = probe if never is None else never | probe
                project_chunk(h_refs[gi % 2], rc, never)


def _qkv_proj(x, mod, gamma, w):
    bsz, s, d = x.shape
    dilations = tuple(dl for _, dl in A_CONFIGS)
    tn = 2 * LANES
    tiles_per_group = A_HEADS * A_HEAD_DIM // tn
    kern = functools.partial(_qkv_kernel, dilations=dilations, tiles_per_group=tiles_per_group)

    def w_spec(k):
        return pl.BlockSpec(
            (d, tn), lambda b, i, j: (0, (j // tiles_per_group * 3 + k) * tiles_per_group + j % tiles_per_group))

    return pl.pallas_call(
        kern,
        grid=(bsz, s // ATT_TILE, A_GROUPS * tiles_per_group),
        in_specs=[
            pl.BlockSpec((1, ATT_TILE, d), lambda b, i, j: (b, i, 0)),
            pl.BlockSpec((1, 6, d), lambda b, i, j: (b, 0, 0)),
            pl.BlockSpec((1, d), lambda b, i, j: (0, 0)),
            w_spec(0), w_spec(1), w_spec(2),
        ],
        out_specs=pl.BlockSpec(
            (1, 1, 2, ATT_TILE, PAIR_W),
            lambda b, i, j: (j // tiles_per_group, b, j % tiles_per_group, i, 0)),
        out_shape=jax.ShapeDtypeStruct((A_GROUPS, bsz, N_PAIRS, s, PAIR_W), MXU_DTYPE),
        scratch_shapes=[pltpu.VMEM((ATT_TILE, d), MXU_DTYPE),
                        pltpu.VMEM((ATT_TILE, d), MXU_DTYPE),
                        pltpu.VMEM((d // LANES, ATT_TILE, LANES), f32),
                        pltpu.VMEM((d // LANES, ATT_TILE, LANES), f32)],
        compiler_params=pltpu.CompilerParams(
            dimension_semantics=("parallel", "parallel", "arbitrary"),
            vmem_limit_bytes=VMEM_LIMIT),
        name="qkv_proj",
    )(x, mod, gamma, w, w, w)


def _bias_kernel(tab_ref, bkt_ref, o_ref):
    bkt = bkt_ref[0]
    hits = [(bkt >= k) & (bkt < k + 1) for k in range(N_BUCKETS)]
    for h in range(A_HEADS):
        col = pl.program_id(0) * A_HEADS + h
        acc = jnp.full(bkt.shape, NEG_INF, f32)
        for k in range(N_BUCKETS):
            acc = jnp.where(hits[k], tab_ref[k, col] * LOG2E, acc)
        o_ref[0, h] = acc


def _t5_bucket(dist):
    max_exact = N_BUCKETS // 2
    n = jnp.maximum(dist, max_exact).astype(f32)
    large = max_exact + (jnp.log(n / max_exact) / math.log(MAX_DISTANCE / max_exact)
                         * (N_BUCKETS - max_exact)).astype(jnp.int32)
    large = jnp.minimum(large, N_BUCKETS - 1)
    return jnp.where(dist < max_exact, dist, large)


def _rel_bias(rel_bias):
    qi = jnp.arange(ATT_BLK)[:, None]
    ki = jnp.arange(2 * ATT_BLK)[None, :]
    steps = qi + ATT_BLK - ki
    band = (steps >= 0) & (steps <= ATT_BLK)
    bucket = jnp.stack([
        jnp.where(band, _t5_bucket(jnp.clip(steps, 0, ATT_BLK) * dl), -1)
        for _, dl in A_CONFIGS]).astype(jnp.int32)
    return pl.pallas_call(
        _bias_kernel,
        grid=(A_GROUPS,),
        in_specs=[
            pl.BlockSpec(memory_space=pltpu.SMEM),
            pl.BlockSpec((1, ATT_BLK, 2 * ATT_BLK), lambda g: (g, 0, 0)),
        ],
        out_specs=pl.BlockSpec((1, A_HEADS, ATT_BLK, 2 * ATT_BLK), lambda g: (g, 0, 0, 0)),
        out_shape=jax.ShapeDtypeStruct((A_GROUPS, A_HEADS, ATT_BLK, 2 * ATT_BLK), f32),
        name="rel_bias",
    )(rel_bias, bucket)


def _attn_kernel(qkv_ref, bias_ref, out_ref, prev0, prev1, prev2, o1, l1, n1, o2, l2, n2, biasp):
    first = pl.program_id(2) == 0

    @pl.when(first)
    def _():
        prev0[...] = jnp.zeros_like(prev0)
        prev1[...] = jnp.zeros_like(prev1)
        prev2[...] = jnp.zeros_like(prev2)

    for g in range(A_GROUPS):
        for hh in range(2):
            biasp[g, hh, :, 0:ATT_BLK] = jnp.where(first, NEG_INF, bias_ref[g, hh, :, 0:ATT_BLK])
            biasp[g, hh, :, ATT_BLK:2 * ATT_BLK] = bias_ref[g, hh, :, ATT_BLK:2 * ATT_BLK]

    lane = lax.broadcasted_iota(jnp.int32, (1, LANES), 1)
    lo = lane < A_HEAD_DIM
    qmask = (jnp.where(lo, 1.0, 0.0).astype(MXU_DTYPE), jnp.where(lo, 0.0, 1.0).astype(MXU_DTYPE))

    def keys_values(g, rc, prev_ref, rp, lanes):
        if prev_ref is None:
            return qkv_ref[g, 0, 0, rp:rp + 2 * ATT_BLK, lanes]
        off = lanes.start - LANES
        return jnp.concatenate([prev_ref[rp:rp + ATT_BLK, off:off + LANES],
                                qkv_ref[g, 0, 0, rc:rc + ATT_BLK, lanes]], axis=0)

    def scores(g, rc, prev_ref, rp):
        q = qkv_ref[g, 0, 0, rc:rc + ATT_BLK, 0:LANES]
        k = keys_values(g, rc, prev_ref, rp, slice(LANES, 2 * LANES))
        bias = bias_ref if prev_ref is None else biasp
        return [_mm_nt(q * qmask[hh], k) + bias[g, hh] for hh in range(2)]

    def attend(g, rc, prev_ref, rp, s):
        v = keys_values(g, rc, prev_ref, rp, slice(2 * LANES, 3 * LANES))
        accs, dens, ms = [], [], []
        for sh in s:
            m = jnp.max(sh, axis=-1, keepdims=True)
            p = jnp.exp2(sh - m)
            dens.append(jnp.sum(p, axis=-1, keepdims=True))
            accs.append(_mm(p.astype(MXU_DTYPE), v))
            ms.append(m)
        acc = jnp.where(lo, accs[0], accs[1])
        return (acc, jnp.broadcast_to(jnp.where(lo, ms[0], ms[1]), acc.shape),
                jnp.broadcast_to(jnp.where(lo, dens[0], dens[1]), acc.shape))

    def store(a_ref, m_ref, d_ref, rows):
        def post(acc, m, den):
            a_ref[rows, :] = acc
            m_ref[rows, :] = m
            d_ref[rows, :] = den
        return post

    def merge(rows):
        def post(acc0, m0, den0):
            m1, m2 = l1[rows, :], l2[rows, :]
            mx = jnp.maximum(jnp.maximum(m0, m1), m2)
            e0, e1, e2 = jnp.exp2(m0 - mx), jnp.exp2(m1 - mx), jnp.exp2(m2 - mx)
            num = e0 * acc0 + e1 * o1[rows, :] + e2 * o2[rows, :]
            den = e0 * den0 + e1 * n1[rows, :] + e2 * n2[rows, :]
            out_ref[0, rows, :] = (num * (1.0 / den)).astype(out_ref.dtype)
        return post

    units = []
    d2 = A_CONFIGS[2][1]
    for u in range(ATT_UNITS):
        units.append((2, u * ATT_BLK, prev2, u * ATT_BLK, store(o2, l2, n2, pl.ds(u, ATT_BLK, stride=d2))))
    d1 = A_CONFIGS[1][1]
    per = ATT_UNITS // d1
    for u in range(ATT_UNITS):
        r, q = divmod(u, per)
        rc = u * ATT_BLK
        post = store(o1, l1, n1, pl.ds(q * ATT_BLK * d1 + r, ATT_BLK, stride=d1))
        if q == 0:
            units.append((1, rc, prev1, (r * per + per - 1) * ATT_BLK, post))
        else:
            units.append((1, rc, None, rc - ATT_BLK, post))
    for u in range(ATT_UNITS):
        rc = u * ATT_BLK
        post = merge(slice(rc, rc + ATT_BLK))
        units.append((0, rc, prev0, 0, post) if u == 0 else (0, rc, None, rc - ATT_BLK, post))

    pending = [scores(*un[:4]) for un in units[:ATT_PIPELINE]]
    for idx, un in enumerate(units):
        if idx + ATT_PIPELINE < len(units):
            pending.append(scores(*units[idx + ATT_PIPELINE][:4]))
        un[4](*attend(*un[:4], pending.pop(0)))

    prev0[...] = qkv_ref[0, 0, 0, ATT_TILE - ATT_BLK:ATT_TILE, LANES:3 * LANES]
    prev1[...] = qkv_ref[1, 0, 0, :, LANES:3 * LANES]
    prev2[...] = qkv_ref[2, 0, 0, :, LANES:3 * LANES]


def _attention(qkv, bias):
    _, bsz, _, s, _ = qkv.shape
    return pl.pallas_call(
        _attn_kernel,
        grid=(bsz, N_PAIRS, s // ATT_TILE),
        in_specs=[
            pl.BlockSpec((A_GROUPS, 1, 1, ATT_TILE, PAIR_W), lambda b, p, i: (0, b, p, i, 0)),
            pl.BlockSpec((A_GROUPS, 2, ATT_BLK, 2 * ATT_BLK), lambda b, p, i: (0, p, 0, 0)),
        ],
        out_specs=pl.BlockSpec((1, ATT_TILE, LANES), lambda b, p, i: (b, i, p)),
        out_shape=jax.ShapeDtypeStruct((bsz, s, A_HEADS * A_HEAD_DIM), MXU_DTYPE),
        scratch_shapes=[
            pltpu.VMEM((ATT_BLK, 2 * LANES), MXU_DTYPE),
            pltpu.VMEM((ATT_TILE, 2 * LANES), MXU_DTYPE),
            pltpu.VMEM((ATT_TILE, 2 * LANES), MXU_DTYPE),
            pltpu.VMEM((ATT_TILE, LANES), f32),
            pltpu.VMEM((ATT_TILE, LANES), f32),
            pltpu.VMEM((ATT_TILE, LANES), f32),
            pltpu.VMEM((ATT_TILE, LANES), f32),
            pltpu.VMEM((ATT_TILE, LANES), f32),
            pltpu.VMEM((ATT_TILE, LANES), f32),
            pltpu.VMEM((A_GROUPS, 2, ATT_BLK, 2 * ATT_BLK), f32),
        ],
        compiler_params=pltpu.CompilerParams(
            dimension_semantics=("parallel", "parallel", "arbitrary"),
            vmem_limit_bytes=VMEM_LIMIT),
        name="dilated_attn",
    )(qkv, bias)


def _ffn_kernel(*refs, final, mixer):
    refs = list(refs)
    x_ref, halo_ref = refs[:2]
    del refs[:2]
    if mixer:
        o_ref, ohalo_ref, wout_ref = refs[:3]
        del refs[:3]
    mod_ref, gam_ref, wup_ref, cw_ref, cb_ref, wdn_ref = refs[:6]
    del refs[:6]
    if final:
        gfin_ref = refs.pop(0)
    out_ref, h_ref, u0_ref, u1_ref, act_ref, acc_ref = refs[:6]
    if mixer:
        x1_ref = refs[6]
    u_refs = (u0_ref, u1_ref)
    gam, scale, shift = gam_ref[...], mod_ref[0, 4:5, :], mod_ref[0, 3:4, :]
    d_ff = wdn_ref.shape[0]
    nchunk = d_ff // FFN_TF
    tm = x_ref.shape[1]

    if mixer:
        x1 = x_ref[0] + mod_ref[0, 2:3, :] * _mm(o_ref[0], wout_ref[...])
        x1_ref[...] = x1
        halo = halo_ref[0] + mod_ref[0, 2:3, :] * _mm(ohalo_ref[0], wout_ref[...])[O_HALO - HALO:, :]
    else:
        x1, halo = x_ref[0], halo_ref[0]
    halo = _modnorm(halo, gam, scale, shift)
    halo = jnp.where(pl.program_id(1) == 0, 0.0, halo)
    hp = _regroup(_modnorm(x1, gam, scale, shift), SUBLANES)
    for back in range(1, CONV_W):
        last = hp[tm - back * SUBLANES:tm - (back - 1) * SUBLANES, :]
        lead = jnp.concatenate([halo[HALO - back:HALO - back + 1, :], last[:SUBLANES - 1, :]], axis=0)
        h_ref[(CONV_W - 1 - back) * SUBLANES:(CONV_W - back) * SUBLANES, :] = lead.astype(h_ref.dtype)
    h_ref[LEAD:LEAD + tm, :] = hp.astype(h_ref.dtype)

    def halves(c):
        return [slice(half * d_ff + c * FFN_TF, half * d_ff + (c + 1) * FFN_TF) for half in range(2)]

    def up(c):
        for half, cols in enumerate(halves(c)):
            u_refs[c % 2][:, half * FFN_TF:(half + 1) * FFN_TF] = _mm(h_ref[...], wup_ref[:, cols])

    def conv_act(c):
        u_ref = u_refs[c % 2]
        for rb in range(tm // FFN_ROWS):
            ab = []
            for half, cols in enumerate(halves(c)):
                v = cb_ref[:, cols]
                for t in range(CONV_W):
                    off = rb * FFN_ROWS + t * SUBLANES
                    v = v + cw_ref[t:t + 1, cols] * u_ref[off:off + FFN_ROWS, half * FFN_TF:(half + 1) * FFN_TF]
                ab.append(v)
            act_ref[rb * FFN_ROWS:(rb + 1) * FFN_ROWS, c * FFN_TF:(c + 1) * FFN_TF] = (
                _silu(ab[0]) * ab[1]).astype(act_ref.dtype)

    def down(first, last):
        cols = slice(first * FFN_TF, last * FFN_TF)
        return _mm(act_ref[:, cols], wdn_ref[cols, :])

    up(0)
    start, pending = 0, None
    for c in range(nchunk):
        if c + 1 < nchunk:
            up(c + 1)
        if pending is not None:
            if pending[0] == 0:
                acc_ref[...] = down(*pending)
            else:
                acc_ref[...] += down(*pending)
            pending = None
        if c + 1 - start == FFN_DOWN_GROUP and c + 1 < nchunk:
            pending, start = (start, c + 1), c + 1
        conv_act(c)

    ffn = down(start, nchunk)
    if start > 0:
        ffn = ffn + acc_ref[...]
    ffn = _regroup(ffn, tm // SUBLANES)
    y = (x1_ref[...] if mixer else x_ref[0]) + mod_ref[0, 5:6, :] * ffn
    if final:
        ms = jnp.mean(y * y, axis=-1, keepdims=True)
        y = y * lax.rsqrt(ms + EPS) * gfin_ref[...]
    out_ref[0] = y


def _conv_ffn(x, mod, gamma, w_up, conv_w, conv_b, w_down, gamma_final=None, mixer=None):
    bsz, s, d = x.shape
    d_ff = w_down.shape[0]
    final = gamma_final is not None
    rows = FFN_TM
    tiles_per_halo = rows // HALO
    resident = dict(pipeline_mode=pl.Buffered(1))
    in_specs = [
        pl.BlockSpec((1, rows, d), lambda b, i: (b, i, 0)),
        pl.BlockSpec((1, HALO, d), lambda b, i: (b, jnp.maximum(i * tiles_per_halo - 1, 0), 0)),
    ]
    args = [x, x]
    if mixer is not None:
        o, w_out = mixer
        k = o.shape[-1]
        in_specs += [
            pl.BlockSpec((1, rows, k), lambda b, i: (b, i, 0)),
            pl.BlockSpec((1, O_HALO, k), lambda b, i: (b, jnp.maximum(i * (rows // O_HALO) - 1, 0), 0)),
            pl.BlockSpec((k, d), lambda b, i: (0, 0), **resident),
        ]
        args += [o, o, w_out.astype(MXU_DTYPE)]
    in_specs += [
        pl.BlockSpec((1, 6, d), lambda b, i: (b, 0, 0)),
        pl.BlockSpec((1, d), lambda b, i: (0, 0)),
        pl.BlockSpec((d, 2 * d_ff), lambda b, i: (0, 0), **resident),
        pl.BlockSpec((CONV_W, 2 * d_ff), lambda b, i: (0, 0)),
        pl.BlockSpec((1, 2 * d_ff), lambda b, i: (0, 0)),
        pl.BlockSpec((d_ff, d), lambda b, i: (0, 0), **resident),
    ]
    args += [mod, gamma, w_up.astype(MXU_DTYPE), conv_w, conv_b[None, :], w_down.astype(MXU_DTYPE)]
    if final:
        in_specs.append(pl.BlockSpec((1, d), lambda b, i: (0, 0)))
        args.append(gamma_final)
    return pl.pallas_call(
        functools.partial(_ffn_kernel, final=final, mixer=mixer is not None),
        grid=(bsz, s // rows),
        in_specs=in_specs,
        out_specs=pl.BlockSpec((1, rows, d), lambda b, i: (b, i, 0)),
        out_shape=jax.ShapeDtypeStruct((bsz, s, d), f32),
        scratch_shapes=[
            pltpu.VMEM((rows + LEAD, d), MXU_DTYPE),
            pltpu.VMEM((rows + LEAD, 2 * FFN_TF), f32),
            pltpu.VMEM((rows + LEAD, 2 * FFN_TF), f32),
            pltpu.VMEM((rows, d_ff), MXU_DTYPE),
            pltpu.VMEM((rows, d), f32),
        ] + ([pltpu.VMEM((rows, d), f32)] if mixer is not None else []),
        compiler_params=pltpu.CompilerParams(
            dimension_semantics=("parallel", "parallel"), vmem_limit_bytes=VMEM_LIMIT),
        name="conv_ffn_final" if final else "conv_ffn",
    )(*args)


def _gla_kernel(x_ref, mod_ref, gam_ref, win_ref, wglr_ref, wgate_ref, bgate_ref, gnorm_ref, wout_ref,
                out_ref, h_ref, proj0_ref, proj1_ref, gk0_ref, gk1_ref, og0_ref, og1_ref, state_ref):
    @pl.when(pl.program_id(1) == 0)
    def _():
        state_ref[...] = jnp.zeros_like(state_ref)

    proj_refs, gk_refs, og_refs = (proj0_ref, proj1_ref), (gk0_ref, gk1_ref), (og0_ref, og1_ref)
    nsub = x_ref.shape[1] // GLA_TM
    nchunk = GLA_TM // B_CHUNK
    n_main = win_ref.shape[1]

    ri = lax.broadcasted_iota(jnp.int32, (B_CHUNK, B_CHUNK), 0)
    ci = lax.broadcasted_iota(jnp.int32, (B_CHUNK, B_CHUNK), 1)
    causal = ci <= ri
    tri = causal.astype(MXU_DTYPE)
    gnorm = gnorm_ref[...]
    qscale = B_DK ** -0.5
    r_off = 2 * B_QK + B_V
    heads = [slice(hd * B_DK, (hd + 1) * B_DK) for hd in range(B_HEADS)]

    def project_steps(k):
        rows = slice(k * GLA_TM, (k + 1) * GLA_TM)

        def normalize():
            h = _modnorm(x_ref[0, rows, :], gam_ref[...], mod_ref[0, 1:2, :], mod_ref[0, 0:1, :])
            h_ref[...] = h.astype(h_ref.dtype)

        def piece(cols):
            def run():
                proj_refs[k % 2][:, cols] = _mm(h_ref[...], win_ref[:, cols])
            return run

        def gate():
            glr = _mm(h_ref[...], wglr_ref[...]).astype(MXU_DTYPE)
            z = _mm(glr, wgate_ref[...]) + bgate_ref[...]
            gk_refs[k % 2][...] = (jnp.minimum(z, 0.0) - jnp.log1p(jnp.exp(-jnp.abs(z)))) * (1.0 / B_TAU)

        pieces = [piece(slice(c0, c0 + GLA_PROJ_COLS)) for c0 in range(0, n_main, GLA_PROJ_COLS)]
        return [normalize] + pieces + [gate]

    def local(k, c):
        proj_ref, gk_ref = proj_refs[k % 2], gk_refs[k % 2]
        rows = slice(c * B_CHUNK, (c + 1) * B_CHUNK)
        gk = gk_ref[rows, :]
        g_hi = gk.astype(MXU_DTYPE)
        rem = gk - g_hi.astype(f32)
        g_mid = rem.astype(MXU_DTYPE)
        g_lo = (rem - g_mid.astype(f32)).astype(MXU_DTYPE)
        bcum = _mm(tri, g_hi) + _mm(tri, g_mid) + _mm(tri, g_lo)
        blast = bcum[B_CHUNK - 1:B_CHUNK, :]
        q_t = ((proj_ref[rows, 0:B_QK] * qscale) * jnp.exp(bcum)).astype(MXU_DTYPE)
        kk = proj_ref[rows, B_QK:2 * B_QK]
        k_t = (kk * jnp.exp(-bcum)).astype(MXU_DTYPE)
        k_d = (kk * jnp.exp(blast - bcum)).astype(MXU_DTYPE)
        v = [proj_ref[rows, 2 * B_QK + hd * B_DV:2 * B_QK + (hd + 1) * B_DV].astype(MXU_DTYPE)
             for hd in range(B_HEADS)]
        a = [jnp.where(causal, _mm_nt(q_t[:, ks], k_t[:, ks]), 0.0).astype(MXU_DTYPE) for ks in heads]
        kv = [_mm_tn(k_d[:, ks], v[hd]) for hd, ks in enumerate(heads)]
        o_intra = [_mm(a[hd], v[hd]) for hd in range(B_HEADS)]
        return q_t, blast, kv, o_intra

    def recur(k, c, loc, state):
        q_t, blast, kv, o_intra = loc
        rows = slice(c * B_CHUNK, (c + 1) * B_CHUNK)
        new_state = []
        for hd, ks in enumerate(heads):
            o = o_intra[hd] + _mm(q_t[:, ks], state[hd].astype(MXU_DTYPE))
            decay = jnp.exp(jnp.broadcast_to(blast[:, ks], (B_DK, B_DK)).T)
            new_state.append(jnp.concatenate([decay] * (B_DV // B_DK), axis=1) * state[hd] + kv[hd])
            o = o * lax.rsqrt(jnp.mean(o * o, axis=-1, keepdims=True) + EPS) * gnorm
            r = proj_refs[k % 2][rows, r_off + hd * B_DV:r_off + (hd + 1) * B_DV]
            og_refs[k % 2][rows, hd * B_DV:(hd + 1) * B_DV] = (o * _silu(r)).astype(og_refs[k % 2].dtype)
        return new_state

    for step in project_steps(0):
        step()
    state = [state_ref[hd] for hd in range(B_HEADS)]
    for k in range(nsub):
        ahead = project_steps(k + 1) if k + 1 < nsub else []
        loc = local(k, 0)
        for c in range(nchunk):
            nxt = local(k, c + 1) if c + 1 < nchunk else None
            state = recur(k, c, loc, state)
            loc = nxt
            take = -(-len(ahead) // (nchunk - c))
            for step in ahead[:take]:
                step()
            ahead = ahead[take:]
        rows = slice(k * GLA_TM, (k + 1) * GLA_TM)
        out_ref[0, rows, :] = x_ref[0, rows, :] + mod_ref[0, 2:3, :] * _mm(og_refs[k % 2][...], wout_ref[...])
    for hd in range(B_HEADS):
        state_ref[hd] = state[hd]


def _gla_layer(x, mod, gamma, w_in, w_gate, b_gate, g_norm, w_out):
    bsz, s, d = x.shape
    glr0 = 2 * B_QK + B_V
    w_main = jnp.concatenate([w_in[:, :glr0], w_in[:, glr0 + B_GATE_RANK:]], axis=1).astype(MXU_DTYPE)
    w_glr = jnp.pad(w_in[:, glr0:glr0 + B_GATE_RANK], ((0, 0), (0, LANES - B_GATE_RANK))).astype(MXU_DTYPE)
    w_gate_p = jnp.pad(w_gate, ((0, LANES - B_GATE_RANK), (0, 0))).astype(MXU_DTYPE)
    n_main = w_main.shape[1]
    rows = GLA_SUB * GLA_TM
    const = lambda b, i: (0, 0)
    resident = dict(pipeline_mode=pl.Buffered(1))
    return pl.pallas_call(
        _gla_kernel,
        grid=(bsz, s // rows),
        in_specs=[
            pl.BlockSpec((1, rows, d), lambda b, i: (b, i, 0)),
            pl.BlockSpec((1, 6, d), lambda b, i: (b, 0, 0)),
            pl.BlockSpec((1, d), const),
            pl.BlockSpec((d, n_main), const, **resident),
            pl.BlockSpec((d, LANES), const),
            pl.BlockSpec((LANES, B_QK), const),
            pl.BlockSpec((1, B_QK), const),
            pl.BlockSpec((1, B_DV), const),
            pl.BlockSpec((B_V, d), const, **resident),
        ],
        out_specs=pl.BlockSpec((1, rows, d), lambda b, i: (b, i, 0)),
        out_shape=jax.ShapeDtypeStruct((bsz, s, d), f32),
        scratch_shapes=[
            pltpu.VMEM((GLA_TM, d), MXU_DTYPE),
            pltpu.VMEM((GLA_TM, n_main), f32),
            pltpu.VMEM((GLA_TM, n_main), f32),
            pltpu.VMEM((GLA_TM, B_QK), f32),
            pltpu.VMEM((GLA_TM, B_QK), f32),
            pltpu.VMEM((GLA_TM, B_V), MXU_DTYPE),
            pltpu.VMEM((GLA_TM, B_V), MXU_DTYPE),
            pltpu.VMEM((B_HEADS, B_DK, B_DV), f32),
        ],
        compiler_params=pltpu.CompilerParams(
            dimension_semantics=("parallel", "arbitrary"), vmem_limit_bytes=VMEM_LIMIT),
        name="gla_layer",
    )(x, mod, gamma, w_main, w_glr, w_gate_p, b_gate[None, :], g_norm[None, :], w_out.astype(MXU_DTYPE))


def _qkv_weight(w_in):
    width = A_HEADS * A_HEAD_DIM
    col_scale = jnp.tile(jnp.repeat(jnp.array([A_HEAD_DIM ** -0.5 * LOG2E, 1.0, 1.0], f32), width), A_GROUPS)
    return (w_in * col_scale[None, :]).astype(MXU_DTYPE)


def kernel(x, c, w_in_a, w_out_a, rel_bias, w_in_b, w_gate_b, b_gate_b, gnorm_b, w_out_b, norm_mix, norm_ffn, w_ada, b_ada, w_up, conv_w, conv_b, w_down, norm_final):
    depth = w_ada.shape[0]
    mod = _adaln(c, w_ada, b_ada)
    bias = _rel_bias(rel_bias)
    for i in range(depth):
        gam_mix = norm_mix[i][None, :]
        j = i // 2
        mixer = None
        if i % 2 == 0:
            qkv = _qkv_proj(x, mod[i], gam_mix, _qkv_weight(w_in_a[j]))
            mixer = (_attention(qkv, bias), w_out_a[j])
        else:
            x = _gla_layer(x, mod[i], gam_mix, w_in_b[j], w_gate_b[j], b_gate_b[j], gnorm_b[j], w_out_b[j])
        last = i == depth - 1
        x = _conv_ffn(x, mod[i], norm_ffn[i][None, :], w_up[i], conv_w[i], conv_b[i], w_down[i],
                      gamma_final=norm_final[None, :] if last else None, mixer=mixer)
    return x
```

```python
import functools
import math

import jax
import jax.numpy as jnp
from jax import lax
from jax.experimental import pallas as pl
from jax.experimental.pallas import tpu as pltpu

A_CONFIGS = ((128, 1), (512, 4), (2048, 16))
A_GROUPS = len(A_CONFIGS)
A_HEADS = 16
A_HEAD_DIM = 64
N_BUCKETS = 32
MAX_DISTANCE = 2048
B_HEADS = 4
B_DK = 128
B_DV = 256
B_QK = B_HEADS * B_DK
B_V = B_HEADS * B_DV
B_GATE_RANK = 16
B_TAU = 16.0
B_CHUNK = 64
CONV_W = 3
EPS = 1e-6
NEG_INF = -1e30
LOG2E = math.log2(math.e)

LANES = 128
SUBLANES = 8
MXU_DTYPE = jnp.bfloat16
ATT_BLK = 128
ATT_TILE = 2048
ATT_UNITS = ATT_TILE // ATT_BLK
ATT_PIPELINE = 3
PAIR_W = 3 * LANES
N_PAIRS = A_HEADS // 2
QKV_ROWS = 1024
PROBE_ROWS = 2 * SUBLANES
FFN_TM = 512
FFN_TF = 256
FFN_ROWS = 256
FFN_DOWN_GROUP = 11
HALO = SUBLANES
LEAD = (CONV_W - 1) * SUBLANES
O_HALO = 2 * SUBLANES
GLA_TM = 512
GLA_SUB = 2
GLA_PROJ_COLS = 256
VMEM_LIMIT = 56 * 1024 * 1024

f32 = jnp.float32


def _mm(a, b):
    return jnp.dot(a, b, preferred_element_type=f32)


def _mm_nt(a, b):
    return lax.dot_general(a, b, (((1,), (1,)), ((), ())), preferred_element_type=f32)


def _mm_tn(a, b):
    return lax.dot_general(a, b, (((0,), (0,)), ((), ())), preferred_element_type=f32)


def _modnorm(x, gamma, scale, shift):
    ms = jnp.mean(x * x, axis=-1, keepdims=True)
    y = x * lax.rsqrt(ms + EPS) * gamma
    return y * (1.0 + scale) + shift


def _silu(x):
    return x * (1.0 / (1.0 + jnp.exp(-x)))


def _regroup(t, groups):
    rows, d = t.shape
    return t.reshape(groups, rows // groups, d).swapaxes(0, 1).reshape(rows, d)


def _adaln_kernel(c_ref, w_ref, b_ref, o_ref):
    s = _silu(c_ref[...]).astype(MXU_DTYPE)
    o_ref[0] = _mm(s, w_ref[0].astype(MXU_DTYPE)) + b_ref[0]


def _adaln(c, w_ada, b_ada):
    depth, d, n = w_ada.shape
    bsz = c.shape[0]
    rows = 8 * pl.cdiv(bsz, 8)
    c_pad = jnp.pad(c, ((0, rows - bsz), (0, 0)))
    out = pl.pallas_call(
        _adaln_kernel,
        grid=(depth, n // d),
        in_specs=[
            pl.BlockSpec((rows, d), lambda l, j: (0, 0)),
            pl.BlockSpec((1, d, d), lambda l, j: (l, 0, j)),
            pl.BlockSpec((1, 1, d), lambda l, j: (l, 0, j)),
        ],
        out_specs=pl.BlockSpec((1, rows, d), lambda l, j: (l, 0, j)),
        out_shape=jax.ShapeDtypeStruct((depth, rows, n), f32),
        compiler_params=pltpu.CompilerParams(
            dimension_semantics=("parallel", "parallel"), vmem_limit_bytes=VMEM_LIMIT),
        name="adaln",
    )(c_pad, w_ada, b_ada.reshape(depth, 1, n))
    return out[:, :bsz].reshape(depth, bsz, n // d, d)


def _qkv_kernel(x_ref, mod_ref, gam_ref, wq_ref, wk_ref, wv_ref, o_ref, ha_ref, hb_ref, sa_ref, sb_ref, *,
                dilations, tiles_per_group):
    j = pl.program_id(2)
    g = j // tiles_per_group
    t = j % tiles_per_group
    nlb = sa_ref.shape[0]
    chunks_per_step = ATT_UNITS // tiles_per_group
    h_refs, stage_refs = [ha_ref, hb_ref], [sa_ref, sb_ref]

    @pl.when(j == 0)
    def _():
        for c in range(ATT_UNITS):
            rows = slice(c * ATT_BLK, (c + 1) * ATT_BLK)
            hn = _modnorm(x_ref[0, rows, :], gam_ref[...], mod_ref[0, 1:2, :], mod_ref[0, 0:1, :])
            ha_ref[rows, :] = hn.astype(ha_ref.dtype)
            for lb in range(nlb):
                sa_ref[lb, rows, :] = hn[:, lb * LANES:(lb + 1) * LANES]

    def deinterleave_chunk(gi, cc):
        d, dp = dilations[gi], dilations[gi - 1]
        rel = d // dp
        per = ATT_UNITS // d
        c = t * chunks_per_step + cc
        r, q = c // per, c % per
        start = (r % dp) * (ATT_TILE // dp) + q * (ATT_BLK * rel) + r // dp
        rows = pl.ds(start, ATT_BLK, stride=rel)
        r0 = pl.multiple_of(c * ATT_BLK, ATT_BLK)
        parts = [stage_refs[(gi - 1) % 2][lb, rows, :] for lb in range(nlb)]
        if gi + 1 < len(dilations):
            for lb in range(nlb):
                stage_refs[gi % 2][lb, pl.ds(r0, ATT_BLK), :] = parts[lb]
        dst_ref = h_refs[gi % 2]
        dst_ref[pl.ds(r0, ATT_BLK), :] = jnp.concatenate(parts, axis=1).astype(dst_ref.dtype)
        probe = dst_ref[pl.ds(r0, PROBE_ROWS), 0:LANES]
        return (probe != probe) & (probe == probe)

    def project_chunk(src_ref, rc, never):
        rows = slice(rc * QKV_ROWS, (rc + 1) * QKV_ROWS)
        lhs = src_ref[rows, :]
        for k, w_ref in enumerate((wq_ref, wk_ref, wv_ref)):
            res = _mm(lhs, w_ref[...]).astype(o_ref.dtype)
            lanes = slice(k * LANES, (k + 1) * LANES)
            o_ref[0, 0, 1, rows, lanes] = res[:, LANES:]
            if never is not None and k == 0:
                top = rc * QKV_ROWS + PROBE_ROWS
                o_ref[0, 0, 0, rc * QKV_ROWS:top, lanes] = jnp.where(
                    never, jnp.zeros_like(res[:PROBE_ROWS, :LANES]), res[:PROBE_ROWS, :LANES])
                o_ref[0, 0, 0, top:(rc + 1) * QKV_ROWS, lanes] = res[PROBE_ROWS:, :LANES]
            else:
                o_ref[0, 0, 0, rows, lanes] = res[:, :LANES]

    assert dilations[0] == 1
    assert all(d % dp == 0 for dp, d in zip(dilations, dilations[1:]))
    n_mm = ATT_TILE // QKV_ROWS
    for gi in range(len(dilations)):
        @pl.when(g == gi)
        def _(gi=gi):
            slots = max(n_mm - 1, 1)
            shares = [list(range(chunks_per_step))[sl::slots] for sl in range(slots)] + [[]]
            for rc in range(n_mm):
                never = None
                if gi + 1 < len(dilations):
                    for cc in shares[rc]:
                        probe = deinterleave_chunk(gi + 1, cc)
                        never = probe if never is None else never | probe
                project_chunk(h_refs[gi % 2], rc, never)


def _qkv_proj(x, mod, gamma, w):
    bsz, s, d = x.shape
    dilations = tuple(dl for _, dl in A_CONFIGS)
    tn = 2 * LANES
    tiles_per_group = A_HEADS * A_HEAD_DIM // tn
    kern = functools.partial(_qkv_kernel, dilations=dilations, tiles_per_group=tiles_per_group)

    def w_spec(k):
        return pl.BlockSpec(
            (d, tn), lambda b, i, j: (0, (j // tiles_per_group * 3 + k) * tiles_per_group + j % tiles_per_group))

    return pl.pallas_call(
        kern,
        grid=(bsz, s // ATT_TILE, A_GROUPS * tiles_per_group),
        in_specs=[
            pl.BlockSpec((1, ATT_TILE, d), lambda b, i, j: (b, i, 0)),
            pl.BlockSpec((1, 6, d), lambda b, i, j: (b, 0, 0)),
            pl.BlockSpec((1, d), lambda b, i, j: (0, 0)),
            w_spec(0), w_spec(1), w_spec(2),
        ],
        out_specs=pl.BlockSpec(
            (1, 1, 2, ATT_TILE, PAIR_W),
            lambda b, i, j: (j // tiles_per_group, b, j % tiles_per_group, i, 0)),
        out_shape=jax.ShapeDtypeStruct((A_GROUPS, bsz, N_PAIRS, s, PAIR_W), MXU_DTYPE),
        scratch_shapes=[pltpu.VMEM((ATT_TILE, d), MXU_DTYPE),
                        pltpu.VMEM((ATT_TILE, d), MXU_DTYPE),
                        pltpu.VMEM((d // LANES, ATT_TILE, LANES), f32),
                        pltpu.VMEM((d // LANES, ATT_TILE, LANES), f32)],
        compiler_params=pltpu.CompilerParams(
            dimension_semantics=("parallel", "parallel", "arbitrary"),
            vmem_limit_bytes=VMEM_LIMIT),
        name="qkv_proj",
    )(x, mod, gamma, w, w, w)


def _bias_kernel(tab_ref, bkt_ref, o_ref):
    bkt = bkt_ref[0]
    hits = [(bkt >= k) & (bkt < k + 1) for k in range(N_BUCKETS)]
    for h in range(A_HEADS):
        col = pl.program_id(0) * A_HEADS + h
        acc = jnp.full(bkt.shape, NEG_INF, f32)
        for k in range(N_BUCKETS):
            acc = jnp.where(hits[k], tab_ref[k, col] * LOG2E, acc)
        o_ref[0, h] = acc


def _t5_bucket(dist):
    max_exact = N_BUCKETS // 2
    n = jnp.maximum(dist, max_exact).astype(f32)
    large = max_exact + (jnp.log(n / max_exact) / math.log(MAX_DISTANCE / max_exact)
                         * (N_BUCKETS - max_exact)).astype(jnp.int32)
    large = jnp.minimum(large, N_BUCKETS - 1)
    return jnp.where(dist < max_exact, dist, large)


def _rel_bias(rel_bias):
    qi = jnp.arange(ATT_BLK)[:, None]
    ki = jnp.arange(2 * ATT_BLK)[None, :]
    steps = qi + ATT_BLK - ki
    band = (steps >= 0) & (steps <= ATT_BLK)
    bucket = jnp.stack([
        jnp.where(band, _t5_bucket(jnp.clip(steps, 0, ATT_BLK) * dl), -1)
        for _, dl in A_CONFIGS]).astype(jnp.int32)
    return pl.pallas_call(
        _bias_kernel,
        grid=(A_GROUPS,),
        in_specs=[
            pl.BlockSpec(memory_space=pltpu.SMEM),
            pl.BlockSpec((1, ATT_BLK, 2 * ATT_BLK), lambda g: (g, 0, 0)),
        ],
        out_specs=pl.BlockSpec((1, A_HEADS, ATT_BLK, 2 * ATT_BLK), lambda g: (g, 0, 0, 0)),
        out_shape=jax.ShapeDtypeStruct((A_GROUPS, A_HEADS, ATT_BLK, 2 * ATT_BLK), f32),
        name="rel_bias",
    )(rel_bias, bucket)


def _attn_kernel(qkv_ref, bias_ref, out_ref, prev0, prev1, prev2, o1, l1, n1, o2, l2, n2, biasp):
    first = pl.program_id(2) == 0

    @pl.when(first)
    def _():
        prev0[...] = jnp.zeros_like(prev0)
        prev1[...] = jnp.zeros_like(prev1)
        prev2[...] = jnp.zeros_like(prev2)

    for g in range(A_GROUPS):
        for hh in range(2):
            biasp[g, hh, :, 0:ATT_BLK] = jnp.where(first, NEG_INF, bias_ref[g, hh, :, 0:ATT_BLK])
            biasp[g, hh, :, ATT_BLK:2 * ATT_BLK] = bias_ref[g, hh, :, ATT_BLK:2 * ATT_BLK]

    lane = lax.broadcasted_iota(jnp.int32, (1, LANES), 1)
    lo = lane < A_HEAD_DIM
    qmask = (jnp.where(lo, 1.0, 0.0).astype(MXU_DTYPE), jnp.where(lo, 0.0, 1.0).astype(MXU_DTYPE))

    def keys_values(g, rc, prev_ref, rp, lanes):
        if prev_ref is None:
            return qkv_ref[g, 0, 0, rp:rp + 2 * ATT_BLK, lanes]
        off = lanes.start - LANES
        return jnp.concatenate([prev_ref[rp:rp + ATT_BLK, off:off + LANES],
                                qkv_ref[g, 0, 0, rc:rc + ATT_BLK, lanes]], axis=0)

    def scores(g, rc, prev_ref, rp):
        q = qkv_ref[g, 0, 0, rc:rc + ATT_BLK, 0:LANES]
        k = keys_values(g, rc, prev_ref, rp, slice(LANES, 2 * LANES))
        bias = bias_ref if prev_ref is None else biasp
        return [_mm_nt(q * qmask[hh], k) + bias[g, hh] for hh in range(2)]

    def attend(g, rc, prev_ref, rp, s):
        v = keys_values(g, rc, prev_ref, rp, slice(2 * LANES, 3 * LANES))
        accs, dens, ms = [], [], []
        for sh in s:
            m = jnp.max(sh, axis=-1, keepdims=True)
            p = jnp.exp2(sh - m)
            dens.append(jnp.sum(p, axis=-1, keepdims=True))
            accs.append(_mm(p.astype(MXU_DTYPE), v))
            ms.append(m)
        acc = jnp.where(lo, accs[0], accs[1])
        return (acc, jnp.broadcast_to(jnp.where(lo, ms[0], ms[1]), acc.shape),
                jnp.broadcast_to(jnp.where(lo, dens[0], dens[1]), acc.shape))

    def store(a_ref, m_ref, d_ref, rows):
        def post(acc, m, den):
            a_ref[rows, :] = acc
            m_ref[rows, :] = m
            d_ref[rows, :] = den
        return post

    def merge(rows):
        def post(acc0, m0, den0):
            m1, m2 = l1[rows, :], l2[rows, :]
            mx = jnp.maximum(jnp.maximum(m0, m1), m2)
            e0, e1, e2 = jnp.exp2(m0 - mx), jnp.exp2(m1 - mx), jnp.exp2(m2 - mx)
            num = e0 * acc0 + e1 * o1[rows, :] + e2 * o2[rows, :]
            den = e0 * den0 + e1 * n1[rows, :] + e2 * n2[rows, :]
            out_ref[0, rows, :] = (num * (1.0 / den)).astype(out_ref.dtype)
        return post

    units = []
    d2 = A_CONFIGS[2][1]
    for u in range(ATT_UNITS):
        units.append((2, u * ATT_BLK, prev2, u * ATT_BLK, store(o2, l2, n2, pl.ds(u, ATT_BLK, stride=d2))))
    d1 = A_CONFIGS[1][1]
    per = ATT_UNITS // d1
    for u in range(ATT_UNITS):
        r, q = divmod(u, per)
        rc = u * ATT_BLK
        post = store(o1, l1, n1, pl.ds(q * ATT_BLK * d1 + r, ATT_BLK, stride=d1))
        if q == 0:
            units.append((1, rc, prev1, (r * per + per - 1) * ATT_BLK, post))
        else:
            units.append((1, rc, None, rc - ATT_BLK, post))
    for u in range(ATT_UNITS):
        rc = u * ATT_BLK
        post = merge(slice(rc, rc + ATT_BLK))
        units.append((0, rc, prev0, 0, post) if u == 0 else (0, rc, None, rc - ATT_BLK, post))

    pending = [scores(*un[:4]) for un in units[:ATT_PIPELINE]]
    for idx, un in enumerate(units):
        if idx + ATT_PIPELINE < len(units):
            pending.append(scores(*units[idx + ATT_PIPELINE][:4]))
        un[4](*attend(*un[:4], pending.pop(0)))

    prev0[...] = qkv_ref[0, 0, 0, ATT_TILE - ATT_BLK:ATT_TILE, LANES:3 * LANES]
    prev1[...] = qkv_ref[1, 0, 0, :, LANES:3 * LANES]
    prev2[...] = qkv_ref[2, 0, 0, :, LANES:3 * LANES]


def _attention(qkv, bias):
    _, bsz, _, s, _ = qkv.shape
    return pl.pallas_call(
        _attn_kernel,
        grid=(bsz, N_PAIRS, s // ATT_TILE),
        in_specs=[
            pl.BlockSpec((A_GROUPS, 1, 1, ATT_TILE, PAIR_W), lambda b, p, i: (0, b, p, i, 0)),
            pl.BlockSpec((A_GROUPS, 2, ATT_BLK, 2 * ATT_BLK), lambda b, p, i: (0, p, 0, 0)),
        ],
        out_specs=pl.BlockSpec((1, ATT_TILE, LANES), lambda b, p, i: (b, i, p)),
        out_shape=jax.ShapeDtypeStruct((bsz, s, A_HEADS * A_HEAD_DIM), MXU_DTYPE),
        scratch_shapes=[
            pltpu.VMEM((ATT_BLK, 2 * LANES), MXU_DTYPE),
            pltpu.VMEM((ATT_TILE, 2 * LANES), MXU_DTYPE),
            pltpu.VMEM((ATT_TILE, 2 * LANES), MXU_DTYPE),
            pltpu.VMEM((ATT_TILE, LANES), f32),
            pltpu.VMEM((ATT_TILE, LANES), f32),
            pltpu.VMEM((ATT_TILE, LANES), f32),
            pltpu.VMEM((ATT_TILE, LANES), f32),
            pltpu.VMEM((ATT_TILE, LANES), f32),
            pltpu.VMEM((ATT_TILE, LANES), f32),
            pltpu.VMEM((A_GROUPS, 2, ATT_BLK, 2 * ATT_BLK), f32),
        ],
        compiler_params=pltpu.CompilerParams(
            dimension_semantics=("parallel", "parallel", "arbitrary"),
            vmem_limit_bytes=VMEM_LIMIT),
        name="dilated_attn",
    )(qkv, bias)


def _ffn_kernel(*refs, final, mixer):
    refs = list(refs)
    x_ref, halo_ref = refs[:2]
    del refs[:2]
    if mixer:
        o_ref, ohalo_ref, wout_ref = refs[:3]
        del refs[:3]
    mod_ref, gam_ref, wup_ref, cw_ref, cb_ref, wdn_ref = refs[:6]
    del refs[:6]
    if final:
        gfin_ref = refs.pop(0)
    out_ref, h_ref, u0_ref, u1_ref, act_ref, acc_ref = refs[:6]
    if mixer:
        x1_ref = refs[6]
    u_refs = (u0_ref, u1_ref)
    gam, scale, shift = gam_ref[...], mod_ref[0, 4:5, :], mod_ref[0, 3:4, :]
    d_ff = wdn_ref.shape[0]
    nchunk = d_ff // FFN_TF
    tm = x_ref.shape[1]

    if mixer:
        x1 = x_ref[0] + mod_ref[0, 2:3, :] * _mm(o_ref[0], wout_ref[...])
        x1_ref[...] = x1
        halo = halo_ref[0] + mod_ref[0, 2:3, :] * _mm(ohalo_ref[0], wout_ref[...])[O_HALO - HALO:, :]
    else:
        x1, halo = x_ref[0], halo_ref[0]
    halo = _modnorm(halo, gam, scale, shift)
    halo = jnp.where(pl.program_id(1) == 0, 0.0, halo)
    hp = _regroup(_modnorm(x1, gam, scale, shift), SUBLANES)
    for back in range(1, CONV_W):
        last = hp[tm - back * SUBLANES:tm - (back - 1) * SUBLANES, :]
        lead = jnp.concatenate([halo[HALO - back:HALO - back + 1, :], last[:SUBLANES - 1, :]], axis=0)
        h_ref[(CONV_W - 1 - back) * SUBLANES:(CONV_W - back) * SUBLANES, :] = lead.astype(h_ref.dtype)
    h_ref[LEAD:LEAD + tm, :] = hp.astype(h_ref.dtype)

    def halves(c):
        return [slice(half * d_ff + c * FFN_TF, half * d_ff + (c + 1) * FFN_TF) for half in range(2)]

    def up(c):
        for half, cols in enumerate(halves(c)):
            u_refs[c % 2][:, half * FFN_TF:(half + 1) * FFN_TF] = _mm(h_ref[...], wup_ref[:, cols])

    def conv_act(c):
        u_ref = u_refs[c % 2]
        for rb in range(tm // FFN_ROWS):
            ab = []
            for half, cols in enumerate(halves(c)):
                v = cb_ref[:, cols]
                for t in range(CONV_W):
                    off = rb * FFN_ROWS + t * SUBLANES
                    v = v + cw_ref[t:t + 1, cols] * u_ref[off:off + FFN_ROWS, half * FFN_TF:(half + 1) * FFN_TF]
                ab.append(v)
            act_ref[rb * FFN_ROWS:(rb + 1) * FFN_ROWS, c * FFN_TF:(c + 1) * FFN_TF] = (
                _silu(ab[0]) * ab[1]).astype(act_ref.dtype)

    def down(first, last):
        cols = slice(first * FFN_TF, last * FFN_TF)
        return _mm(act_ref[:, cols], wdn_ref[cols, :])

    up(0)
    start, pending = 0, None
    for c in range(nchunk):
        if c + 1 < nchunk:
            up(c + 1)
        if pending is not None:
            if pending[0] == 0:
                acc_ref[...] = down(*pending)
            else:
                acc_ref[...] += down(*pending)
            pending = None
        if c + 1 - start == FFN_DOWN_GROUP and c + 1 < nchunk:
            pending, start = (start, c + 1), c + 1
        conv_act(c)

    ffn = down(start, nchunk)
    if start > 0:
        ffn = ffn + acc_ref[...]
    ffn = _regroup(ffn, tm // SUBLANES)
    y = (x1_ref[...] if mixer else x_ref[0]) + mod_ref[0, 5:6, :] * ffn
    if final:
        ms = jnp.mean(y * y, axis=-1, keepdims=True)
        y = y * lax.rsqrt(ms + EPS) * gfin_ref[...]
    out_ref[0] = y


def _conv_ffn(x, mod, gamma, w_up, conv_w, conv_b, w_down, gamma_final=None, mixer=None):
    bsz, s, d = x.shape
    d_ff = w_down.shape[0]
    final = gamma_final is not None
    rows = FFN_TM
    tiles_per_halo = rows // HALO
    resident = dict(pipeline_mode=pl.Buffered(1))
    in_specs = [
        pl.BlockSpec((1, rows, d), lambda b, i: (b, i, 0)),
        pl.BlockSpec((1, HALO, d), lambda b, i: (b, jnp.maximum(i * tiles_per_halo - 1, 0), 0)),
    ]
    args = [x, x]
    if mixer is not None:
        o, w_out = mixer
        k = o.shape[-1]
        in_specs += [
            pl.BlockSpec((1, rows, k), lambda b, i: (b, i, 0)),
            pl.BlockSpec((1, O_HALO, k), lambda b, i: (b, jnp.maximum(i * (rows // O_HALO) - 1, 0), 0)),
            pl.BlockSpec((k, d), lambda b, i: (0, 0), **resident),
        ]
        args += [o, o, w_out.astype(MXU_DTYPE)]
    in_specs += [
        pl.BlockSpec((1, 6, d), lambda b, i: (b, 0, 0)),
        pl.BlockSpec((1, d), lambda b, i: (0, 0)),
        pl.BlockSpec((d, 2 * d_ff), lambda b, i: (0, 0), **resident),
        pl.BlockSpec((CONV_W, 2 * d_ff), lambda b, i: (0, 0)),
        pl.BlockSpec((1, 2 * d_ff), lambda b, i: (0, 0)),
        pl.BlockSpec((d_ff, d), lambda b, i: (0, 0), **resident),
    ]
    args += [mod, gamma, w_up.astype(MXU_DTYPE), conv_w, conv_b[None, :], w_down.astype(MXU_DTYPE)]
    if final:
        in_specs.append(pl.BlockSpec((1, d), lambda b, i: (0, 0)))
        args.append(gamma_final)
    return pl.pallas_call(
        functools.partial(_ffn_kernel, final=final, mixer=mixer is not None),
        grid=(bsz, s // rows),
        in_specs=in_specs,
        out_specs=pl.BlockSpec((1, rows, d), lambda b, i: (b, i, 0)),
        out_shape=jax.ShapeDtypeStruct((bsz, s, d), f32),
        scratch_shapes=[
            pltpu.VMEM((rows + LEAD, d), MXU_DTYPE),
            pltpu.VMEM((rows + LEAD, 2 * FFN_TF), f32),
            pltpu.VMEM((rows + LEAD, 2 * FFN_TF), f32),
            pltpu.VMEM((rows, d_ff), MXU_DTYPE),
            pltpu.VMEM((rows, d), f32),
        ] + ([pltpu.VMEM((rows, d), f32)] if mixer is not None else []),
        compiler_params=pltpu.CompilerParams(
            dimension_semantics=("parallel", "parallel"), vmem_limit_bytes=VMEM_LIMIT),
        name="conv_ffn_final" if final else "conv_ffn",
    )(*args)


def _gla_kernel(x_ref, mod_ref, gam_ref, win_ref, wglr_ref, wgate_ref, bgate_ref, gnorm_ref, wout_ref,
                out_ref, h_ref, proj0_ref, proj1_ref, gk0_ref, gk1_ref, og0_ref, og1_ref, state_ref):
    @pl.when(pl.program_id(1) == 0)
    def _():
        state_ref[...] = jnp.zeros_like(state_ref)

    proj_refs, gk_refs, og_refs = (proj0_ref, proj1_ref), (gk0_ref, gk1_ref), (og0_ref, og1_ref)
    nsub = x_ref.shape[1] // GLA_TM
    nchunk = GLA_TM // B_CHUNK
    n_main = win_ref.shape[1]

    ri = lax.broadcasted_iota(jnp.int32, (B_CHUNK, B_CHUNK), 0)
    ci = lax.broadcasted_iota(jnp.int32, (B_CHUNK, B_CHUNK), 1)
    causal = ci <= ri
    tri = causal.astype(MXU_DTYPE)
    gnorm = gnorm_ref[...]
    qscale = B_DK ** -0.5
    r_off = 2 * B_QK + B_V
    heads = [slice(hd * B_DK, (hd + 1) * B_DK) for hd in range(B_HEADS)]

    def project_steps(k):
        rows = slice(k * GLA_TM, (k + 1) * GLA_TM)

        def normalize():
            h = _modnorm(x_ref[0, rows, :], gam_ref[...], mod_ref[0, 1:2, :], mod_ref[0, 0:1, :])
            h_ref[...] = h.astype(h_ref.dtype)

        def piece(cols):
            def run():
                proj_refs[k % 2][:, cols] = _mm(h_ref[...], win_ref[:, cols])
            return run

        def gate():
            glr = _mm(h_ref[...], wglr_ref[...]).astype(MXU_DTYPE)
            z = _mm(glr, wgate_ref[...]) + bgate_ref[...]
            gk_refs[k % 2][...] = (jnp.minimum(z, 0.0) - jnp.log1p(jnp.exp(-jnp.abs(z)))) * (1.0 / B_TAU)

        pieces = [piece(slice(c0, c0 + GLA_PROJ_COLS)) for c0 in range(0, n_main, GLA_PROJ_COLS)]
        return [normalize] + pieces + [gate]

    def local(k, c):
        proj_ref, gk_ref = proj_refs[k % 2], gk_refs[k % 2]
        rows = slice(c * B_CHUNK, (c + 1) * B_CHUNK)
        gk = gk_ref[rows, :]
        g_hi = gk.astype(MXU_DTYPE)
        rem = gk - g_hi.astype(f32)
        g_mid = rem.astype(MXU_DTYPE)
        g_lo = (rem - g_mid.astype(f32)).astype(MXU_DTYPE)
        bcum = _mm(tri, g_hi) + _mm(tri, g_mid) + _mm(tri, g_lo)
        blast = bcum[B_CHUNK - 1:B_CHUNK, :]
        q_t = ((proj_ref[rows, 0:B_QK] * qscale) * jnp.exp(bcum)).astype(MXU_DTYPE)
        kk = proj_ref[rows, B_QK:2 * B_QK]
        k_t = (kk * jnp.exp(-bcum)).astype(MXU_DTYPE)
        k_d = (kk * jnp.exp(blast - bcum)).astype(MXU_DTYPE)
        v = [proj_ref[rows, 2 * B_QK + hd * B_DV:2 * B_QK + (hd + 1) * B_DV].astype(MXU_DTYPE)
             for hd in range(B_HEADS)]
        a = [jnp.where(causal, _mm_nt(q_t[:, ks], k_t[:, ks]), 0.0).astype(MXU_DTYPE) for ks in heads]
        kv = [_mm_tn(k_d[:, ks], v[hd]) for hd, ks in enumerate(heads)]
        o_intra = [_mm(a[hd], v[hd]) for hd in range(B_HEADS)]
        return q_t, blast, kv, o_intra

    def recur(k, c, loc, state):
        q_t, blast, kv, o_intra = loc
        rows = slice(c * B_CHUNK, (c + 1) * B_CHUNK)
        new_state = []
        for hd, ks in enumerate(heads):
            o = o_intra[hd] + _mm(q_t[:, ks], state[hd].astype(MXU_DTYPE))
            decay = jnp.exp(jnp.broadcast_to(blast[:, ks], (B_DK, B_DK)).T)
            new_state.append(jnp.concatenate([decay] * (B_DV // B_DK), axis=1) * state[hd] + kv[hd])
            o = o * lax.rsqrt(jnp.mean(o * o, axis=-1, keepdims=True) + EPS) * gnorm
            r = proj_refs[k % 2][rows, r_off + hd * B_DV:r_off + (hd + 1) * B_DV]
            og_refs[k % 2][rows, hd * B_DV:(hd + 1) * B_DV] = (o * _silu(r)).astype(og_refs[k % 2].dtype)
        return new_state

    for step in project_steps(0):
        step()
    state = [state_ref[hd] for hd in range(B_HEADS)]
    for k in range(nsub):
        ahead = project_steps(k + 1) if k + 1 < nsub else []
        loc = local(k, 0)
        for c in range(nchunk):
            nxt = local(k, c + 1) if c + 1 < nchunk else None
            state = recur(k, c, loc, state)
            loc = nxt
            take = -(-len(ahead) // (nchunk - c))
            for step in ahead[:take]:
                step()
            ahead = ahead[take:]
        rows = slice(k * GLA_TM, (k + 1) * GLA_TM)
        out_ref[0, rows, :] = x_ref[0, rows, :] + mod_ref[0, 2:3, :] * _mm(og_refs[k % 2][...], wout_ref[...])
    for hd in range(B_HEADS):
        state_ref[hd] = state[hd]


def _gla_layer(x, mod, gamma, w_in, w_gate, b_gate, g_norm, w_out):
    bsz, s, d = x.shape
    glr0 = 2 * B_QK + B_V
    w_main = jnp.concatenate([w_in[:, :glr0], w_in[:, glr0 + B_GATE_RANK:]], axis=1).astype(MXU_DTYPE)
    w_glr = jnp.pad(w_in[:, glr0:glr0 + B_GATE_RANK], ((0, 0), (0, LANES - B_GATE_RANK))).astype(MXU_DTYPE)
    w_gate_p = jnp.pad(w_gate, ((0, LANES - B_GATE_RANK), (0, 0))).astype(MXU_DTYPE)
    n_main = w_main.shape[1]
    rows = GLA_SUB * GLA_TM
    const = lambda b, i: (0, 0)
    resident = dict(pipeline_mode=pl.Buffered(1))
    return pl.pallas_call(
        _gla_kernel,
        grid=(bsz, s // rows),
        in_specs=[
            pl.BlockSpec((1, rows, d), lambda b, i: (b, i, 0)),
            pl.BlockSpec((1, 6, d), lambda b, i: (b, 0, 0)),
            pl.BlockSpec((1, d), const),
            pl.BlockSpec((d, n_main), const, **resident),
            pl.BlockSpec((d, LANES), const),
            pl.BlockSpec((LANES, B_QK), const),
            pl.BlockSpec((1, B_QK), const),
            pl.BlockSpec((1, B_DV), const),
            pl.BlockSpec((B_V, d), const, **resident),
        ],
        out_specs=pl.BlockSpec((1, rows, d), lambda b, i: (b, i, 0)),
        out_shape=jax.ShapeDtypeStruct((bsz, s, d), f32),
        scratch_shapes=[
            pltpu.VMEM((GLA_TM, d), MXU_DTYPE),
            pltpu.VMEM((GLA_TM, n_main), f32),
            pltpu.VMEM((GLA_TM, n_main), f32),
            pltpu.VMEM((GLA_TM, B_QK), f32),
            pltpu.VMEM((GLA_TM, B_QK), f32),
            pltpu.VMEM((GLA_TM, B_V), MXU_DTYPE),
            pltpu.VMEM((GLA_TM, B_V), MXU_DTYPE),
            pltpu.VMEM((B_HEADS, B_DK, B_DV), f32),
        ],
        compiler_params=pltpu.CompilerParams(
            dimension_semantics=("parallel", "arbitrary"), vmem_limit_bytes=VMEM_LIMIT),
        name="gla_layer",
    )(x, mod, gamma, w_main, w_glr, w_gate_p, b_gate[None, :], g_norm[None, :], w_out.astype(MXU_DTYPE))


def _qkv_weight(w_in):
    width = A_HEADS * A_HEAD_DIM
    col_scale = jnp.tile(jnp.repeat(jnp.array([A_HEAD_DIM ** -0.5 * LOG2E, 1.0, 1.0], f32), width), A_GROUPS)
    return (w_in * col_scale[None, :]).astype(MXU_DTYPE)


def kernel(x, c, w_in_a, w_out_a, rel_bias, w_in_b, w_gate_b, b_gate_b, gnorm_b, w_out_b, norm_mix, norm_ffn, w_ada, b_ada, w_up, conv_w, conv_b, w_down, norm_final):
    depth = w_ada.shape[0]
    mod = _adaln(c, w_ada, b_ada)
    bias = _rel_bias(rel_bias)
    for i in range(depth):
        gam_mix = norm_mix[i][None, :]
        j = i // 2
        mixer = None
        if i % 2 == 0:
            qkv = _qkv_proj(x, mod[i], gam_mix, _qkv_weight(w_in_a[j]))
            mixer = (_attention(qkv, bias), w_out_a[j])
        else:
            x = _gla_layer(x, mod[i], gam_mix, w_in_b[j], w_gate_b[j], b_gate_b[j], gnorm_b[j], w_out_b[j])
        last = i == depth - 1
        x = _conv_ffn(x, mod[i], norm_ffn[i][None, :], w_up[i], conv_w[i], conv_b[i], w_down[i],
                      gamma_final=norm_final[None, :] if last else None, mixer=mixer)
    return x
```

```python
import functools
import math

import jax
import jax.numpy as jnp
from jax import lax
from jax.experimental import pallas as pl
from jax.experimental.pallas import tpu as pltpu

A_CONFIGS = ((128, 1), (512, 4), (2048, 16))
A_GROUPS = len(A_CONFIGS)
A_HEADS = 16
A_HEAD_DIM = 64
N_BUCKETS = 32
MAX_DISTANCE = 2048
B_HEADS = 4
B_DK = 128
B_DV = 256
B_QK = B_HEADS * B_DK
B_V = B_HEADS * B_DV
B_GATE_RANK = 16
B_TAU = 16.0
B_CHUNK = 64
CONV_W = 3
EPS = 1e-6
NEG_INF = -1e30
LOG2E = math.log2(math.e)

LANES = 128
SUBLANES = 8
MXU_DTYPE = jnp.bfloat16
ATT_BLK = 128
ATT_TILE = 2048
ATT_UNITS = ATT_TILE // ATT_BLK
ATT_PIPELINE = 3
PAIR_W = 3 * LANES
N_PAIRS = A_HEADS // 2
QKV_ROWS = 1024
PROBE_ROWS = 2 * SUBLANES
FFN_TM = 512
FFN_TF = 256
FFN_ROWS = 256
HALO = SUBLANES
LEAD = (CONV_W - 1) * SUBLANES
O_HALO = 2 * SUBLANES
GLA_TM = 512
GLA_SUB = 2
GLA_PROJ_COLS = 256
VMEM_LIMIT = 56 * 1024 * 1024

f32 = jnp.float32


def _mm(a, b):
    return jnp.dot(a, b, preferred_element_type=f32)


def _mm_nt(a, b):
    return lax.dot_general(a, b, (((1,), (1,)), ((), ())), preferred_element_type=f32)


def _mm_tn(a, b):
    return lax.dot_general(a, b, (((0,), (0,)), ((), ())), preferred_element_type=f32)


def _modnorm(x, gamma, scale, shift):
    ms = jnp.mean(x * x, axis=-1, keepdims=True)
    y = x * lax.rsqrt(ms + EPS) * gamma
    return y * (1.0 + scale) + shift


def _silu(x):
    return x * (1.0 / (1.0 + jnp.exp(-x)))


def _regroup(t, groups):
    rows, d = t.shape
    return t.reshape(groups, rows // groups, d).swapaxes(0, 1).reshape(rows, d)


def _adaln_kernel(c_ref, w_ref, b_ref, o_ref):
    s = _silu(c_ref[...]).astype(MXU_DTYPE)
    o_ref[0] = _mm(s, w_ref[0].astype(MXU_DTYPE)) + b_ref[0]


def _adaln(c, w_ada, b_ada):
    depth, d, n = w_ada.shape
    bsz = c.shape[0]
    rows = 8 * pl.cdiv(bsz, 8)
    c_pad = jnp.pad(c, ((0, rows - bsz), (0, 0)))
    out = pl.pallas_call(
        _adaln_kernel,
        grid=(depth, n // d),
        in_specs=[
            pl.BlockSpec((rows, d), lambda l, j: (0, 0)),
            pl.BlockSpec((1, d, d), lambda l, j: (l, 0, j)),
            pl.BlockSpec((1, 1, d), lambda l, j: (l, 0, j)),
        ],
        out_specs=pl.BlockSpec((1, rows, d), lambda l, j: (l, 0, j)),
        out_shape=jax.ShapeDtypeStruct((depth, rows, n), f32),
        compiler_params=pltpu.CompilerParams(
            dimension_semantics=("parallel", "parallel"), vmem_limit_bytes=VMEM_LIMIT),
        name="adaln",
    )(c_pad, w_ada, b_ada.reshape(depth, 1, n))
    return out[:, :bsz].reshape(depth, bsz, n // d, d)


def _qkv_kernel(x_ref, mod_ref, gam_ref, wq_ref, wk_ref, wv_ref, o_ref, ha_ref, hb_ref, sa_ref, sb_ref, *,
                dilations, tiles_per_group):
    j = pl.program_id(2)
    g = j // tiles_per_group
    t = j % tiles_per_group
    nlb = sa_ref.shape[0]
    chunks_per_step = ATT_UNITS // tiles_per_group
    h_refs, stage_refs = [ha_ref, hb_ref], [sa_ref, sb_ref]

    @pl.when(j == 0)
    def _():
        for c in range(ATT_UNITS):
            rows = slice(c * ATT_BLK, (c + 1) * ATT_BLK)
            hn = _modnorm(x_ref[0, rows, :], gam_ref[...], mod_ref[0, 1:2, :], mod_ref[0, 0:1, :])
            ha_ref[rows, :] = hn.astype(ha_ref.dtype)
            for lb in range(nlb):
                sa_ref[lb, rows, :] = hn[:, lb * LANES:(lb + 1) * LANES]

    def deinterleave_chunk(gi, cc):
        d, dp = dilations[gi], dilations[gi - 1]
        rel = d // dp
        per = ATT_UNITS // d
        c = t * chunks_per_step + cc
        r, q = c // per, c % per
        start = (r % dp) * (ATT_TILE // dp) + q * (ATT_BLK * rel) + r // dp
        rows = pl.ds(start, ATT_BLK, stride=rel)
        r0 = pl.multiple_of(c * ATT_BLK, ATT_BLK)
        parts = [stage_refs[(gi - 1) % 2][lb, rows, :] for lb in range(nlb)]
        if gi + 1 < len(dilations):
            for lb in range(nlb):
                stage_refs[gi % 2][lb, pl.ds(r0, ATT_BLK), :] = parts[lb]
        dst_ref = h_refs[gi % 2]
        dst_ref[pl.ds(r0, ATT_BLK), :] = jnp.concatenate(parts, axis=1).astype(dst_ref.dtype)
        probe = dst_ref[pl.ds(r0, PROBE_ROWS), 0:LANES]
        return (probe != probe) & (probe == probe)

    def project_chunk(src_ref, rc, never):
        rows = slice(rc * QKV_ROWS, (rc + 1) * QKV_ROWS)
        lhs = src_ref[rows, :]
        for k, w_ref in enumerate((wq_ref, wk_ref, wv_ref)):
            res = _mm(lhs, w_ref[...]).astype(o_ref.dtype)
            lanes = slice(k * LANES, (k + 1) * LANES)
            o_ref[0, 0, 1, rows, lanes] = res[:, LANES:]
            if never is not None and k == 0:
                top = rc * QKV_ROWS + PROBE_ROWS
                o_ref[0, 0, 0, rc * QKV_ROWS:top, lanes] = jnp.where(
                    never, jnp.zeros_like(res[:PROBE_ROWS, :LANES]), res[:PROBE_ROWS, :LANES])
                o_ref[0, 0, 0, top:(rc + 1) * QKV_ROWS, lanes] = res[PROBE_ROWS:, :LANES]
            else:
                o_ref[0, 0, 0, rows, lanes] = res[:, :LANES]

    assert dilations[0] == 1
    assert all(d % dp == 0 for dp, d in zip(dilations, dilations[1:]))
    n_mm = ATT_TILE // QKV_ROWS
    for gi in range(len(dilations)):
        @pl.when(g == gi)
        def _(gi=gi):
            slots = max(n_mm - 1, 1)
            shares = [list(range(chunks_per_step))[sl::slots] for sl in range(slots)] + [[]]
            for rc in range(n_mm):
                never = None
                if gi + 1 < len(dilations):
                    for cc in shares[rc]:
                        probe = deinterleave_chunk(gi + 1, cc)
                        never = probe if never is None else never | probe
                project_chunk(h_refs[gi % 2], rc, never)


def _qkv_proj(x, mod, gamma, w):
    bsz, s, d = x.shape
    dilations = tuple(dl for _, dl in A_CONFIGS)
    tn = 2 * LANES
    tiles_per_group = A_HEADS * A_HEAD_DIM // tn
    kern = functools.partial(_qkv_kernel, dilations=dilations, tiles_per_group=tiles_per_group)

    def w_spec(k):
        return pl.BlockSpec(
            (d, tn), lambda b, i, j: (0, (j // tiles_per_group * 3 + k) * tiles_per_group + j % tiles_per_group))

    return pl.pallas_call(
        kern,
        grid=(bsz, s // ATT_TILE, A_GROUPS * tiles_per_group),
        in_specs=[
            pl.BlockSpec((1, ATT_TILE, d), lambda b, i, j: (b, i, 0)),
            pl.BlockSpec((1, 6, d), lambda b, i, j: (b, 0, 0)),
            pl.BlockSpec((1, d), lambda b, i, j: (0, 0)),
            w_spec(0), w_spec(1), w_spec(2),
        ],
        out_specs=pl.BlockSpec(
            (1, 1, 2, ATT_TILE, PAIR_W),
            lambda b, i, j: (j // tiles_per_group, b, j % tiles_per_group, i, 0)),
        out_shape=jax.ShapeDtypeStruct((A_GROUPS, bsz, N_PAIRS, s, PAIR_W), MXU_DTYPE),
        scratch_shapes=[pltpu.VMEM((ATT_TILE, d), MXU_DTYPE),
                        pltpu.VMEM((ATT_TILE, d), MXU_DTYPE),
                        pltpu.VMEM((d // LANES, ATT_TILE, LANES), f32),
                        pltpu.VMEM((d // LANES, ATT_TILE, LANES), f32)],
        compiler_params=pltpu.CompilerParams(
            dimension_semantics=("parallel", "parallel", "arbitrary"),
            vmem_limit_bytes=VMEM_LIMIT),
        name="qkv_proj",
    )(x, mod, gamma, w, w, w)


def _bias_kernel(tab_ref, bkt_ref, o_ref):
    bkt = bkt_ref[0]
    hits = [(bkt >= k) & (bkt < k + 1) for k in range(N_BUCKETS)]
    for h in range(A_HEADS):
        col = pl.program_id(0) * A_HEADS + h
        acc = jnp.full(bkt.shape, NEG_INF, f32)
        for k in range(N_BUCKETS):
            acc = jnp.where(hits[k], tab_ref[k, col] * LOG2E, acc)
        o_ref[0, h] = acc


def _t5_bucket(dist):
    max_exact = N_BUCKETS // 2
    n = jnp.maximum(dist, max_exact).astype(f32)
    large = max_exact + (jnp.log(n / max_exact) / math.log(MAX_DISTANCE / max_exact)
                         * (N_BUCKETS - max_exact)).astype(jnp.int32)
    large = jnp.minimum(large, N_BUCKETS - 1)
    return jnp.where(dist < max_exact, dist, large)


def _rel_bias(rel_bias):
    qi = jnp.arange(ATT_BLK)[:, None]
    ki = jnp.arange(2 * ATT_BLK)[None, :]
    steps = qi + ATT_BLK - ki
    band = (steps >= 0) & (steps <= ATT_BLK)
    bucket = jnp.stack([
        jnp.where(band, _t5_bucket(jnp.clip(steps, 0, ATT_BLK) * dl), -1)
        for _, dl in A_CONFIGS]).astype(jnp.int32)
    return pl.pallas_call(
        _bias_kernel,
        grid=(A_GROUPS,),
        in_specs=[
            pl.BlockSpec(memory_space=pltpu.SMEM),
            pl.BlockSpec((1, ATT_BLK, 2 * ATT_BLK), lambda g: (g, 0, 0)),
        ],
        out_specs=pl.BlockSpec((1, A_HEADS, ATT_BLK, 2 * ATT_BLK), lambda g: (g, 0, 0, 0)),
        out_shape=jax.ShapeDtypeStruct((A_GROUPS, A_HEADS, ATT_BLK, 2 * ATT_BLK), f32),
        name="rel_bias",
    )(rel_bias, bucket)


def _attn_kernel(qkv_ref, bias_ref, out_ref, prev0, prev1, prev2, o1, l1, n1, o2, l2, n2, biasp):
    first = pl.program_id(2) == 0

    @pl.when(first)
    def _():
        prev0[...] = jnp.zeros_like(prev0)
        prev1[...] = jnp.zeros_like(prev1)
        prev2[...] = jnp.zeros_like(prev2)

    for g in range(A_GROUPS):
        for hh in range(2):
            biasp[g, hh, :, 0:ATT_BLK] = jnp.where(first, NEG_INF, bias_ref[g, hh, :, 0:ATT_BLK])
            biasp[g, hh, :, ATT_BLK:2 * ATT_BLK] = bias_ref[g, hh, :, ATT_BLK:2 * ATT_BLK]

    lane = lax.broadcasted_iota(jnp.int32, (1, LANES), 1)
    lo = lane < A_HEAD_DIM
    qmask = (jnp.where(lo, 1.0, 0.0).astype(MXU_DTYPE), jnp.where(lo, 0.0, 1.0).astype(MXU_DTYPE))

    def keys_values(g, rc, prev_ref, rp, lanes):
        if prev_ref is None:
            return qkv_ref[g, 0, 0, rp:rp + 2 * ATT_BLK, lanes]
        off = lanes.start - LANES
        return jnp.concatenate([prev_ref[rp:rp + ATT_BLK, off:off + LANES],
                                qkv_ref[g, 0, 0, rc:rc + ATT_BLK, lanes]], axis=0)

    def scores(g, rc, prev_ref, rp):
        q = qkv_ref[g, 0, 0, rc:rc + ATT_BLK, 0:LANES]
        k = keys_values(g, rc, prev_ref, rp, slice(LANES, 2 * LANES))
        bias = bias_ref if prev_ref is None else biasp
        return [_mm_nt(q * qmask[hh], k) + bias[g, hh] for hh in range(2)]

    def attend(g, rc, prev_ref, rp, s):
        v = keys_values(g, rc, prev_ref, rp, slice(2 * LANES, 3 * LANES))
        accs, dens, ms = [], [], []
        for sh in s:
            m = jnp.max(sh, axis=-1, keepdims=True)
            p = jnp.exp2(sh - m)
            dens.append(jnp.sum(p, axis=-1, keepdims=True))
            accs.append(_mm(p.astype(MXU_DTYPE), v))
            ms.append(m)
        acc = jnp.where(lo, accs[0], accs[1])
        return (acc, jnp.broadcast_to(jnp.where(lo, ms[0], ms[1]), acc.shape),
                jnp.broadcast_to(jnp.where(lo, dens[0], dens[1]), acc.shape))

    def store(a_ref, m_ref, d_ref, rows):
        def post(acc, m, den):
            a_ref[rows, :] = acc
            m_ref[rows, :] = m
            d_ref[rows, :] = den
        return post

    def merge(rows):
        def post(acc0, m0, den0):
            m1, m2 = l1[rows, :], l2[rows, :]
            mx = jnp.maximum(jnp.maximum(m0, m1), m2)
            e0, e1, e2 = jnp.exp2(m0 - mx), jnp.exp2(m1 - mx), jnp.exp2(m2 - mx)
            num = e0 * acc0 + e1 * o1[rows, :] + e2 * o2[rows, :]
            den = e0 * den0 + e1 * n1[rows, :] + e2 * n2[rows, :]
            out_ref[0, rows, :] = (num * (1.0 / den)).astype(out_ref.dtype)
        return post

    units = []
    d2 = A_CONFIGS[2][1]
    for u in range(ATT_UNITS):
        units.append((2, u * ATT_BLK, prev2, u * ATT_BLK, store(o2, l2, n2, pl.ds(u, ATT_BLK, stride=d2))))
    d1 = A_CONFIGS[1][1]
    per = ATT_UNITS // d1
    for u in range(ATT_UNITS):
        r, q = divmod(u, per)
        rc = u * ATT_BLK
        post = store(o1, l1, n1, pl.ds(q * ATT_BLK * d1 + r, ATT_BLK, stride=d1))
        if q == 0:
            units.append((1, rc, prev1, (r * per + per - 1) * ATT_BLK, post))
        else:
            units.append((1, rc, None, rc - ATT_BLK, post))
    for u in range(ATT_UNITS):
        rc = u * ATT_BLK
        post = merge(slice(rc, rc + ATT_BLK))
        units.append((0, rc, prev0, 0, post) if u == 0 else (0, rc, None, rc - ATT_BLK, post))

    pending = [scores(*un[:4]) for un in units[:ATT_PIPELINE]]
    for idx, un in enumerate(units):
        if idx + ATT_PIPELINE < len(units):
            pending.append(scores(*units[idx + ATT_PIPELINE][:4]))
        un[4](*attend(*un[:4], pending.pop(0)))

    prev0[...] = qkv_ref[0, 0, 0, ATT_TILE - ATT_BLK:ATT_TILE, LANES:3 * LANES]
    prev1[...] = qkv_ref[1, 0, 0, :, LANES:3 * LANES]
    prev2[...] = qkv_ref[2, 0, 0, :, LANES:3 * LANES]


def _attention(qkv, bias):
    _, bsz, _, s, _ = qkv.shape
    return pl.pallas_call(
        _attn_kernel,
        grid=(bsz, N_PAIRS, s // ATT_TILE),
        in_specs=[
            pl.BlockSpec((A_GROUPS, 1, 1, ATT_TILE, PAIR_W), lambda b, p, i: (0, b, p, i, 0)),
            pl.BlockSpec((A_GROUPS, 2, ATT_BLK, 2 * ATT_BLK), lambda b, p, i: (0, p, 0, 0)),
        ],
        out_specs=pl.BlockSpec((1, ATT_TILE, LANES), lambda b, p, i: (b, i, p)),
        out_shape=jax.ShapeDtypeStruct((bsz, s, A_HEADS * A_HEAD_DIM), MXU_DTYPE),
        scratch_shapes=[
            pltpu.VMEM((ATT_BLK, 2 * LANES), MXU_DTYPE),
            pltpu.VMEM((ATT_TILE, 2 * LANES), MXU_DTYPE),
            pltpu.VMEM((ATT_TILE, 2 * LANES), MXU_DTYPE),
            pltpu.VMEM((ATT_TILE, LANES), f32),
            pltpu.VMEM((ATT_TILE, LANES), f32),
            pltpu.VMEM((ATT_TILE, LANES), f32),
            pltpu.VMEM((ATT_TILE, LANES), f32),
            pltpu.VMEM((ATT_TILE, LANES), f32),
            pltpu.VMEM((ATT_TILE, LANES), f32),
            pltpu.VMEM((A_GROUPS, 2, ATT_BLK, 2 * ATT_BLK), f32),
        ],
        compiler_params=pltpu.CompilerParams(
            dimension_semantics=("parallel", "parallel", "arbitrary"),
            vmem_limit_bytes=VMEM_LIMIT),
        name="dilated_attn",
    )(qkv, bias)


def _ffn_kernel(*refs, final, mixer):
    refs = list(refs)
    x_ref, halo_ref = refs[:2]
    del refs[:2]
    if mixer:
        o_ref, ohalo_ref, wout_ref = refs[:3]
        del refs[:3]
    mod_ref, gam_ref, wup_ref, cw_ref, cb_ref, wdn_ref = refs[:6]
    del refs[:6]
    if final:
        gfin_ref = refs.pop(0)
    out_ref, h_ref, u0_ref, u1_ref, act_ref = refs[:5]
    if mixer:
        x1_ref = refs[5]
    u_refs = (u0_ref, u1_ref)
    gam, scale, shift = gam_ref[...], mod_ref[0, 4:5, :], mod_ref[0, 3:4, :]
    d_ff = wdn_ref.shape[0]
    nchunk = d_ff // FFN_TF
    tm = x_ref.shape[1]

    if mixer:
        x1 = x_ref[0] + mod_ref[0, 2:3, :] * _mm(o_ref[0], wout_ref[...])
        x1_ref[...] = x1
        halo = halo_ref[0] + mod_ref[0, 2:3, :] * _mm(ohalo_ref[0], wout_ref[...])[O_HALO - HALO:, :]
    else:
        x1, halo = x_ref[0], halo_ref[0]
    halo = _modnorm(halo, gam, scale, shift)
    halo = jnp.where(pl.program_id(1) == 0, 0.0, halo)
    hp = _regroup(_modnorm(x1, gam, scale, shift), SUBLANES)
    for back in range(1, CONV_W):
        last = hp[tm - back * SUBLANES:tm - (back - 1) * SUBLANES, :]
        lead = jnp.concatenate([halo[HALO - back:HALO - back + 1, :], last[:SUBLANES - 1, :]], axis=0)
        h_ref[(CONV_W - 1 - back) * SUBLANES:(CONV_W - back) * SUBLANES, :] = lead.astype(h_ref.dtype)
    h_ref[LEAD:LEAD + tm, :] = hp.astype(h_ref.dtype)

    def halves(c):
        return [slice(half * d_ff + c * FFN_TF, half * d_ff + (c + 1) * FFN_TF) for half in range(2)]

    def up(c):
        for half, cols in enumerate(halves(c)):
            u_refs[c % 2][:, half * FFN_TF:(half + 1) * FFN_TF] = _mm(h_ref[...], wup_ref[:, cols])

    def conv_act(c):
        u_ref = u_refs[c % 2]
        for rb in range(tm // FFN_ROWS):
            ab = []
            for half, cols in enumerate(halves(c)):
                v = cb_ref[:, cols]
                for t in range(CONV_W):
                    off = rb * FFN_ROWS + t * SUBLANES
                    v = v + cw_ref[t:t + 1, cols] * u_ref[off:off + FFN_ROWS, half * FFN_TF:(half + 1) * FFN_TF]
                ab.append(v)
            act_ref[rb * FFN_ROWS:(rb + 1) * FFN_ROWS, c * FFN_TF:(c + 1) * FFN_TF] = (
                _silu(ab[0]) * ab[1]).astype(act_ref.dtype)

    up(0)
    for c in range(nchunk):
        if c + 1 < nchunk:
            up(c + 1)
        conv_act(c)

    ffn = _regroup(_mm(act_ref[...], wdn_ref[...]), tm // SUBLANES)
    y = (x1_ref[...] if mixer else x_ref[0]) + mod_ref[0, 5:6, :] * ffn
    if final:
        ms = jnp.mean(y * y, axis=-1, keepdims=True)
        y = y * lax.rsqrt(ms + EPS) * gfin_ref[...]
    out_ref[0] = y


def _conv_ffn(x, mod, gamma, w_up, conv_w, conv_b, w_down, gamma_final=None, mixer=None):
    bsz, s, d = x.shape
    d_ff = w_down.shape[0]
    final = gamma_final is not None
    rows = FFN_TM if mixer is not None else 2 * FFN_TM
    tiles_per_halo = rows // HALO
    resident = dict(pipeline_mode=pl.Buffered(1))
    in_specs = [
        pl.BlockSpec((1, rows, d), lambda b, i: (b, i, 0)),
        pl.BlockSpec((1, HALO, d), lambda b, i: (b, jnp.maximum(i * tiles_per_halo - 1, 0), 0)),
    ]
    args = [x, x]
    if mixer is not None:
        o, w_out = mixer
        k = o.shape[-1]
        in_specs += [
            pl.BlockSpec((1, rows, k), lambda b, i: (b, i, 0)),
            pl.BlockSpec((1, O_HALO, k), lambda b, i: (b, jnp.maximum(i * (rows // O_HALO) - 1, 0), 0)),
            pl.BlockSpec((k, d), lambda b, i: (0, 0), **resident),
        ]
        args += [o, o, w_out.astype(MXU_DTYPE)]
    in_specs += [
        pl.BlockSpec((1, 6, d), lambda b, i: (b, 0, 0)),
        pl.BlockSpec((1, d), lambda b, i: (0, 0)),
        pl.BlockSpec((d, 2 * d_ff), lambda b, i: (0, 0), **resident),
        pl.BlockSpec((CONV_W, 2 * d_ff), lambda b, i: (0, 0)),
        pl.BlockSpec((1, 2 * d_ff), lambda b, i: (0, 0)),
        pl.BlockSpec((d_ff, d), lambda b, i: (0, 0), **resident),
    ]
    args += [mod, gamma, w_up.astype(MXU_DTYPE), conv_w, conv_b[None, :], w_down.astype(MXU_DTYPE)]
    if final:
        in_specs.append(pl.BlockSpec((1, d), lambda b, i: (0, 0)))
        args.append(gamma_final)
    return pl.pallas_call(
        functools.partial(_ffn_kernel, final=final, mixer=mixer is not None),
        grid=(bsz, s // rows),
        in_specs=in_specs,
        out_specs=pl.BlockSpec((1, rows, d), lambda b, i: (b, i, 0)),
        out_shape=jax.ShapeDtypeStruct((bsz, s, d), f32),
        scratch_shapes=[
            pltpu.VMEM((rows + LEAD, d), MXU_DTYPE),
            pltpu.VMEM((rows + LEAD, 2 * FFN_TF), f32),
            pltpu.VMEM((rows + LEAD, 2 * FFN_TF), f32),
            pltpu.VMEM((rows, d_ff), MXU_DTYPE),
        ] + ([pltpu.VMEM((rows, d), f32)] if mixer is not None else []),
        compiler_params=pltpu.CompilerParams(
            dimension_semantics=("parallel", "parallel"), vmem_limit_bytes=VMEM_LIMIT),
        name="conv_ffn_final" if final else "conv_ffn",
    )(*args)


def _gla_kernel(x_ref, mod_ref, gam_ref, win_ref, wglr_ref, wgate_ref, bgate_ref, gnorm_ref, wout_ref,
                out_ref, h_ref, proj0_ref, proj1_ref, gk0_ref, gk1_ref, og0_ref, og1_ref, state_ref):
    @pl.when(pl.program_id(1) == 0)
    def _():
        state_ref[...] = jnp.zeros_like(state_ref)

    proj_refs, gk_refs, og_refs = (proj0_ref, proj1_ref), (gk0_ref, gk1_ref), (og0_ref, og1_ref)
    nsub = x_ref.shape[1] // GLA_TM
    nchunk = GLA_TM // B_CHUNK
    n_main = win_ref.shape[1]

    ri = lax.broadcasted_iota(jnp.int32, (B_CHUNK, B_CHUNK), 0)
    ci = lax.broadcasted_iota(jnp.int32, (B_CHUNK, B_CHUNK), 1)
    causal = ci <= ri
    tri = causal.astype(MXU_DTYPE)
    gnorm = gnorm_ref[...]
    qscale = B_DK ** -0.5
    r_off = 2 * B_QK + B_V
    heads = [slice(hd * B_DK, (hd + 1) * B_DK) for hd in range(B_HEADS)]

    def project_steps(k):
        rows = slice(k * GLA_TM, (k + 1) * GLA_TM)

        def normalize():
            h = _modnorm(x_ref[0, rows, :], gam_ref[...], mod_ref[0, 1:2, :], mod_ref[0, 0:1, :])
            h_ref[...] = h.astype(h_ref.dtype)

        def piece(cols):
            def run():
                proj_refs[k % 2][:, cols] = _mm(h_ref[...], win_ref[:, cols])
            return run

        def gate():
            glr = _mm(h_ref[...], wglr_ref[...]).astype(MXU_DTYPE)
            z = _mm(glr, wgate_ref[...]) + bgate_ref[...]
            gk_refs[k % 2][...] = (jnp.minimum(z, 0.0) - jnp.log1p(jnp.exp(-jnp.abs(z)))) * (1.0 / B_TAU)

        pieces = [piece(slice(c0, c0 + GLA_PROJ_COLS)) for c0 in range(0, n_main, GLA_PROJ_COLS)]
        return [normalize] + pieces + [gate]

    def local(k, c):
        proj_ref, gk_ref = proj_refs[k % 2], gk_refs[k % 2]
        rows = slice(c * B_CHUNK, (c + 1) * B_CHUNK)
        gk = gk_ref[rows, :]
        g_hi = gk.astype(MXU_DTYPE)
        rem = gk - g_hi.astype(f32)
        g_mid = rem.astype(MXU_DTYPE)
        g_lo = (rem - g_mid.astype(f32)).astype(MXU_DTYPE)
        bcum = _mm(tri, g_hi) + _mm(tri, g_mid) + _mm(tri, g_lo)
        blast = bcum[B_CHUNK - 1:B_CHUNK, :]
        q_t = ((proj_ref[rows, 0:B_QK] * qscale) * jnp.exp(bcum)).astype(MXU_DTYPE)
        kk = proj_ref[rows, B_QK:2 * B_QK]
        k_t = (kk * jnp.exp(-bcum)).astype(MXU_DTYPE)
        k_d = (kk * jnp.exp(blast - bcum)).astype(MXU_DTYPE)
        v = [proj_ref[rows, 2 * B_QK + hd * B_DV:2 * B_QK + (hd + 1) * B_DV].astype(MXU_DTYPE)
             for hd in range(B_HEADS)]
        a = [jnp.where(causal, _mm_nt(q_t[:, ks], k_t[:, ks]), 0.0).astype(MXU_DTYPE) for ks in heads]
        kv = [_mm_tn(k_d[:, ks], v[hd]) for hd, ks in enumerate(heads)]
        o_intra = [_mm(a[hd], v[hd]) for hd in range(B_HEADS)]
        return q_t, blast, kv, o_intra

    def recur(k, c, loc, state):
        q_t, blast, kv, o_intra = loc
        rows = slice(c * B_CHUNK, (c + 1) * B_CHUNK)
        new_state = []
        for hd, ks in enumerate(heads):
            o = o_intra[hd] + _mm(q_t[:, ks], state[hd].astype(MXU_DTYPE))
            decay = jnp.exp(jnp.broadcast_to(blast[:, ks], (B_DK, B_DK)).T)
            new_state.append(jnp.concatenate([decay] * (B_DV // B_DK), axis=1) * state[hd] + kv[hd])
            o = o * lax.rsqrt(jnp.mean(o * o, axis=-1, keepdims=True) + EPS) * gnorm
            r = proj_refs[k % 2][rows, r_off + hd * B_DV:r_off + (hd + 1) * B_DV]
            og_refs[k % 2][rows, hd * B_DV:(hd + 1) * B_DV] = (o * _silu(r)).astype(og_refs[k % 2].dtype)
        return new_state

    for step in project_steps(0):
        step()
    state = [state_ref[hd] for hd in range(B_HEADS)]
    for k in range(nsub):
        ahead = project_steps(k + 1) if k + 1 < nsub else []
        loc = local(k, 0)
        for c in range(nchunk):
            nxt = local(k, c + 1) if c + 1 < nchunk else None
            state = recur(k, c, loc, state)
            loc = nxt
            take = -(-len(ahead) // (nchunk - c))
            for step in ahead[:take]:
                step()
            ahead = ahead[take:]
        rows = slice(k * GLA_TM, (k + 1) * GLA_TM)
        out_ref[0, rows, :] = x_ref[0, rows, :] + mod_ref[0, 2:3, :] * _mm(og_refs[k % 2][...], wout_ref[...])
    for hd in range(B_HEADS):
        state_ref[hd] = state[hd]


def _gla_layer(x, mod, gamma, w_in, w_gate, b_gate, g_norm, w_out):
    bsz, s, d = x.shape
    glr0 = 2 * B_QK + B_V
    w_main = jnp.concatenate([w_in[:, :glr0], w_in[:, glr0 + B_GATE_RANK:]], axis=1).astype(MXU_DTYPE)
    w_glr = jnp.pad(w_in[:, glr0:glr0 + B_GATE_RANK], ((0, 0), (0, LANES - B_GATE_RANK))).astype(MXU_DTYPE)
    w_gate_p = jnp.pad(w_gate, ((0, LANES - B_GATE_RANK), (0, 0))).astype(MXU_DTYPE)
    n_main = w_main.shape[1]
    rows = GLA_SUB * GLA_TM
    const = lambda b, i: (0, 0)
    resident = dict(pipeline_mode=pl.Buffered(1))
    return pl.pallas_call(
        _gla_kernel,
        grid=(bsz, s // rows),
        in_specs=[
            pl.BlockSpec((1, rows, d), lambda b, i: (b, i, 0)),
            pl.BlockSpec((1, 6, d), lambda b, i: (b, 0, 0)),
            pl.BlockSpec((1, d), const),
            pl.BlockSpec((d, n_main), const, **resident),
            pl.BlockSpec((d, LANES), const),
            pl.BlockSpec((LANES, B_QK), const),
            pl.BlockSpec((1, B_QK), const),
            pl.BlockSpec((1, B_DV), const),
            pl.BlockSpec((B_V, d), const, **resident),
        ],
        out_specs=pl.BlockSpec((1, rows, d), lambda b, i: (b, i, 0)),
        out_shape=jax.ShapeDtypeStruct((bsz, s, d), f32),
        scratch_shapes=[
            pltpu.VMEM((GLA_TM, d), MXU_DTYPE),
            pltpu.VMEM((GLA_TM, n_main), f32),
            pltpu.VMEM((GLA_TM, n_main), f32),
            pltpu.VMEM((GLA_TM, B_QK), f32),
            pltpu.VMEM((GLA_TM, B_QK), f32),
            pltpu.VMEM((GLA_TM, B_V), MXU_DTYPE),
            pltpu.VMEM((GLA_TM, B_V), MXU_DTYPE),
            pltpu.VMEM((B_HEADS, B_DK, B_DV), f32),
        ],
        compiler_params=pltpu.CompilerParams(
            dimension_semantics=("parallel", "arbitrary"), vmem_limit_bytes=VMEM_LIMIT),
        name="gla_layer",
    )(x, mod, gamma, w_main, w_glr, w_gate_p, b_gate[None, :], g_norm[None, :], w_out.astype(MXU_DTYPE))


def _qkv_weight(w_in):
    width = A_HEADS * A_HEAD_DIM
    col_scale = jnp.tile(jnp.repeat(jnp.array([A_HEAD_DIM ** -0.5 * LOG2E, 1.0, 1.0], f32), width), A_GROUPS)
    return (w_in * col_scale[None, :]).astype(MXU_DTYPE)


def kernel(x, c, w_in_a, w_out_a, rel_bias, w_in_b, w_gate_b, b_gate_b, gnorm_b, w_out_b, norm_mix, norm_ffn, w_ada, b_ada, w_up, conv_w, conv_b, w_down, norm_final):
    depth = w_ada.shape[0]
    mod = _adaln(c, w_ada, b_ada)
    bias = _rel_bias(rel_bias)
    for i in range(depth):
        gam_mix = norm_mix[i][None, :]
        j = i // 2
        mixer = None
        if i % 2 == 0:
            qkv = _qkv_proj(x, mod[i], gam_mix, _qkv_weight(w_in_a[j]))
            mixer = (_attention(qkv, bias), w_out_a[j])
        else:
            x = _gla_layer(x, mod[i], gam_mix, w_in_b[j], w_gate_b[j], b_gate_b[j], gnorm_b[j], w_out_b[j])
        last = i == depth - 1
        x = _conv_ffn(x, mod[i], norm_ffn[i][None, :], w_up[i], conv_w[i], conv_b[i], w_down[i],
                      gamma_final=norm_final[None, :] if last else None, mixer=mixer)
    return x
```

```python
import functools
import math

import jax
import jax.numpy as jnp
from jax import lax
from jax.experimental import pallas as pl
from jax.experimental.pallas import tpu as pltpu

A_CONFIGS = ((128, 1), (512, 4), (2048, 16))
A_GROUPS = len(A_CONFIGS)
A_HEADS = 16
A_HEAD_DIM = 64
N_BUCKETS = 32
MAX_DISTANCE = 2048
B_HEADS = 4
B_DK = 128
B_DV = 256
B_QK = B_HEADS * B_DK
B_V = B_HEADS * B_DV
B_GATE_RANK = 16
B_TAU = 16.0
B_CHUNK = 64
CONV_W = 3
EPS = 1e-6
NEG_INF = -1e30
LOG2E = math.log2(math.e)

LANES = 128
SUBLANES = 8
MXU_DTYPE = jnp.bfloat16
ATT_BLK = 128
ATT_TILE = 2048
ATT_UNITS = ATT_TILE // ATT_BLK
ATT_PIPELINE = 3
PAIR_W = 3 * LANES
N_PAIRS = A_HEADS // 2
QKV_ROWS = 1024
PROBE_ROWS = 2 * SUBLANES
FFN_TM = 1024
FFN_TF = 256
FFN_ROWS = 256
HALO = SUBLANES
LEAD = (CONV_W - 1) * SUBLANES
O_HALO = 2 * SUBLANES
GLA_TM = 512
GLA_SUB = 2
GLA_PROJ_COLS = 256
VMEM_LIMIT = 56 * 1024 * 1024

f32 = jnp.float32


def _mm(a, b):
    return jnp.dot(a, b, preferred_element_type=f32)


def _mm_nt(a, b):
    return lax.dot_general(a, b, (((1,), (1,)), ((), ())), preferred_element_type=f32)


def _mm_tn(a, b):
    return lax.dot_general(a, b, (((0,), (0,)), ((), ())), preferred_element_type=f32)


def _modnorm(x, gamma, scale, shift):
    ms = jnp.mean(x * x, axis=-1, keepdims=True)
    y = x * lax.rsqrt(ms + EPS) * gamma
    return y * (1.0 + scale) + shift


def _silu(x):
    return x * (1.0 / (1.0 + jnp.exp(-x)))


def _regroup(t, groups):
    rows, d = t.shape
    return t.reshape(groups, rows // groups, d).swapaxes(0, 1).reshape(rows, d)


def _adaln_kernel(c_ref, w_ref, b_ref, o_ref):
    s = _silu(c_ref[...]).astype(MXU_DTYPE)
    o_ref[0] = _mm(s, w_ref[0].astype(MXU_DTYPE)) + b_ref[0]


def _adaln(c, w_ada, b_ada):
    depth, d, n = w_ada.shape
    bsz = c.shape[0]
    rows = 8 * pl.cdiv(bsz, 8)
    c_pad = jnp.pad(c, ((0, rows - bsz), (0, 0)))
    out = pl.pallas_call(
        _adaln_kernel,
        grid=(depth, n // d),
        in_specs=[
            pl.BlockSpec((rows, d), lambda l, j: (0, 0)),
            pl.BlockSpec((1, d, d), lambda l, j: (l, 0, j)),
            pl.BlockSpec((1, 1, d), lambda l, j: (l, 0, j)),
        ],
        out_specs=pl.BlockSpec((1, rows, d), lambda l, j: (l, 0, j)),
        out_shape=jax.ShapeDtypeStruct((depth, rows, n), f32),
        compiler_params=pltpu.CompilerParams(
            dimension_semantics=("parallel", "parallel"), vmem_limit_bytes=VMEM_LIMIT),
        name="adaln",
    )(c_pad, w_ada, b_ada.reshape(depth, 1, n))
    return out[:, :bsz].reshape(depth, bsz, n // d, d)


def _qkv_kernel(x_ref, mod_ref, gam_ref, wq_ref, wk_ref, wv_ref, o_ref, ha_ref, hb_ref, sa_ref, sb_ref, *,
                dilations, tiles_per_group):
    j = pl.program_id(2)
    g = j // tiles_per_group
    t = j % tiles_per_group
    nlb = sa_ref.shape[0]
    chunks_per_step = ATT_UNITS // tiles_per_group
    h_refs, stage_refs = [ha_ref, hb_ref], [sa_ref, sb_ref]

    @pl.when(j == 0)
    def _():
        for c in range(ATT_UNITS):
            rows = slice(c * ATT_BLK, (c + 1) * ATT_BLK)
            hn = _modnorm(x_ref[0, rows, :], gam_ref[...], mod_ref[0, 1:2, :], mod_ref[0, 0:1, :])
            ha_ref[rows, :] = hn.astype(ha_ref.dtype)
            for lb in range(nlb):
                sa_ref[lb, rows, :] = hn[:, lb * LANES:(lb + 1) * LANES]

    def deinterleave_chunk(gi, cc):
        d, dp = dilations[gi], dilations[gi - 1]
        rel = d // dp
        per = ATT_UNITS // d
        c = t * chunks_per_step + cc
        r, q = c // per, c % per
        start = (r % dp) * (ATT_TILE // dp) + q * (ATT_BLK * rel) + r // dp
        rows = pl.ds(start, ATT_BLK, stride=rel)
        r0 = pl.multiple_of(c * ATT_BLK, ATT_BLK)
        parts = [stage_refs[(gi - 1) % 2][lb, rows, :] for lb in range(nlb)]
        if gi + 1 < len(dilations):
            for lb in range(nlb):
                stage_refs[gi % 2][lb, pl.ds(r0, ATT_BLK), :] = parts[lb]
        dst_ref = h_refs[gi % 2]
        dst_ref[pl.ds(r0, ATT_BLK), :] = jnp.concatenate(parts, axis=1).astype(dst_ref.dtype)
        probe = dst_ref[pl.ds(r0, PROBE_ROWS), 0:LANES]
        return (probe != probe) & (probe == probe)

    def project_chunk(src_ref, rc, never):
        rows = slice(rc * QKV_ROWS, (rc + 1) * QKV_ROWS)
        lhs = src_ref[rows, :]
        for k, w_ref in enumerate((wq_ref, wk_ref, wv_ref)):
            res = _mm(lhs, w_ref[...]).astype(o_ref.dtype)
            lanes = slice(k * LANES, (k + 1) * LANES)
            o_ref[0, 0, 1, rows, lanes] = res[:, LANES:]
            if never is not None and k == 0:
                top = rc * QKV_ROWS + PROBE_ROWS
                o_ref[0, 0, 0, rc * QKV_ROWS:top, lanes] = jnp.where(
                    never, jnp.zeros_like(res[:PROBE_ROWS, :LANES]), res[:PROBE_ROWS, :LANES])
                o_ref[0, 0, 0, top:(rc + 1) * QKV_ROWS, lanes] = res[PROBE_ROWS:, :LANES]
            else:
                o_ref[0, 0, 0, rows, lanes] = res[:, :LANES]

    assert dilations[0] == 1
    assert all(d % dp == 0 for dp, d in zip(dilations, dilations[1:]))
    n_mm = ATT_TILE // QKV_ROWS
    for gi in range(len(dilations)):
        @pl.when(g == gi)
        def _(gi=gi):
            slots = max(n_mm - 1, 1)
            shares = [list(range(chunks_per_step))[sl::slots] for sl in range(slots)] + [[]]
            for rc in range(n_mm):
                never = None
                if gi + 1 < len(dilations):
                    for cc in shares[rc]:
                        probe = deinterleave_chunk(gi + 1, cc)
                        never = probe if never is None else never | probe
                project_chunk(h_refs[gi % 2], rc, never)


def _qkv_proj(x, mod, gamma, w):
    bsz, s, d = x.shape
    dilations = tuple(dl for _, dl in A_CONFIGS)
    tn = 2 * LANES
    tiles_per_group = A_HEADS * A_HEAD_DIM // tn
    kern = functools.partial(_qkv_kernel, dilations=dilations, tiles_per_group=tiles_per_group)

    def w_spec(k):
        return pl.BlockSpec(
            (d, tn), lambda b, i, j: (0, (j // tiles_per_group * 3 + k) * tiles_per_group + j % tiles_per_group))

    return pl.pallas_call(
        kern,
        grid=(bsz, s // ATT_TILE, A_GROUPS * tiles_per_group),
        in_specs=[
            pl.BlockSpec((1, ATT_TILE, d), lambda b, i, j: (b, i, 0)),
            pl.BlockSpec((1, 6, d), lambda b, i, j: (b, 0, 0)),
            pl.BlockSpec((1, d), lambda b, i, j: (0, 0)),
            w_spec(0), w_spec(1), w_spec(2),
        ],
        out_specs=pl.BlockSpec(
            (1, 1, 2, ATT_TILE, PAIR_W),
            lambda b, i, j: (j // tiles_per_group, b, j % tiles_per_group, i, 0)),
        out_shape=jax.ShapeDtypeStruct((A_GROUPS, bsz, N_PAIRS, s, PAIR_W), MXU_DTYPE),
        scratch_shapes=[pltpu.VMEM((ATT_TILE, d), MXU_DTYPE),
                        pltpu.VMEM((ATT_TILE, d), MXU_DTYPE),
                        pltpu.VMEM((d // LANES, ATT_TILE, LANES), f32),
                        pltpu.VMEM((d // LANES, ATT_TILE, LANES), f32)],
        compiler_params=pltpu.CompilerParams(
            dimension_semantics=("parallel", "parallel", "arbitrary"),
            vmem_limit_bytes=VMEM_LIMIT),
        name="qkv_proj",
    )(x, mod, gamma, w, w, w)


def _bias_kernel(tab_ref, bkt_ref, o_ref):
    bkt = bkt_ref[0]
    hits = [(bkt >= k) & (bkt < k + 1) for k in range(N_BUCKETS)]
    for h in range(A_HEADS):
        col = pl.program_id(0) * A_HEADS + h
        acc = jnp.full(bkt.shape, NEG_INF, f32)
        for k in range(N_BUCKETS):
            acc = jnp.where(hits[k], tab_ref[k, col] * LOG2E, acc)
        o_ref[0, h] = acc


def _t5_bucket(dist):
    max_exact = N_BUCKETS // 2
    n = jnp.maximum(dist, max_exact).astype(f32)
    large = max_exact + (jnp.log(n / max_exact) / math.log(MAX_DISTANCE / max_exact)
                         * (N_BUCKETS - max_exact)).astype(jnp.int32)
    large = jnp.minimum(large, N_BUCKETS - 1)
    return jnp.where(dist < max_exact, dist, large)


def _rel_bias(rel_bias):
    qi = jnp.arange(ATT_BLK)[:, None]
    ki = jnp.arange(2 * ATT_BLK)[None, :]
    steps = qi + ATT_BLK - ki
    band = (steps >= 0) & (steps <= ATT_BLK)
    bucket = jnp.stack([
        jnp.where(band, _t5_bucket(jnp.clip(steps, 0, ATT_BLK) * dl), -1)
        for _, dl in A_CONFIGS]).astype(jnp.int32)
    return pl.pallas_call(
        _bias_kernel,
        grid=(A_GROUPS,),
        in_specs=[
            pl.BlockSpec(memory_space=pltpu.SMEM),
            pl.BlockSpec((1, ATT_BLK, 2 * ATT_BLK), lambda g: (g, 0, 0)),
        ],
        out_specs=pl.BlockSpec((1, A_HEADS, ATT_BLK, 2 * ATT_BLK), lambda g: (g, 0, 0, 0)),
        out_shape=jax.ShapeDtypeStruct((A_GROUPS, A_HEADS, ATT_BLK, 2 * ATT_BLK), f32),
        name="rel_bias",
    )(rel_bias, bucket)


def _attn_kernel(qkv_ref, bias_ref, out_ref, prev0, prev1, prev2, o1, l1, n1, o2, l2, n2, biasp):
    first = pl.program_id(2) == 0

    @pl.when(first)
    def _():
        prev0[...] = jnp.zeros_like(prev0)
        prev1[...] = jnp.zeros_like(prev1)
        prev2[...] = jnp.zeros_like(prev2)

    for g in range(A_GROUPS):
        for hh in range(2):
            biasp[g, hh, :, 0:ATT_BLK] = jnp.where(first, NEG_INF, bias_ref[g, hh, :, 0:ATT_BLK])
            biasp[g, hh, :, ATT_BLK:2 * ATT_BLK] = bias_ref[g, hh, :, ATT_BLK:2 * ATT_BLK]

    lane = lax.broadcasted_iota(jnp.int32, (1, LANES), 1)
    lo = lane < A_HEAD_DIM
    qmask = (jnp.where(lo, 1.0, 0.0).astype(MXU_DTYPE), jnp.where(lo, 0.0, 1.0).astype(MXU_DTYPE))

    def keys_values(g, rc, prev_ref, rp, lanes):
        if prev_ref is None:
            return qkv_ref[g, 0, 0, rp:rp + 2 * ATT_BLK, lanes]
        off = lanes.start - LANES
        return jnp.concatenate([prev_ref[rp:rp + ATT_BLK, off:off + LANES],
                                qkv_ref[g, 0, 0, rc:rc + ATT_BLK, lanes]], axis=0)

    def scores(g, rc, prev_ref, rp):
        q = qkv_ref[g, 0, 0, rc:rc + ATT_BLK, 0:LANES]
        k = keys_values(g, rc, prev_ref, rp, slice(LANES, 2 * LANES))
        bias = bias_ref if prev_ref is None else biasp
        return [_mm_nt(q * qmask[hh], k) + bias[g, hh] for hh in range(2)]

    def attend(g, rc, prev_ref, rp, s):
        v = keys_values(g, rc, prev_ref, rp, slice(2 * LANES, 3 * LANES))
        accs, dens, ms = [], [], []
        for sh in s:
            m = jnp.max(sh, axis=-1, keepdims=True)
            p = jnp.exp2(sh - m)
            dens.append(jnp.sum(p, axis=-1, keepdims=True))
            accs.append(_mm(p.astype(MXU_DTYPE), v))
            ms.append(m)
        acc = jnp.where(lo, accs[0], accs[1])
        return (acc, jnp.broadcast_to(jnp.where(lo, ms[0], ms[1]), acc.shape),
                jnp.broadcast_to(jnp.where(lo, dens[0], dens[1]), acc.shape))

    def store(a_ref, m_ref, d_ref, rows):
        def post(acc, m, den):
            a_ref[rows, :] = acc
            m_ref[rows, :] = m
            d_ref[rows, :] = den
        return post

    def merge(rows):
        def post(acc0, m0, den0):
            m1, m2 = l1[rows, :], l2[rows, :]
            mx = jnp.maximum(jnp.maximum(m0, m1), m2)
            e0, e1, e2 = jnp.exp2(m0 - mx), jnp.exp2(m1 - mx), jnp.exp2(m2 - mx)
            num = e0 * acc0 + e1 * o1[rows, :] + e2 * o2[rows, :]
            den = e0 * den0 + e1 * n1[rows, :] + e2 * n2[rows, :]
            out_ref[0, rows, :] = (num * (1.0 / den)).astype(out_ref.dtype)
        return post

    units = []
    d2 = A_CONFIGS[2][1]
    for u in range(ATT_UNITS):
        units.append((2, u * ATT_BLK, prev2, u * ATT_BLK, store(o2, l2, n2, pl.ds(u, ATT_BLK, stride=d2))))
    d1 = A_CONFIGS[1][1]
    per = ATT_UNITS // d1
    for u in range(ATT_UNITS):
        r, q = divmod(u, per)
        rc = u * ATT_BLK
        post = store(o1, l1, n1, pl.ds(q * ATT_BLK * d1 + r, ATT_BLK, stride=d1))
        if q == 0:
            units.append((1, rc, prev1, (r * per + per - 1) * ATT_BLK, post))
        else:
            units.append((1, rc, None, rc - ATT_BLK, post))
    for u in range(ATT_UNITS):
        rc = u * ATT_BLK
        post = merge(slice(rc, rc + ATT_BLK))
        units.append((0, rc, prev0, 0, post) if u == 0 else (0, rc, None, rc - ATT_BLK, post))

    pending = [scores(*un[:4]) for un in units[:ATT_PIPELINE]]
    for idx, un in enumerate(units):
        if idx + ATT_PIPELINE < len(units):
            pending.append(scores(*units[idx + ATT_PIPELINE][:4]))
        un[4](*attend(*un[:4], pending.pop(0)))

    prev0[...] = qkv_ref[0, 0, 0, ATT_TILE - ATT_BLK:ATT_TILE, LANES:3 * LANES]
    prev1[...] = qkv_ref[1, 0, 0, :, LANES:3 * LANES]
    prev2[...] = qkv_ref[2, 0, 0, :, LANES:3 * LANES]


def _attention(qkv, bias):
    _, bsz, _, s, _ = qkv.shape
    return pl.pallas_call(
        _attn_kernel,
        grid=(bsz, N_PAIRS, s // ATT_TILE),
        in_specs=[
            pl.BlockSpec((A_GROUPS, 1, 1, ATT_TILE, PAIR_W), lambda b, p, i: (0, b, p, i, 0)),
            pl.BlockSpec((A_GROUPS, 2, ATT_BLK, 2 * ATT_BLK), lambda b, p, i: (0, p, 0, 0)),
        ],
        out_specs=pl.BlockSpec((1, ATT_TILE, LANES), lambda b, p, i: (b, i, p)),
        out_shape=jax.ShapeDtypeStruct((bsz, s, A_HEADS * A_HEAD_DIM), MXU_DTYPE),
        scratch_shapes=[
            pltpu.VMEM((ATT_BLK, 2 * LANES), MXU_DTYPE),
            pltpu.VMEM((ATT_TILE, 2 * LANES), MXU_DTYPE),
            pltpu.VMEM((ATT_TILE, 2 * LANES), MXU_DTYPE),
            pltpu.VMEM((ATT_TILE, LANES), f32),
            pltpu.VMEM((ATT_TILE, LANES), f32),
            pltpu.VMEM((ATT_TILE, LANES), f32),
            pltpu.VMEM((ATT_TILE, LANES), f32),
            pltpu.VMEM((ATT_TILE, LANES), f32),
            pltpu.VMEM((ATT_TILE, LANES), f32),
            pltpu.VMEM((A_GROUPS, 2, ATT_BLK, 2 * ATT_BLK), f32),
        ],
        compiler_params=pltpu.CompilerParams(
            dimension_semantics=("parallel", "parallel", "arbitrary"),
            vmem_limit_bytes=VMEM_LIMIT),
        name="dilated_attn",
    )(qkv, bias)


def _ffn_kernel(*refs, final, mixer):
    refs = list(refs)
    x_ref, halo_ref = refs[:2]
    del refs[:2]
    if mixer:
        o_ref, ohalo_ref, wout_ref = refs[:3]
        del refs[:3]
    mod_ref, gam_ref, wup_ref, cw_ref, cb_ref, wdn_ref = refs[:6]
    del refs[:6]
    if final:
        gfin_ref = refs.pop(0)
    out_ref, h_ref, u0_ref, u1_ref, act_ref = refs[:5]
    if mixer:
        x1_ref = refs[5]
    u_refs = (u0_ref, u1_ref)
    gam, scale, shift = gam_ref[...], mod_ref[0, 4:5, :], mod_ref[0, 3:4, :]
    d_ff = wdn_ref.shape[0]
    nchunk = d_ff // FFN_TF
    tm = x_ref.shape[1]

    if mixer:
        x1 = x_ref[0] + mod_ref[0, 2:3, :] * _mm(o_ref[0], wout_ref[...])
        x1_ref[...] = x1
        halo = halo_ref[0] + mod_ref[0, 2:3, :] * _mm(ohalo_ref[0], wout_ref[...])[O_HALO - HALO:, :]
    else:
        x1, halo = x_ref[0], halo_ref[0]
    halo = _modnorm(halo, gam, scale, shift)
    halo = jnp.where(pl.program_id(1) == 0, 0.0, halo)
    hp = _regroup(_modnorm(x1, gam, scale, shift), SUBLANES)
    for back in range(1, CONV_W):
        last = hp[tm - back * SUBLANES:tm - (back - 1) * SUBLANES, :]
        lead = jnp.concatenate([halo[HALO - back:HALO - back + 1, :], last[:SUBLANES - 1, :]], axis=0)
        h_ref[(CONV_W - 1 - back) * SUBLANES:(CONV_W - back) * SUBLANES, :] = lead.astype(h_ref.dtype)
    h_ref[LEAD:LEAD + tm, :] = hp.astype(h_ref.dtype)

    def halves(c):
        return [slice(half * d_ff + c * FFN_TF, half * d_ff + (c + 1) * FFN_TF) for half in range(2)]

    def up(c):
        for half, cols in enumerate(halves(c)):
            u_refs[c % 2][:, half * FFN_TF:(half + 1) * FFN_TF] = _mm(h_ref[...], wup_ref[:, cols])

    def conv_act(c):
        u_ref = u_refs[c % 2]
        for rb in range(tm // FFN_ROWS):
            ab = []
            for half, cols in enumerate(halves(c)):
                v = cb_ref[:, cols]
                for t in range(CONV_W):
                    off = rb * FFN_ROWS + t * SUBLANES
                    v = v + cw_ref[t:t + 1, cols] * u_ref[off:off + FFN_ROWS, half * FFN_TF:(half + 1) * FFN_TF]
                ab.append(v)
            act_ref[rb * FFN_ROWS:(rb + 1) * FFN_ROWS, c * FFN_TF:(c + 1) * FFN_TF] = (
                _silu(ab[0]) * ab[1]).astype(act_ref.dtype)

    up(0)
    for c in range(nchunk):
        if c + 1 < nchunk:
            up(c + 1)
        conv_act(c)

    ffn = _regroup(_mm(act_ref[...], wdn_ref[...]), tm // SUBLANES)
    y = (x1_ref[...] if mixer else x_ref[0]) + mod_ref[0, 5:6, :] * ffn
    if final:
        ms = jnp.mean(y * y, axis=-1, keepdims=True)
        y = y * lax.rsqrt(ms + EPS) * gfin_ref[...]
    out_ref[0] = y


def _conv_ffn(x, mod, gamma, w_up, conv_w, conv_b, w_down, gamma_final=None, mixer=None):
    bsz, s, d = x.shape
    d_ff = w_down.shape[0]
    final = gamma_final is not None
    rows = FFN_TM
    tiles_per_halo = rows // HALO
    resident = dict(pipeline_mode=pl.Buffered(1))
    in_specs = [
        pl.BlockSpec((1, rows, d), lambda b, i: (b, i, 0)),
        pl.BlockSpec((1, HALO, d), lambda b, i: (b, jnp.maximum(i * tiles_per_halo - 1, 0), 0)),
    ]
    args = [x, x]
    if mixer is not None:
        o, w_out = mixer
        k = o.shape[-1]
        in_specs += [
            pl.BlockSpec((1, rows, k), lambda b, i: (b, i, 0)),
            pl.BlockSpec((1, O_HALO, k), lambda b, i: (b, jnp.maximum(i * (rows // O_HALO) - 1, 0), 0)),
            pl.BlockSpec((k, d), lambda b, i: (0, 0), **resident),
        ]
        args += [o, o, w_out.astype(MXU_DTYPE)]
    in_specs += [
        pl.BlockSpec((1, 6, d), lambda b, i: (b, 0, 0)),
        pl.BlockSpec((1, d), lambda b, i: (0, 0)),
        pl.BlockSpec((d, 2 * d_ff), lambda b, i: (0, 0), **resident),
        pl.BlockSpec((CONV_W, 2 * d_ff), lambda b, i: (0, 0)),
        pl.BlockSpec((1, 2 * d_ff), lambda b, i: (0, 0)),
        pl.BlockSpec((d_ff, d), lambda b, i: (0, 0), **resident),
    ]
    args += [mod, gamma, w_up.astype(MXU_DTYPE), conv_w, conv_b[None, :], w_down.astype(MXU_DTYPE)]
    if final:
        in_specs.append(pl.BlockSpec((1, d), lambda b, i: (0, 0)))
        args.append(gamma_final)
    return pl.pallas_call(
        functools.partial(_ffn_kernel, final=final, mixer=mixer is not None),
        grid=(bsz, s // rows),
        in_specs=in_specs,
        out_specs=pl.BlockSpec((1, rows, d), lambda b, i: (b, i, 0)),
        out_shape=jax.ShapeDtypeStruct((bsz, s, d), f32),
        scratch_shapes=[
            pltpu.VMEM((rows + LEAD, d), MXU_DTYPE),
            pltpu.VMEM((rows + LEAD, 2 * FFN_TF), f32),
            pltpu.VMEM((rows + LEAD, 2 * FFN_TF), f32),
            pltpu.VMEM((rows, d_ff), MXU_DTYPE),
        ] + ([pltpu.VMEM((rows, d), f32)] if mixer is not None else []),
        compiler_params=pltpu.CompilerParams(
            dimension_semantics=("parallel", "parallel"), vmem_limit_bytes=VMEM_LIMIT),
        name="conv_ffn_final" if final else "conv_ffn",
    )(*args)


def _gla_kernel(x_ref, mod_ref, gam_ref, win_ref, wglr_ref, wgate_ref, bgate_ref, gnorm_ref, wout_ref,
                out_ref, h_ref, proj0_ref, proj1_ref, gk0_ref, gk1_ref, og0_ref, og1_ref, state_ref):
    @pl.when(pl.program_id(1) == 0)
    def _():
        state_ref[...] = jnp.zeros_like(state_ref)

    proj_refs, gk_refs, og_refs = (proj0_ref, proj1_ref), (gk0_ref, gk1_ref), (og0_ref, og1_ref)
    nsub = x_ref.shape[1] // GLA_TM
    nchunk = GLA_TM // B_CHUNK
    n_main = win_ref.shape[1]

    ri = lax.broadcasted_iota(jnp.int32, (B_CHUNK, B_CHUNK), 0)
    ci = lax.broadcasted_iota(jnp.int32, (B_CHUNK, B_CHUNK), 1)
    causal = ci <= ri
    tri = causal.astype(MXU_DTYPE)
    gnorm = gnorm_ref[...]
    qscale = B_DK ** -0.5
    r_off = 2 * B_QK + B_V
    heads = [slice(hd * B_DK, (hd + 1) * B_DK) for hd in range(B_HEADS)]

    def project_steps(k):
        rows = slice(k * GLA_TM, (k + 1) * GLA_TM)

        def normalize():
            h = _modnorm(x_ref[0, rows, :], gam_ref[...], mod_ref[0, 1:2, :], mod_ref[0, 0:1, :])
            h_ref[...] = h.astype(h_ref.dtype)

        def piece(cols):
            def run():
                proj_refs[k % 2][:, cols] = _mm(h_ref[...], win_ref[:, cols])
            return run

        def gate():
            glr = _mm(h_ref[...], wglr_ref[...]).astype(MXU_DTYPE)
            z = _mm(glr, wgate_ref[...]) + bgate_ref[...]
            gk_refs[k % 2][...] = (jnp.minimum(z, 0.0) - jnp.log1p(jnp.exp(-jnp.abs(z)))) * (1.0 / B_TAU)

        pieces = [piece(slice(c0, c0 + GLA_PROJ_COLS)) for c0 in range(0, n_main, GLA_PROJ_COLS)]
        return [normalize] + pieces + [gate]

    def local(k, c):
        proj_ref, gk_ref = proj_refs[k % 2], gk_refs[k % 2]
        rows = slice(c * B_CHUNK, (c + 1) * B_CHUNK)
        gk = gk_ref[rows, :]
        g_hi = gk.astype(MXU_DTYPE)
        rem = gk - g_hi.astype(f32)
        g_mid = rem.astype(MXU_DTYPE)
        g_lo = (rem - g_mid.astype(f32)).astype(MXU_DTYPE)
        bcum = _mm(tri, g_hi) + _mm(tri, g_mid) + _mm(tri, g_lo)
        blast = bcum[B_CHUNK - 1:B_CHUNK, :]
        q_t = ((proj_ref[rows, 0:B_QK] * qscale) * jnp.exp(bcum)).astype(MXU_DTYPE)
        kk = proj_ref[rows, B_QK:2 * B_QK]
        k_t = (kk * jnp.exp(-bcum)).astype(MXU_DTYPE)
        k_d = (kk * jnp.exp(blast - bcum)).astype(MXU_DTYPE)
        v = [proj_ref[rows, 2 * B_QK + hd * B_DV:2 * B_QK + (hd + 1) * B_DV].astype(MXU_DTYPE)
             for hd in range(B_HEADS)]
        a = [jnp.where(causal, _mm_nt(q_t[:, ks], k_t[:, ks]), 0.0).astype(MXU_DTYPE) for ks in heads]
        kv = [_mm_tn(k_d[:, ks], v[hd]) for hd, ks in enumerate(heads)]
        o_intra = [_mm(a[hd], v[hd]) for hd in range(B_HEADS)]
        return q_t, blast, kv, o_intra

    def recur(k, c, loc, state):
        q_t, blast, kv, o_intra = loc
        rows = slice(c * B_CHUNK, (c + 1) * B_CHUNK)
        new_state = []
        for hd, ks in enumerate(heads):
            o = o_intra[hd] + _mm(q_t[:, ks], state[hd].astype(MXU_DTYPE))
            decay = jnp.exp(jnp.broadcast_to(blast[:, ks], (B_DK, B_DK)).T)
            new_state.append(jnp.concatenate([decay] * (B_DV // B_DK), axis=1) * state[hd] + kv[hd])
            o = o * lax.rsqrt(jnp.mean(o * o, axis=-1, keepdims=True) + EPS) * gnorm
            r = proj_refs[k % 2][rows, r_off + hd * B_DV:r_off + (hd + 1) * B_DV]
            og_refs[k % 2][rows, hd * B_DV:(hd + 1) * B_DV] = (o * _silu(r)).astype(og_refs[k % 2].dtype)
        return new_state

    for step in project_steps(0):
        step()
    state = [state_ref[hd] for hd in range(B_HEADS)]
    for k in range(nsub):
        ahead = project_steps(k + 1) if k + 1 < nsub else []
        loc = local(k, 0)
        for c in range(nchunk):
            nxt = local(k, c + 1) if c + 1 < nchunk else None
            state = recur(k, c, loc, state)
            loc = nxt
            take = -(-len(ahead) // (nchunk - c))
            for step in ahead[:take]:
                step()
            ahead = ahead[take:]
        rows = slice(k * GLA_TM, (k + 1) * GLA_TM)
        out_ref[0, rows, :] = x_ref[0, rows, :] + mod_ref[0, 2:3, :] * _mm(og_refs[k % 2][...], wout_ref[...])
    for hd in range(B_HEADS):
        state_ref[hd] = state[hd]


def _gla_layer(x, mod, gamma, w_in, w_gate, b_gate, g_norm, w_out):
    bsz, s, d = x.shape
    glr0 = 2 * B_QK + B_V
    w_main = jnp.concatenate([w_in[:, :glr0], w_in[:, glr0 + B_GATE_RANK:]], axis=1).astype(MXU_DTYPE)
    w_glr = jnp.pad(w_in[:, glr0:glr0 + B_GATE_RANK], ((0, 0), (0, LANES - B_GATE_RANK))).astype(MXU_DTYPE)
    w_gate_p = jnp.pad(w_gate, ((0, LANES - B_GATE_RANK), (0, 0))).astype(MXU_DTYPE)
    n_main = w_main.shape[1]
    rows = GLA_SUB * GLA_TM
    const = lambda b, i: (0, 0)
    resident = dict(pipeline_mode=pl.Buffered(1))
    return pl.pallas_call(
        _gla_kernel,
        grid=(bsz, s // rows),
        in_specs=[
            pl.BlockSpec((1, rows, d), lambda b, i: (b, i, 0)),
            pl.BlockSpec((1, 6, d), lambda b, i: (b, 0, 0)),
            pl.BlockSpec((1, d), const),
            pl.BlockSpec((d, n_main), const, **resident),
            pl.BlockSpec((d, LANES), const),
            pl.BlockSpec((LANES, B_QK), const),
            pl.BlockSpec((1, B_QK), const),
            pl.BlockSpec((1, B_DV), const),
            pl.BlockSpec((B_V, d), const, **resident),
        ],
        out_specs=pl.BlockSpec((1, rows, d), lambda b, i: (b, i, 0)),
        out_shape=jax.ShapeDtypeStruct((bsz, s, d), f32),
        scratch_shapes=[
            pltpu.VMEM((GLA_TM, d), MXU_DTYPE),
            pltpu.VMEM((GLA_TM, n_main), f32),
            pltpu.VMEM((GLA_TM, n_main), f32),
            pltpu.VMEM((GLA_TM, B_QK), f32),
            pltpu.VMEM((GLA_TM, B_QK), f32),
            pltpu.VMEM((GLA_TM, B_V), MXU_DTYPE),
            pltpu.VMEM((GLA_TM, B_V), MXU_DTYPE),
            pltpu.VMEM((B_HEADS, B_DK, B_DV), f32),
        ],
        compiler_params=pltpu.CompilerParams(
            dimension_semantics=("parallel", "arbitrary"), vmem_limit_bytes=VMEM_LIMIT),
        name="gla_layer",
    )(x, mod, gamma, w_main, w_glr, w_gate_p, b_gate[None, :], g_norm[None, :], w_out.astype(MXU_DTYPE))


def _qkv_weight(w_in):
    width = A_HEADS * A_HEAD_DIM
    col_scale = jnp.tile(jnp.repeat(jnp.array([A_HEAD_DIM ** -0.5 * LOG2E, 1.0, 1.0], f32), width), A_GROUPS)
    return (w_in * col_scale[None, :]).astype(MXU_DTYPE)


def kernel(x, c, w_in_a, w_out_a, rel_bias, w_in_b, w_gate_b, b_gate_b, gnorm_b, w_out_b, norm_mix, norm_ffn, w_ada, b_ada, w_up, conv_w, conv_b, w_down, norm_final):
    depth = w_ada.shape[0]
    mod = _adaln(c, w_ada, b_ada)
    bias = _rel_bias(rel_bias)
    for i in range(depth):
        gam_mix = norm_mix[i][None, :]
        j = i // 2
        mixer = None
        if i % 2 == 0:
            qkv = _qkv_proj(x, mod[i], gam_mix, _qkv_weight(w_in_a[j]))
            mixer = (_attention(qkv, bias), w_out_a[j])
        else:
            x = _gla_layer(x, mod[i], gam_mix, w_in_b[j], w_gate_b[j], b_gate_b[j], gnorm_b[j], w_out_b[j])
        last = i == depth - 1
        x = _conv_ffn(x, mod[i], norm_ffn[i][None, :], w_up[i], conv_w[i], conv_b[i], w_down[i],
                      gamma_final=norm_final[None, :] if last else None, mixer=mixer)
    return x
```

```python
import functools
import math

import jax
import jax.numpy as jnp
from jax import lax
from jax.experimental import pallas as pl
from jax.experimental.pallas import tpu as pltpu

A_CONFIGS = ((128, 1), (512, 4), (2048, 16))
A_GROUPS = len(A_CONFIGS)
A_HEADS = 16
A_HEAD_DIM = 64
N_BUCKETS = 32
MAX_DISTANCE = 2048
B_HEADS = 4
B_DK = 128
B_DV = 256
B_QK = B_HEADS * B_DK
B_V = B_HEADS * B_DV
B_GATE_RANK = 16
B_TAU = 16.0
B_CHUNK = 64
CONV_W = 3
EPS = 1e-6
NEG_INF = -1e30
LOG2E = math.log2(math.e)

LANES = 128
SUBLANES = 8
MXU_DTYPE = jnp.bfloat16
ATT_BLK = 128
ATT_TILE = 2048
ATT_UNITS = ATT_TILE // ATT_BLK
ATT_PIPELINE = 3
PAIR_W = 3 * LANES
N_PAIRS = A_HEADS // 2
QKV_ROWS = 1024
PROBE_ROWS = 2 * SUBLANES
FFN_TM = 1024
FFN_TF = 256
FFN_ROWS = 256
HALO = SUBLANES
LEAD = (CONV_W - 1) * SUBLANES
O_HALO = 2 * SUBLANES
GLA_TM = 512
GLA_SUB = 2
GLA_PROJ_COLS = 256
VMEM_LIMIT = 56 * 1024 * 1024

f32 = jnp.float32


def _mm(a, b):
    return jnp.dot(a, b, preferred_element_type=f32)


def _mm_nt(a, b):
    return lax.dot_general(a, b, (((1,), (1,)), ((), ())), preferred_element_type=f32)


def _mm_tn(a, b):
    return lax.dot_general(a, b, (((0,), (0,)), ((), ())), preferred_element_type=f32)


def _modnorm(x, gamma, scale, shift):
    ms = jnp.mean(x * x, axis=-1, keepdims=True)
    y = x * lax.rsqrt(ms + EPS) * gamma
    return y * (1.0 + scale) + shift


def _silu(x):
    return x * (1.0 / (1.0 + jnp.exp(-x)))


def _regroup(t, groups):
    rows, d = t.shape
    return t.reshape(groups, rows // groups, d).swapaxes(0, 1).reshape(rows, d)


def _adaln_kernel(c_ref, w_ref, b_ref, o_ref):
    s = _silu(c_ref[...]).astype(MXU_DTYPE)
    o_ref[0] = _mm(s, w_ref[0].astype(MXU_DTYPE)) + b_ref[0]


def _adaln(c, w_ada, b_ada):
    depth, d, n = w_ada.shape
    bsz = c.shape[0]
    rows = 8 * pl.cdiv(bsz, 8)
    c_pad = jnp.pad(c, ((0, rows - bsz), (0, 0)))
    out = pl.pallas_call(
        _adaln_kernel,
        grid=(depth, n // d),
        in_specs=[
            pl.BlockSpec((rows, d), lambda l, j: (0, 0)),
            pl.BlockSpec((1, d, d), lambda l, j: (l, 0, j)),
            pl.BlockSpec((1, 1, d), lambda l, j: (l, 0, j)),
        ],
        out_specs=pl.BlockSpec((1, rows, d), lambda l, j: (l, 0, j)),
        out_shape=jax.ShapeDtypeStruct((depth, rows, n), f32),
        compiler_params=pltpu.CompilerParams(
            dimension_semantics=("parallel", "parallel"), vmem_limit_bytes=VMEM_LIMIT),
        name="adaln",
    )(c_pad, w_ada, b_ada.reshape(depth, 1, n))
    return out[:, :bsz].reshape(depth, bsz, n // d, d)


def _qkv_kernel(x_ref, mod_ref, gam_ref, wq_ref, wk_ref, wv_ref, o_ref, ha_ref, hb_ref, sa_ref, sb_ref, *,
                dilations, tiles_per_group):
    j = pl.program_id(2)
    g = j // tiles_per_group
    t = j % tiles_per_group
    nlb = sa_ref.shape[0]
    chunks_per_step = ATT_UNITS // tiles_per_group
    h_refs, stage_refs = [ha_ref, hb_ref], [sa_ref, sb_ref]

    @pl.when(j == 0)
    def _():
        for c in range(ATT_UNITS):
            rows = slice(c * ATT_BLK, (c + 1) * ATT_BLK)
            hn = _modnorm(x_ref[0, rows, :], gam_ref[...], mod_ref[0, 1:2, :], mod_ref[0, 0:1, :])
            ha_ref[rows, :] = hn.astype(ha_ref.dtype)
            for lb in range(nlb):
                sa_ref[lb, rows, :] = hn[:, lb * LANES:(lb + 1) * LANES]

    def deinterleave_chunk(gi, cc):
        d, dp = dilations[gi], dilations[gi - 1]
        rel = d // dp
        per = ATT_UNITS // d
        c = t * chunks_per_step + cc
        r, q = c // per, c % per
        start = (r % dp) * (ATT_TILE // dp) + q * (ATT_BLK * rel) + r // dp
        rows = pl.ds(start, ATT_BLK, stride=rel)
        r0 = pl.multiple_of(c * ATT_BLK, ATT_BLK)
        parts = [stage_refs[(gi - 1) % 2][lb, rows, :] for lb in range(nlb)]
        if gi + 1 < len(dilations):
            for lb in range(nlb):
                stage_refs[gi % 2][lb, pl.ds(r0, ATT_BLK), :] = parts[lb]
        dst_ref = h_refs[gi % 2]
        dst_ref[pl.ds(r0, ATT_BLK), :] = jnp.concatenate(parts, axis=1).astype(dst_ref.dtype)
        probe = dst_ref[pl.ds(r0, PROBE_ROWS), 0:LANES]
        return (probe != probe) & (probe == probe)

    def project_chunk(src_ref, rc, never):
        rows = slice(rc * QKV_ROWS, (rc + 1) * QKV_ROWS)
        lhs = src_ref[rows, :]
        for k, w_ref in enumerate((wq_ref, wk_ref, wv_ref)):
            res = _mm(lhs, w_ref[...]).astype(o_ref.dtype)
            lanes = slice(k * LANES, (k + 1) * LANES)
            o_ref[0, 0, 1, rows, lanes] = res[:, LANES:]
            if never is not None and k == 0:
                top = rc * QKV_ROWS + PROBE_ROWS
                o_ref[0, 0, 0, rc * QKV_ROWS:top, lanes] = jnp.where(
                    never, jnp.zeros_like(res[:PROBE_ROWS, :LANES]), res[:PROBE_ROWS, :LANES])
                o_ref[0, 0, 0, top:(rc + 1) * QKV_ROWS, lanes] = res[PROBE_ROWS:, :LANES]
            else:
                o_ref[0, 0, 0, rows, lanes] = res[:, :LANES]

    assert dilations[0] == 1
    assert all(d % dp == 0 for dp, d in zip(dilations, dilations[1:]))
    n_mm = ATT_TILE // QKV_ROWS
    for gi in range(len(dilations)):
        @pl.when(g == gi)
        def _(gi=gi):
            slots = max(n_mm - 1, 1)
            shares = [list(range(chunks_per_step))[sl::slots] for sl in range(slots)] + [[]]
            for rc in range(n_mm):
                never = None
                if gi + 1 < len(dilations):
                    for cc in shares[rc]:
                        probe = deinterleave_chunk(gi + 1, cc)
                        never = probe if never is None else never | probe
                project_chunk(h_refs[gi % 2], rc, never)


def _qkv_proj(x, mod, gamma, w):
    bsz, s, d = x.shape
    dilations = tuple(dl for _, dl in A_CONFIGS)
    tn = 2 * LANES
    tiles_per_group = A_HEADS * A_HEAD_DIM // tn
    kern = functools.partial(_qkv_kernel, dilations=dilations, tiles_per_group=tiles_per_group)

    def w_spec(k):
        return pl.BlockSpec(
            (d, tn), lambda b, i, j: (0, (j // tiles_per_group * 3 + k) * tiles_per_group + j % tiles_per_group))

    return pl.pallas_call(
        kern,
        grid=(bsz, s // ATT_TILE, A_GROUPS * tiles_per_group),
        in_specs=[
            pl.BlockSpec((1, ATT_TILE, d), lambda b, i, j: (b, i, 0)),
            pl.BlockSpec((1, 6, d), lambda b, i, j: (b, 0, 0)),
            pl.BlockSpec((1, d), lambda b, i, j: (0, 0)),
            w_spec(0), w_spec(1), w_spec(2),
        ],
        out_specs=pl.BlockSpec(
            (1, 1, 2, ATT_TILE, PAIR_W),
            lambda b, i, j: (j // tiles_per_group, b, j % tiles_per_group, i, 0)),
        out_shape=jax.ShapeDtypeStruct((A_GROUPS, bsz, N_PAIRS, s, PAIR_W), MXU_DTYPE),
        scratch_shapes=[pltpu.VMEM((ATT_TILE, d), MXU_DTYPE),
                        pltpu.VMEM((ATT_TILE, d), MXU_DTYPE),
                        pltpu.VMEM((d // LANES, ATT_TILE, LANES), f32),
                        pltpu.VMEM((d // LANES, ATT_TILE, LANES), f32)],
        compiler_params=pltpu.CompilerParams(
            dimension_semantics=("parallel", "parallel", "arbitrary"),
            vmem_limit_bytes=VMEM_LIMIT),
        name="qkv_proj",
    )(x, mod, gamma, w, w, w)


def _bias_kernel(tab_ref, bkt_ref, o_ref):
    bkt = bkt_ref[0]
    hits = [(bkt >= k) & (bkt < k + 1) for k in range(N_BUCKETS)]
    for h in range(A_HEADS):
        col = pl.program_id(0) * A_HEADS + h
        acc = jnp.full(bkt.shape, NEG_INF, f32)
        for k in range(N_BUCKETS):
            acc = jnp.where(hits[k], tab_ref[k, col] * LOG2E, acc)
        o_ref[0, h] = acc


def _t5_bucket(dist):
    max_exact = N_BUCKETS // 2
    n = jnp.maximum(dist, max_exact).astype(f32)
    large = max_exact + (jnp.log(n / max_exact) / math.log(MAX_DISTANCE / max_exact)
                         * (N_BUCKETS - max_exact)).astype(jnp.int32)
    large = jnp.minimum(large, N_BUCKETS - 1)
    return jnp.where(dist < max_exact, dist, large)


def _rel_bias(rel_bias):
    qi = jnp.arange(ATT_BLK)[:, None]
    ki = jnp.arange(2 * ATT_BLK)[None, :]
    steps = qi + ATT_BLK - ki
    band = (steps >= 0) & (steps <= ATT_BLK)
    bucket = jnp.stack([
        jnp.where(band, _t5_bucket(jnp.clip(steps, 0, ATT_BLK) * dl), -1)
        for _, dl in A_CONFIGS]).astype(jnp.int32)
    return pl.pallas_call(
        _bias_kernel,
        grid=(A_GROUPS,),
        in_specs=[
            pl.BlockSpec(memory_space=pltpu.SMEM),
            pl.BlockSpec((1, ATT_BLK, 2 * ATT_BLK), lambda g: (g, 0, 0)),
        ],
        out_specs=pl.BlockSpec((1, A_HEADS, ATT_BLK, 2 * ATT_BLK), lambda g: (g, 0, 0, 0)),
        out_shape=jax.ShapeDtypeStruct((A_GROUPS, A_HEADS, ATT_BLK, 2 * ATT_BLK), f32),
        name="rel_bias",
    )(rel_bias, bucket)


def _attn_kernel(qkv_ref, prevq_ref, bias_ref, out_ref, o1, l1, n1, o2, l2, n2, biasp):
    first = pl.program_id(2) == 0

    for g in range(A_GROUPS):
        for hh in range(2):
            biasp[g, hh, :, 0:ATT_BLK] = jnp.where(first, NEG_INF, bias_ref[g, hh, :, 0:ATT_BLK])
            biasp[g, hh, :, ATT_BLK:2 * ATT_BLK] = bias_ref[g, hh, :, ATT_BLK:2 * ATT_BLK]

    lane = lax.broadcasted_iota(jnp.int32, (1, LANES), 1)
    lo = lane < A_HEAD_DIM
    qmask = (jnp.where(lo, 1.0, 0.0).astype(MXU_DTYPE), jnp.where(lo, 0.0, 1.0).astype(MXU_DTYPE))

    def keys_values(g, rc, in_prev_tile, rp, lanes):
        if not in_prev_tile:
            return qkv_ref[g, 0, 0, rp:rp + 2 * ATT_BLK, lanes]
        return jnp.concatenate([prevq_ref[g, 0, 0, rp:rp + ATT_BLK, lanes],
                                qkv_ref[g, 0, 0, rc:rc + ATT_BLK, lanes]], axis=0)

    def scores(g, rc, in_prev_tile, rp):
        q = qkv_ref[g, 0, 0, rc:rc + ATT_BLK, 0:LANES]
        k = keys_values(g, rc, in_prev_tile, rp, slice(LANES, 2 * LANES))
        bias = biasp if in_prev_tile else bias_ref
        return [_mm_nt(q * qmask[hh], k) + bias[g, hh] for hh in range(2)]

    def attend(g, rc, in_prev_tile, rp, s):
        v = keys_values(g, rc, in_prev_tile, rp, slice(2 * LANES, 3 * LANES))
        accs, dens, ms = [], [], []
        for sh in s:
            m = jnp.max(sh, axis=-1, keepdims=True)
            p = jnp.exp2(sh - m)
            dens.append(jnp.sum(p, axis=-1, keepdims=True))
            accs.append(_mm(p.astype(MXU_DTYPE), v))
            ms.append(m)
        acc = jnp.where(lo, accs[0], accs[1])
        return (acc, jnp.broadcast_to(jnp.where(lo, ms[0], ms[1]), acc.shape),
                jnp.broadcast_to(jnp.where(lo, dens[0], dens[1]), acc.shape))

    def store(a_ref, m_ref, d_ref, rows):
        def post(acc, m, den):
            a_ref[rows, :] = acc
            m_ref[rows, :] = m
            d_ref[rows, :] = den
        return post

    def merge(rows):
        def post(acc0, m0, den0):
            m1, m2 = l1[rows, :], l2[rows, :]
            mx = jnp.maximum(jnp.maximum(m0, m1), m2)
            e0, e1, e2 = jnp.exp2(m0 - mx), jnp.exp2(m1 - mx), jnp.exp2(m2 - mx)
            num = e0 * acc0 + e1 * o1[rows, :] + e2 * o2[rows, :]
            den = e0 * den0 + e1 * n1[rows, :] + e2 * n2[rows, :]
            out_ref[0, rows, :] = (num * (1.0 / den)).astype(out_ref.dtype)
        return post

    units = []
    d2 = A_CONFIGS[2][1]
    for u in range(ATT_UNITS):
        units.append((2, u * ATT_BLK, True, u * ATT_BLK, store(o2, l2, n2, pl.ds(u, ATT_BLK, stride=d2))))
    d1 = A_CONFIGS[1][1]
    per = ATT_UNITS // d1
    for u in range(ATT_UNITS):
        r, q = divmod(u, per)
        rc = u * ATT_BLK
        post = store(o1, l1, n1, pl.ds(q * ATT_BLK * d1 + r, ATT_BLK, stride=d1))
        if q == 0:
            units.append((1, rc, True, (r * per + per - 1) * ATT_BLK, post))
        else:
            units.append((1, rc, False, rc - ATT_BLK, post))
    for u in range(ATT_UNITS):
        rc = u * ATT_BLK
        post = merge(slice(rc, rc + ATT_BLK))
        units.append((0, rc, True, ATT_TILE - ATT_BLK, post) if u == 0 else (0, rc, False, rc - ATT_BLK, post))

    pending = [scores(*un[:4]) for un in units[:ATT_PIPELINE]]
    for idx, un in enumerate(units):
        if idx + ATT_PIPELINE < len(units):
            pending.append(scores(*units[idx + ATT_PIPELINE][:4]))
        un[4](*attend(*un[:4], pending.pop(0)))


def _attention(qkv, bias):
    _, bsz, _, s, _ = qkv.shape
    return pl.pallas_call(
        _attn_kernel,
        grid=(bsz, N_PAIRS, s // ATT_TILE),
        in_specs=[
            pl.BlockSpec((A_GROUPS, 1, 1, ATT_TILE, PAIR_W), lambda b, p, i: (0, b, p, i, 0)),
            pl.BlockSpec((A_GROUPS, 1, 1, ATT_TILE, PAIR_W), lambda b, p, i: (0, b, p, jnp.maximum(i - 1, 0), 0)),
            pl.BlockSpec((A_GROUPS, 2, ATT_BLK, 2 * ATT_BLK), lambda b, p, i: (0, p, 0, 0)),
        ],
        out_specs=pl.BlockSpec((1, ATT_TILE, LANES), lambda b, p, i: (b, i, p)),
        out_shape=jax.ShapeDtypeStruct((bsz, s, A_HEADS * A_HEAD_DIM), MXU_DTYPE),
        scratch_shapes=[
            pltpu.VMEM((ATT_TILE, LANES), f32),
            pltpu.VMEM((ATT_TILE, LANES), f32),
            pltpu.VMEM((ATT_TILE, LANES), f32),
            pltpu.VMEM((ATT_TILE, LANES), f32),
            pltpu.VMEM((ATT_TILE, LANES), f32),
            pltpu.VMEM((ATT_TILE, LANES), f32),
            pltpu.VMEM((A_GROUPS, 2, ATT_BLK, 2 * ATT_BLK), f32),
        ],
        compiler_params=pltpu.CompilerParams(
            dimension_semantics=("parallel", "parallel", "arbitrary"),
            vmem_limit_bytes=VMEM_LIMIT),
        name="dilated_attn",
    )(qkv, qkv, bias)


def _ffn_kernel(*refs, final, mixer):
    refs = list(refs)
    x_ref, halo_ref = refs[:2]
    del refs[:2]
    if mixer:
        o_ref, ohalo_ref, wout_ref = refs[:3]
        del refs[:3]
    mod_ref, gam_ref, wup_ref, cw_ref, cb_ref, wdn_ref = refs[:6]
    del refs[:6]
    if final:
        gfin_ref = refs.pop(0)
    out_ref, h_ref, u0_ref, u1_ref, act_ref = refs[:5]
    if mixer:
        x1_ref = refs[5]
    u_refs = (u0_ref, u1_ref)
    gam, scale, shift = gam_ref[...], mod_ref[0, 4:5, :], mod_ref[0, 3:4, :]
    d_ff = wdn_ref.shape[0]
    nchunk = d_ff // FFN_TF
    tm = x_ref.shape[1]

    if mixer:
        x1 = x_ref[0] + mod_ref[0, 2:3, :] * _mm(o_ref[0], wout_ref[...])
        x1_ref[...] = x1
        halo = halo_ref[0] + mod_ref[0, 2:3, :] * _mm(ohalo_ref[0], wout_ref[...])[O_HALO - HALO:, :]
    else:
        x1, halo = x_ref[0], halo_ref[0]
    halo = _modnorm(halo, gam, scale, shift)
    halo = jnp.where(pl.program_id(1) == 0, 0.0, halo)
    hp = _regroup(_modnorm(x1, gam, scale, shift), SUBLANES)
    for back in range(1, CONV_W):
        last = hp[tm - back * SUBLANES:tm - (back - 1) * SUBLANES, :]
        lead = jnp.concatenate([halo[HALO - back:HALO - back + 1, :], last[:SUBLANES - 1, :]], axis=0)
        h_ref[(CONV_W - 1 - back) * SUBLANES:(CONV_W - back) * SUBLANES, :] = lead.astype(h_ref.dtype)
    h_ref[LEAD:LEAD + tm, :] = hp.astype(h_ref.dtype)

    def halves(c):
        return [slice(half * d_ff + c * FFN_TF, half * d_ff + (c + 1) * FFN_TF) for half in range(2)]

    def up(c):
        for half, cols in enumerate(halves(c)):
            u_refs[c % 2][:, half * FFN_TF:(half + 1) * FFN_TF] = _mm(h_ref[...], wup_ref[:, cols])

    def conv_act(c):
        u_ref = u_refs[c % 2]
        for rb in range(tm // FFN_ROWS):
            ab = []
            for half, cols in enumerate(halves(c)):
                v = cb_ref[:, cols]
                for t in range(CONV_W):
                    off = rb * FFN_ROWS + t * SUBLANES
                    v = v + cw_ref[t:t + 1, cols] * u_ref[off:off + FFN_ROWS, half * FFN_TF:(half + 1) * FFN_TF]
                ab.append(v)
            act_ref[rb * FFN_ROWS:(rb + 1) * FFN_ROWS, c * FFN_TF:(c + 1) * FFN_TF] = (
                _silu(ab[0]) * ab[1]).astype(act_ref.dtype)

    up(0)
    for c in range(nchunk):
        if c + 1 < nchunk:
            up(c + 1)
        conv_act(c)

    ffn = _regroup(_mm(act_ref[...], wdn_ref[...]), tm // SUBLANES)
    y = (x1_ref[...] if mixer else x_ref[0]) + mod_ref[0, 5:6, :] * ffn
    if final:
        ms = jnp.mean(y * y, axis=-1, keepdims=True)
        y = y * lax.rsqrt(ms + EPS) * gfin_ref[...]
    out_ref[0] = y


def _conv_ffn(x, mod, gamma, w_up, conv_w, conv_b, w_down, gamma_final=None, mixer=None):
    bsz, s, d = x.shape
    d_ff = w_down.shape[0]
    final = gamma_final is not None
    rows = FFN_TM
    tiles_per_halo = rows // HALO
    resident = dict(pipeline_mode=pl.Buffered(1))
    in_specs = [
        pl.BlockSpec((1, rows, d), lambda b, i: (b, i, 0)),
        pl.BlockSpec((1, HALO, d), lambda b, i: (b, jnp.maximum(i * tiles_per_halo - 1, 0), 0)),
    ]
    args = [x, x]
    if mixer is not None:
        o, w_out = mixer
        k = o.shape[-1]
        in_specs += [
            pl.BlockSpec((1, rows, k), lambda b, i: (b, i, 0)),
            pl.BlockSpec((1, O_HALO, k), lambda b, i: (b, jnp.maximum(i * (rows // O_HALO) - 1, 0), 0)),
            pl.BlockSpec((k, d), lambda b, i: (0, 0), **resident),
        ]
        args += [o, o, w_out.astype(MXU_DTYPE)]
    in_specs += [
        pl.BlockSpec((1, 6, d), lambda b, i: (b, 0, 0)),
        pl.BlockSpec((1, d), lambda b, i: (0, 0)),
        pl.BlockSpec((d, 2 * d_ff), lambda b, i: (0, 0), **resident),
        pl.BlockSpec((CONV_W, 2 * d_ff), lambda b, i: (0, 0)),
        pl.BlockSpec((1, 2 * d_ff), lambda b, i: (0, 0)),
        pl.BlockSpec((d_ff, d), lambda b, i: (0, 0), **resident),
    ]
    args += [mod, gamma, w_up.astype(MXU_DTYPE), conv_w, conv_b[None, :], w_down.astype(MXU_DTYPE)]
    if final:
        in_specs.append(pl.BlockSpec((1, d), lambda b, i: (0, 0)))
        args.append(gamma_final)
    return pl.pallas_call(
        functools.partial(_ffn_kernel, final=final, mixer=mixer is not None),
        grid=(bsz, s // rows),
        in_specs=in_specs,
        out_specs=pl.BlockSpec((1, rows, d), lambda b, i: (b, i, 0)),
        out_shape=jax.ShapeDtypeStruct((bsz, s, d), f32),
        scratch_shapes=[
            pltpu.VMEM((rows + LEAD, d), MXU_DTYPE),
            pltpu.VMEM((rows + LEAD, 2 * FFN_TF), f32),
            pltpu.VMEM((rows + LEAD, 2 * FFN_TF), f32),
            pltpu.VMEM((rows, d_ff), MXU_DTYPE),
        ] + ([pltpu.VMEM((rows, d), f32)] if mixer is not None else []),
        compiler_params=pltpu.CompilerParams(
            dimension_semantics=("parallel", "parallel"), vmem_limit_bytes=VMEM_LIMIT),
        name="conv_ffn_final" if final else "conv_ffn",
    )(*args)


def _gla_kernel(x_ref, mod_ref, gam_ref, win_ref, wglr_ref, wgate_ref, bgate_ref, gnorm_ref, wout_ref,
                out_ref, h_ref, proj0_ref, proj1_ref, gk0_ref, gk1_ref, og0_ref, og1_ref, state_ref):
    @pl.when(pl.program_id(1) == 0)
    def _():
        state_ref[...] = jnp.zeros_like(state_ref)

    proj_refs, gk_refs, og_refs = (proj0_ref, proj1_ref), (gk0_ref, gk1_ref), (og0_ref, og1_ref)
    nsub = x_ref.shape[1] // GLA_TM
    nchunk = GLA_TM // B_CHUNK
    n_main = win_ref.shape[1]

    ri = lax.broadcasted_iota(jnp.int32, (B_CHUNK, B_CHUNK), 0)
    ci = lax.broadcasted_iota(jnp.int32, (B_CHUNK, B_CHUNK), 1)
    causal = ci <= ri
    tri = causal.astype(MXU_DTYPE)
    gnorm = gnorm_ref[...]
    qscale = B_DK ** -0.5
    r_off = 2 * B_QK + B_V
    heads = [slice(hd * B_DK, (hd + 1) * B_DK) for hd in range(B_HEADS)]

    def project_steps(k):
        rows = slice(k * GLA_TM, (k + 1) * GLA_TM)

        def normalize():
            h = _modnorm(x_ref[0, rows, :], gam_ref[...], mod_ref[0, 1:2, :], mod_ref[0, 0:1, :])
            h_ref[...] = h.astype(h_ref.dtype)

        def piece(cols):
            def run():
                proj_refs[k % 2][:, cols] = _mm(h_ref[...], win_ref[:, cols])
            return run

        def gate():
            glr = _mm(h_ref[...], wglr_ref[...]).astype(MXU_DTYPE)
            z = _mm(glr, wgate_ref[...]) + bgate_ref[...]
            gk_refs[k % 2][...] = (jnp.minimum(z, 0.0) - jnp.log1p(jnp.exp(-jnp.abs(z)))) * (1.0 / B_TAU)

        pieces = [piece(slice(c0, c0 + GLA_PROJ_COLS)) for c0 in range(0, n_main, GLA_PROJ_COLS)]
        return [normalize] + pieces + [gate]

    def local(k, c):
        proj_ref, gk_ref = proj_refs[k % 2], gk_refs[k % 2]
        rows = slice(c * B_CHUNK, (c + 1) * B_CHUNK)
        gk = gk_ref[rows, :]
        g_hi = gk.astype(MXU_DTYPE)
        rem = gk - g_hi.astype(f32)
        g_mid = rem.astype(MXU_DTYPE)
        g_lo = (rem - g_mid.astype(f32)).astype(MXU_DTYPE)
        bcum = _mm(tri, g_hi) + _mm(tri, g_mid) + _mm(tri, g_lo)
        blast = bcum[B_CHUNK - 1:B_CHUNK, :]
        q_t = ((proj_ref[rows, 0:B_QK] * qscale) * jnp.exp(bcum)).astype(MXU_DTYPE)
        kk = proj_ref[rows, B_QK:2 * B_QK]
        k_t = (kk * jnp.exp(-bcum)).astype(MXU_DTYPE)
        k_d = (kk * jnp.exp(blast - bcum)).astype(MXU_DTYPE)
        v = [proj_ref[rows, 2 * B_QK + hd * B_DV:2 * B_QK + (hd + 1) * B_DV].astype(MXU_DTYPE)
             for hd in range(B_HEADS)]
        a = [jnp.where(causal, _mm_nt(q_t[:, ks], k_t[:, ks]), 0.0).astype(MXU_DTYPE) for ks in heads]
        kv = [_mm_tn(k_d[:, ks], v[hd]) for hd, ks in enumerate(heads)]
        o_intra = [_mm(a[hd], v[hd]) for hd in range(B_HEADS)]
        return q_t, blast, kv, o_intra

    def recur(k, c, loc, state):
        q_t, blast, kv, o_intra = loc
        rows = slice(c * B_CHUNK, (c + 1) * B_CHUNK)
        new_state = []
        for hd, ks in enumerate(heads):
            o = o_intra[hd] + _mm(q_t[:, ks], state[hd].astype(MXU_DTYPE))
            decay = jnp.exp(jnp.broadcast_to(blast[:, ks], (B_DK, B_DK)).T)
            new_state.append(jnp.concatenate([decay] * (B_DV // B_DK), axis=1) * state[hd] + kv[hd])
            o = o * lax.rsqrt(jnp.mean(o * o, axis=-1, keepdims=True) + EPS) * gnorm
            r = proj_refs[k % 2][rows, r_off + hd * B_DV:r_off + (hd + 1) * B_DV]
            og_refs[k % 2][rows, hd * B_DV:(hd + 1) * B_DV] = (o * _silu(r)).astype(og_refs[k % 2].dtype)
        return new_state

    for step in project_steps(0):
        step()
    state = [state_ref[hd] for hd in range(B_HEADS)]
    for k in range(nsub):
        ahead = project_steps(k + 1) if k + 1 < nsub else []
        loc = local(k, 0)
        for c in range(nchunk):
            nxt = local(k, c + 1) if c + 1 < nchunk else None
            state = recur(k, c, loc, state)
            loc = nxt
            take = -(-len(ahead) // (nchunk - c))
            for step in ahead[:take]:
                step()
            ahead = ahead[take:]
        rows = slice(k * GLA_TM, (k + 1) * GLA_TM)
        out_ref[0, rows, :] = x_ref[0, rows, :] + mod_ref[0, 2:3, :] * _mm(og_refs[k % 2][...], wout_ref[...])
    for hd in range(B_HEADS):
        state_ref[hd] = state[hd]


def _gla_layer(x, mod, gamma, w_in, w_gate, b_gate, g_norm, w_out):
    bsz, s, d = x.shape
    glr0 = 2 * B_QK + B_V
    w_main = jnp.concatenate([w_in[:, :glr0], w_in[:, glr0 + B_GATE_RANK:]], axis=1).astype(MXU_DTYPE)
    w_glr = jnp.pad(w_in[:, glr0:glr0 + B_GATE_RANK], ((0, 0), (0, LANES - B_GATE_RANK))).astype(MXU_DTYPE)
    w_gate_p = jnp.pad(w_gate, ((0, LANES - B_GATE_RANK), (0, 0))).astype(MXU_DTYPE)
    n_main = w_main.shape[1]
    rows = GLA_SUB * GLA_TM
    const = lambda b, i: (0, 0)
    resident = dict(pipeline_mode=pl.Buffered(1))
    return pl.pallas_call(
        _gla_kernel,
        grid=(bsz, s // rows),
        in_specs=[
            pl.BlockSpec((1, rows, d), lambda b, i: (b, i, 0)),
            pl.BlockSpec((1, 6, d), lambda b, i: (b, 0, 0)),
            pl.BlockSpec((1, d), const),
            pl.BlockSpec((d, n_main), const, **resident),
            pl.BlockSpec((d, LANES), const),
            pl.BlockSpec((LANES, B_QK), const),
            pl.BlockSpec((1, B_QK), const),
            pl.BlockSpec((1, B_DV), const),
            pl.BlockSpec((B_V, d), const, **resident),
        ],
        out_specs=pl.BlockSpec((1, rows, d), lambda b, i: (b, i, 0)),
        out_shape=jax.ShapeDtypeStruct((bsz, s, d), f32),
        scratch_shapes=[
            pltpu.VMEM((GLA_TM, d), MXU_DTYPE),
            pltpu.VMEM((GLA_TM, n_main), f32),
            pltpu.VMEM((GLA_TM, n_main), f32),
            pltpu.VMEM((GLA_TM, B_QK), f32),
            pltpu.VMEM((GLA_TM, B_QK), f32),
            pltpu.VMEM((GLA_TM, B_V), MXU_DTYPE),
            pltpu.VMEM((GLA_TM, B_V), MXU_DTYPE),
            pltpu.VMEM((B_HEADS, B_DK, B_DV), f32),
        ],
        compiler_params=pltpu.CompilerParams(
            dimension_semantics=("parallel", "arbitrary"), vmem_limit_bytes=VMEM_LIMIT),
        name="gla_layer",
    )(x, mod, gamma, w_main, w_glr, w_gate_p, b_gate[None, :], g_norm[None, :], w_out.astype(MXU_DTYPE))


def _qkv_weight(w_in):
    width = A_HEADS * A_HEAD_DIM
    col_scale = jnp.tile(jnp.repeat(jnp.array([A_HEAD_DIM ** -0.5 * LOG2E, 1.0, 1.0], f32), width), A_GROUPS)
    return (w_in * col_scale[None, :]).astype(MXU_DTYPE)


def kernel(x, c, w_in_a, w_out_a, rel_bias, w_in_b, w_gate_b, b_gate_b, gnorm_b, w_out_b, norm_mix, norm_ffn, w_ada, b_ada, w_up, conv_w, conv_b, w_down, norm_final):
    depth = w_ada.shape[0]
    mod = _adaln(c, w_ada, b_ada)
    bias = _rel_bias(rel_bias)
    for i in range(depth):
        gam_mix = norm_mix[i][None, :]
        j = i // 2
        mixer = None
        if i % 2 == 0:
            qkv = _qkv_proj(x, mod[i], gam_mix, _qkv_weight(w_in_a[j]))
            mixer = (_attention(qkv, bias), w_out_a[j])
        else:
            x = _gla_layer(x, mod[i], gam_mix, w_in_b[j], w_gate_b[j], b_gate_b[j], gnorm_b[j], w_out_b[j])
        last = i == depth - 1
        x = _conv_ffn(x, mod[i], norm_ffn[i][None, :], w_up[i], conv_w[i], conv_b[i], w_down[i],
                      gamma_final=norm_final[None, :] if last else None, mixer=mixer)
    return x
```

```python
import functools
import math

import jax
import jax.numpy as jnp
from jax import lax
from jax.experimental import pallas as pl
from jax.experimental.pallas import tpu as pltpu

A_CONFIGS = ((128, 1), (512, 4), (2048, 16))
A_GROUPS = len(A_CONFIGS)
A_HEADS = 16
A_HEAD_DIM = 64
N_BUCKETS = 32
MAX_DISTANCE = 2048
B_HEADS = 4
B_DK = 128
B_DV = 256
B_QK = B_HEADS * B_DK
B_V = B_HEADS * B_DV
B_GATE_RANK = 16
B_TAU = 16.0
B_CHUNK = 64
CONV_W = 3
EPS = 1e-6
NEG_INF = -1e30
LOG2E = math.log2(math.e)

LANES = 128
SUBLANES = 8
MXU_DTYPE = jnp.bfloat16
ATT_BLK = 128
ATT_TILE = 2048
ATT_UNITS = ATT_TILE // ATT_BLK
ATT_PIPELINE = 3
PAIR_W = 3 * LANES
N_PAIRS = A_HEADS // 2
QKV_ROWS = 1024
PROBE_ROWS = 2 * SUBLANES
FFN_TM = 1024
FFN_TF = 256
FFN_ROWS = 256
HALO = SUBLANES
LEAD = (CONV_W - 1) * SUBLANES
O_HALO = 2 * SUBLANES
GLA_TM = 512
GLA_SUB = 2
GLA_PROJ_COLS = 256
VMEM_LIMIT = 56 * 1024 * 1024

f32 = jnp.float32


def _mm(a, b):
    return jnp.dot(a, b, preferred_element_type=f32)


def _mm_nt(a, b):
    return lax.dot_general(a, b, (((1,), (1,)), ((), ())), preferred_element_type=f32)


def _mm_tn(a, b):
    return lax.dot_general(a, b, (((0,), (0,)), ((), ())), preferred_element_type=f32)


def _modnorm(x, gamma, scale, shift):
    ms = jnp.mean(x * x, axis=-1, keepdims=True)
    y = x * lax.rsqrt(ms + EPS) * gamma
    return y * (1.0 + scale) + shift


def _silu(x):
    return x * (1.0 / (1.0 + jnp.exp(-x)))


def _regroup(t, groups):
    rows, d = t.shape
    return t.reshape(groups, rows // groups, d).swapaxes(0, 1).reshape(rows, d)


def _adaln_kernel(c_ref, w_ref, b_ref, o_ref):
    s = _silu(c_ref[...]).astype(MXU_DTYPE)
    o_ref[0] = _mm(s, w_ref[0].astype(MXU_DTYPE)) + b_ref[0]


def _adaln(c, w_ada, b_ada):
    depth, d, n = w_ada.shape
    bsz = c.shape[0]
    rows = 8 * pl.cdiv(bsz, 8)
    c_pad = jnp.pad(c, ((0, rows - bsz), (0, 0)))
    out = pl.pallas_call(
        _adaln_kernel,
        grid=(depth, n // d),
        in_specs=[
            pl.BlockSpec((rows, d), lambda l, j: (0, 0)),
            pl.BlockSpec((1, d, d), lambda l, j: (l, 0, j)),
            pl.BlockSpec((1, 1, d), lambda l, j: (l, 0, j)),
        ],
        out_specs=pl.BlockSpec((1, rows, d), lambda l, j: (l, 0, j)),
        out_shape=jax.ShapeDtypeStruct((depth, rows, n), f32),
        compiler_params=pltpu.CompilerParams(
            dimension_semantics=("parallel", "parallel"), vmem_limit_bytes=VMEM_LIMIT),
        name="adaln",
    )(c_pad, w_ada, b_ada.reshape(depth, 1, n))
    return out[:, :bsz].reshape(depth, bsz, n // d, d)


def _qkv_kernel(x_ref, mod_ref, gam_ref, wq_ref, wk_ref, wv_ref, o_ref, ha_ref, hb_ref, sa_ref, sb_ref, *,
                dilations, tiles_per_group):
    j = pl.program_id(2)
    g = j // tiles_per_group
    t = j % tiles_per_group
    nlb = sa_ref.shape[0]
    chunks_per_step = ATT_UNITS // tiles_per_group
    h_refs, stage_refs = [ha_ref, hb_ref], [sa_ref, sb_ref]

    @pl.when(j == 0)
    def _():
        for c in range(ATT_UNITS):
            rows = slice(c * ATT_BLK, (c + 1) * ATT_BLK)
            hn = _modnorm(x_ref[0, rows, :], gam_ref[...], mod_ref[0, 1:2, :], mod_ref[0, 0:1, :])
            ha_ref[rows, :] = hn.astype(ha_ref.dtype)
            for lb in range(nlb):
                sa_ref[lb, rows, :] = hn[:, lb * LANES:(lb + 1) * LANES]

    def deinterleave_chunk(gi, cc):
        d, dp = dilations[gi], dilations[gi - 1]
        rel = d // dp
        per = ATT_UNITS // d
        c = t * chunks_per_step + cc
        r, q = c // per, c % per
        start = (r % dp) * (ATT_TILE // dp) + q * (ATT_BLK * rel) + r // dp
        rows = pl.ds(start, ATT_BLK, stride=rel)
        r0 = pl.multiple_of(c * ATT_BLK, ATT_BLK)
        parts = [stage_refs[(gi - 1) % 2][lb, rows, :] for lb in range(nlb)]
        if gi + 1 < len(dilations):
            for lb in range(nlb):
                stage_refs[gi % 2][lb, pl.ds(r0, ATT_BLK), :] = parts[lb]
        dst_ref = h_refs[gi % 2]
        dst_ref[pl.ds(r0, ATT_BLK), :] = jnp.concatenate(parts, axis=1).astype(dst_ref.dtype)
        probe = dst_ref[pl.ds(r0, PROBE_ROWS), 0:LANES]
        return (probe != probe) & (probe == probe)

    def project_chunk(src_ref, rc, never):
        rows = slice(rc * QKV_ROWS, (rc + 1) * QKV_ROWS)
        lhs = src_ref[rows, :]
        for k, w_ref in enumerate((wq_ref, wk_ref, wv_ref)):
            res = _mm(lhs, w_ref[...]).astype(o_ref.dtype)
            lanes = slice(k * LANES, (k + 1) * LANES)
            o_ref[0, 0, 1, rows, lanes] = res[:, LANES:]
            if never is not None and k == 0:
                top = rc * QKV_ROWS + PROBE_ROWS
                o_ref[0, 0, 0, rc * QKV_ROWS:top, lanes] = jnp.where(
                    never, jnp.zeros_like(res[:PROBE_ROWS, :LANES]), res[:PROBE_ROWS, :LANES])
                o_ref[0, 0, 0, top:(rc + 1) * QKV_ROWS, lanes] = res[PROBE_ROWS:, :LANES]
            else:
                o_ref[0, 0, 0, rows, lanes] = res[:, :LANES]

    assert dilations[0] == 1
    assert all(d % dp == 0 for dp, d in zip(dilations, dilations[1:]))
    n_mm = ATT_TILE // QKV_ROWS
    for gi in range(len(dilations)):
        @pl.when(g == gi)
        def _(gi=gi):
            shares = [list(range(chunks_per_step))[sl::n_mm] for sl in range(n_mm)]
            for rc in range(n_mm):
                never = None
                if gi + 1 < len(dilations):
                    for cc in shares[rc]:
                        probe = deinterleave_chunk(gi + 1, cc)
                        never = probe if never is None else never | probe
                project_chunk(h_refs[gi % 2], rc, never)


def _qkv_proj(x, mod, gamma, w):
    bsz, s, d = x.shape
    dilations = tuple(dl for _, dl in A_CONFIGS)
    tn = 2 * LANES
    tiles_per_group = A_HEADS * A_HEAD_DIM // tn
    kern = functools.partial(_qkv_kernel, dilations=dilations, tiles_per_group=tiles_per_group)

    def w_spec(k):
        return pl.BlockSpec(
            (d, tn), lambda b, i, j: (0, (j // tiles_per_group * 3 + k) * tiles_per_group + j % tiles_per_group))

    return pl.pallas_call(
        kern,
        grid=(bsz, s // ATT_TILE, A_GROUPS * tiles_per_group),
        in_specs=[
            pl.BlockSpec((1, ATT_TILE, d), lambda b, i, j: (b, i, 0)),
            pl.BlockSpec((1, 6, d), lambda b, i, j: (b, 0, 0)),
            pl.BlockSpec((1, d), lambda b, i, j: (0, 0)),
            w_spec(0), w_spec(1), w_spec(2),
        ],
        out_specs=pl.BlockSpec(
            (1, 1, 2, ATT_TILE, PAIR_W),
            lambda b, i, j: (j // tiles_per_group, b, j % tiles_per_group, i, 0)),
        out_shape=jax.ShapeDtypeStruct((A_GROUPS, bsz, N_PAIRS, s, PAIR_W), MXU_DTYPE),
        scratch_shapes=[pltpu.VMEM((ATT_TILE, d), MXU_DTYPE),
                        pltpu.VMEM((ATT_TILE, d), MXU_DTYPE),
                        pltpu.VMEM((d // LANES, ATT_TILE, LANES), f32),
                        pltpu.VMEM((d // LANES, ATT_TILE, LANES), f32)],
        compiler_params=pltpu.CompilerParams(
            dimension_semantics=("parallel", "parallel", "arbitrary"),
            vmem_limit_bytes=VMEM_LIMIT),
        name="qkv_proj",
    )(x, mod, gamma, w, w, w)


def _bias_kernel(tab_ref, bkt_ref, o_ref):
    bkt = bkt_ref[0]
    hits = [(bkt >= k) & (bkt < k + 1) for k in range(N_BUCKETS)]
    for h in range(A_HEADS):
        col = pl.program_id(0) * A_HEADS + h
        acc = jnp.full(bkt.shape, NEG_INF, f32)
        for k in range(N_BUCKETS):
            acc = jnp.where(hits[k], tab_ref[k, col] * LOG2E, acc)
        o_ref[0, h] = acc


def _t5_bucket(dist):
    max_exact = N_BUCKETS // 2
    n = jnp.maximum(dist, max_exact).astype(f32)
    large = max_exact + (jnp.log(n / max_exact) / math.log(MAX_DISTANCE / max_exact)
                         * (N_BUCKETS - max_exact)).astype(jnp.int32)
    large = jnp.minimum(large, N_BUCKETS - 1)
    return jnp.where(dist < max_exact, dist, large)


def _rel_bias(rel_bias):
    qi = jnp.arange(ATT_BLK)[:, None]
    ki = jnp.arange(2 * ATT_BLK)[None, :]
    steps = qi + ATT_BLK - ki
    band = (steps >= 0) & (steps <= ATT_BLK)
    bucket = jnp.stack([
        jnp.where(band, _t5_bucket(jnp.clip(steps, 0, ATT_BLK) * dl), -1)
        for _, dl in A_CONFIGS]).astype(jnp.int32)
    return pl.pallas_call(
        _bias_kernel,
        grid=(A_GROUPS,),
        in_specs=[
            pl.BlockSpec(memory_space=pltpu.SMEM),
            pl.BlockSpec((1, ATT_BLK, 2 * ATT_BLK), lambda g: (g, 0, 0)),
        ],
        out_specs=pl.BlockSpec((1, A_HEADS, ATT_BLK, 2 * ATT_BLK), lambda g: (g, 0, 0, 0)),
        out_shape=jax.ShapeDtypeStruct((A_GROUPS, A_HEADS, ATT_BLK, 2 * ATT_BLK), f32),
        name="rel_bias",
    )(rel_bias, bucket)


def _attn_kernel(qkv_ref, prevq_ref, bias_ref, out_ref, o1, l1, n1, o2, l2, n2, biasp):
    first = pl.program_id(2) == 0

    for g in range(A_GROUPS):
        for hh in range(2):
            biasp[g, hh, :, 0:ATT_BLK] = jnp.where(first, NEG_INF, bias_ref[g, hh, :, 0:ATT_BLK])
            biasp[g, hh, :, ATT_BLK:2 * ATT_BLK] = bias_ref[g, hh, :, ATT_BLK:2 * ATT_BLK]

    lane = lax.broadcasted_iota(jnp.int32, (1, LANES), 1)
    lo = lane < A_HEAD_DIM
    qmask = (jnp.where(lo, 1.0, 0.0).astype(MXU_DTYPE), jnp.where(lo, 0.0, 1.0).astype(MXU_DTYPE))

    def keys_values(g, rc, in_prev_tile, rp, lanes):
        if not in_prev_tile:
            return qkv_ref[g, 0, 0, rp:rp + 2 * ATT_BLK, lanes]
        return jnp.concatenate([prevq_ref[g, 0, 0, rp:rp + ATT_BLK, lanes],
                                qkv_ref[g, 0, 0, rc:rc + ATT_BLK, lanes]], axis=0)

    def scores(g, rc, in_prev_tile, rp):
        q = qkv_ref[g, 0, 0, rc:rc + ATT_BLK, 0:LANES]
        k = keys_values(g, rc, in_prev_tile, rp, slice(LANES, 2 * LANES))
        bias = biasp if in_prev_tile else bias_ref
        return [_mm_nt(q * qmask[hh], k) + bias[g, hh] for hh in range(2)]

    def attend(g, rc, in_prev_tile, rp, s):
        v = keys_values(g, rc, in_prev_tile, rp, slice(2 * LANES, 3 * LANES))
        accs, dens, ms = [], [], []
        for sh in s:
            m = jnp.max(sh, axis=-1, keepdims=True)
            p = jnp.exp2(sh - m)
            dens.append(jnp.sum(p, axis=-1, keepdims=True))
            accs.append(_mm(p.astype(MXU_DTYPE), v))
            ms.append(m)
        acc = jnp.where(lo, accs[0], accs[1])
        return (acc, jnp.broadcast_to(jnp.where(lo, ms[0], ms[1]), acc.shape),
                jnp.broadcast_to(jnp.where(lo, dens[0], dens[1]), acc.shape))

    def store(a_ref, m_ref, d_ref, rows):
        def post(acc, m, den):
            a_ref[rows, :] = acc
            m_ref[rows, :] = m
            d_ref[rows, :] = den
        return post

    def merge(rows):
        def post(acc0, m0, den0):
            m1, m2 = l1[rows, :], l2[rows, :]
            mx = jnp.maximum(jnp.maximum(m0, m1), m2)
            e0, e1, e2 = jnp.exp2(m0 - mx), jnp.exp2(m1 - mx), jnp.exp2(m2 - mx)
            num = e0 * acc0 + e1 * o1[rows, :] + e2 * o2[rows, :]
            den = e0 * den0 + e1 * n1[rows, :] + e2 * n2[rows, :]
            out_ref[0, rows, :] = (num * (1.0 / den)).astype(out_ref.dtype)
        return post

    units = []
    d2 = A_CONFIGS[2][1]
    for u in range(ATT_UNITS):
        units.append((2, u * ATT_BLK, True, u * ATT_BLK, store(o2, l2, n2, pl.ds(u, ATT_BLK, stride=d2))))
    d1 = A_CONFIGS[1][1]
    per = ATT_UNITS // d1
    for u in range(ATT_UNITS):
        r, q = divmod(u, per)
        rc = u * ATT_BLK
        post = store(o1, l1, n1, pl.ds(q * ATT_BLK * d1 + r, ATT_BLK, stride=d1))
        if q == 0:
            units.append((1, rc, True, (r * per + per - 1) * ATT_BLK, post))
        else:
            units.append((1, rc, False, rc - ATT_BLK, post))
    for u in range(ATT_UNITS):
        rc = u * ATT_BLK
        post = merge(slice(rc, rc + ATT_BLK))
        units.append((0, rc, True, ATT_TILE - ATT_BLK, post) if u == 0 else (0, rc, False, rc - ATT_BLK, post))

    pending = [scores(*un[:4]) for un in units[:ATT_PIPELINE]]
    for idx, un in enumerate(units):
        if idx + ATT_PIPELINE < len(units):
            pending.append(scores(*units[idx + ATT_PIPELINE][:4]))
        un[4](*attend(*un[:4], pending.pop(0)))


def _attention(qkv, bias):
    _, bsz, _, s, _ = qkv.shape
    return pl.pallas_call(
        _attn_kernel,
        grid=(bsz, N_PAIRS, s // ATT_TILE),
        in_specs=[
            pl.BlockSpec((A_GROUPS, 1, 1, ATT_TILE, PAIR_W), lambda b, p, i: (0, b, p, i, 0)),
            pl.BlockSpec((A_GROUPS, 1, 1, ATT_TILE, PAIR_W), lambda b, p, i: (0, b, p, jnp.maximum(i - 1, 0), 0)),
            pl.BlockSpec((A_GROUPS, 2, ATT_BLK, 2 * ATT_BLK), lambda b, p, i: (0, p, 0, 0)),
        ],
        out_specs=pl.BlockSpec((1, ATT_TILE, LANES), lambda b, p, i: (b, i, p)),
        out_shape=jax.ShapeDtypeStruct((bsz, s, A_HEADS * A_HEAD_DIM), MXU_DTYPE),
        scratch_shapes=[
            pltpu.VMEM((ATT_TILE, LANES), f32),
            pltpu.VMEM((ATT_TILE, LANES), f32),
            pltpu.VMEM((ATT_TILE, LANES), f32),
            pltpu.VMEM((ATT_TILE, LANES), f32),
            pltpu.VMEM((ATT_TILE, LANES), f32),
            pltpu.VMEM((ATT_TILE, LANES), f32),
            pltpu.VMEM((A_GROUPS, 2, ATT_BLK, 2 * ATT_BLK), f32),
        ],
        compiler_params=pltpu.CompilerParams(
            dimension_semantics=("parallel", "parallel", "arbitrary"),
            vmem_limit_bytes=VMEM_LIMIT),
        name="dilated_attn",
    )(qkv, qkv, bias)


def _ffn_kernel(*refs, final, mixer):
    refs = list(refs)
    x_ref, halo_ref = refs[:2]
    del refs[:2]
    if mixer:
        o_ref, ohalo_ref, wout_ref = refs[:3]
        del refs[:3]
    mod_ref, gam_ref, wup_ref, cw_ref, cb_ref, wdn_ref = refs[:6]
    del refs[:6]
    if final:
        gfin_ref = refs.pop(0)
    out_ref, h_ref, u0_ref, u1_ref, act_ref = refs[:5]
    if mixer:
        x1_ref = refs[5]
    u_refs = (u0_ref, u1_ref)
    gam, scale, shift = gam_ref[...], mod_ref[0, 4:5, :], mod_ref[0, 3:4, :]
    d_ff = wdn_ref.shape[0]
    nchunk = d_ff // FFN_TF
    tm = x_ref.shape[1]

    if mixer:
        x1 = x_ref[0] + mod_ref[0, 2:3, :] * _mm(o_ref[0], wout_ref[...])
        x1_ref[...] = x1
        halo = halo_ref[0] + mod_ref[0, 2:3, :] * _mm(ohalo_ref[0], wout_ref[...])[O_HALO - HALO:, :]
    else:
        x1, halo = x_ref[0], halo_ref[0]
    halo = _modnorm(halo, gam, scale, shift)
    halo = jnp.where(pl.program_id(1) == 0, 0.0, halo)
    hp = _regroup(_modnorm(x1, gam, scale, shift), SUBLANES)
    for back in range(1, CONV_W):
        last = hp[tm - back * SUBLANES:tm - (back - 1) * SUBLANES, :]
        lead = jnp.concatenate([halo[HALO - back:HALO - back + 1, :], last[:SUBLANES - 1, :]], axis=0)
        h_ref[(CONV_W - 1 - back) * SUBLANES:(CONV_W - back) * SUBLANES, :] = lead.astype(h_ref.dtype)
    h_ref[LEAD:LEAD + tm, :] = hp.astype(h_ref.dtype)

    def halves(c):
        return [slice(half * d_ff + c * FFN_TF, half * d_ff + (c + 1) * FFN_TF) for half in range(2)]

    def up(c):
        for half, cols in enumerate(halves(c)):
            u_refs[c % 2][:, half * FFN_TF:(half + 1) * FFN_TF] = _mm(h_ref[...], wup_ref[:, cols])

    def conv_act(c):
        u_ref = u_refs[c % 2]
        for rb in range(tm // FFN_ROWS):
            ab = []
            for half, cols in enumerate(halves(c)):
                v = cb_ref[:, cols]
                for t in range(CONV_W):
                    off = rb * FFN_ROWS + t * SUBLANES
                    v = v + cw_ref[t:t + 1, cols] * u_ref[off:off + FFN_ROWS, half * FFN_TF:(half + 1) * FFN_TF]
                ab.append(v)
            act_ref[rb * FFN_ROWS:(rb + 1) * FFN_ROWS, c * FFN_TF:(c + 1) * FFN_TF] = (
                _silu(ab[0]) * ab[1]).astype(act_ref.dtype)

    up(0)
    for c in range(nchunk):
        if c + 1 < nchunk:
            up(c + 1)
        conv_act(c)

    ffn = _regroup(_mm(act_ref[...], wdn_ref[...]), tm // SUBLANES)
    y = (x1_ref[...] if mixer else x_ref[0]) + mod_ref[0, 5:6, :] * ffn
    if final:
        ms = jnp.mean(y * y, axis=-1, keepdims=True)
        y = y * lax.rsqrt(ms + EPS) * gfin_ref[...]
    out_ref[0] = y


def _conv_ffn(x, mod, gamma, w_up, conv_w, conv_b, w_down, gamma_final=None, mixer=None):
    bsz, s, d = x.shape
    d_ff = w_down.shape[0]
    final = gamma_final is not None
    rows = FFN_TM
    tiles_per_halo = rows // HALO
    resident = dict(pipeline_mode=pl.Buffered(1))
    in_specs = [
        pl.BlockSpec((1, rows, d), lambda b, i: (b, i, 0)),
        pl.BlockSpec((1, HALO, d), lambda b, i: (b, jnp.maximum(i * tiles_per_halo - 1, 0), 0)),
    ]
    args = [x, x]
    if mixer is not None:
        o, w_out = mixer
        k = o.shape[-1]
        in_specs += [
            pl.BlockSpec((1, rows, k), lambda b, i: (b, i, 0)),
            pl.BlockSpec((1, O_HALO, k), lambda b, i: (b, jnp.maximum(i * (rows // O_HALO) - 1, 0), 0)),
            pl.BlockSpec((k, d), lambda b, i: (0, 0), **resident),
        ]
        args += [o, o, w_out.astype(MXU_DTYPE)]
    in_specs += [
        pl.BlockSpec((1, 6, d), lambda b, i: (b, 0, 0)),
        pl.BlockSpec((1, d), lambda b, i: (0, 0)),
        pl.BlockSpec((d, 2 * d_ff), lambda b, i: (0, 0), **resident),
        pl.BlockSpec((CONV_W, 2 * d_ff), lambda b, i: (0, 0)),
        pl.BlockSpec((1, 2 * d_ff), lambda b, i: (0, 0)),
        pl.BlockSpec((d_ff, d), lambda b, i: (0, 0), **resident),
    ]
    args += [mod, gamma, w_up.astype(MXU_DTYPE), conv_w, conv_b[None, :], w_down.astype(MXU_DTYPE)]
    if final:
        in_specs.append(pl.BlockSpec((1, d), lambda b, i: (0, 0)))
        args.append(gamma_final)
    return pl.pallas_call(
        functools.partial(_ffn_kernel, final=final, mixer=mixer is not None),
        grid=(bsz, s // rows),
        in_specs=in_specs,
        out_specs=pl.BlockSpec((1, rows, d), lambda b, i: (b, i, 0)),
        out_shape=jax.ShapeDtypeStruct((bsz, s, d), f32),
        scratch_shapes=[
            pltpu.VMEM((rows + LEAD, d), MXU_DTYPE),
            pltpu.VMEM((rows + LEAD, 2 * FFN_TF), f32),
            pltpu.VMEM((rows + LEAD, 2 * FFN_TF), f32),
            pltpu.VMEM((rows, d_ff), MXU_DTYPE),
        ] + ([pltpu.VMEM((rows, d), f32)] if mixer is not None else []),
        compiler_params=pltpu.CompilerParams(
            dimension_semantics=("parallel", "parallel"), vmem_limit_bytes=VMEM_LIMIT),
        name="conv_ffn_final" if final else "conv_ffn",
    )(*args)


def _gla_kernel(x_ref, mod_ref, gam_ref, win_ref, wglr_ref, wgate_ref, bgate_ref, gnorm_ref, wout_ref,
                out_ref, h_ref, proj0_ref, proj1_ref, gk0_ref, gk1_ref, og0_ref, og1_ref, state_ref):
    @pl.when(pl.program_id(1) == 0)
    def _():
        state_ref[...] = jnp.zeros_like(state_ref)

    proj_refs, gk_refs, og_refs = (proj0_ref, proj1_ref), (gk0_ref, gk1_ref), (og0_ref, og1_ref)
    nsub = x_ref.shape[1] // GLA_TM
    nchunk = GLA_TM // B_CHUNK
    n_main = win_ref.shape[1]

    ri = lax.broadcasted_iota(jnp.int32, (B_CHUNK, B_CHUNK), 0)
    ci = lax.broadcasted_iota(jnp.int32, (B_CHUNK, B_CHUNK), 1)
    causal = ci <= ri
    tri = causal.astype(MXU_DTYPE)
    gnorm = gnorm_ref[...]
    qscale = B_DK ** -0.5
    r_off = 2 * B_QK + B_V
    heads = [slice(hd * B_DK, (hd + 1) * B_DK) for hd in range(B_HEADS)]

    def project_steps(k):
        rows = slice(k * GLA_TM, (k + 1) * GLA_TM)

        def normalize():
            h = _modnorm(x_ref[0, rows, :], gam_ref[...], mod_ref[0, 1:2, :], mod_ref[0, 0:1, :])
            h_ref[...] = h.astype(h_ref.dtype)

        def piece(cols):
            def run():
                proj_refs[k % 2][:, cols] = _mm(h_ref[...], win_ref[:, cols])
            return run

        def gate():
            glr = _mm(h_ref[...], wglr_ref[...]).astype(MXU_DTYPE)
            z = _mm(glr, wgate_ref[...]) + bgate_ref[...]
            gk_refs[k % 2][...] = (jnp.minimum(z, 0.0) - jnp.log1p(jnp.exp(-jnp.abs(z)))) * (1.0 / B_TAU)

        pieces = [piece(slice(c0, c0 + GLA_PROJ_COLS)) for c0 in range(0, n_main, GLA_PROJ_COLS)]
        return [normalize] + pieces + [gate]

    def local(k, c):
        proj_ref, gk_ref = proj_refs[k % 2], gk_refs[k % 2]
        rows = slice(c * B_CHUNK, (c + 1) * B_CHUNK)
        gk = gk_ref[rows, :]
        g_hi = gk.astype(MXU_DTYPE)
        rem = gk - g_hi.astype(f32)
        g_mid = rem.astype(MXU_DTYPE)
        g_lo = (rem - g_mid.astype(f32)).astype(MXU_DTYPE)
        bcum = _mm(tri, g_hi) + _mm(tri, g_mid) + _mm(tri, g_lo)
        blast = bcum[B_CHUNK - 1:B_CHUNK, :]
        q_t = ((proj_ref[rows, 0:B_QK] * qscale) * jnp.exp(bcum)).astype(MXU_DTYPE)
        kk = proj_ref[rows, B_QK:2 * B_QK]
        k_t = (kk * jnp.exp(-bcum)).astype(MXU_DTYPE)
        k_d = (kk * jnp.exp(blast - bcum)).astype(MXU_DTYPE)
        v = [proj_ref[rows, 2 * B_QK + hd * B_DV:2 * B_QK + (hd + 1) * B_DV].astype(MXU_DTYPE)
             for hd in range(B_HEADS)]
        a = [jnp.where(causal, _mm_nt(q_t[:, ks], k_t[:, ks]), 0.0).astype(MXU_DTYPE) for ks in heads]
        kv = [_mm_tn(k_d[:, ks], v[hd]) for hd, ks in enumerate(heads)]
        o_intra = [_mm(a[hd], v[hd]) for hd in range(B_HEADS)]
        return q_t, blast, kv, o_intra

    def recur(k, c, loc, state):
        q_t, blast, kv, o_intra = loc
        rows = slice(c * B_CHUNK, (c + 1) * B_CHUNK)
        new_state = []
        for hd, ks in enumerate(heads):
            o = o_intra[hd] + _mm(q_t[:, ks], state[hd].astype(MXU_DTYPE))
            decay = jnp.exp(jnp.broadcast_to(blast[:, ks], (B_DK, B_DK)).T)
            new_state.append(jnp.concatenate([decay] * (B_DV // B_DK), axis=1) * state[hd] + kv[hd])
            o = o * lax.rsqrt(jnp.mean(o * o, axis=-1, keepdims=True) + EPS) * gnorm
            r = proj_refs[k % 2][rows, r_off + hd * B_DV:r_off + (hd + 1) * B_DV]
            og_refs[k % 2][rows, hd * B_DV:(hd + 1) * B_DV] = (o * _silu(r)).astype(og_refs[k % 2].dtype)
        return new_state

    for step in project_steps(0):
        step()
    state = [state_ref[hd] for hd in range(B_HEADS)]
    for k in range(nsub):
        ahead = project_steps(k + 1) if k + 1 < nsub else []
        loc = local(k, 0)
        for c in range(nchunk):
            nxt = local(k, c + 1) if c + 1 < nchunk else None
            state = recur(k, c, loc, state)
            loc = nxt
            take = -(-len(ahead) // (nchunk - c))
            for step in ahead[:take]:
                step()
            ahead = ahead[take:]
        rows = slice(k * GLA_TM, (k + 1) * GLA_TM)
        out_ref[0, rows, :] = x_ref[0, rows, :] + mod_ref[0, 2:3, :] * _mm(og_refs[k % 2][...], wout_ref[...])
    for hd in range(B_HEADS):
        state_ref[hd] = state[hd]


def _gla_layer(x, mod, gamma, w_in, w_gate, b_gate, g_norm, w_out):
    bsz, s, d = x.shape
    glr0 = 2 * B_QK + B_V
    w_main = jnp.concatenate([w_in[:, :glr0], w_in[:, glr0 + B_GATE_RANK:]], axis=1).astype(MXU_DTYPE)
    w_glr = jnp.pad(w_in[:, glr0:glr0 + B_GATE_RANK], ((0, 0), (0, LANES - B_GATE_RANK))).astype(MXU_DTYPE)
    w_gate_p = jnp.pad(w_gate, ((0, LANES - B_GATE_RANK), (0, 0))).astype(MXU_DTYPE)
    n_main = w_main.shape[1]
    rows = GLA_SUB * GLA_TM
    const = lambda b, i: (0, 0)
    resident = dict(pipeline_mode=pl.Buffered(1))
    return pl.pallas_call(
        _gla_kernel,
        grid=(bsz, s // rows),
        in_specs=[
            pl.BlockSpec((1, rows, d), lambda b, i: (b, i, 0)),
            pl.BlockSpec((1, 6, d), lambda b, i: (b, 0, 0)),
            pl.BlockSpec((1, d), const),
            pl.BlockSpec((d, n_main), const, **resident),
            pl.BlockSpec((d, LANES), const),
            pl.BlockSpec((LANES, B_QK), const),
            pl.BlockSpec((1, B_QK), const),
            pl.BlockSpec((1, B_DV), const),
            pl.BlockSpec((B_V, d), const, **resident),
        ],
        out_specs=pl.BlockSpec((1, rows, d), lambda b, i: (b, i, 0)),
        out_shape=jax.ShapeDtypeStruct((bsz, s, d), f32),
        scratch_shapes=[
            pltpu.VMEM((GLA_TM, d), MXU_DTYPE),
            pltpu.VMEM((GLA_TM, n_main), f32),
            pltpu.VMEM((GLA_TM, n_main), f32),
            pltpu.VMEM((GLA_TM, B_QK), f32),
            pltpu.VMEM((GLA_TM, B_QK), f32),
            pltpu.VMEM((GLA_TM, B_V), MXU_DTYPE),
            pltpu.VMEM((GLA_TM, B_V), MXU_DTYPE),
            pltpu.VMEM((B_HEADS, B_DK, B_DV), f32),
        ],
        compiler_params=pltpu.CompilerParams(
            dimension_semantics=("parallel", "arbitrary"), vmem_limit_bytes=VMEM_LIMIT),
        name="gla_layer",
    )(x, mod, gamma, w_main, w_glr, w_gate_p, b_gate[None, :], g_norm[None, :], w_out.astype(MXU_DTYPE))


def _qkv_weight(w_in):
    width = A_HEADS * A_HEAD_DIM
    col_scale = jnp.tile(jnp.repeat(jnp.array([A_HEAD_DIM ** -0.5 * LOG2E, 1.0, 1.0], f32), width), A_GROUPS)
    return (w_in * col_scale[None, :]).astype(MXU_DTYPE)


def kernel(x, c, w_in_a, w_out_a, rel_bias, w_in_b, w_gate_b, b_gate_b, gnorm_b, w_out_b, norm_mix, norm_ffn, w_ada, b_ada, w_up, conv_w, conv_b, w_down, norm_final):
    depth = w_ada.shape[0]
    mod = _adaln(c, w_ada, b_ada)
    bias = _rel_bias(rel_bias)
    for i in range(depth):
        gam_mix = norm_mix[i][None, :]
        j = i // 2
        mixer = None
        if i % 2 == 0:
            qkv = _qkv_proj(x, mod[i], gam_mix, _qkv_weight(w_in_a[j]))
            mixer = (_attention(qkv, bias), w_out_a[j])
        else:
            x = _gla_layer(x, mod[i], gam_mix, w_in_b[j], w_gate_b[j], b_gate_b[j], gnorm_b[j], w_out_b[j])
        last = i == depth - 1
        x = _conv_ffn(x, mod[i], norm_ffn[i][None, :], w_up[i], conv_w[i], conv_b[i], w_down[i],
                      gamma_final=norm_final[None, :] if last else None, mixer=mixer)
    return x
```
